```python
import jax
import jax.numpy as jnp
from jax import lax
import numpy as np


D_MODEL = 2048
BATCH = 8
SEQ = 2048
DEPTH = 2

HEAD_DIM = 64
GM_HEADS = 8
GM_WIDTH = GM_HEADS * HEAD_DIM
CHUNK = 128
RW_HEADS = 8
RW_WIDTH = RW_HEADS * HEAD_DIM
DECAY_LORA = 64
AAA_LORA = 64
GATE_LORA = 128
SB_HEADS = 16
SB_WIDTH = SB_HEADS * HEAD_DIM
Q_BLOCK = 128
N_BRANCH = 3
D_FF = 5632
N_EXPERTS = 8
TOP_K = 2
D_FF_EXPERT = D_FF // TOP_K
N_DENSE = (DEPTH + 1) // 2
N_MOE = DEPTH // 2
RMS_EPS = 1e-6
LN_EPS = 1e-5
RW_GN_EPS = 64e-5

GM_COLS = 2 * GM_WIDTH
RW_COLS = 3 * RW_WIDTH + DECAY_LORA + AAA_LORA + GATE_LORA
SB_COLS = 3 * SB_WIDTH
GATE_COLS = N_BRANCH * D_MODEL
GM_OFF = 0
RW_OFF = GM_OFF + GM_COLS
SB_OFF = RW_OFF + RW_COLS
GATE_OFF = SB_OFF + SB_COLS
N_IN = GATE_OFF + GATE_COLS
RW_SPLITS = [RW_WIDTH, 2 * RW_WIDTH, 3 * RW_WIDTH, 3 * RW_WIDTH + DECAY_LORA, 3 * RW_WIDTH + DECAY_LORA + AAA_LORA]

F32 = jnp.float32

kernel_name = "hybrid_gmlp_rwkv7_stickbreak_moe_block"


def rmsnorm(x, g):
    xf = x.astype(F32)
    y = xf * lax.rsqrt(jnp.mean(xf * xf, axis=-1, keepdims=True) + RMS_EPS)
    return (y * g.astype(F32)).astype(x.dtype)


def layernorm(x, g, b, eps):
    xf = x.astype(F32)
    mu = jnp.mean(xf, axis=-1, keepdims=True)
    var = jnp.mean(jnp.square(xf - mu), axis=-1, keepdims=True)
    return ((xf - mu) * lax.rsqrt(var + eps) * g.astype(F32) + b.astype(F32)).astype(x.dtype)


def swiglu(h, w1, w3, w2):
    return (jax.nn.silu(h @ w1) * (h @ w3)) @ w2


def chunked_spatial_gating(u, v, ln_g, ln_b, w_s, b_s):
    B, T, _ = u.shape
    n_chunks = T // CHUNK
    u = jax.nn.gelu(u)
    v = layernorm(jax.nn.gelu(v), ln_g, ln_b, LN_EPS)
    causal = jnp.tril(jnp.ones((CHUNK, CHUNK), dtype=bool))
    w = jnp.where(causal, w_s, jnp.zeros_like(w_s))
    vc = v.reshape(B, n_chunks, CHUNK, GM_HEADS, HEAD_DIM)
    mixed = jnp.einsum('hts,bcshd->bcthd', w, vc) + b_s.T[None, None, :, :, None]
    return u * mixed.reshape(B, T, GM_WIDTH)


def rwkv7_time_mix(p, mu, w0, w2, a0, a2, g2, k_k, k_a, r_k, ln_g, ln_b):
    B, T, _ = p.shape
    p_prev = jnp.pad(p, ((0, 0), (1, 0), (0, 0)))[:, :-1]
    p = p + (p_prev - p) * mu
    r, k, v, wl, al, gl = jnp.split(p, RW_SPLITS, axis=-1)
    d = (w0 + jnp.tanh(wl) @ w2).astype(F32)
    log_w = -jnp.exp(-jax.nn.softplus(-d) - 0.5)
    a = jax.nn.sigmoid(a0 + al @ a2)
    g = jax.nn.sigmoid(gl) @ g2

    def heads(t):
        return t.reshape(B, T, RW_HEADS, HEAD_DIM).astype(F32)

    r, k, v, a, decay = heads(r), heads(k), heads(v), heads(a), heads(jnp.exp(log_w))
    kk = k * k_k.reshape(RW_HEADS, HEAD_DIM).astype(F32)
    kk = kk / jnp.maximum(jnp.sqrt(jnp.sum(kk * kk, axis=-1, keepdims=True)), 1e-12)
    k = k * (1.0 + (a - 1.0) * k_a.reshape(RW_HEADS, HEAD_DIM).astype(F32))

    def step(S, inp):
        r_t, k_t, v_t, kk_t, a_t, w_t = inp
        sa = jnp.einsum('bhvk,bhk->bhv', S, -kk_t)
        S = (S * w_t[:, :, None, :]
             + sa[..., None] * (kk_t * a_t)[:, :, None, :]
             + v_t[..., None] * k_t[:, :, None, :])
        y_t = jnp.einsum('bhvk,bhk->bhv', S, r_t)
        return S, y_t

    def seq_first(t):
        return jnp.moveaxis(t, 1, 0)

    S0 = jnp.zeros((B, RW_HEADS, HEAD_DIM, HEAD_DIM), F32)
    _, y = lax.scan(step, S0, (seq_first(r), seq_first(k), seq_first(v),
                               seq_first(kk), seq_first(a), seq_first(decay)))
    y = jnp.moveaxis(y, 0, 1)
    m = jnp.mean(y, axis=-1, keepdims=True)
    var = jnp.mean(jnp.square(y - m), axis=-1, keepdims=True)
    y = ((y - m) * lax.rsqrt(var + RW_GN_EPS)).reshape(B, T, RW_WIDTH) * ln_g.astype(F32) + ln_b.astype(F32)
    bonus = jnp.sum(r * k * r_k.astype(F32), axis=-1, keepdims=True) * v
    y = (y + bonus.reshape(B, T, RW_WIDTH)) * g.astype(F32)
    return y.astype(p.dtype)


def stick_breaking_attention(q, k, v):
    B, T, H, Dh = q.shape
    scale = Dh ** -0.5
    outs = []
    for i in range(T // Q_BLOCK):
        q0, q1 = i * Q_BLOCK, (i + 1) * Q_BLOCK
        qb, kb, vb = q[:, q0:q1], k[:, :q1], v[:, :q1]
        z = jnp.einsum('bthd,bshd->bhts', qb, kb, preferred_element_type=F32) * scale
        t_pos = q0 + jnp.arange(Q_BLOCK)[:, None]
        s_pos = jnp.arange(q1)[None, :]
        causal = s_pos < t_pos
        log_not = jnp.where(causal, jax.nn.log_sigmoid(-z), 0.0)
        after = lax.cumsum(log_not, axis=3, reverse=True) - log_not
        att = jnp.where(causal, jnp.exp(jax.nn.log_sigmoid(z) + after), 0.0)
        outs.append(jnp.einsum('bhts,bshd->bthd', att.astype(v.dtype), vb))
    return jnp.concatenate(outs, axis=1)


def moe_swiglu(h, router_w, router_b, w1, w3, w2):
    B, T, Dm = h.shape
    hf = h.reshape(B * T, Dm)
    logits = (hf @ router_w).astype(F32) + router_b.astype(F32)
    top_val, top_idx = lax.top_k(logits, TOP_K)
    top_p = jax.nn.softmax(top_val, axis=-1)
    combine = jnp.sum(jax.nn.one_hot(top_idx, N_EXPERTS, dtype=F32) * top_p[..., None], axis=1)
    out = jnp.zeros_like(hf)
    for e in range(N_EXPERTS):
        out = out + combine[:, e:e + 1].astype(h.dtype) * swiglu(hf, w1[e], w3[e], w2[e])
    return out.reshape(B, T, Dm)


def setup_inputs(seed: int = 0) -> dict:
    key = jax.random.key(seed)
    keys = jax.random.split(key, 40)
    counter = [0]

    def nxt():
        kk = keys[counter[0]]
        counter[0] += 1
        return kk

    def nrm(shape, scale):
        return scale * jax.random.normal(nxt(), shape, F32)

    L = DEPTH
    return {
        "x": nrm((BATCH, SEQ, D_MODEL), 1.0),
        "norm_mix": 1.0 + nrm((L, D_MODEL), 0.02),
        "w_in": nrm((L, D_MODEL, N_IN), D_MODEL ** -0.5),
        "gate_b": nrm((L, N_BRANCH, D_MODEL), 0.1),
        "gm_ln_g": 1.0 + nrm((L, GM_WIDTH), 0.02),
        "gm_ln_b": nrm((L, GM_WIDTH), 0.02),
        "gm_ws": nrm((L, GM_HEADS, CHUNK, CHUNK), CHUNK ** -0.5),
        "gm_bs": 1.0 + nrm((L, GM_HEADS, CHUNK), 0.02),
        "rw_mu": jax.random.uniform(nxt(), (L, RW_COLS), F32),
        "rw_w0": jax.random.uniform(nxt(), (L, RW_WIDTH), F32, -6.0, 1.0),
        "rw_w2": nrm((L, DECAY_LORA, RW_WIDTH), 0.5 * DECAY_LORA ** -0.5),
        "rw_a0": nrm((L, RW_WIDTH), 0.1),
        "rw_a2": nrm((L, AAA_LORA, RW_WIDTH), 0.5 * AAA_LORA ** -0.5),
        "rw_g2": nrm((L, GATE_LORA, RW_WIDTH), GATE_LORA ** -0.5),
        "rw_kk": 0.85 + nrm((L, RW_WIDTH), 0.05),
        "rw_ka": 1.0 + nrm((L, RW_WIDTH), 0.05),
        "rw_rk": nrm((L, RW_HEADS, HEAD_DIM), 0.1),
        "rw_ln_g": 1.0 + nrm((L, RW_WIDTH), 0.02),
        "rw_ln_b": nrm((L, RW_WIDTH), 0.02),
        "p_gm": nrm((L, GM_WIDTH, D_MODEL), GM_WIDTH ** -0.5),
        "p_rw": nrm((L, RW_WIDTH, D_MODEL), RW_WIDTH ** -0.5),
        "p_sb": nrm((L, SB_WIDTH, D_MODEL), SB_WIDTH ** -0.5),
        "w_o": nrm((L, D_MODEL, D_MODEL), D_MODEL ** -0.5),
        "norm_ffn": 1.0 + nrm((L, D_MODEL), 0.02),
        "ffn_w1": nrm((N_DENSE, D_MODEL, D_FF), D_MODEL ** -0.5),
        "ffn_w3": nrm((N_DENSE, D_MODEL, D_FF), D_MODEL ** -0.5),
        "ffn_w2": nrm((N_DENSE, D_FF, D_MODEL), D_FF ** -0.5),
        "router_w": nrm((N_MOE, D_MODEL, N_EXPERTS), D_MODEL ** -0.5),
        "router_b": nrm((N_MOE, N_EXPERTS), 0.01),
        "moe_w1": nrm((N_MOE, N_EXPERTS, D_MODEL, D_FF_EXPERT), D_MODEL ** -0.5),
        "moe_w3": nrm((N_MOE, N_EXPERTS, D_MODEL, D_FF_EXPERT), D_MODEL ** -0.5),
        "moe_w2": nrm((N_MOE, N_EXPERTS, D_FF_EXPERT, D_MODEL), D_FF_EXPERT ** -0.5),
        "norm_out": 1.0 + nrm((D_MODEL,), 0.02),
    }


def reference(x, norm_mix, w_in, gate_b, gm_ln_g, gm_ln_b, gm_ws, gm_bs, rw_mu, rw_w0, rw_w2,
              rw_a0, rw_a2, rw_g2, rw_kk, rw_ka, rw_rk, rw_ln_g, rw_ln_b, p_gm, p_rw, p_sb, w_o,
              norm_ffn, ffn_w1, ffn_w3, ffn_w2, router_w, router_b, moe_w1, moe_w3, moe_w2, norm_out):
    B, T, _ = x.shape
    h = x
    for l in range(DEPTH):
        n = rmsnorm(h, norm_mix[l])
        proj = n @ w_in[l]
        y_gm = chunked_spatial_gating(proj[..., GM_OFF:GM_OFF + GM_WIDTH],
                                      proj[..., GM_OFF + GM_WIDTH:RW_OFF],
                                      gm_ln_g[l], gm_ln_b[l], gm_ws[l], gm_bs[l])
        y_rw = rwkv7_time_mix(proj[..., RW_OFF:SB_OFF], rw_mu[l], rw_w0[l], rw_w2[l], rw_a0[l],
                              rw_a2[l], rw_g2[l], rw_kk[l], rw_ka[l], rw_rk[l], rw_ln_g[l], rw_ln_b[l])
        sbp = proj[..., SB_OFF:GATE_OFF].reshape(B, T, 3, SB_HEADS, HEAD_DIM)
        y_sb = stick_breaking_attention(sbp[:, :, 0], sbp[:, :, 1], sbp[:, :, 2]).reshape(B, T, SB_WIDTH)
        gates = jax.nn.sigmoid(proj[..., GATE_OFF:].reshape(B, T, N_BRANCH, D_MODEL) + gate_b[l])
        merged = (gates[:, :, 0] * (y_gm @ p_gm[l])
                  + gates[:, :, 1] * (y_rw @ p_rw[l])
                  + gates[:, :, 2] * (y_sb @ p_sb[l]))
        h = h + merged @ w_o[l]
        n = rmsnorm(h, norm_ffn[l])
        j = l // 2
        if l % 2 == 0:
            h = h + swiglu(n, ffn_w1[j], ffn_w3[j], ffn_w2[j])
        else:
            h = h + moe_swiglu(n, router_w[j], router_b[j], moe_w1[j], moe_w3[j], moe_w2[j])
    return rmsnorm(h, norm_out)
```

```python
import functools

import jax
import jax.numpy as jnp
from jax import lax
from jax.experimental import pallas as pl
from jax.experimental.pallas import tpu as pltpu

F32 = jnp.float32
BF16 = jnp.bfloat16

D_MODEL = 2048
DEPTH = 2
HEAD_DIM = 64
GM_HEADS = 8
GM_WIDTH = GM_HEADS * HEAD_DIM
CHUNK = 128
RW_HEADS = 8
RW_WIDTH = RW_HEADS * HEAD_DIM
DECAY_LORA = 64
AAA_LORA = 64
GATE_LORA = 128
SB_HEADS = 16
SB_WIDTH = SB_HEADS * HEAD_DIM
N_BRANCH = 3
D_FF = 5632
N_EXPERTS = 8
TOP_K = 2
D_FF_EXPERT = D_FF // TOP_K
RMS_EPS = 1e-6
LN_EPS = 1e-5
RW_GN_EPS = 64e-5

GM_COLS = 2 * GM_WIDTH
RW_COLS = 3 * RW_WIDTH + DECAY_LORA + AAA_LORA + GATE_LORA
SB_COLS = 3 * SB_WIDTH
GATE_COLS = N_BRANCH * D_MODEL
GM_OFF = 0
RW_OFF = GM_OFF + GM_COLS
SB_OFF = RW_OFF + RW_COLS
GATE_OFF = SB_OFF + SB_COLS
N_IN = GATE_OFF + GATE_COLS

V7X_VMEM_BYTES = 64 * 1024 * 1024
LANES = 128
BF16_SUBLANES = 16
RW_CHUNK = 64
RW_TILE = 128
SB_TILE = 256
ROUTER_LANES = 128


def _vmem_limit(nbytes):
    return int(min(max(nbytes, 32 * 1024 * 1024), V7X_VMEM_BYTES - 6 * 1024 * 1024))


def _dot(a, b):
    return jnp.dot(a, b, preferred_element_type=F32)


def _dot_nt(a, b):
    return lax.dot_general(a, b, (((1,), (1,)), ((), ())), preferred_element_type=F32)


def _dot_tn(a, b):
    return lax.dot_general(a, b, (((0,), (0,)), ((), ())), preferred_element_type=F32)


def _dot_split(x, ones_bf16):
    hi = x.astype(BF16)
    lo = (x - hi.astype(F32)).astype(BF16)
    return _dot(hi, ones_bf16) + _dot(lo, ones_bf16)


def _rms(x, g):
    ms = jnp.mean(x * x, axis=-1, keepdims=True)
    return x * lax.rsqrt(ms + RMS_EPS) * g


def _rmsnorm_kernel(x_ref, g_ref, o_ref):
    o_ref[...] = _rms(x_ref[...], g_ref[...]).astype(o_ref.dtype)


def rmsnorm_call(x, g, out_dtype, tm=512):
    M, D = x.shape
    tm = min(tm, M)
    return pl.pallas_call(
        _rmsnorm_kernel,
        grid=(M // tm,),
        in_specs=[pl.BlockSpec((tm, D), lambda i: (i, 0)),
                  pl.BlockSpec((1, D), lambda i: (0, 0))],
        out_specs=pl.BlockSpec((tm, D), lambda i: (i, 0)),
        out_shape=jax.ShapeDtypeStruct((M, D), out_dtype),
        compiler_params=pltpu.CompilerParams(dimension_semantics=("parallel",)),
        name="rmsnorm",
    )(x, g.reshape(1, D))


def _mm_kernel(a_ref, w_ref, o_ref):
    o_ref[...] = _dot(a_ref[...], w_ref[...]).astype(o_ref.dtype)


def _mm_gate_kernel(a_ref, w_ref, b_ref, o_ref):
    o_ref[...] = jax.nn.sigmoid(_dot(a_ref[...], w_ref[...]) + b_ref[...]).astype(o_ref.dtype)


def matmul_call(a, w, out_dtype, tn, bias=None, tm=1024, name="matmul"):
    M, K = a.shape
    N = w.shape[1]
    tm = min(tm, M)
    in_specs = [pl.BlockSpec((tm, K), lambda i, j: (i, 0)),
                pl.BlockSpec((K, tn), lambda i, j: (0, j))]
    args = [a, w]
    kern = _mm_kernel
    if bias is not None:
        in_specs.append(pl.BlockSpec((1, tn), lambda i, j: (0, j)))
        args.append(bias.reshape(1, N))
        kern = _mm_gate_kernel
    osz = jnp.dtype(out_dtype).itemsize
    need = 2 * (tm * K * 2 + K * tn * 2 + tm * tn * osz) + 2 * tm * tn * 4
    return pl.pallas_call(
        kern,
        grid=(M // tm, N // tn),
        in_specs=in_specs,
        out_specs=pl.BlockSpec((tm, tn), lambda i, j: (i, j)),
        out_shape=jax.ShapeDtypeStruct((M, N), out_dtype),
        compiler_params=pltpu.CompilerParams(
            dimension_semantics=("parallel", "arbitrary"),
            vmem_limit_bytes=_vmem_limit(need + (8 << 20))),
        name=name,
    )(*args)


def _gm_kernel(uv_ref, lng_ref, lnb_ref, ws_ref, bias_ref, o_ref, *, n_chunks):
    ii = lax.broadcasted_iota(jnp.int32, (CHUNK, CHUNK), 0)
    jj = lax.broadcasted_iota(jnp.int32, (CHUNK, CHUNK), 1)
    causal = jj <= ii
    ws = [jnp.where(causal, ws_ref[h], 0.0).astype(BF16) for h in range(GM_HEADS)]
    for c in range(n_chunks):
        rows = slice(c * CHUNK, (c + 1) * CHUNK)
        u = jax.nn.gelu(uv_ref[rows, 0:GM_WIDTH].astype(F32))
        v = jax.nn.gelu(uv_ref[rows, GM_WIDTH:2 * GM_WIDTH].astype(F32))
        mu = jnp.mean(v, axis=-1, keepdims=True)
        vc = v - mu
        var = jnp.mean(vc * vc, axis=-1, keepdims=True)
        vn = (vc * lax.rsqrt(var + LN_EPS) * lng_ref[...] + lnb_ref[...]).astype(BF16)
        mixed = jnp.concatenate(
            [_dot(ws[h], vn[:, h * HEAD_DIM:(h + 1) * HEAD_DIM]) for h in range(GM_HEADS)], axis=-1)
        o_ref[rows, :] = (u * (mixed + bias_ref[...])).astype(o_ref.dtype)


def gm_call(uv, ln_g, ln_b, w_s, b_s, n_chunks=2):
    M = uv.shape[0]
    tt = CHUNK * n_chunks
    bias = jnp.repeat(b_s.T, HEAD_DIM, axis=1)
    return pl.pallas_call(
        functools.partial(_gm_kernel, n_chunks=n_chunks),
        grid=(M // tt,),
        in_specs=[pl.BlockSpec((tt, GM_COLS), lambda i: (i, 0)),
                  pl.BlockSpec((1, GM_WIDTH), lambda i: (0, 0)),
                  pl.BlockSpec((1, GM_WIDTH), lambda i: (0, 0)),
                  pl.BlockSpec((GM_HEADS, CHUNK, CHUNK), lambda i: (0, 0, 0)),
                  pl.BlockSpec((CHUNK, GM_WIDTH), lambda i: (0, 0))],
        out_specs=pl.BlockSpec((tt, GM_WIDTH), lambda i: (i, 0)),
        out_shape=jax.ShapeDtypeStruct((M, GM_WIDTH), BF16),
        compiler_params=pltpu.CompilerParams(dimension_semantics=("parallel",)),
        name="gmlp_gating",
    )(uv, ln_g.reshape(1, -1), ln_b.reshape(1, -1), w_s, bias)


def _rw_kernel(p_ref, prev_ref, mu_ref, w0_ref, w2_ref, a0_ref, a2_ref, g2_ref, kk_ref, ka_ref,
               rk_ref, lng_ref, lnb_ref, hsum_ref, o_ref, st_ref, y_ref):
    C = RW_CHUNK
    i = pl.program_id(1)

    @pl.when(i == 0)
    def _():
        st_ref[...] = jnp.zeros_like(st_ref)

    hsum = hsum_ref[...]

    p = p_ref[...].astype(F32)
    prev_row = prev_ref[BF16_SUBLANES - 1:BF16_SUBLANES, :].astype(F32)
    prev_row = jnp.where(i == 0, 0.0, prev_row)
    row = lax.broadcasted_iota(jnp.int32, p.shape, 0)
    p_prev = jnp.where(row == 0, prev_row, pltpu.roll(p, 1, axis=0))
    ps = p + (p_prev - p) * mu_ref[...]
    W = RW_WIDTH
    r = ps[:, 0:W]
    k = ps[:, W:2 * W]
    v = ps[:, 2 * W:3 * W]
    wl = ps[:, 3 * W:3 * W + DECAY_LORA]
    al = ps[:, 3 * W + DECAY_LORA:3 * W + DECAY_LORA + AAA_LORA]
    gl = ps[:, 3 * W + DECAY_LORA + AAA_LORA:RW_COLS]

    d = w0_ref[...] + _dot(jnp.tanh(wl).astype(BF16), w2_ref[...])
    log_w = -jnp.exp(-jax.nn.softplus(-d) - 0.5)
    a = jax.nn.sigmoid(a0_ref[...] + _dot(al.astype(BF16), a2_ref[...]))
    g = _dot(jax.nn.sigmoid(gl).astype(BF16), g2_ref[...])
    kk = k * kk_ref[...]
    kk = kk / jnp.maximum(jnp.sqrt(_dot_split(kk * kk, hsum)), 1e-12)
    k = k * (1.0 + (a - 1.0) * ka_ref[...])
    bonus = _dot_split(r * k * rk_ref[...], hsum) * v

    ti = lax.broadcasted_iota(jnp.int32, (C, C), 0)
    tj = lax.broadcasted_iota(jnp.int32, (C, C), 1)
    tri_incl = (tj <= ti).astype(BF16)
    strict = tj < ti
    incl = tj <= ti
    eye = ti == tj
    blk_masks = []
    s = 2
    while s < C:
        blk_masks.append((ti // (2 * s) == tj // (2 * s)) & (ti // s != tj // s))
        s *= 2
    pair_mask = (ti // 2 == tj // 2) & strict

    for c in range(RW_TILE // C):
        rows = slice(c * C, (c + 1) * C)
        lw = log_w[rows]
        cum = _cumsum_rows(lw, tri_incl)
        gam = jnp.exp(cum)
        gam_prev = jnp.exp(cum - lw)
        gam_inv = jnp.exp(-cum)
        g_end = gam[C - 1:C, :]
        rh = (r[rows] * gam).astype(BF16)
        ah = (-kk[rows] * gam_prev).astype(BF16)
        bh_f = kk[rows] * a[rows] * gam_inv
        kh_f = k[rows] * gam_inv
        bh = bh_f.astype(BF16)
        kh = kh_f.astype(BF16)
        bt = (bh_f * g_end).astype(BF16)
        kt = (kh_f * g_end).astype(BF16)
        vb = v[rows].astype(BF16)
        for h in range(RW_HEADS):
            hs = slice(h * HEAD_DIM, (h + 1) * HEAD_DIM)
            ar = jnp.concatenate([ah[:, hs], rh[:, hs]], axis=0)
            bk = jnp.concatenate([bh[:, hs], kh[:, hs]], axis=0)
            prod = _dot_nt(ar, bk)
            lab = jnp.where(strict, prod[0:C, 0:C], 0.0)
            lak = jnp.where(strict, prod[0:C, C:2 * C], 0.0)
            mrb = jnp.where(incl, prod[C:2 * C, 0:C], 0.0)
            mrk = jnp.where(incl, prod[C:2 * C, C:2 * C], 0.0)
            tinv = jnp.where(eye, 1.0, jnp.where(pair_mask, lab, 0.0))
            for bm in blk_masks:
                lo = jnp.where(bm, lab, 0.0).astype(BF16)
                tb = tinv.astype(BF16)
                tinv = tinv + _dot(_dot(tb, lo).astype(BF16), tb)
            vh = vb[:, hs]
            lv = _dot(jnp.concatenate([lak, mrk], axis=0).astype(BF16), vh)
            rhs1 = jnp.concatenate([ah[:, hs].astype(F32), lv[0:C]], axis=1).astype(BF16)
            apup = _dot(tinv.astype(BF16), rhs1).astype(BF16)
            ry = _dot(mrb.astype(BF16), apup) + jnp.concatenate(
                [rh[:, hs].astype(F32), lv[C:2 * C]], axis=1)
            gh = _dot_tn(bt[:, hs], apup)
            g_mat = gh[:, 0:HEAD_DIM] + jnp.where(eye, g_end[:, hs], 0.0)
            h_mat = gh[:, HEAD_DIM:2 * HEAD_DIM] + _dot_tn(kt[:, hs], vh)
            st = st_ref[h]
            stb = st.astype(BF16)
            y_ref[rows, hs] = _dot(ry[:, 0:HEAD_DIM].astype(BF16), stb) + ry[:, HEAD_DIM:2 * HEAD_DIM]
            st_ref[h] = _dot(g_mat.astype(BF16), stb) + h_mat

    y = y_ref[...]
    inv_n = 1.0 / HEAD_DIM
    m = _dot_split(y, hsum) * inv_n
    yc = y - m
    var = _dot_split(yc * yc, hsum) * inv_n
    yn = yc * lax.rsqrt(var + RW_GN_EPS) * lng_ref[...] + lnb_ref[...]
    o_ref[...] = ((yn + bonus) * g).astype(o_ref.dtype)


def _cumsum_rows(x, tri_incl_bf16):
    hi = x.astype(BF16)
    lo = (x - hi.astype(F32)).astype(BF16)
    return _dot(tri_incl_bf16, hi) + _dot(tri_incl_bf16, lo)


def rw_call(p, B, T, mu, w0, w2, a0, a2, g2, k_k, k_a, r_k, ln_g, ln_b):
    nt = T // RW_TILE
    sub = RW_TILE // BF16_SUBLANES
    hid = jnp.arange(RW_WIDTH) // HEAD_DIM
    hsum = (hid[:, None] == hid[None, :]).astype(BF16)
    row = lambda a: a.reshape(1, -1).astype(F32)
    vec = lambda n: pl.BlockSpec((1, n), lambda b, i: (0, 0))
    return pl.pallas_call(
        _rw_kernel,
        grid=(B, nt),
        in_specs=[pl.BlockSpec((None, RW_TILE, RW_COLS), lambda b, i: (b, i, 0)),
                  pl.BlockSpec((None, BF16_SUBLANES, RW_COLS),
                               lambda b, i: (b, jnp.maximum(i * sub - 1, 0), 0)),
                  vec(RW_COLS), vec(RW_WIDTH),
                  pl.BlockSpec((DECAY_LORA, RW_WIDTH), lambda b, i: (0, 0)),
                  vec(RW_WIDTH),
                  pl.BlockSpec((AAA_LORA, RW_WIDTH), lambda b, i: (0, 0)),
                  pl.BlockSpec((GATE_LORA, RW_WIDTH), lambda b, i: (0, 0)),
                  vec(RW_WIDTH), vec(RW_WIDTH), vec(RW_WIDTH), vec(RW_WIDTH), vec(RW_WIDTH),
                  pl.BlockSpec((RW_WIDTH, RW_WIDTH), lambda b, i: (0, 0))],
        out_specs=pl.BlockSpec((None, RW_TILE, RW_WIDTH), lambda b, i: (b, i, 0)),
        out_shape=jax.ShapeDtypeStruct((B, T, RW_WIDTH), BF16),
        scratch_shapes=[pltpu.VMEM((RW_HEADS, HEAD_DIM, HEAD_DIM), F32),
                        pltpu.VMEM((RW_TILE, RW_WIDTH), F32)],
        compiler_params=pltpu.CompilerParams(dimension_semantics=("parallel", "arbitrary")),
        name="rwkv7_time_mix",
    )(p, p, row(mu), row(w0), w2.astype(BF16), row(a0), a2.astype(BF16), g2.astype(BF16),
      row(k_k), row(k_a), row(r_k), row(ln_g), row(ln_b), hsum)


def _sb_block(q, kj, vj, carry, suffix_ones, causal):
    z = _dot_nt(q, kj)
    sp = jnp.maximum(z, 0.0) + jnp.log1p(jnp.exp(-jnp.abs(z)))
    log_not = -sp
    if causal is not None:
        log_not = jnp.where(causal, log_not, 0.0)
    after = _dot_split(log_not, suffix_ones) + carry
    att = jnp.exp(z - sp + after)
    if causal is not None:
        att = jnp.where(causal, att, 0.0)
    return _dot(att.astype(BF16), vj), jnp.sum(log_not, axis=-1, keepdims=True)


def _sb_kernel(q_ref, k_ref, v_ref, o_ref, acc_ref, carry_ref, *, T):
    tq = SB_TILE
    nq = T // tq
    ti = lax.broadcasted_iota(jnp.int32, (tq, tq), 0)
    tj = lax.broadcasted_iota(jnp.int32, (tq, tq), 1)
    suffix_ones = (ti > tj).astype(BF16)
    causal = tj < ti
    scale = HEAD_DIM ** -0.5
    for hh in range(LANES // HEAD_DIM):
        cs = slice(hh * HEAD_DIM, (hh + 1) * HEAD_DIM)

        def q_body(i, _):
            r0 = pl.multiple_of(i * tq, tq)
            q = q_ref[pl.ds(r0, tq), cs] * scale
            out, rs = _sb_block(q, k_ref[pl.ds(r0, tq), cs], v_ref[pl.ds(r0, tq), cs],
                                jnp.zeros((tq, 1), F32), suffix_ones, causal)
            acc_ref[...] = out
            carry_ref[...] = rs

            def k_body(jr, _):
                c0 = pl.multiple_of((i - 1 - jr) * tq, tq)
                out, rs = _sb_block(q, k_ref[pl.ds(c0, tq), cs], v_ref[pl.ds(c0, tq), cs],
                                    carry_ref[...], suffix_ones, None)
                acc_ref[...] += out
                carry_ref[...] += rs
                return 0

            lax.fori_loop(0, i, k_body, 0)
            o_ref[pl.ds(r0, tq), cs] = acc_ref[...].astype(o_ref.dtype)
            return 0

        lax.fori_loop(0, nq, q_body, 0)


def sb_call(qkv, B, T):
    npair = SB_WIDTH // LANES
    blk = lambda off: pl.BlockSpec((None, T, LANES), lambda b, hp: (b, 0, off + hp))
    return pl.pallas_call(
        functools.partial(_sb_kernel, T=T),
        grid=(B, npair),
        in_specs=[blk(0), blk(npair), blk(2 * npair)],
        out_specs=pl.BlockSpec((None, T, LANES), lambda b, hp: (b, 0, hp)),
        out_shape=jax.ShapeDtypeStruct((B, T, SB_WIDTH), BF16),
        scratch_shapes=[pltpu.VMEM((SB_TILE, HEAD_DIM), F32),
                        pltpu.VMEM((SB_TILE, 1), F32)],
        compiler_params=pltpu.CompilerParams(dimension_semantics=("parallel", "parallel")),
        name="stick_breaking_attention",
    )(qkv, qkv, qkv)


def _merge_kernel(ygm_ref, yrw_ref, ysb_ref, gate_ref, h_ref, pgm_ref, prw_ref, psb_ref, wo_ref,
                  g_ref, ho_ref, no_ref):
    D = D_MODEL
    merged = gate_ref[:, 0:D].astype(F32) * _dot(ygm_ref[...], pgm_ref[...])
    merged += gate_ref[:, D:2 * D].astype(F32) * _dot(yrw_ref[...], prw_ref[...])
    merged += gate_ref[:, 2 * D:3 * D].astype(F32) * _dot(ysb_ref[...], psb_ref[...])
    hn = h_ref[...] + _dot(merged.astype(BF16), wo_ref[...])
    ho_ref[...] = hn
    no_ref[...] = _rms(hn, g_ref[...]).astype(no_ref.dtype)


def _resident(shape):
    return pl.BlockSpec(shape, lambda i: (0,) * len(shape), pipeline_mode=pl.Buffered(1))


def merge_call(y_gm, y_rw, y_sb, gates, h, p_gm, p_rw, p_sb, w_o, g_next, tm=256):
    M, D = h.shape
    tm = min(tm, M)
    rowblk = lambda n: pl.BlockSpec((tm, n), lambda i: (i, 0))
    wbytes = 2 * (GM_WIDTH + RW_WIDTH + SB_WIDTH + D) * D
    need = wbytes + 2 * tm * (2 * (GM_WIDTH + RW_WIDTH + SB_WIDTH + GATE_COLS) + 4 * D + 4 * D + 2 * D) \
        + 6 * tm * D * 4
    return pl.pallas_call(
        _merge_kernel,
        grid=(M // tm,),
        in_specs=[rowblk(GM_WIDTH), rowblk(RW_WIDTH), rowblk(SB_WIDTH), rowblk(GATE_COLS), rowblk(D),
                  _resident((GM_WIDTH, D)), _resident((RW_WIDTH, D)), _resident((SB_WIDTH, D)),
                  _resident((D, D)), _resident((1, D))],
        out_specs=[rowblk(D), rowblk(D)],
        out_shape=[jax.ShapeDtypeStruct((M, D), F32), jax.ShapeDtypeStruct((M, D), BF16)],
        compiler_params=pltpu.CompilerParams(dimension_semantics=("parallel",),
                                             vmem_limit_bytes=_vmem_limit(need)),
        name="merge_out_proj",
    )(y_gm, y_rw, y_sb, gates, h, p_gm, p_rw, p_sb, w_o, g_next.reshape(1, D))


def _ffn_kernel(n_ref, h_ref, w1_ref, w3_ref, w2_ref, g_ref, *refs, emit_h):
    if emit_h:
        ho_ref, no_ref, acc_ref = refs
    else:
        no_ref, acc_ref = refs
    f = pl.program_id(1)

    @pl.when(f == 0)
    def _():
        acc_ref[...] = jnp.zeros_like(acc_ref)

    n = n_ref[...]
    act = (jax.nn.silu(_dot(n, w1_ref[...])) * _dot(n, w3_ref[...])).astype(BF16)
    acc_ref[...] += _dot(act, w2_ref[...])

    @pl.when(f == pl.num_programs(1) - 1)
    def _():
        hn = h_ref[...] + acc_ref[...]
        if emit_h:
            ho_ref[...] = hn
        no_ref[...] = _rms(hn, g_ref[...]).astype(no_ref.dtype)


def ffn_call(n, h, w1, w3, w2, g_next, n_dtype, emit_h, tm=512, tf=512):
    M, D = h.shape
    F = w1.shape[1]
    tm = min(tm, M)
    rowblk = pl.BlockSpec((tm, D), lambda i, f: (i, 0))
    out_specs = [rowblk]
    out_shape = [jax.ShapeDtypeStruct((M, D), n_dtype)]
    if emit_h:
        out_specs = [rowblk, rowblk]
        out_shape = [jax.ShapeDtypeStruct((M, D), F32)] + out_shape
    need = 2 * tm * D * (2 + 4 + 4 * len(out_shape)) + tm * D * 4 + 2 * 3 * D * tf * 2 + 4 * tm * tf * 4
    return pl.pallas_call(
        functools.partial(_ffn_kernel, emit_h=emit_h),
        grid=(M // tm, F // tf),
        in_specs=[rowblk, rowblk,
                  pl.BlockSpec((D, tf), lambda i, f: (0, f)),
                  pl.BlockSpec((D, tf), lambda i, f: (0, f)),
                  pl.BlockSpec((tf, D), lambda i, f: (f, 0)),
                  pl.BlockSpec((1, D), lambda i, f: (0, 0))],
        out_specs=out_specs,
        out_shape=out_shape,
        scratch_shapes=[pltpu.VMEM((tm, D), F32)],
        compiler_params=pltpu.CompilerParams(dimension_semantics=("parallel", "arbitrary"),
                                             vmem_limit_bytes=_vmem_limit(need + (8 << 20))),
        name="swiglu_ffn",
    )(n, h, w1, w3, w2, g_next.reshape(1, D))


def _router_kernel(h_ref, g_ref, wh_ref, wl_ref, b_ref, o_ref):
    n = _rms(h_ref[...], g_ref[...])
    nh = n.astype(BF16)
    nl = (n - nh.astype(F32)).astype(BF16)
    logits = _dot(nh, wh_ref[...]) + _dot(nl, wh_ref[...]) + _dot(nh, wl_ref[...]) + b_ref[...]
    lane = lax.broadcasted_iota(jnp.int32, logits.shape, 1)
    neg = jnp.float32(-jnp.inf)
    logits = jnp.where(lane < N_EXPERTS, logits, neg)
    m1 = jnp.max(logits, axis=-1, keepdims=True)
    i1 = jnp.min(jnp.where(logits == m1, lane, ROUTER_LANES), axis=-1, keepdims=True)
    rest = jnp.where(lane == i1, neg, logits)
    m2 = jnp.max(rest, axis=-1, keepdims=True)
    i2 = jnp.min(jnp.where(rest == m2, lane, ROUTER_LANES), axis=-1, keepdims=True)
    e2 = jnp.exp(m2 - m1)
    p1 = 1.0 / (1.0 + e2)
    p2 = e2 / (1.0 + e2)
    o_ref[...] = jnp.where(lane == i1, p1, 0.0) + jnp.where(lane == i2, p2, 0.0)


def router_call(h, g, router_w, router_b, tm=512):
    M, D = h.shape
    tm = min(tm, M)
    wpad = jnp.zeros((D, ROUTER_LANES), F32).at[:, :N_EXPERTS].set(router_w)
    w_hi = wpad.astype(BF16)
    w_lo = (wpad - w_hi.astype(F32)).astype(BF16)
    bpad = jnp.zeros((1, ROUTER_LANES), F32).at[0, :N_EXPERTS].set(router_b)
    const = lambda shape: pl.BlockSpec(shape, lambda i: (0, 0))
    return pl.pallas_call(
        _router_kernel,
        grid=(M // tm,),
        in_specs=[pl.BlockSpec((tm, D), lambda i: (i, 0)), const((1, D)),
                  const((D, ROUTER_LANES)), const((D, ROUTER_LANES)), const((1, ROUTER_LANES))],
        out_specs=pl.BlockSpec((tm, ROUTER_LANES), lambda i: (i, 0)),
        out_shape=jax.ShapeDtypeStruct((M, ROUTER_LANES), F32),
        compiler_params=pltpu.CompilerParams(dimension_semantics=("parallel",)),
        name="moe_router",
    )(h, g.reshape(1, D), w_hi, w_lo, bpad)


def _moe_dense_kernel(n_ref, c_ref, w1_ref, w3_ref, w2_ref, o_ref):
    e = pl.program_id(1)
    f = pl.program_id(2)

    @pl.when((e == 0) & (f == 0))
    def _():
        o_ref[...] = jnp.zeros_like(o_ref)

    lane = lax.broadcasted_iota(jnp.int32, c_ref.shape, 1)
    ce = jnp.sum(jnp.where(lane == e, c_ref[...], 0.0), axis=-1, keepdims=True)
    n = n_ref[...]
    act = jax.nn.silu(_dot(n, w1_ref[...])) * _dot(n, w3_ref[...])
    o_ref[...] += _dot((act * ce).astype(BF16), w2_ref[...])


def moe_dense_call(n, comb, w1, w3, w2, tm=1024, tf=256):
    M, D = n.shape
    E, _, Fe = w1.shape
    tm = min(tm, M)
    need = 2 * tm * D * 2 + 2 * tm * D * 4 + 2 * 3 * D * tf * 2 + 4 * tm * tf * 4
    return pl.pallas_call(
        _moe_dense_kernel,
        grid=(M // tm, E, Fe // tf),
        in_specs=[pl.BlockSpec((tm, D), lambda i, e, f: (i, 0)),
                  pl.BlockSpec((tm, ROUTER_LANES), lambda i, e, f: (i, 0)),
                  pl.BlockSpec((None, D, tf), lambda i, e, f: (e, 0, f)),
                  pl.BlockSpec((None, D, tf), lambda i, e, f: (e, 0, f)),
                  pl.BlockSpec((None, tf, D), lambda i, e, f: (e, f, 0))],
        out_specs=pl.BlockSpec((tm, D), lambda i, e, f: (i, 0)),
        out_shape=jax.ShapeDtypeStruct((M, D), F32),
        compiler_params=pltpu.CompilerParams(
            dimension_semantics=("parallel", "arbitrary", "arbitrary"),
            vmem_limit_bytes=_vmem_limit(need + (8 << 20))),
        name="moe_experts",
    )(n, comb, w1, w3, w2)


def _addnorm_kernel(h_ref, d_ref, g_ref, *refs, emit_h):
    hn = h_ref[...] + d_ref[...]
    if emit_h:
        refs[0][...] = hn
    refs[-1][...] = _rms(hn, g_ref[...]).astype(refs[-1].dtype)


def addnorm_call(h, delta, g_next, n_dtype, emit_h, tm=512):
    M, D = h.shape
    tm = min(tm, M)
    rowblk = pl.BlockSpec((tm, D), lambda i: (i, 0))
    out_specs = [rowblk]
    out_shape = [jax.ShapeDtypeStruct((M, D), n_dtype)]
    if emit_h:
        out_specs = [rowblk, rowblk]
        out_shape = [jax.ShapeDtypeStruct((M, D), F32)] + out_shape
    return pl.pallas_call(
        functools.partial(_addnorm_kernel, emit_h=emit_h),
        grid=(M // tm,),
        in_specs=[rowblk, rowblk, pl.BlockSpec((1, D), lambda i: (0, 0))],
        out_specs=out_specs,
        out_shape=out_shape,
        compiler_params=pltpu.CompilerParams(dimension_semantics=("parallel",),
                                             vmem_limit_bytes=_vmem_limit(10 * tm * D * 4)),
        name="residual_rmsnorm",
    )(h, delta, g_next.reshape(1, D))


def kernel(x, norm_mix, w_in, gate_b, gm_ln_g, gm_ln_b, gm_ws, gm_bs, rw_mu, rw_w0, rw_w2, rw_a0, rw_a2,
           rw_g2, rw_kk, rw_ka, rw_rk, rw_ln_g, rw_ln_b, p_gm, p_rw, p_sb, w_o, norm_ffn, ffn_w1, ffn_w3,
           ffn_w2, router_w, router_b, moe_w1, moe_w3, moe_w2, norm_out):
    B, T, D = x.shape
    M = B * T
    bf = lambda a: a.astype(BF16)
    h = x.reshape(M, D)
    n = rmsnorm_call(h, norm_mix[0], BF16)
    for l in range(DEPTH):
        last = l == DEPTH - 1
        g_next = norm_out if last else norm_mix[l + 1]
        n_dtype = x.dtype if last else BF16
        w = w_in[l]
        uv = matmul_call(n, bf(w[:, GM_OFF:RW_OFF]), BF16, tn=512, name="in_proj_gm")
        p = matmul_call(n, bf(w[:, RW_OFF:SB_OFF]), BF16, tn=256, name="in_proj_rw")
        qkv = matmul_call(n, bf(w[:, SB_OFF:GATE_OFF]), BF16, tn=512, name="in_proj_sb")
        gates = matmul_call(n, bf(w[:, GATE_OFF:]), BF16, tn=512, bias=gate_b[l].reshape(-1),
                            name="in_proj_gate")
        y_gm = gm_call(uv, gm_ln_g[l], gm_ln_b[l], gm_ws[l], gm_bs[l])
        y_rw = rw_call(p.reshape(B, T, RW_COLS), B, T, rw_mu[l], rw_w0[l], rw_w2[l], rw_a0[l], rw_a2[l],
                       rw_g2[l], rw_kk[l], rw_ka[l], rw_rk[l], rw_ln_g[l], rw_ln_b[l]).reshape(M, RW_WIDTH)
        y_sb = sb_call(qkv.reshape(B, T, SB_COLS), B, T).reshape(M, SB_WIDTH)
        h, n2 = merge_call(y_gm, y_rw, y_sb, gates, h, bf(p_gm[l]), bf(p_rw[l]), bf(p_sb[l]), bf(w_o[l]),
                           norm_ffn[l])
        j = l // 2
        if l % 2 == 0:
            outs = ffn_call(n2, h, bf(ffn_w1[j]), bf(ffn_w3[j]), bf(ffn_w2[j]), g_next, n_dtype,
                            emit_h=not last)
        else:
            comb = router_call(h, norm_ffn[l], router_w[j], router_b[j])
            delta = moe_dense_call(n2, comb, bf(moe_w1[j]), bf(moe_w3[j]), bf(moe_w2[j]))
            outs = addnorm_call(h, delta, g_next, n_dtype, emit_h=not last)
        if last:
            n = outs[0]
        else:
            h, n = outs
    return n.reshape(B, T, D)
```

```python
import functools

import jax
import jax.numpy as jnp
from jax import lax
from jax.experimental import pallas as pl
from jax.experimental.pallas import tpu as pltpu

F32 = jnp.float32
BF16 = jnp.bfloat16

D_MODEL = 2048
DEPTH = 2
HEAD_DIM = 64
GM_HEADS = 8
GM_WIDTH = GM_HEADS * HEAD_DIM
CHUNK = 128
RW_HEADS = 8
RW_WIDTH = RW_HEADS * HEAD_DIM
DECAY_LORA = 64
AAA_LORA = 64
GATE_LORA = 128
SB_HEADS = 16
SB_WIDTH = SB_HEADS * HEAD_DIM
N_BRANCH = 3
D_FF = 5632
N_EXPERTS = 8
TOP_K = 2
D_FF_EXPERT = D_FF // TOP_K
RMS_EPS = 1e-6
LN_EPS = 1e-5
RW_GN_EPS = 64e-5

GM_COLS = 2 * GM_WIDTH
RW_COLS = 3 * RW_WIDTH + DECAY_LORA + AAA_LORA + GATE_LORA
SB_COLS = 3 * SB_WIDTH
GATE_COLS = N_BRANCH * D_MODEL
GM_OFF = 0
RW_OFF = GM_OFF + GM_COLS
SB_OFF = RW_OFF + RW_COLS
GATE_OFF = SB_OFF + SB_COLS
N_IN = GATE_OFF + GATE_COLS

V7X_VMEM_BYTES = 64 * 1024 * 1024
LANES = 128
BF16_SUBLANES = 16
RW_CHUNK = 64
RW_TILE = 256
RW_GROUP = 4
SB_TILE = 256
SB_EXP_UNDERFLOW = -104.0
ROUTER_LANES = 128


def _vmem_limit(nbytes):
    return int(min(max(nbytes, 32 * 1024 * 1024), V7X_VMEM_BYTES - 6 * 1024 * 1024))


def _dot(a, b):
    return jnp.dot(a, b, preferred_element_type=F32)


def _dot_nt(a, b):
    return lax.dot_general(a, b, (((1,), (1,)), ((), ())), preferred_element_type=F32)


def _dot_tn(a, b):
    return lax.dot_general(a, b, (((0,), (0,)), ((), ())), preferred_element_type=F32)


def _dot_split(x, ones_bf16):
    hi = x.astype(BF16)
    lo = (x - hi.astype(F32)).astype(BF16)
    return _dot(hi, ones_bf16) + _dot(lo, ones_bf16)


def _rms(x, g):
    ms = jnp.mean(x * x, axis=-1, keepdims=True)
    return x * lax.rsqrt(ms + RMS_EPS) * g


def _rmsnorm_kernel(x_ref, g_ref, o_ref):
    o_ref[...] = _rms(x_ref[...], g_ref[...]).astype(o_ref.dtype)


def rmsnorm_call(x, g, out_dtype, tm=512):
    M, D = x.shape
    tm = min(tm, M)
    return pl.pallas_call(
        _rmsnorm_kernel,
        grid=(M // tm,),
        in_specs=[pl.BlockSpec((tm, D), lambda i: (i, 0)),
                  pl.BlockSpec((1, D), lambda i: (0, 0))],
        out_specs=pl.BlockSpec((tm, D), lambda i: (i, 0)),
        out_shape=jax.ShapeDtypeStruct((M, D), out_dtype),
        compiler_params=pltpu.CompilerParams(dimension_semantics=("parallel",)),
        name="rmsnorm",
    )(x, g.reshape(1, D))


def _mm_kernel(a_ref, w_ref, o_ref):
    o_ref[...] = _dot(a_ref[...], w_ref[...]).astype(o_ref.dtype)


def _mm_gate_kernel(a_ref, w_ref, b_ref, o_ref):
    o_ref[...] = jax.nn.sigmoid(_dot(a_ref[...], w_ref[...]) + b_ref[...]).astype(o_ref.dtype)


def matmul_call(a, w, out_dtype, tn, bias=None, tm=1024, name="matmul"):
    M, K = a.shape
    N = w.shape[1]
    tm = min(tm, M)
    in_specs = [pl.BlockSpec((tm, K), lambda i, j: (i, 0)),
                pl.BlockSpec((K, tn), lambda i, j: (0, j))]
    args = [a, w]
    kern = _mm_kernel
    if bias is not None:
        in_specs.append(pl.BlockSpec((1, tn), lambda i, j: (0, j)))
        args.append(bias.reshape(1, N))
        kern = _mm_gate_kernel
    osz = jnp.dtype(out_dtype).itemsize
    need = 2 * (tm * K * 2 + K * tn * 2 + tm * tn * osz) + 2 * tm * tn * 4
    return pl.pallas_call(
        kern,
        grid=(M // tm, N // tn),
        in_specs=in_specs,
        out_specs=pl.BlockSpec((tm, tn), lambda i, j: (i, j)),
        out_shape=jax.ShapeDtypeStruct((M, N), out_dtype),
        compiler_params=pltpu.CompilerParams(
            dimension_semantics=("parallel", "arbitrary"),
            vmem_limit_bytes=_vmem_limit(need + (8 << 20))),
        name=name,
    )(*args)


def _gm_kernel(uv_ref, lng_ref, lnb_ref, ws_ref, bias_ref, o_ref, *, n_chunks):
    ii = lax.broadcasted_iota(jnp.int32, (CHUNK, CHUNK), 0)
    jj = lax.broadcasted_iota(jnp.int32, (CHUNK, CHUNK), 1)
    causal = jj <= ii
    ws = [jnp.where(causal, ws_ref[h], 0.0).astype(BF16) for h in range(GM_HEADS)]
    for c in range(n_chunks):
        rows = slice(c * CHUNK, (c + 1) * CHUNK)
        u = jax.nn.gelu(uv_ref[rows, 0:GM_WIDTH].astype(F32))
        v = jax.nn.gelu(uv_ref[rows, GM_WIDTH:2 * GM_WIDTH].astype(F32))
        mu = jnp.mean(v, axis=-1, keepdims=True)
        vc = v - mu
        var = jnp.mean(vc * vc, axis=-1, keepdims=True)
        vn = (vc * lax.rsqrt(var + LN_EPS) * lng_ref[...] + lnb_ref[...]).astype(BF16)
        mixed = jnp.concatenate(
            [_dot(ws[h], vn[:, h * HEAD_DIM:(h + 1) * HEAD_DIM]) for h in range(GM_HEADS)], axis=-1)
        o_ref[rows, :] = (u * (mixed + bias_ref[...])).astype(o_ref.dtype)


def gm_call(uv, ln_g, ln_b, w_s, b_s, n_chunks=2):
    M = uv.shape[0]
    tt = CHUNK * n_chunks
    bias = jnp.repeat(b_s.T, HEAD_DIM, axis=1)
    return pl.pallas_call(
        functools.partial(_gm_kernel, n_chunks=n_chunks),
        grid=(M // tt,),
        in_specs=[pl.BlockSpec((tt, GM_COLS), lambda i: (i, 0)),
                  pl.BlockSpec((1, GM_WIDTH), lambda i: (0, 0)),
                  pl.BlockSpec((1, GM_WIDTH), lambda i: (0, 0)),
                  pl.BlockSpec((GM_HEADS, CHUNK, CHUNK), lambda i: (0, 0, 0)),
                  pl.BlockSpec((CHUNK, GM_WIDTH), lambda i: (0, 0))],
        out_specs=pl.BlockSpec((tt, GM_WIDTH), lambda i: (i, 0)),
        out_shape=jax.ShapeDtypeStruct((M, GM_WIDTH), BF16),
        compiler_params=pltpu.CompilerParams(dimension_semantics=("parallel",)),
        name="gmlp_gating",
    )(uv, ln_g.reshape(1, -1), ln_b.reshape(1, -1), w_s, bias)


def _rw_kernel(p_ref, prev_ref, mu_ref, w0_ref, w2_ref, a0_ref, a2_ref, g2_ref, kk_ref, ka_ref,
               rk_ref, lng_ref, lnb_ref, hsum_ref, o_ref, st_ref, y_ref):
    C = RW_CHUNK
    i = pl.program_id(1)

    @pl.when(i == 0)
    def _():
        st_ref[...] = jnp.zeros_like(st_ref)

    hsum = hsum_ref[...]

    p = p_ref[...].astype(F32)
    prev_row = prev_ref[BF16_SUBLANES - 1:BF16_SUBLANES, :].astype(F32)
    prev_row = jnp.where(i == 0, 0.0, prev_row)
    row = lax.broadcasted_iota(jnp.int32, p.shape, 0)
    p_prev = jnp.where(row == 0, prev_row, pltpu.roll(p, 1, axis=0))
    ps = p + (p_prev - p) * mu_ref[...]
    W = RW_WIDTH
    r = ps[:, 0:W]
    k = ps[:, W:2 * W]
    v = ps[:, 2 * W:3 * W]
    wl = ps[:, 3 * W:3 * W + DECAY_LORA]
    al = ps[:, 3 * W + DECAY_LORA:3 * W + DECAY_LORA + AAA_LORA]
    gl = ps[:, 3 * W + DECAY_LORA + AAA_LORA:RW_COLS]

    d = w0_ref[...] + _dot(jnp.tanh(wl).astype(BF16), w2_ref[...])
    log_w = -jnp.exp(-jax.nn.softplus(-d) - 0.5)
    a = jax.nn.sigmoid(a0_ref[...] + _dot(al.astype(BF16), a2_ref[...]))
    g = _dot(jax.nn.sigmoid(gl).astype(BF16), g2_ref[...])
    kk = k * kk_ref[...]
    kk = kk / jnp.maximum(jnp.sqrt(_dot_split(kk * kk, hsum)), 1e-12)
    k = k * (1.0 + (a - 1.0) * ka_ref[...])
    bonus = _dot_split(r * k * rk_ref[...], hsum) * v

    HD = HEAD_DIM
    GW = RW_GROUP * HD
    ci = lax.broadcasted_iota(jnp.int32, (C, C), 0)
    cj = lax.broadcasted_iota(jnp.int32, (C, C), 1)
    tri_incl = (cj <= ci).astype(BF16)
    ti = lax.broadcasted_iota(jnp.int32, (C, GW), 0)
    tj = lax.broadcasted_iota(jnp.int32, (C, GW), 1) % HD
    strict = tj < ti
    incl = tj <= ti
    eye = ti == tj
    tile_eye = eye.astype(BF16)
    blk_masks = []
    s = 2
    while s < C:
        blk_masks.append((ti // (2 * s) == tj // (2 * s)) & (ti // s != tj // s))
        s *= 2
    pair_mask = (ti // 2 == tj // 2) & strict
    gi = lax.broadcasted_iota(jnp.int32, (GW, GW), 0)
    gj = lax.broadcasted_iota(jnp.int32, (GW, GW), 1)
    bd_mask = gi // HD == gj // HD
    bd_mask_b = jnp.where(bd_mask, 1.0, 0.0).astype(BF16)
    gw_eye = gi == gj

    def bd(x):
        xb = x.astype(BF16)
        return jnp.where(bd_mask_b > 0, jnp.concatenate([xb] * RW_GROUP, axis=0), 0)

    def bd_of_full(full):
        return jnp.where(bd_mask, full, 0.0)

    n_chunks = RW_TILE // C
    n_groups = RW_WIDTH // GW
    systems = [(c, q) for c in range(n_chunks) for q in range(n_groups)]

    pre = []
    for c in range(n_chunks):
        rows = slice(c * C, (c + 1) * C)
        lw = log_w[rows]
        cum = _cumsum_rows(lw, tri_incl)
        gam = jnp.exp(cum)
        gam_prev = jnp.exp(cum - lw)
        gam_inv = jnp.exp(-cum)
        g_end = gam[C - 1:C, :]
        bh_f = kk[rows] * a[rows] * gam_inv
        kh_f = k[rows] * gam_inv
        pre.append(dict(
            rh=(r[rows] * gam).astype(BF16), ah=(-kk[rows] * gam_prev).astype(BF16),
            bh=bh_f.astype(BF16), kh=kh_f.astype(BF16),
            bt=(bh_f * g_end).astype(BF16), kt=(kh_f * g_end).astype(BF16),
            v=v[rows].astype(BF16), g_end=g_end))

    def grp(c, q, name):
        return pre[c][name][:, q * GW:(q + 1) * GW]

    st = {}
    for (c, q) in systems:
        ar = jnp.concatenate([grp(c, q, "ah"), grp(c, q, "rh")], axis=0)
        rb = bd_of_full(_dot_tn(grp(c, q, "bh"), tile_eye)).astype(BF16)
        rk = bd_of_full(_dot_tn(grp(c, q, "kh"), tile_eye)).astype(BF16)
        pb = _dot(ar, rb)
        pk = _dot(ar, rk)
        lab = jnp.where(strict, pb[0:C], 0.0)
        st[(c, q)] = dict(
            lab=lab, mrb=jnp.where(incl, pb[C:2 * C], 0.0).astype(BF16),
            lm=jnp.concatenate([jnp.where(strict, pk[0:C], 0.0), jnp.where(incl, pk[C:2 * C], 0.0)],
                               axis=0).astype(BF16),
            tinv=jnp.where(eye, 1.0, jnp.where(pair_mask, lab, 0.0)))

    for bm in blk_masks:
        xs = {}
        for key in systems:
            d_ = st[key]
            xs[key] = _dot(d_["tinv"].astype(BF16), bd(jnp.where(bm, d_["lab"], 0.0))).astype(BF16)
        for key in systems:
            d_ = st[key]
            d_["tinv"] = d_["tinv"] + _dot(xs[key], bd(d_["tinv"]))

    for (c, q) in systems:
        d_ = st[(c, q)]
        lv = _dot(d_["lm"], bd(grp(c, q, "v")))
        tb = d_["tinv"].astype(BF16)
        ap = _dot(tb, bd(grp(c, q, "ah"))).astype(BF16)
        up = _dot(tb, bd(lv[0:C])).astype(BF16)
        d_["rp"] = (grp(c, q, "rh").astype(F32) + _dot(d_["mrb"], bd(ap))).astype(BF16)
        d_["ypp"] = _dot(d_["mrb"], bd(up)) + lv[C:2 * C]
        g_end = pre[c]["g_end"][:, q * GW:(q + 1) * GW]
        gt = _dot_tn(ap, grp(c, q, "bt"))
        d_["gt"] = jnp.where(gw_eye, gt + g_end, bd_of_full(gt)).astype(BF16)
        ht = bd_of_full(_dot_tn(up, grp(c, q, "bt")) + _dot_tn(grp(c, q, "v"), grp(c, q, "kt")))
        d_["ht"] = sum(ht[hh * HD:(hh + 1) * HD] for hh in range(1, RW_GROUP)) + ht[0:HD]

    for q in range(n_groups):
        s_cur = st_ref[q]
        for c in range(n_chunks):
            d_ = st[(c, q)]
            y_ref[c * C:(c + 1) * C, q * GW:(q + 1) * GW] = _dot_nt(d_["rp"], bd(s_cur)) + d_["ypp"]
            s_cur = _dot(s_cur.astype(BF16), d_["gt"]) + d_["ht"]
        st_ref[q] = s_cur

    y = y_ref[...]
    inv_n = 1.0 / HEAD_DIM
    m = _dot_split(y, hsum) * inv_n
    yc = y - m
    var = _dot_split(yc * yc, hsum) * inv_n
    yn = yc * lax.rsqrt(var + RW_GN_EPS) * lng_ref[...] + lnb_ref[...]
    o_ref[...] = ((yn + bonus) * g).astype(o_ref.dtype)


def _cumsum_rows(x, tri_incl_bf16):
    hi = x.astype(BF16)
    lo = (x - hi.astype(F32)).astype(BF16)
    return _dot(tri_incl_bf16, hi) + _dot(tri_incl_bf16, lo)


def rw_call(p, B, T, mu, w0, w2, a0, a2, g2, k_k, k_a, r_k, ln_g, ln_b):
    nt = T // RW_TILE
    sub = RW_TILE // BF16_SUBLANES
    hid = jnp.arange(RW_WIDTH) // HEAD_DIM
    hsum = (hid[:, None] == hid[None, :]).astype(BF16)
    row = lambda a: a.reshape(1, -1).astype(F32)
    vec = lambda n: pl.BlockSpec((1, n), lambda b, i: (0, 0))
    return pl.pallas_call(
        _rw_kernel,
        grid=(B, nt),
        in_specs=[pl.BlockSpec((None, RW_TILE, RW_COLS), lambda b, i: (b, i, 0)),
                  pl.BlockSpec((None, BF16_SUBLANES, RW_COLS),
                               lambda b, i: (b, jnp.maximum(i * sub - 1, 0), 0)),
                  vec(RW_COLS), vec(RW_WIDTH),
                  pl.BlockSpec((DECAY_LORA, RW_WIDTH), lambda b, i: (0, 0)),
                  vec(RW_WIDTH),
                  pl.BlockSpec((AAA_LORA, RW_WIDTH), lambda b, i: (0, 0)),
                  pl.BlockSpec((GATE_LORA, RW_WIDTH), lambda b, i: (0, 0)),
                  vec(RW_WIDTH), vec(RW_WIDTH), vec(RW_WIDTH), vec(RW_WIDTH), vec(RW_WIDTH),
                  pl.BlockSpec((RW_WIDTH, RW_WIDTH), lambda b, i: (0, 0))],
        out_specs=pl.BlockSpec((None, RW_TILE, RW_WIDTH), lambda b, i: (b, i, 0)),
        out_shape=jax.ShapeDtypeStruct((B, T, RW_WIDTH), BF16),
        scratch_shapes=[pltpu.VMEM((RW_HEADS // RW_GROUP, HEAD_DIM, RW_GROUP * HEAD_DIM), F32),
                        pltpu.VMEM((RW_TILE, RW_WIDTH), F32)],
        compiler_params=pltpu.CompilerParams(dimension_semantics=("parallel", "arbitrary")),
        name="rwkv7_time_mix",
    )(p, p, row(mu), row(w0), w2.astype(BF16), row(a0), a2.astype(BF16), g2.astype(BF16),
      row(k_k), row(k_a), row(r_k), row(ln_g), row(ln_b), hsum)


def _sb_step(qs, ks, vs, carries, suffix_ones, causal):
    n = range(len(qs))
    zs = [_dot_nt(qs[h], ks[h]) for h in n]
    sps = [jnp.maximum(z, 0.0) + jnp.log1p(jnp.exp(-jnp.abs(z))) for z in zs]
    log_nots = [-sp for sp in sps]
    if causal is not None:
        log_nots = [jnp.where(causal, ln, 0.0) for ln in log_nots]
    his = [ln.astype(BF16) for ln in log_nots]
    los = [(log_nots[h] - his[h].astype(F32)).astype(BF16) for h in n]
    sufs = [_dot(his[h], suffix_ones) + _dot(los[h], suffix_ones) for h in n]
    rowsums = [jnp.sum(ln, axis=-1, keepdims=True) for ln in log_nots]
    atts = [jnp.exp(zs[h] - sps[h] + (sufs[h] + carries[h])) for h in n]
    if causal is not None:
        atts = [jnp.where(causal, att, 0.0) for att in atts]
    outs = [_dot(atts[h].astype(BF16), vs[h]) for h in n]
    return outs, rowsums


def _sb_kernel(q_ref, k_ref, v_ref, o_ref, acc_ref, carry_ref, *, T):
    tq = SB_TILE
    nq = T // tq
    ti = lax.broadcasted_iota(jnp.int32, (tq, tq), 0)
    tj = lax.broadcasted_iota(jnp.int32, (tq, tq), 1)
    suffix_ones = (ti > tj).astype(BF16)
    causal = tj < ti
    scale = HEAD_DIM ** -0.5
    heads = [slice(hh * HEAD_DIM, (hh + 1) * HEAD_DIM) for hh in range(LANES // HEAD_DIM)]

    def q_body(i, _):
        r0 = pl.multiple_of(i * tq, tq)
        qs = [q_ref[pl.ds(r0, tq), cs] * scale for cs in heads]
        outs, rss = _sb_step(qs, [k_ref[pl.ds(r0, tq), cs] for cs in heads],
                             [v_ref[pl.ds(r0, tq), cs] for cs in heads],
                             [jnp.zeros((tq, 1), F32)] * len(heads), suffix_ones, causal)
        live = jnp.float32(-jnp.inf)
        for hh, cs in enumerate(heads):
            acc_ref[:, cs] = outs[hh]
            carry_ref[hh] = rss[hh]
            live = jnp.maximum(live, jnp.max(rss[hh]))

        def k_cond(state):
            jr, live = state
            return (jr < i) & (live > SB_EXP_UNDERFLOW)

        def k_body(state):
            jr, _ = state
            c0 = pl.multiple_of((i - 1 - jr) * tq, tq)
            carries = [carry_ref[hh] for hh in range(len(heads))]
            outs, rss = _sb_step(qs, [k_ref[pl.ds(c0, tq), cs] for cs in heads],
                                 [v_ref[pl.ds(c0, tq), cs] for cs in heads],
                                 carries, suffix_ones, None)
            live = jnp.float32(-jnp.inf)
            for hh, cs in enumerate(heads):
                acc_ref[:, cs] += outs[hh]
                carry = carries[hh] + rss[hh]
                carry_ref[hh] = carry
                live = jnp.maximum(live, jnp.max(carry))
            return jr + 1, live

        lax.while_loop(k_cond, k_body, (jnp.int32(0), live))
        o_ref[pl.ds(r0, tq), :] = acc_ref[...].astype(o_ref.dtype)
        return 0

    lax.fori_loop(0, nq, q_body, 0)


def sb_call(qkv, B, T):
    npair = SB_WIDTH // LANES
    blk = lambda off: pl.BlockSpec((None, T, LANES), lambda b, hp: (b, 0, off + hp))
    return pl.pallas_call(
        functools.partial(_sb_kernel, T=T),
        grid=(B, npair),
        in_specs=[blk(0), blk(npair), blk(2 * npair)],
        out_specs=pl.BlockSpec((None, T, LANES), lambda b, hp: (b, 0, hp)),
        out_shape=jax.ShapeDtypeStruct((B, T, SB_WIDTH), BF16),
        scratch_shapes=[pltpu.VMEM((SB_TILE, LANES), F32),
                        pltpu.VMEM((LANES // HEAD_DIM, SB_TILE, 1), F32)],
        compiler_params=pltpu.CompilerParams(dimension_semantics=("parallel", "parallel")),
        name="stick_breaking_attention",
    )(qkv, qkv, qkv)


def _merge_kernel(ygm_ref, yrw_ref, ysb_ref, gate_ref, h_ref, pgm_ref, prw_ref, psb_ref, wo_ref,
                  g_ref, ho_ref, no_ref):
    D = D_MODEL
    merged = gate_ref[:, 0:D].astype(F32) * _dot(ygm_ref[...], pgm_ref[...])
    merged += gate_ref[:, D:2 * D].astype(F32) * _dot(yrw_ref[...], prw_ref[...])
    merged += gate_ref[:, 2 * D:3 * D].astype(F32) * _dot(ysb_ref[...], psb_ref[...])
    hn = h_ref[...] + _dot(merged.astype(BF16), wo_ref[...])
    ho_ref[...] = hn
    no_ref[...] = _rms(hn, g_ref[...]).astype(no_ref.dtype)


def _resident(shape):
    return pl.BlockSpec(shape, lambda i: (0,) * len(shape), pipeline_mode=pl.Buffered(1))


def merge_call(y_gm, y_rw, y_sb, gates, h, p_gm, p_rw, p_sb, w_o, g_next, tm=256):
    M, D = h.shape
    tm = min(tm, M)
    rowblk = lambda n: pl.BlockSpec((tm, n), lambda i: (i, 0))
    wbytes = 2 * (GM_WIDTH + RW_WIDTH + SB_WIDTH + D) * D
    need = wbytes + 2 * tm * (2 * (GM_WIDTH + RW_WIDTH + SB_WIDTH + GATE_COLS) + 4 * D + 4 * D + 2 * D) \
        + 6 * tm * D * 4
    return pl.pallas_call(
        _merge_kernel,
        grid=(M // tm,),
        in_specs=[rowblk(GM_WIDTH), rowblk(RW_WIDTH), rowblk(SB_WIDTH), rowblk(GATE_COLS), rowblk(D),
                  _resident((GM_WIDTH, D)), _resident((RW_WIDTH, D)), _resident((SB_WIDTH, D)),
                  _resident((D, D)), _resident((1, D))],
        out_specs=[rowblk(D), rowblk(D)],
        out_shape=[jax.ShapeDtypeStruct((M, D), F32), jax.ShapeDtypeStruct((M, D), BF16)],
        compiler_params=pltpu.CompilerParams(dimension_semantics=("parallel",),
                                             vmem_limit_bytes=_vmem_limit(need)),
        name="merge_out_proj",
    )(y_gm, y_rw, y_sb, gates, h, p_gm, p_rw, p_sb, w_o, g_next.reshape(1, D))


def _ffn_kernel(n_ref, h_ref, w1_ref, w3_ref, w2_ref, g_ref, *refs, emit_h):
    if emit_h:
        ho_ref, no_ref, acc_ref = refs
    else:
        no_ref, acc_ref = refs
    f = pl.program_id(1)

    @pl.when(f == 0)
    def _():
        acc_ref[...] = jnp.zeros_like(acc_ref)

    n = n_ref[...]
    act = (jax.nn.silu(_dot(n, w1_ref[...])) * _dot(n, w3_ref[...])).astype(BF16)
    acc_ref[...] += _dot(act, w2_ref[...])

    @pl.when(f == pl.num_programs(1) - 1)
    def _():
        hn = h_ref[...] + acc_ref[...]
        if emit_h:
            ho_ref[...] = hn
        no_ref[...] = _rms(hn, g_ref[...]).astype(no_ref.dtype)


def ffn_call(n, h, w1, w3, w2, g_next, n_dtype, emit_h, tm=512, tf=512):
    M, D = h.shape
    F = w1.shape[1]
    tm = min(tm, M)
    rowblk = pl.BlockSpec((tm, D), lambda i, f: (i, 0))
    out_specs = [rowblk]
    out_shape = [jax.ShapeDtypeStruct((M, D), n_dtype)]
    if emit_h:
        out_specs = [rowblk, rowblk]
        out_shape = [jax.ShapeDtypeStruct((M, D), F32)] + out_shape
    need = 2 * tm * D * (2 + 4 + 4 * len(out_shape)) + tm * D * 4 + 2 * 3 * D * tf * 2 + 4 * tm * tf * 4
    return pl.pallas_call(
        functools.partial(_ffn_kernel, emit_h=emit_h),
        grid=(M // tm, F // tf),
        in_specs=[rowblk, rowblk,
                  pl.BlockSpec((D, tf), lambda i, f: (0, f)),
                  pl.BlockSpec((D, tf), lambda i, f: (0, f)),
                  pl.BlockSpec((tf, D), lambda i, f: (f, 0)),
                  pl.BlockSpec((1, D), lambda i, f: (0, 0))],
        out_specs=out_specs,
        out_shape=out_shape,
        scratch_shapes=[pltpu.VMEM((tm, D), F32)],
        compiler_params=pltpu.CompilerParams(dimension_semantics=("parallel", "arbitrary"),
                                             vmem_limit_bytes=_vmem_limit(need + (8 << 20))),
        name="swiglu_ffn",
    )(n, h, w1, w3, w2, g_next.reshape(1, D))


def _router_kernel(h_ref, g_ref, wh_ref, wl_ref, b_ref, o_ref):
    n = _rms(h_ref[...], g_ref[...])
    nh = n.astype(BF16)
    nl = (n - nh.astype(F32)).astype(BF16)
    logits = _dot(nh, wh_ref[...]) + _dot(nl, wh_ref[...]) + _dot(nh, wl_ref[...]) + b_ref[...]
    lane = lax.broadcasted_iota(jnp.int32, logits.shape, 1)
    neg = jnp.float32(-jnp.inf)
    logits = jnp.where(lane < N_EXPERTS, logits, neg)
    m1 = jnp.max(logits, axis=-1, keepdims=True)
    i1 = jnp.min(jnp.where(logits == m1, lane, ROUTER_LANES), axis=-1, keepdims=True)
    rest = jnp.where(lane == i1, neg, logits)
    m2 = jnp.max(rest, axis=-1, keepdims=True)
    i2 = jnp.min(jnp.where(rest == m2, lane, ROUTER_LANES), axis=-1, keepdims=True)
    e2 = jnp.exp(m2 - m1)
    p1 = 1.0 / (1.0 + e2)
    p2 = e2 / (1.0 + e2)
    o_ref[...] = jnp.where(lane == i1, p1, 0.0) + jnp.where(lane == i2, p2, 0.0)


def router_call(h, g, router_w, router_b, tm=512):
    M, D = h.shape
    tm = min(tm, M)
    wpad = jnp.zeros((D, ROUTER_LANES), F32).at[:, :N_EXPERTS].set(router_w)
    w_hi = wpad.astype(BF16)
    w_lo = (wpad - w_hi.astype(F32)).astype(BF16)
    bpad = jnp.zeros((1, ROUTER_LANES), F32).at[0, :N_EXPERTS].set(router_b)
    const = lambda shape: pl.BlockSpec(shape, lambda i: (0, 0))
    return pl.pallas_call(
        _router_kernel,
        grid=(M // tm,),
        in_specs=[pl.BlockSpec((tm, D), lambda i: (i, 0)), const((1, D)),
                  const((D, ROUTER_LANES)), const((D, ROUTER_LANES)), const((1, ROUTER_LANES))],
        out_specs=pl.BlockSpec((tm, ROUTER_LANES), lambda i: (i, 0)),
        out_shape=jax.ShapeDtypeStruct((M, ROUTER_LANES), F32),
        compiler_params=pltpu.CompilerParams(dimension_semantics=("parallel",)),
        name="moe_router",
    )(h, g.reshape(1, D), w_hi, w_lo, bpad)


def _moe_dense_kernel(n_ref, c_ref, w1_ref, w3_ref, w2_ref, o_ref):
    e = pl.program_id(1)
    f = pl.program_id(2)

    @pl.when((e == 0) & (f == 0))
    def _():
        o_ref[...] = jnp.zeros_like(o_ref)

    lane = lax.broadcasted_iota(jnp.int32, c_ref.shape, 1)
    ce = jnp.sum(jnp.where(lane == e, c_ref[...], 0.0), axis=-1, keepdims=True)
    n = n_ref[...]
    act = jax.nn.silu(_dot(n, w1_ref[...])) * _dot(n, w3_ref[...])
    o_ref[...] += _dot((act * ce).astype(BF16), w2_ref[...])


def moe_dense_call(n, comb, w1, w3, w2, tm=1024, tf=256):
    M, D = n.shape
    E, _, Fe = w1.shape
    tm = min(tm, M)
    need = 2 * tm * D * 2 + 2 * tm * D * 4 + 2 * 3 * D * tf * 2 + 4 * tm * tf * 4
    return pl.pallas_call(
        _moe_dense_kernel,
        grid=(M // tm, E, Fe // tf),
        in_specs=[pl.BlockSpec((tm, D), lambda i, e, f: (i, 0)),
                  pl.BlockSpec((tm, ROUTER_LANES), lambda i, e, f: (i, 0)),
                  pl.BlockSpec((None, D, tf), lambda i, e, f: (e, 0, f)),
                  pl.BlockSpec((None, D, tf), lambda i, e, f: (e, 0, f)),
                  pl.BlockSpec((None, tf, D), lambda i, e, f: (e, f, 0))],
        out_specs=pl.BlockSpec((tm, D), lambda i, e, f: (i, 0)),
        out_shape=jax.ShapeDtypeStruct((M, D), F32),
        compiler_params=pltpu.CompilerParams(
            dimension_semantics=("parallel", "arbitrary", "arbitrary"),
            vmem_limit_bytes=_vmem_limit(need + (8 << 20))),
        name="moe_experts",
    )(n, comb, w1, w3, w2)


def _addnorm_kernel(h_ref, d_ref, g_ref, *refs, emit_h):
    hn = h_ref[...] + d_ref[...]
    if emit_h:
        refs[0][...] = hn
    refs[-1][...] = _rms(hn, g_ref[...]).astype(refs[-1].dtype)


def addnorm_call(h, delta, g_next, n_dtype, emit_h, tm=512):
    M, D = h.shape
    tm = min(tm, M)
    rowblk = pl.BlockSpec((tm, D), lambda i: (i, 0))
    out_specs = [rowblk]
    out_shape = [jax.ShapeDtypeStruct((M, D), n_dtype)]
    if emit_h:
        out_specs = [rowblk, rowblk]
        out_shape = [jax.ShapeDtypeStruct((M, D), F32)] + out_shape
    return pl.pallas_call(
        functools.partial(_addnorm_kernel, emit_h=emit_h),
        grid=(M // tm,),
        in_specs=[rowblk, rowblk, pl.BlockSpec((1, D), lambda i: (0, 0))],
        out_specs=out_specs,
        out_shape=out_shape,
        compiler_params=pltpu.CompilerParams(dimension_semantics=("parallel",),
                                             vmem_limit_bytes=_vmem_limit(10 * tm * D * 4)),
        name="residual_rmsnorm",
    )(h, delta, g_next.reshape(1, D))


def kernel(x, norm_mix, w_in, gate_b, gm_ln_g, gm_ln_b, gm_ws, gm_bs, rw_mu, rw_w0, rw_w2, rw_a0, rw_a2,
           rw_g2, rw_kk, rw_ka, rw_rk, rw_ln_g, rw_ln_b, p_gm, p_rw, p_sb, w_o, norm_ffn, ffn_w1, ffn_w3,
           ffn_w2, router_w, router_b, moe_w1, moe_w3, moe_w2, norm_out):
    B, T, D = x.shape
    M = B * T
    bf = lambda a: a.astype(BF16)
    h = x.reshape(M, D)
    n = rmsnorm_call(h, norm_mix[0], BF16)
    for l in range(DEPTH):
        last = l == DEPTH - 1
        g_next = norm_out if last else norm_mix[l + 1]
        n_dtype = x.dtype if last else BF16
        w = w_in[l]
        uv = matmul_call(n, bf(w[:, GM_OFF:RW_OFF]), BF16, tn=512, name="in_proj_gm")
        p = matmul_call(n, bf(w[:, RW_OFF:SB_OFF]), BF16, tn=256, name="in_proj_rw")
        qkv = matmul_call(n, bf(w[:, SB_OFF:GATE_OFF]), BF16, tn=512, name="in_proj_sb")
        gates = matmul_call(n, bf(w[:, GATE_OFF:]), BF16, tn=512, bias=gate_b[l].reshape(-1),
                            name="in_proj_gate")
        y_gm = gm_call(uv, gm_ln_g[l], gm_ln_b[l], gm_ws[l], gm_bs[l])
        y_rw = rw_call(p.reshape(B, T, RW_COLS), B, T, rw_mu[l], rw_w0[l], rw_w2[l], rw_a0[l], rw_a2[l],
                       rw_g2[l], rw_kk[l], rw_ka[l], rw_rk[l], rw_ln_g[l], rw_ln_b[l]).reshape(M, RW_WIDTH)
        y_sb = sb_call(qkv.reshape(B, T, SB_COLS), B, T).reshape(M, SB_WIDTH)
        h, n2 = merge_call(y_gm, y_rw, y_sb, gates, h, bf(p_gm[l]), bf(p_rw[l]), bf(p_sb[l]), bf(w_o[l]),
                           norm_ffn[l])
        j = l // 2
        if l % 2 == 0:
            outs = ffn_call(n2, h, bf(ffn_w1[j]), bf(ffn_w3[j]), bf(ffn_w2[j]), g_next, n_dtype,
                            emit_h=not last)
        else:
            comb = router_call(h, norm_ffn[l], router_w[j], router_b[j])
            delta = moe_dense_call(n2, comb, bf(moe_w1[j]), bf(moe_w3[j]), bf(moe_w2[j]))
            outs = addnorm_call(h, delta, g_next, n_dtype, emit_h=not last)
        if last:
            n = outs[0]
        else:
            h, n = outs
    return n.reshape(B, T, D)
```

```python
import functools

import jax
import jax.numpy as jnp
from jax import lax
from jax.experimental import pallas as pl
from jax.experimental.pallas import tpu as pltpu

F32 = jnp.float32
BF16 = jnp.bfloat16

D_MODEL = 2048
DEPTH = 2
HEAD_DIM = 64
GM_HEADS = 8
GM_WIDTH = GM_HEADS * HEAD_DIM
CHUNK = 128
RW_HEADS = 8
RW_WIDTH = RW_HEADS * HEAD_DIM
DECAY_LORA = 64
AAA_LORA = 64
GATE_LORA = 128
SB_HEADS = 16
SB_WIDTH = SB_HEADS * HEAD_DIM
N_BRANCH = 3
D_FF = 5632
N_EXPERTS = 8
TOP_K = 2
D_FF_EXPERT = D_FF // TOP_K
RMS_EPS = 1e-6
LN_EPS = 1e-5
RW_GN_EPS = 64e-5

GM_COLS = 2 * GM_WIDTH
RW_COLS = 3 * RW_WIDTH + DECAY_LORA + AAA_LORA + GATE_LORA
SB_COLS = 3 * SB_WIDTH
GATE_COLS = N_BRANCH * D_MODEL
GM_OFF = 0
RW_OFF = GM_OFF + GM_COLS
SB_OFF = RW_OFF + RW_COLS
GATE_OFF = SB_OFF + SB_COLS
N_IN = GATE_OFF + GATE_COLS

V7X_VMEM_BYTES = 64 * 1024 * 1024
LANES = 128
BF16_SUBLANES = 16
RW_CHUNK = 64
RW_TILE = 256
RW_GROUP = 4
SB_TILE = 256
SB_EXP_UNDERFLOW = -104.0
ROUTER_LANES = 128
MOE_TM = 512
MOE_TF = 256


def _vmem_limit(nbytes):
    return int(min(max(nbytes, 32 * 1024 * 1024), V7X_VMEM_BYTES - 6 * 1024 * 1024))


def _dot(a, b):
    return jnp.dot(a, b, preferred_element_type=F32)


def _dot_nt(a, b):
    return lax.dot_general(a, b, (((1,), (1,)), ((), ())), preferred_element_type=F32)


def _dot_tn(a, b):
    return lax.dot_general(a, b, (((0,), (0,)), ((), ())), preferred_element_type=F32)


def _dot_split(x, ones_bf16):
    hi = x.astype(BF16)
    lo = (x - hi.astype(F32)).astype(BF16)
    return _dot(hi, ones_bf16) + _dot(lo, ones_bf16)


def _rms(x, g):
    ms = jnp.mean(x * x, axis=-1, keepdims=True)
    return x * lax.rsqrt(ms + RMS_EPS) * g


def _rmsnorm_kernel(x_ref, g_ref, o_ref):
    o_ref[...] = _rms(x_ref[...], g_ref[...]).astype(o_ref.dtype)


def rmsnorm_call(x, g, out_dtype, tm=512):
    M, D = x.shape
    tm = min(tm, M)
    return pl.pallas_call(
        _rmsnorm_kernel,
        grid=(M // tm,),
        in_specs=[pl.BlockSpec((tm, D), lambda i: (i, 0)),
                  pl.BlockSpec((1, D), lambda i: (0, 0))],
        out_specs=pl.BlockSpec((tm, D), lambda i: (i, 0)),
        out_shape=jax.ShapeDtypeStruct((M, D), out_dtype),
        compiler_params=pltpu.CompilerParams(dimension_semantics=("parallel",)),
        name="rmsnorm",
    )(x, g.reshape(1, D))


def _mm_kernel(a_ref, w_ref, o_ref):
    o_ref[...] = _dot(a_ref[...], w_ref[...]).astype(o_ref.dtype)


def _mm_gate_kernel(a_ref, w_ref, b_ref, o_ref):
    o_ref[...] = jax.nn.sigmoid(_dot(a_ref[...], w_ref[...]) + b_ref[...]).astype(o_ref.dtype)


def matmul_call(a, w, out_dtype, tn, bias=None, tm=1024, name="matmul"):
    M, K = a.shape
    N = w.shape[1]
    tm = min(tm, M)
    in_specs = [pl.BlockSpec((tm, K), lambda i, j: (i, 0)),
                pl.BlockSpec((K, tn), lambda i, j: (0, j))]
    args = [a, w]
    kern = _mm_kernel
    if bias is not None:
        in_specs.append(pl.BlockSpec((1, tn), lambda i, j: (0, j)))
        args.append(bias.reshape(1, N))
        kern = _mm_gate_kernel
    osz = jnp.dtype(out_dtype).itemsize
    need = 2 * (tm * K * 2 + K * tn * 2 + tm * tn * osz) + 2 * tm * tn * 4
    return pl.pallas_call(
        kern,
        grid=(M // tm, N // tn),
        in_specs=in_specs,
        out_specs=pl.BlockSpec((tm, tn), lambda i, j: (i, j)),
        out_shape=jax.ShapeDtypeStruct((M, N), out_dtype),
        compiler_params=pltpu.CompilerParams(
            dimension_semantics=("parallel", "arbitrary"),
            vmem_limit_bytes=_vmem_limit(need + (8 << 20))),
        name=name,
    )(*args)


def _gm_kernel(uv_ref, lng_ref, lnb_ref, ws_ref, bias_ref, o_ref, *, n_chunks):
    ii = lax.broadcasted_iota(jnp.int32, (CHUNK, CHUNK), 0)
    jj = lax.broadcasted_iota(jnp.int32, (CHUNK, CHUNK), 1)
    causal = jj <= ii
    ws = [jnp.where(causal, ws_ref[h], 0.0).astype(BF16) for h in range(GM_HEADS)]
    for c in range(n_chunks):
        rows = slice(c * CHUNK, (c + 1) * CHUNK)
        u = jax.nn.gelu(uv_ref[rows, 0:GM_WIDTH].astype(F32))
        v = jax.nn.gelu(uv_ref[rows, GM_WIDTH:2 * GM_WIDTH].astype(F32))
        mu = jnp.mean(v, axis=-1, keepdims=True)
        vc = v - mu
        var = jnp.mean(vc * vc, axis=-1, keepdims=True)
        vn = (vc * lax.rsqrt(var + LN_EPS) * lng_ref[...] + lnb_ref[...]).astype(BF16)
        mixed = jnp.concatenate(
            [_dot(ws[h], vn[:, h * HEAD_DIM:(h + 1) * HEAD_DIM]) for h in range(GM_HEADS)], axis=-1)
        o_ref[rows, :] = (u * (mixed + bias_ref[...])).astype(o_ref.dtype)


def gm_call(uv, ln_g, ln_b, w_s, b_s, n_chunks=2):
    M = uv.shape[0]
    tt = CHUNK * n_chunks
    bias = jnp.repeat(b_s.T, HEAD_DIM, axis=1)
    return pl.pallas_call(
        functools.partial(_gm_kernel, n_chunks=n_chunks),
        grid=(M // tt,),
        in_specs=[pl.BlockSpec((tt, GM_COLS), lambda i: (i, 0)),
                  pl.BlockSpec((1, GM_WIDTH), lambda i: (0, 0)),
                  pl.BlockSpec((1, GM_WIDTH), lambda i: (0, 0)),
                  pl.BlockSpec((GM_HEADS, CHUNK, CHUNK), lambda i: (0, 0, 0)),
                  pl.BlockSpec((CHUNK, GM_WIDTH), lambda i: (0, 0))],
        out_specs=pl.BlockSpec((tt, GM_WIDTH), lambda i: (i, 0)),
        out_shape=jax.ShapeDtypeStruct((M, GM_WIDTH), BF16),
        compiler_params=pltpu.CompilerParams(dimension_semantics=("parallel",)),
        name="gmlp_gating",
    )(uv, ln_g.reshape(1, -1), ln_b.reshape(1, -1), w_s, bias)


def _rw_kernel(p_ref, prev_ref, mu_ref, w0_ref, w2_ref, a0_ref, a2_ref, g2_ref, kk_ref, ka_ref,
               rk_ref, lng_ref, lnb_ref, hsum_ref, o_ref, st_ref, y_ref):
    C = RW_CHUNK
    i = pl.program_id(1)

    @pl.when(i == 0)
    def _():
        st_ref[...] = jnp.zeros_like(st_ref)

    hsum = hsum_ref[...]

    p = p_ref[...].astype(F32)
    prev_row = prev_ref[BF16_SUBLANES - 1:BF16_SUBLANES, :].astype(F32)
    prev_row = jnp.where(i == 0, 0.0, prev_row)
    row = lax.broadcasted_iota(jnp.int32, p.shape, 0)
    p_prev = jnp.where(row == 0, prev_row, pltpu.roll(p, 1, axis=0))
    ps = p + (p_prev - p) * mu_ref[...]
    W = RW_WIDTH
    r = ps[:, 0:W]
    k = ps[:, W:2 * W]
    v = ps[:, 2 * W:3 * W]
    wl = ps[:, 3 * W:3 * W + DECAY_LORA]
    al = ps[:, 3 * W + DECAY_LORA:3 * W + DECAY_LORA + AAA_LORA]
    gl = ps[:, 3 * W + DECAY_LORA + AAA_LORA:RW_COLS]

    d = w0_ref[...] + _dot(jnp.tanh(wl).astype(BF16), w2_ref[...])
    log_w = -jnp.exp(-jax.nn.softplus(-d) - 0.5)
    a = jax.nn.sigmoid(a0_ref[...] + _dot(al.astype(BF16), a2_ref[...]))
    g = _dot(jax.nn.sigmoid(gl).astype(BF16), g2_ref[...])
    kk = k * kk_ref[...]
    kk = kk / jnp.maximum(jnp.sqrt(_dot_split(kk * kk, hsum)), 1e-12)
    k = k * (1.0 + (a - 1.0) * ka_ref[...])
    bonus = _dot_split(r * k * rk_ref[...], hsum) * v

    HD = HEAD_DIM
    GW = RW_GROUP * HD
    ci = lax.broadcasted_iota(jnp.int32, (C, C), 0)
    cj = lax.broadcasted_iota(jnp.int32, (C, C), 1)
    tri_incl = (cj <= ci).astype(BF16)
    ti = lax.broadcasted_iota(jnp.int32, (C, GW), 0)
    tj = lax.broadcasted_iota(jnp.int32, (C, GW), 1) % HD
    strict = tj < ti
    incl = tj <= ti
    eye = ti == tj
    tile_eye = eye.astype(BF16)
    blk_masks = []
    s = 2
    while s < C:
        blk_masks.append((ti // (2 * s) == tj // (2 * s)) & (ti // s != tj // s))
        s *= 2
    pair_mask = (ti // 2 == tj // 2) & strict
    gi = lax.broadcasted_iota(jnp.int32, (GW, GW), 0)
    gj = lax.broadcasted_iota(jnp.int32, (GW, GW), 1)
    bd_mask = gi // HD == gj // HD
    bd_mask_b = jnp.where(bd_mask, 1.0, 0.0).astype(BF16)
    gw_eye = gi == gj

    def bd(x):
        xb = x.astype(BF16)
        return jnp.where(bd_mask_b > 0, jnp.concatenate([xb] * RW_GROUP, axis=0), 0)

    def bd_of_full(full):
        return jnp.where(bd_mask, full, 0.0)

    n_chunks = RW_TILE // C
    n_groups = RW_WIDTH // GW
    systems = [(c, q) for c in range(n_chunks) for q in range(n_groups)]

    pre = []
    for c in range(n_chunks):
        rows = slice(c * C, (c + 1) * C)
        lw = log_w[rows]
        cum = _cumsum_rows(lw, tri_incl)
        gam = jnp.exp(cum)
        gam_prev = jnp.exp(cum - lw)
        gam_inv = jnp.exp(-cum)
        g_end = gam[C - 1:C, :]
        bh_f = kk[rows] * a[rows] * gam_inv
        kh_f = k[rows] * gam_inv
        pre.append(dict(
            rh=(r[rows] * gam).astype(BF16), ah=(-kk[rows] * gam_prev).astype(BF16),
            bh=bh_f.astype(BF16), kh=kh_f.astype(BF16),
            bt=(bh_f * g_end).astype(BF16), kt=(kh_f * g_end).astype(BF16),
            v=v[rows].astype(BF16), g_end=g_end))

    def grp(c, q, name):
        return pre[c][name][:, q * GW:(q + 1) * GW]

    st = {}
    for (c, q) in systems:
        ar = jnp.concatenate([grp(c, q, "ah"), grp(c, q, "rh")], axis=0)
        rb = bd_of_full(_dot_tn(grp(c, q, "bh"), tile_eye)).astype(BF16)
        rk = bd_of_full(_dot_tn(grp(c, q, "kh"), tile_eye)).astype(BF16)
        pb = _dot(ar, rb)
        pk = _dot(ar, rk)
        lab = jnp.where(strict, pb[0:C], 0.0)
        st[(c, q)] = dict(
            lab=lab, mrb=jnp.where(incl, pb[C:2 * C], 0.0).astype(BF16),
            lm=jnp.concatenate([jnp.where(strict, pk[0:C], 0.0), jnp.where(incl, pk[C:2 * C], 0.0)],
                               axis=0).astype(BF16),
            tinv=jnp.where(eye, 1.0, jnp.where(pair_mask, lab, 0.0)))

    for bm in blk_masks:
        xs = {}
        for key in systems:
            d_ = st[key]
            xs[key] = _dot(d_["tinv"].astype(BF16), bd(jnp.where(bm, d_["lab"], 0.0))).astype(BF16)
        for key in systems:
            d_ = st[key]
            d_["tinv"] = d_["tinv"] + _dot(xs[key], bd(d_["tinv"]))

    for (c, q) in systems:
        d_ = st[(c, q)]
        lv = _dot(d_["lm"], bd(grp(c, q, "v")))
        tb = d_["tinv"].astype(BF16)
        ap = _dot(tb, bd(grp(c, q, "ah"))).astype(BF16)
        up = _dot(tb, bd(lv[0:C])).astype(BF16)
        d_["rp"] = (grp(c, q, "rh").astype(F32) + _dot(d_["mrb"], bd(ap))).astype(BF16)
        d_["ypp"] = _dot(d_["mrb"], bd(up)) + lv[C:2 * C]
        g_end = pre[c]["g_end"][:, q * GW:(q + 1) * GW]
        gt = _dot_tn(ap, grp(c, q, "bt"))
        d_["gt"] = jnp.where(gw_eye, gt + g_end, bd_of_full(gt)).astype(BF16)
        ht = bd_of_full(_dot_tn(up, grp(c, q, "bt")) + _dot_tn(grp(c, q, "v"), grp(c, q, "kt")))
        d_["ht"] = sum(ht[hh * HD:(hh + 1) * HD] for hh in range(1, RW_GROUP)) + ht[0:HD]

    for q in range(n_groups):
        s_cur = st_ref[q]
        for c in range(n_chunks):
            d_ = st[(c, q)]
            y_ref[c * C:(c + 1) * C, q * GW:(q + 1) * GW] = _dot_nt(d_["rp"], bd(s_cur)) + d_["ypp"]
            s_cur = _dot(s_cur.astype(BF16), d_["gt"]) + d_["ht"]
        st_ref[q] = s_cur

    y = y_ref[...]
    inv_n = 1.0 / HEAD_DIM
    m = _dot_split(y, hsum) * inv_n
    yc = y - m
    var = _dot_split(yc * yc, hsum) * inv_n
    yn = yc * lax.rsqrt(var + RW_GN_EPS) * lng_ref[...] + lnb_ref[...]
    o_ref[...] = ((yn + bonus) * g).astype(o_ref.dtype)


def _cumsum_rows(x, tri_incl_bf16):
    hi = x.astype(BF16)
    lo = (x - hi.astype(F32)).astype(BF16)
    return _dot(tri_incl_bf16, hi) + _dot(tri_incl_bf16, lo)


def rw_call(p, B, T, mu, w0, w2, a0, a2, g2, k_k, k_a, r_k, ln_g, ln_b):
    nt = T // RW_TILE
    sub = RW_TILE // BF16_SUBLANES
    hid = jnp.arange(RW_WIDTH) // HEAD_DIM
    hsum = (hid[:, None] == hid[None, :]).astype(BF16)
    row = lambda a: a.reshape(1, -1).astype(F32)
    vec = lambda n: pl.BlockSpec((1, n), lambda b, i: (0, 0))
    return pl.pallas_call(
        _rw_kernel,
        grid=(B, nt),
        in_specs=[pl.BlockSpec((None, RW_TILE, RW_COLS), lambda b, i: (b, i, 0)),
                  pl.BlockSpec((None, BF16_SUBLANES, RW_COLS),
                               lambda b, i: (b, jnp.maximum(i * sub - 1, 0), 0)),
                  vec(RW_COLS), vec(RW_WIDTH),
                  pl.BlockSpec((DECAY_LORA, RW_WIDTH), lambda b, i: (0, 0)),
                  vec(RW_WIDTH),
                  pl.BlockSpec((AAA_LORA, RW_WIDTH), lambda b, i: (0, 0)),
                  pl.BlockSpec((GATE_LORA, RW_WIDTH), lambda b, i: (0, 0)),
                  vec(RW_WIDTH), vec(RW_WIDTH), vec(RW_WIDTH), vec(RW_WIDTH), vec(RW_WIDTH),
                  pl.BlockSpec((RW_WIDTH, RW_WIDTH), lambda b, i: (0, 0))],
        out_specs=pl.BlockSpec((None, RW_TILE, RW_WIDTH), lambda b, i: (b, i, 0)),
        out_shape=jax.ShapeDtypeStruct((B, T, RW_WIDTH), BF16),
        scratch_shapes=[pltpu.VMEM((RW_HEADS // RW_GROUP, HEAD_DIM, RW_GROUP * HEAD_DIM), F32),
                        pltpu.VMEM((RW_TILE, RW_WIDTH), F32)],
        compiler_params=pltpu.CompilerParams(dimension_semantics=("parallel", "arbitrary")),
        name="rwkv7_time_mix",
    )(p, p, row(mu), row(w0), w2.astype(BF16), row(a0), a2.astype(BF16), g2.astype(BF16),
      row(k_k), row(k_a), row(r_k), row(ln_g), row(ln_b), hsum)


def _sb_step(qs, ks, vs, carries, suffix_ones, causal):
    n = range(len(qs))
    zs = [_dot_nt(qs[h], ks[h]) for h in n]
    sps = [jnp.maximum(z, 0.0) + jnp.log1p(jnp.exp(-jnp.abs(z))) for z in zs]
    log_nots = [-sp for sp in sps]
    if causal is not None:
        log_nots = [jnp.where(causal, ln, 0.0) for ln in log_nots]
    his = [ln.astype(BF16) for ln in log_nots]
    los = [(log_nots[h] - his[h].astype(F32)).astype(BF16) for h in n]
    sufs = [_dot(his[h], suffix_ones) + _dot(los[h], suffix_ones) for h in n]
    rowsums = [jnp.sum(ln, axis=-1, keepdims=True) for ln in log_nots]
    atts = [jnp.exp(zs[h] - sps[h] + (sufs[h] + carries[h])) for h in n]
    if causal is not None:
        atts = [jnp.where(causal, att, 0.0) for att in atts]
    outs = [_dot(atts[h].astype(BF16), vs[h]) for h in n]
    return outs, rowsums


def _sb_kernel(q_ref, k_ref, v_ref, o_ref, acc_ref, carry_ref, *, T):
    tq = SB_TILE
    nq = T // tq
    ti = lax.broadcasted_iota(jnp.int32, (tq, tq), 0)
    tj = lax.broadcasted_iota(jnp.int32, (tq, tq), 1)
    suffix_ones = (ti > tj).astype(BF16)
    causal = tj < ti
    scale = HEAD_DIM ** -0.5
    heads = [slice(hh * HEAD_DIM, (hh + 1) * HEAD_DIM) for hh in range(LANES // HEAD_DIM)]

    def q_body(i, _):
        r0 = pl.multiple_of(i * tq, tq)
        qs = [q_ref[pl.ds(r0, tq), cs] * scale for cs in heads]
        outs, rss = _sb_step(qs, [k_ref[pl.ds(r0, tq), cs] for cs in heads],
                             [v_ref[pl.ds(r0, tq), cs] for cs in heads],
                             [jnp.zeros((tq, 1), F32)] * len(heads), suffix_ones, causal)
        live = jnp.float32(-jnp.inf)
        for hh, cs in enumerate(heads):
            acc_ref[:, cs] = outs[hh]
            carry_ref[hh] = rss[hh]
            live = jnp.maximum(live, jnp.max(rss[hh]))

        def k_cond(state):
            jr, live = state
            return (jr < i) & (live > SB_EXP_UNDERFLOW)

        def k_body(state):
            jr, _ = state
            c0 = pl.multiple_of((i - 1 - jr) * tq, tq)
            carries = [carry_ref[hh] for hh in range(len(heads))]
            outs, rss = _sb_step(qs, [k_ref[pl.ds(c0, tq), cs] for cs in heads],
                                 [v_ref[pl.ds(c0, tq), cs] for cs in heads],
                                 carries, suffix_ones, None)
            live = jnp.float32(-jnp.inf)
            for hh, cs in enumerate(heads):
                acc_ref[:, cs] += outs[hh]
                carry = carries[hh] + rss[hh]
                carry_ref[hh] = carry
                live = jnp.maximum(live, jnp.max(carry))
            return jr + 1, live

        lax.while_loop(k_cond, k_body, (jnp.int32(0), live))
        o_ref[pl.ds(r0, tq), :] = acc_ref[...].astype(o_ref.dtype)
        return 0

    lax.fori_loop(0, nq, q_body, 0)


def sb_call(qkv, B, T):
    npair = SB_WIDTH // LANES
    blk = lambda off: pl.BlockSpec((None, T, LANES), lambda b, hp: (b, 0, off + hp))
    return pl.pallas_call(
        functools.partial(_sb_kernel, T=T),
        grid=(B, npair),
        in_specs=[blk(0), blk(npair), blk(2 * npair)],
        out_specs=pl.BlockSpec((None, T, LANES), lambda b, hp: (b, 0, hp)),
        out_shape=jax.ShapeDtypeStruct((B, T, SB_WIDTH), BF16),
        scratch_shapes=[pltpu.VMEM((SB_TILE, LANES), F32),
                        pltpu.VMEM((LANES // HEAD_DIM, SB_TILE, 1), F32)],
        compiler_params=pltpu.CompilerParams(dimension_semantics=("parallel", "parallel")),
        name="stick_breaking_attention",
    )(qkv, qkv, qkv)


def _merge_kernel(ygm_ref, yrw_ref, ysb_ref, gate_ref, h_ref, pgm_ref, prw_ref, psb_ref, wo_ref,
                  g_ref, ho_ref, no_ref):
    D = D_MODEL
    merged = gate_ref[:, 0:D].astype(F32) * _dot(ygm_ref[...], pgm_ref[...])
    merged += gate_ref[:, D:2 * D].astype(F32) * _dot(yrw_ref[...], prw_ref[...])
    merged += gate_ref[:, 2 * D:3 * D].astype(F32) * _dot(ysb_ref[...], psb_ref[...])
    hn = h_ref[...] + _dot(merged.astype(BF16), wo_ref[...])
    ho_ref[...] = hn
    no_ref[...] = _rms(hn, g_ref[...]).astype(no_ref.dtype)


def _resident(shape):
    return pl.BlockSpec(shape, lambda i: (0,) * len(shape), pipeline_mode=pl.Buffered(1))


def merge_call(y_gm, y_rw, y_sb, gates, h, p_gm, p_rw, p_sb, w_o, g_next, tm=256):
    M, D = h.shape
    tm = min(tm, M)
    rowblk = lambda n: pl.BlockSpec((tm, n), lambda i: (i, 0))
    wbytes = 2 * (GM_WIDTH + RW_WIDTH + SB_WIDTH + D) * D
    need = wbytes + 2 * tm * (2 * (GM_WIDTH + RW_WIDTH + SB_WIDTH + GATE_COLS) + 4 * D + 4 * D + 2 * D) \
        + 6 * tm * D * 4
    return pl.pallas_call(
        _merge_kernel,
        grid=(M // tm,),
        in_specs=[rowblk(GM_WIDTH), rowblk(RW_WIDTH), rowblk(SB_WIDTH), rowblk(GATE_COLS), rowblk(D),
                  _resident((GM_WIDTH, D)), _resident((RW_WIDTH, D)), _resident((SB_WIDTH, D)),
                  _resident((D, D)), _resident((1, D))],
        out_specs=[rowblk(D), rowblk(D)],
        out_shape=[jax.ShapeDtypeStruct((M, D), F32), jax.ShapeDtypeStruct((M, D), BF16)],
        compiler_params=pltpu.CompilerParams(dimension_semantics=("parallel",),
                                             vmem_limit_bytes=_vmem_limit(need)),
        name="merge_out_proj",
    )(y_gm, y_rw, y_sb, gates, h, p_gm, p_rw, p_sb, w_o, g_next.reshape(1, D))


def _ffn_kernel(n_ref, h_ref, w1_ref, w3_ref, w2_ref, g_ref, *refs, emit_h):
    if emit_h:
        ho_ref, no_ref, acc_ref = refs
    else:
        no_ref, acc_ref = refs
    f = pl.program_id(1)

    @pl.when(f == 0)
    def _():
        acc_ref[...] = jnp.zeros_like(acc_ref)

    n = n_ref[...]
    act = (jax.nn.silu(_dot(n, w1_ref[...])) * _dot(n, w3_ref[...])).astype(BF16)
    acc_ref[...] += _dot(act, w2_ref[...])

    @pl.when(f == pl.num_programs(1) - 1)
    def _():
        hn = h_ref[...] + acc_ref[...]
        if emit_h:
            ho_ref[...] = hn
        no_ref[...] = _rms(hn, g_ref[...]).astype(no_ref.dtype)


def ffn_call(n, h, w1, w3, w2, g_next, n_dtype, emit_h, tm=512, tf=512):
    M, D = h.shape
    F = w1.shape[1]
    tm = min(tm, M)
    rowblk = pl.BlockSpec((tm, D), lambda i, f: (i, 0))
    out_specs = [rowblk]
    out_shape = [jax.ShapeDtypeStruct((M, D), n_dtype)]
    if emit_h:
        out_specs = [rowblk, rowblk]
        out_shape = [jax.ShapeDtypeStruct((M, D), F32)] + out_shape
    need = 2 * tm * D * (2 + 4 + 4 * len(out_shape)) + tm * D * 4 + 2 * 3 * D * tf * 2 + 4 * tm * tf * 4
    return pl.pallas_call(
        functools.partial(_ffn_kernel, emit_h=emit_h),
        grid=(M // tm, F // tf),
        in_specs=[rowblk, rowblk,
                  pl.BlockSpec((D, tf), lambda i, f: (0, f)),
                  pl.BlockSpec((D, tf), lambda i, f: (0, f)),
                  pl.BlockSpec((tf, D), lambda i, f: (f, 0)),
                  pl.BlockSpec((1, D), lambda i, f: (0, 0))],
        out_specs=out_specs,
        out_shape=out_shape,
        scratch_shapes=[pltpu.VMEM((tm, D), F32)],
        compiler_params=pltpu.CompilerParams(dimension_semantics=("parallel", "arbitrary"),
                                             vmem_limit_bytes=_vmem_limit(need + (8 << 20))),
        name="swiglu_ffn",
    )(n, h, w1, w3, w2, g_next.reshape(1, D))


def _router_kernel(h_ref, g_ref, wh_ref, wl_ref, b_ref, n_ref, id_ref, p_ref):
    n = _rms(h_ref[...], g_ref[...])
    n_ref[...] = n
    nh = n.astype(BF16)
    nl = (n - nh.astype(F32)).astype(BF16)
    logits = _dot(nh, wh_ref[...]) + _dot(nl, wh_ref[...]) + _dot(nh, wl_ref[...]) + b_ref[...]
    lane = lax.broadcasted_iota(jnp.int32, logits.shape, 1)
    neg = jnp.float32(-jnp.inf)
    logits = jnp.where(lane < N_EXPERTS, logits, neg)
    m1 = jnp.max(logits, axis=-1, keepdims=True)
    i1 = jnp.min(jnp.where(logits == m1, lane, ROUTER_LANES), axis=-1, keepdims=True)
    rest = jnp.where(lane == i1, neg, logits)
    m2 = jnp.max(rest, axis=-1, keepdims=True)
    i2 = jnp.min(jnp.where(rest == m2, lane, ROUTER_LANES), axis=-1, keepdims=True)
    e2 = jnp.exp(m2 - m1)
    p1 = 1.0 / (1.0 + e2)
    p2 = e2 / (1.0 + e2)
    id_ref[...] = jnp.where(lane == 0, i1, jnp.where(lane == 1, i2, 0))
    p_ref[...] = jnp.where(lane == 0, p1, jnp.where(lane == 1, p2, 0.0))


def router_call(h, g, router_w, router_b, tm=512):
    M, D = h.shape
    tm = min(tm, M)
    wpad = jnp.zeros((D, ROUTER_LANES), F32).at[:, :N_EXPERTS].set(router_w)
    w_hi = wpad.astype(BF16)
    w_lo = (wpad - w_hi.astype(F32)).astype(BF16)
    bpad = jnp.zeros((1, ROUTER_LANES), F32).at[0, :N_EXPERTS].set(router_b)
    const = lambda shape: pl.BlockSpec(shape, lambda i: (0, 0))
    return pl.pallas_call(
        _router_kernel,
        grid=(M // tm,),
        in_specs=[pl.BlockSpec((tm, D), lambda i: (i, 0)), const((1, D)),
                  const((D, ROUTER_LANES)), const((D, ROUTER_LANES)), const((1, ROUTER_LANES))],
        out_specs=[pl.BlockSpec((tm, D), lambda i: (i, 0)),
                   pl.BlockSpec((tm, ROUTER_LANES), lambda i: (i, 0)),
                   pl.BlockSpec((tm, ROUTER_LANES), lambda i: (i, 0))],
        out_shape=[jax.ShapeDtypeStruct((M, D), F32),
                   jax.ShapeDtypeStruct((M, ROUTER_LANES), jnp.int32),
                   jax.ShapeDtypeStruct((M, ROUTER_LANES), F32)],
        compiler_params=pltpu.CompilerParams(dimension_semantics=("parallel",)),
        name="moe_router",
    )(h, g.reshape(1, D), w_hi, w_lo, bpad)


def moe_plan(ids, probs, M, tm):
    E = N_EXPERTS
    n_tiles = (TOP_K * M) // tm + E
    e_flat = ids[:, :TOP_K].reshape(-1)
    p_flat = probs[:, :TOP_K].reshape(-1)
    onehot = (e_flat[:, None] == jnp.arange(E, dtype=jnp.int32)[None, :]).astype(jnp.int32)
    csum = jnp.cumsum(onehot, axis=0)
    rank = jnp.sum((csum - onehot) * onehot, axis=1)
    counts = csum[-1]
    tiles_per_e = (counts + tm - 1) // tm
    tile_end = jnp.cumsum(tiles_per_e)
    offs = (tile_end - tiles_per_e) * tm
    dest = offs[e_flat] + rank
    src_token = jnp.zeros((n_tiles * tm,), jnp.int32).at[dest].set(
        jnp.arange(TOP_K * M, dtype=jnp.int32) // TOP_K, unique_indices=True)
    wgt = jnp.zeros((n_tiles * tm,), F32).at[dest].set(p_flat, unique_indices=True)
    n_used = tile_end[-1]
    t = jnp.arange(n_tiles, dtype=jnp.int32)
    tile_e = jnp.searchsorted(tile_end, jnp.minimum(t, n_used - 1), side="right").astype(jnp.int32)
    return (src_token.reshape(n_tiles, 1, tm), wgt.reshape(n_tiles * tm, 1), tile_e,
            n_used.astype(jnp.int32).reshape(1), dest.reshape(M, TOP_K))


def _moe_kernel(tile_e_ref, n_used_ref, src_ref, wgt_ref, x_hbm, w1_ref, w3_ref, w2_ref, o_ref,
                xbuf, xb, acc, sem):
    t = pl.program_id(0)
    f = pl.program_id(1)
    nf = pl.num_programs(1)
    tm = xbuf.shape[0]
    used = t < n_used_ref[0]

    def row_copy(r):
        return pltpu.make_async_copy(x_hbm.at[pl.ds(src_ref[0, r], 1)], xbuf.at[pl.ds(r, 1)], sem)

    @pl.when(used & (f == 0))
    def _():
        def issue(r, c):
            row_copy(r).start()
            return c
        lax.fori_loop(0, tm, issue, 0)
        pltpu.make_async_copy(x_hbm.at[pl.ds(0, tm)], xbuf, sem).wait()
        xb[...] = xbuf[...].astype(BF16)
        acc[...] = jnp.zeros_like(acc)

    @pl.when(used)
    def _():
        x = xb[...]
        act = jax.nn.silu(_dot(x, w1_ref[...])) * _dot(x, w3_ref[...])
        acc[...] += _dot((act * wgt_ref[...]).astype(BF16), w2_ref[...])

    @pl.when(f == nf - 1)
    def _():
        o_ref[...] = jnp.where(used, acc[...], 0.0)


def moe_call(x, src_token, wgt, tile_e, n_used, w1t, w3t, w2, tm):
    M, D = x.shape
    n_tiles = src_token.shape[0]
    E, nf, _, tf = w1t.shape

    def wmap(t, f, tile_e_ref, n_used_ref):
        ff = jnp.where(t < n_used_ref[0], f, nf - 1)
        return tile_e_ref[t], ff

    grid_spec = pltpu.PrefetchScalarGridSpec(
        num_scalar_prefetch=2,
        grid=(n_tiles, nf),
        in_specs=[pl.BlockSpec((None, 1, tm), lambda t, f, te, nu: (t, 0, 0), memory_space=pltpu.SMEM),
                  pl.BlockSpec((tm, 1), lambda t, f, te, nu: (t, 0)),
                  pl.BlockSpec(memory_space=pl.ANY),
                  pl.BlockSpec((None, None, D, tf), lambda t, f, te, nu: wmap(t, f, te, nu) + (0, 0)),
                  pl.BlockSpec((None, None, D, tf), lambda t, f, te, nu: wmap(t, f, te, nu) + (0, 0)),
                  pl.BlockSpec((None, tf, D), lambda t, f, te, nu: wmap(t, f, te, nu) + (0,))],
        out_specs=pl.BlockSpec((tm, D), lambda t, f, te, nu: (t, 0)),
        scratch_shapes=[pltpu.VMEM((tm, D), F32), pltpu.VMEM((tm, D), BF16), pltpu.VMEM((tm, D), F32),
                        pltpu.SemaphoreType.DMA(())])
    need = tm * D * (4 + 2 + 4) + 2 * tm * D * 4 + 2 * 3 * D * tf * 2 + 4 * tm * tf * 4
    return pl.pallas_call(
        _moe_kernel,
        grid_spec=grid_spec,
        out_shape=jax.ShapeDtypeStruct((n_tiles * tm, D), F32),
        compiler_params=pltpu.CompilerParams(dimension_semantics=("arbitrary", "arbitrary"),
                                             vmem_limit_bytes=_vmem_limit(need + (8 << 20))),
        name="moe_experts",
    )(tile_e, n_used, src_token, wgt, x, w1t, w3t, w2)


def _moe_combine_kernel(slot_ref, h_ref, g_ref, y_hbm, *refs, emit_h):
    if emit_h:
        ho_ref, no_ref, ybuf, sem = refs
    else:
        no_ref, ybuf, sem = refs
    tm = h_ref.shape[0]

    def row_copy(r, k):
        return pltpu.make_async_copy(y_hbm.at[pl.ds(slot_ref[0, TOP_K * r + k], 1)],
                                     ybuf.at[pl.ds(k * tm + r, 1)], sem)

    def issue(r, c):
        for k in range(TOP_K):
            row_copy(r, k).start()
        return c

    lax.fori_loop(0, tm, issue, 0)
    pltpu.make_async_copy(y_hbm.at[pl.ds(0, TOP_K * tm)], ybuf, sem).wait()
    hn = h_ref[...]
    for k in range(TOP_K):
        hn = hn + ybuf[k * tm:(k + 1) * tm]
    if emit_h:
        ho_ref[...] = hn
    no_ref[...] = _rms(hn, g_ref[...]).astype(no_ref.dtype)


def moe_combine_call(h, y_slots, slots, g_next, n_dtype, emit_h, tm=256):
    M, D = h.shape
    tm = min(tm, M)
    rowblk = pl.BlockSpec((tm, D), lambda i: (i, 0))
    out_specs = [rowblk]
    out_shape = [jax.ShapeDtypeStruct((M, D), n_dtype)]
    if emit_h:
        out_specs = [rowblk, rowblk]
        out_shape = [jax.ShapeDtypeStruct((M, D), F32)] + out_shape
    return pl.pallas_call(
        functools.partial(_moe_combine_kernel, emit_h=emit_h),
        grid=(M // tm,),
        in_specs=[pl.BlockSpec((None, 1, TOP_K * tm), lambda i: (i, 0, 0), memory_space=pltpu.SMEM),
                  rowblk, pl.BlockSpec((1, D), lambda i: (0, 0)),
                  pl.BlockSpec(memory_space=pl.ANY)],
        out_specs=out_specs,
        out_shape=out_shape,
        scratch_shapes=[pltpu.VMEM((TOP_K * tm, D), F32), pltpu.SemaphoreType.DMA(())],
        compiler_params=pltpu.CompilerParams(dimension_semantics=("arbitrary",),
                                             vmem_limit_bytes=_vmem_limit(12 * tm * D * 4)),
        name="moe_combine",
    )(slots.reshape(M // tm, 1, TOP_K * tm), h, g_next.reshape(1, D), y_slots)


def kernel(x, norm_mix, w_in, gate_b, gm_ln_g, gm_ln_b, gm_ws, gm_bs, rw_mu, rw_w0, rw_w2, rw_a0, rw_a2,
           rw_g2, rw_kk, rw_ka, rw_rk, rw_ln_g, rw_ln_b, p_gm, p_rw, p_sb, w_o, norm_ffn, ffn_w1, ffn_w3,
           ffn_w2, router_w, router_b, moe_w1, moe_w3, moe_w2, norm_out):
    B, T, D = x.shape
    M = B * T
    bf = lambda a: a.astype(BF16)
    h = x.reshape(M, D)
    n = rmsnorm_call(h, norm_mix[0], BF16)
    for l in range(DEPTH):
        last = l == DEPTH - 1
        g_next = norm_out if last else norm_mix[l + 1]
        n_dtype = x.dtype if last else BF16
        w = w_in[l]
        uv = matmul_call(n, bf(w[:, GM_OFF:RW_OFF]), BF16, tn=512, name="in_proj_gm")
        p = matmul_call(n, bf(w[:, RW_OFF:SB_OFF]), BF16, tn=256, name="in_proj_rw")
        qkv = matmul_call(n, bf(w[:, SB_OFF:GATE_OFF]), BF16, tn=512, name="in_proj_sb")
        gates = matmul_call(n, bf(w[:, GATE_OFF:]), BF16, tn=512, bias=gate_b[l].reshape(-1),
                            name="in_proj_gate")
        y_gm = gm_call(uv, gm_ln_g[l], gm_ln_b[l], gm_ws[l], gm_bs[l])
        y_rw = rw_call(p.reshape(B, T, RW_COLS), B, T, rw_mu[l], rw_w0[l], rw_w2[l], rw_a0[l], rw_a2[l],
                       rw_g2[l], rw_kk[l], rw_ka[l], rw_rk[l], rw_ln_g[l], rw_ln_b[l]).reshape(M, RW_WIDTH)
        y_sb = sb_call(qkv.reshape(B, T, SB_COLS), B, T).reshape(M, SB_WIDTH)
        h, n2 = merge_call(y_gm, y_rw, y_sb, gates, h, bf(p_gm[l]), bf(p_rw[l]), bf(p_sb[l]), bf(w_o[l]),
                           norm_ffn[l])
        j = l // 2
        if l % 2 == 0:
            outs = ffn_call(n2, h, bf(ffn_w1[j]), bf(ffn_w3[j]), bf(ffn_w2[j]), g_next, n_dtype,
                            emit_h=not last)
        else:
            nf32, ids, probs = router_call(h, norm_ffn[l], router_w[j], router_b[j])
            tm = min(MOE_TM, M)
            src_token, wgt, tile_e, n_used, slots = moe_plan(ids, probs, M, tm)
            E, _, Fe = moe_w1[j].shape
            col_tiles = lambda w: bf(w.reshape(E, D, Fe // MOE_TF, MOE_TF).transpose(0, 2, 1, 3))
            y_slots = moe_call(nf32, src_token, wgt, tile_e, n_used, col_tiles(moe_w1[j]),
                               col_tiles(moe_w3[j]), bf(moe_w2[j]), tm)
            outs = moe_combine_call(h, y_slots, slots, g_next, n_dtype, emit_h=not last)
        if last:
            n = outs[0]
        else:
            h, n = outs
    return n.reshape(B, T, D)
```

```python
import functools

import jax
import jax.numpy as jnp
from jax import lax
from jax.experimental import pallas as pl
from jax.experimental.pallas import tpu as pltpu

F32 = jnp.float32
BF16 = jnp.bfloat16

D_MODEL = 2048
DEPTH = 2
HEAD_DIM = 64
GM_HEADS = 8
GM_WIDTH = GM_HEADS * HEAD_DIM
CHUNK = 128
RW_HEADS = 8
RW_WIDTH = RW_HEADS * HEAD_DIM
DECAY_LORA = 64
AAA_LORA = 64
GATE_LORA = 128
SB_HEADS = 16
SB_WIDTH = SB_HEADS * HEAD_DIM
N_BRANCH = 3
D_FF = 5632
N_EXPERTS = 8
TOP_K = 2
D_FF_EXPERT = D_FF // TOP_K
RMS_EPS = 1e-6
LN_EPS = 1e-5
RW_GN_EPS = 64e-5

GM_COLS = 2 * GM_WIDTH
RW_COLS = 3 * RW_WIDTH + DECAY_LORA + AAA_LORA + GATE_LORA
SB_COLS = 3 * SB_WIDTH
GATE_COLS = N_BRANCH * D_MODEL
GM_OFF = 0
RW_OFF = GM_OFF + GM_COLS
SB_OFF = RW_OFF + RW_COLS
GATE_OFF = SB_OFF + SB_COLS
N_IN = GATE_OFF + GATE_COLS

V7X_VMEM_BYTES = 64 * 1024 * 1024
LANES = 128
BF16_SUBLANES = 16
RW_CHUNK = 64
RW_TILE = 256
RW_GROUP = 4
SB_TILE = 256
LOG2_E = 1.4426950408889634
SB_EXP2_UNDERFLOW = -151.0
ROUTER_LANES = 128
MOE_TF = 256
MOE_TM = 528
MOE_COMBINE_TM = 256


def _vmem_limit(nbytes):
    return int(min(max(nbytes, 32 * 1024 * 1024), V7X_VMEM_BYTES - 6 * 1024 * 1024))


def _dot(a, b):
    return jnp.dot(a, b, preferred_element_type=F32)


def _dot_nt(a, b):
    return lax.dot_general(a, b, (((1,), (1,)), ((), ())), preferred_element_type=F32)


def _dot_tn(a, b):
    return lax.dot_general(a, b, (((0,), (0,)), ((), ())), preferred_element_type=F32)


def _dot_split(x, ones_bf16):
    hi = x.astype(BF16)
    lo = (x - hi.astype(F32)).astype(BF16)
    return _dot(hi, ones_bf16) + _dot(lo, ones_bf16)


def _rms(x, g):
    ms = jnp.mean(x * x, axis=-1, keepdims=True)
    return x * lax.rsqrt(ms + RMS_EPS) * g


def _rmsnorm_kernel(x_ref, g_ref, o_ref):
    o_ref[...] = _rms(x_ref[...], g_ref[...]).astype(o_ref.dtype)


def rmsnorm_call(x, g, out_dtype, tm=512):
    M, D = x.shape
    tm = min(tm, M)
    return pl.pallas_call(
        _rmsnorm_kernel,
        grid=(M // tm,),
        in_specs=[pl.BlockSpec((tm, D), lambda i: (i, 0)),
                  pl.BlockSpec((1, D), lambda i: (0, 0))],
        out_specs=pl.BlockSpec((tm, D), lambda i: (i, 0)),
        out_shape=jax.ShapeDtypeStruct((M, D), out_dtype),
        compiler_params=pltpu.CompilerParams(dimension_semantics=("parallel",)),
        name="rmsnorm",
    )(x, g.reshape(1, D))


def _mm_kernel(a_ref, w_ref, o_ref):
    o_ref[...] = _dot(a_ref[...], w_ref[...]).astype(o_ref.dtype)


def _mm_gate_kernel(a_ref, w_ref, b_ref, o_ref):
    o_ref[...] = jax.nn.sigmoid(_dot(a_ref[...], w_ref[...]) + b_ref[...]).astype(o_ref.dtype)


def matmul_call(a, w, out_dtype, tn, bias=None, tm=1024, name="matmul"):
    M, K = a.shape
    N = w.shape[1]
    tm = min(tm, M)
    in_specs = [pl.BlockSpec((tm, K), lambda i, j: (i, 0)),
                pl.BlockSpec((K, tn), lambda i, j: (0, j))]
    args = [a, w]
    kern = _mm_kernel
    if bias is not None:
        in_specs.append(pl.BlockSpec((1, tn), lambda i, j: (0, j)))
        args.append(bias.reshape(1, N))
        kern = _mm_gate_kernel
    osz = jnp.dtype(out_dtype).itemsize
    need = 2 * (tm * K * 2 + K * tn * 2 + tm * tn * osz) + 2 * tm * tn * 4
    return pl.pallas_call(
        kern,
        grid=(M // tm, N // tn),
        in_specs=in_specs,
        out_specs=pl.BlockSpec((tm, tn), lambda i, j: (i, j)),
        out_shape=jax.ShapeDtypeStruct((M, N), out_dtype),
        compiler_params=pltpu.CompilerParams(
            dimension_semantics=("parallel", "arbitrary"),
            vmem_limit_bytes=_vmem_limit(need + (8 << 20))),
        name=name,
    )(*args)


def _gm_kernel(uv_ref, lng_ref, lnb_ref, ws_ref, bias_ref, o_ref, *, n_chunks):
    ii = lax.broadcasted_iota(jnp.int32, (CHUNK, CHUNK), 0)
    jj = lax.broadcasted_iota(jnp.int32, (CHUNK, CHUNK), 1)
    causal = jj <= ii
    ws = [jnp.where(causal, ws_ref[h], 0.0).astype(BF16) for h in range(GM_HEADS)]
    for c in range(n_chunks):
        rows = slice(c * CHUNK, (c + 1) * CHUNK)
        u = jax.nn.gelu(uv_ref[rows, 0:GM_WIDTH].astype(F32))
        v = jax.nn.gelu(uv_ref[rows, GM_WIDTH:2 * GM_WIDTH].astype(F32))
        mu = jnp.mean(v, axis=-1, keepdims=True)
        vc = v - mu
        var = jnp.mean(vc * vc, axis=-1, keepdims=True)
        vn = (vc * lax.rsqrt(var + LN_EPS) * lng_ref[...] + lnb_ref[...]).astype(BF16)
        mixed = jnp.concatenate(
            [_dot(ws[h], vn[:, h * HEAD_DIM:(h + 1) * HEAD_DIM]) for h in range(GM_HEADS)], axis=-1)
        o_ref[rows, :] = (u * (mixed + bias_ref[...])).astype(o_ref.dtype)


def gm_call(uv, ln_g, ln_b, w_s, b_s, n_chunks=2):
    M = uv.shape[0]
    tt = CHUNK * n_chunks
    bias = jnp.repeat(b_s.T, HEAD_DIM, axis=1)
    return pl.pallas_call(
        functools.partial(_gm_kernel, n_chunks=n_chunks),
        grid=(M // tt,),
        in_specs=[pl.BlockSpec((tt, GM_COLS), lambda i: (i, 0)),
                  pl.BlockSpec((1, GM_WIDTH), lambda i: (0, 0)),
                  pl.BlockSpec((1, GM_WIDTH), lambda i: (0, 0)),
                  pl.BlockSpec((GM_HEADS, CHUNK, CHUNK), lambda i: (0, 0, 0)),
                  pl.BlockSpec((CHUNK, GM_WIDTH), lambda i: (0, 0))],
        out_specs=pl.BlockSpec((tt, GM_WIDTH), lambda i: (i, 0)),
        out_shape=jax.ShapeDtypeStruct((M, GM_WIDTH), BF16),
        compiler_params=pltpu.CompilerParams(dimension_semantics=("parallel",)),
        name="gmlp_gating",
    )(uv, ln_g.reshape(1, -1), ln_b.reshape(1, -1), w_s, bias)


def _rw_kernel(p_ref, prev_ref, mu_ref, w0_ref, w2_ref, a0_ref, a2_ref, g2_ref, kk_ref, ka_ref,
               rk_ref, lng_ref, lnb_ref, hsum_ref, o_ref, st_ref, y_ref):
    C = RW_CHUNK
    i = pl.program_id(1)

    @pl.when(i == 0)
    def _():
        st_ref[...] = jnp.zeros_like(st_ref)

    hsum = hsum_ref[...]

    p = p_ref[...].astype(F32)
    prev_row = prev_ref[BF16_SUBLANES - 1:BF16_SUBLANES, :].astype(F32)
    prev_row = jnp.where(i == 0, 0.0, prev_row)
    row = lax.broadcasted_iota(jnp.int32, p.shape, 0)
    p_prev = jnp.where(row == 0, prev_row, pltpu.roll(p, 1, axis=0))
    ps = p + (p_prev - p) * mu_ref[...]
    W = RW_WIDTH
    r = ps[:, 0:W]
    k = ps[:, W:2 * W]
    v = ps[:, 2 * W:3 * W]
    wl = ps[:, 3 * W:3 * W + DECAY_LORA]
    al = ps[:, 3 * W + DECAY_LORA:3 * W + DECAY_LORA + AAA_LORA]
    gl = ps[:, 3 * W + DECAY_LORA + AAA_LORA:RW_COLS]

    d = w0_ref[...] + _dot(jnp.tanh(wl).astype(BF16), w2_ref[...])
    log_w = -jnp.exp(-jax.nn.softplus(-d) - 0.5)
    a = jax.nn.sigmoid(a0_ref[...] + _dot(al.astype(BF16), a2_ref[...]))
    g = _dot(jax.nn.sigmoid(gl).astype(BF16), g2_ref[...])
    kk = k * kk_ref[...]
    kk = kk / jnp.maximum(jnp.sqrt(_dot_split(kk * kk, hsum)), 1e-12)
    k = k * (1.0 + (a - 1.0) * ka_ref[...])
    bonus = _dot_split(r * k * rk_ref[...], hsum) * v

    HD = HEAD_DIM
    GW = RW_GROUP * HD
    ci = lax.broadcasted_iota(jnp.int32, (C, C), 0)
    cj = lax.broadcasted_iota(jnp.int32, (C, C), 1)
    tri_incl = (cj <= ci).astype(BF16)
    ti = lax.broadcasted_iota(jnp.int32, (C, GW), 0)
    tj = lax.broadcasted_iota(jnp.int32, (C, GW), 1) % HD
    strict = tj < ti
    incl = tj <= ti
    eye = ti == tj
    tile_eye = eye.astype(BF16)
    blk_masks = []
    s = 2
    while s < C:
        blk_masks.append((ti // (2 * s) == tj // (2 * s)) & (ti // s != tj // s))
        s *= 2
    pair_mask = (ti // 2 == tj // 2) & strict
    gi = lax.broadcasted_iota(jnp.int32, (GW, GW), 0)
    gj = lax.broadcasted_iota(jnp.int32, (GW, GW), 1)
    bd_mask = gi // HD == gj // HD
    bd_mask_b = jnp.where(bd_mask, 1.0, 0.0).astype(BF16)
    gw_eye = gi == gj

    def bd(x):
        xb = x.astype(BF16)
        return jnp.where(bd_mask_b > 0, jnp.concatenate([xb] * RW_GROUP, axis=0), 0)

    def bd_of_full(full):
        return jnp.where(bd_mask, full, 0.0)

    n_chunks = RW_TILE // C
    n_groups = RW_WIDTH // GW
    systems = [(c, q) for c in range(n_chunks) for q in range(n_groups)]

    pre = []
    for c in range(n_chunks):
        rows = slice(c * C, (c + 1) * C)
        lw = log_w[rows]
        cum = _cumsum_rows(lw, tri_incl)
        gam = jnp.exp(cum)
        gam_prev = jnp.exp(cum - lw)
        gam_inv = jnp.exp(-cum)
        g_end = gam[C - 1:C, :]
        bh_f = kk[rows] * a[rows] * gam_inv
        kh_f = k[rows] * gam_inv
        pre.append(dict(
            rh=(r[rows] * gam).astype(BF16), ah=(-kk[rows] * gam_prev).astype(BF16),
            bh=bh_f.astype(BF16), kh=kh_f.astype(BF16),
            bt=(bh_f * g_end).astype(BF16), kt=(kh_f * g_end).astype(BF16),
            v=v[rows].astype(BF16), g_end=g_end))

    def grp(c, q, name):
        return pre[c][name][:, q * GW:(q + 1) * GW]

    st = {}
    for (c, q) in systems:
        ar = jnp.concatenate([grp(c, q, "ah"), grp(c, q, "rh")], axis=0)
        rb = bd_of_full(_dot_tn(grp(c, q, "bh"), tile_eye)).astype(BF16)
        rk = bd_of_full(_dot_tn(grp(c, q, "kh"), tile_eye)).astype(BF16)
        pb = _dot(ar, rb)
        pk = _dot(ar, rk)
        lab = jnp.where(strict, pb[0:C], 0.0)
        st[(c, q)] = dict(
            lab=lab, mrb=jnp.where(incl, pb[C:2 * C], 0.0).astype(BF16),
            lm=jnp.concatenate([jnp.where(strict, pk[0:C], 0.0), jnp.where(incl, pk[C:2 * C], 0.0)],
                               axis=0).astype(BF16),
            tinv=jnp.where(eye, 1.0, jnp.where(pair_mask, lab, 0.0)))

    for bm in blk_masks:
        xs = {}
        for key in systems:
            d_ = st[key]
            xs[key] = _dot(d_["tinv"].astype(BF16), bd(jnp.where(bm, d_["lab"], 0.0))).astype(BF16)
        for key in systems:
            d_ = st[key]
            d_["tinv"] = d_["tinv"] + _dot(xs[key], bd(d_["tinv"]))

    for (c, q) in systems:
        d_ = st[(c, q)]
        lv = _dot(d_["lm"], bd(grp(c, q, "v")))
        tb = d_["tinv"].astype(BF16)
        ap = _dot(tb, bd(grp(c, q, "ah"))).astype(BF16)
        up = _dot(tb, bd(lv[0:C])).astype(BF16)
        d_["rp"] = (grp(c, q, "rh").astype(F32) + _dot(d_["mrb"], bd(ap))).astype(BF16)
        d_["ypp"] = _dot(d_["mrb"], bd(up)) + lv[C:2 * C]
        g_end = pre[c]["g_end"][:, q * GW:(q + 1) * GW]
        gt = _dot_tn(ap, grp(c, q, "bt"))
        d_["gt"] = jnp.where(gw_eye, gt + g_end, bd_of_full(gt)).astype(BF16)
        ht = bd_of_full(_dot_tn(up, grp(c, q, "bt")) + _dot_tn(grp(c, q, "v"), grp(c, q, "kt")))
        d_["ht"] = sum(ht[hh * HD:(hh + 1) * HD] for hh in range(1, RW_GROUP)) + ht[0:HD]

    for q in range(n_groups):
        s_cur = st_ref[q]
        for c in range(n_chunks):
            d_ = st[(c, q)]
            y_ref[c * C:(c + 1) * C, q * GW:(q + 1) * GW] = _dot_nt(d_["rp"], bd(s_cur)) + d_["ypp"]
            s_cur = _dot(s_cur.astype(BF16), d_["gt"]) + d_["ht"]
        st_ref[q] = s_cur

    y = y_ref[...]
    inv_n = 1.0 / HEAD_DIM
    m = _dot_split(y, hsum) * inv_n
    yc = y - m
    var = _dot_split(yc * yc, hsum) * inv_n
    yn = yc * lax.rsqrt(var + RW_GN_EPS) * lng_ref[...] + lnb_ref[...]
    o_ref[...] = ((yn + bonus) * g).astype(o_ref.dtype)


def _cumsum_rows(x, tri_incl_bf16):
    hi = x.astype(BF16)
    lo = (x - hi.astype(F32)).astype(BF16)
    return _dot(tri_incl_bf16, hi) + _dot(tri_incl_bf16, lo)


def rw_call(p, B, T, mu, w0, w2, a0, a2, g2, k_k, k_a, r_k, ln_g, ln_b):
    nt = T // RW_TILE
    sub = RW_TILE // BF16_SUBLANES
    hid = jnp.arange(RW_WIDTH) // HEAD_DIM
    hsum = (hid[:, None] == hid[None, :]).astype(BF16)
    row = lambda a: a.reshape(1, -1).astype(F32)
    vec = lambda n: pl.BlockSpec((1, n), lambda b, i: (0, 0))
    return pl.pallas_call(
        _rw_kernel,
        grid=(B, nt),
        in_specs=[pl.BlockSpec((None, RW_TILE, RW_COLS), lambda b, i: (b, i, 0)),
                  pl.BlockSpec((None, BF16_SUBLANES, RW_COLS),
                               lambda b, i: (b, jnp.maximum(i * sub - 1, 0), 0)),
                  vec(RW_COLS), vec(RW_WIDTH),
                  pl.BlockSpec((DECAY_LORA, RW_WIDTH), lambda b, i: (0, 0)),
                  vec(RW_WIDTH),
                  pl.BlockSpec((AAA_LORA, RW_WIDTH), lambda b, i: (0, 0)),
                  pl.BlockSpec((GATE_LORA, RW_WIDTH), lambda b, i: (0, 0)),
                  vec(RW_WIDTH), vec(RW_WIDTH), vec(RW_WIDTH), vec(RW_WIDTH), vec(RW_WIDTH),
                  pl.BlockSpec((RW_WIDTH, RW_WIDTH), lambda b, i: (0, 0))],
        out_specs=pl.BlockSpec((None, RW_TILE, RW_WIDTH), lambda b, i: (b, i, 0)),
        out_shape=jax.ShapeDtypeStruct((B, T, RW_WIDTH), BF16),
        scratch_shapes=[pltpu.VMEM((RW_HEADS // RW_GROUP, HEAD_DIM, RW_GROUP * HEAD_DIM), F32),
                        pltpu.VMEM((RW_TILE, RW_WIDTH), F32)],
        compiler_params=pltpu.CompilerParams(dimension_semantics=("parallel", "arbitrary")),
        name="rwkv7_time_mix",
    )(p, p, row(mu), row(w0), w2.astype(BF16), row(a0), a2.astype(BF16), g2.astype(BF16),
      row(k_k), row(k_a), row(r_k), row(ln_g), row(ln_b), hsum)


def _sb_scores(qs, ks, masks, suffix_ones):
    n = range(len(qs))
    zs = [_dot_nt(qs[s], ks[s]) * LOG2_E for s in n]
    sps = [jnp.maximum(z, 0.0) + jnp.log2(1.0 + jnp.exp2(-jnp.abs(z))) for z in zs]
    log_nots = [-sp for sp in sps]
    log_nots = [ln if masks[s] is None else jnp.where(masks[s], ln, 0.0) for s, ln in enumerate(log_nots)]
    his = [ln.astype(BF16) for ln in log_nots]
    los = [(log_nots[s] - his[s].astype(F32)).astype(BF16) for s in n]
    sufs = [_dot(his[s], suffix_ones) + _dot(los[s], suffix_ones) for s in n]
    rowsums = [jnp.sum(ln, axis=-1, keepdims=True) for ln in log_nots]
    return zs, sps, sufs, rowsums


def _sb_weighted(scores, s, carry, mask, v):
    zs, sps, sufs, _ = scores
    att = jnp.exp2(zs[s] - sps[s] + (sufs[s] + carry))
    if mask is not None:
        att = jnp.where(mask, att, 0.0)
    return _dot(att.astype(BF16), v)


def _sb_kernel(q_ref, k_ref, v_ref, o_ref, acc_ref, carry_ref, *, T):
    tq = SB_TILE
    nq = T // tq
    ti = lax.broadcasted_iota(jnp.int32, (tq, tq), 0)
    tj = lax.broadcasted_iota(jnp.int32, (tq, tq), 1)
    suffix_ones = (ti > tj).astype(BF16)
    causal = tj < ti
    scale = HEAD_DIM ** -0.5
    heads = [slice(hh * HEAD_DIM, (hh + 1) * HEAD_DIM) for hh in range(LANES // HEAD_DIM)]
    nh = len(heads)

    def q_body(i, _):
        r0 = pl.multiple_of(i * tq, tq)
        has_prev = jnp.where(i > 0, 1.0, 0.0)
        p0 = pl.multiple_of(jnp.maximum(i - 1, 0) * tq, tq)
        qs = [q_ref[pl.ds(r0, tq), cs] * scale for cs in heads]
        ks = [k_ref[pl.ds(r0, tq), cs] for cs in heads] + [k_ref[pl.ds(p0, tq), cs] for cs in heads]
        vs = [v_ref[pl.ds(r0, tq), cs] for cs in heads] + [v_ref[pl.ds(p0, tq), cs] for cs in heads]
        masks = [causal] * nh + [None] * nh
        scores = _sb_scores(qs + qs, ks, masks, suffix_ones)
        rss = scores[3]
        live = jnp.float32(-jnp.inf)
        for hh, cs in enumerate(heads):
            out = _sb_weighted(scores, hh, 0.0, causal, vs[hh])
            out_prev = _sb_weighted(scores, nh + hh, rss[hh], None, vs[nh + hh])
            acc_ref[:, cs] = out + has_prev * out_prev
            carry = rss[hh] + has_prev * rss[nh + hh]
            carry_ref[hh] = carry
            live = jnp.maximum(live, jnp.max(carry))

        def k_cond(state):
            jr, live = state
            return (jr < i) & (live > SB_EXP2_UNDERFLOW)

        def k_body(state):
            jr, _ = state
            c0 = pl.multiple_of((i - 1 - jr) * tq, tq)
            kt = [k_ref[pl.ds(c0, tq), cs] for cs in heads]
            vt = [v_ref[pl.ds(c0, tq), cs] for cs in heads]
            scores = _sb_scores(qs, kt, [None] * nh, suffix_ones)
            live = jnp.float32(-jnp.inf)
            for hh, cs in enumerate(heads):
                carry = carry_ref[hh]
                acc_ref[:, cs] += _sb_weighted(scores, hh, carry, None, vt[hh])
                carry = carry + scores[3][hh]
                carry_ref[hh] = carry
                live = jnp.maximum(live, jnp.max(carry))
            return jr + 1, live

        lax.while_loop(k_cond, k_body, (jnp.int32(1), live))
        o_ref[pl.ds(r0, tq), :] = acc_ref[...].astype(o_ref.dtype)
        return 0

    lax.fori_loop(0, nq, q_body, 0)


def sb_call(qkv, B, T):
    npair = SB_WIDTH // LANES
    blk = lambda off: pl.BlockSpec((None, T, LANES), lambda b, hp: (b, 0, off + hp))
    return pl.pallas_call(
        functools.partial(_sb_kernel, T=T),
        grid=(B, npair),
        in_specs=[blk(0), blk(npair), blk(2 * npair)],
        out_specs=pl.BlockSpec((None, T, LANES), lambda b, hp: (b, 0, hp)),
        out_shape=jax.ShapeDtypeStruct((B, T, SB_WIDTH), BF16),
        scratch_shapes=[pltpu.VMEM((SB_TILE, LANES), F32),
                        pltpu.VMEM((LANES // HEAD_DIM, SB_TILE, 1), F32)],
        compiler_params=pltpu.CompilerParams(dimension_semantics=("parallel", "parallel")),
        name="stick_breaking_attention",
    )(qkv, qkv, qkv)


def _merge_kernel(ygm_ref, yrw_ref, ysb_ref, gate_ref, h_ref, pgm_ref, prw_ref, psb_ref, wo_ref,
                  g_ref, ho_ref, no_ref):
    D = D_MODEL
    merged = gate_ref[:, 0:D].astype(F32) * _dot(ygm_ref[...], pgm_ref[...])
    merged += gate_ref[:, D:2 * D].astype(F32) * _dot(yrw_ref[...], prw_ref[...])
    merged += gate_ref[:, 2 * D:3 * D].astype(F32) * _dot(ysb_ref[...], psb_ref[...])
    hn = h_ref[...] + _dot(merged.astype(BF16), wo_ref[...])
    ho_ref[...] = hn
    no_ref[...] = _rms(hn, g_ref[...]).astype(no_ref.dtype)


def _resident(shape):
    return pl.BlockSpec(shape, lambda i: (0,) * len(shape), pipeline_mode=pl.Buffered(1))


def merge_call(y_gm, y_rw, y_sb, gates, h, p_gm, p_rw, p_sb, w_o, g_next, tm=256):
    M, D = h.shape
    tm = min(tm, M)
    rowblk = lambda n: pl.BlockSpec((tm, n), lambda i: (i, 0))
    wbytes = 2 * (GM_WIDTH + RW_WIDTH + SB_WIDTH + D) * D
    need = wbytes + 2 * tm * (2 * (GM_WIDTH + RW_WIDTH + SB_WIDTH + GATE_COLS) + 4 * D + 4 * D + 2 * D) \
        + 6 * tm * D * 4
    return pl.pallas_call(
        _merge_kernel,
        grid=(M // tm,),
        in_specs=[rowblk(GM_WIDTH), rowblk(RW_WIDTH), rowblk(SB_WIDTH), rowblk(GATE_COLS), rowblk(D),
                  _resident((GM_WIDTH, D)), _resident((RW_WIDTH, D)), _resident((SB_WIDTH, D)),
                  _resident((D, D)), _resident((1, D))],
        out_specs=[rowblk(D), rowblk(D)],
        out_shape=[jax.ShapeDtypeStruct((M, D), F32), jax.ShapeDtypeStruct((M, D), BF16)],
        compiler_params=pltpu.CompilerParams(dimension_semantics=("parallel",),
                                             vmem_limit_bytes=_vmem_limit(need)),
        name="merge_out_proj",
    )(y_gm, y_rw, y_sb, gates, h, p_gm, p_rw, p_sb, w_o, g_next.reshape(1, D))


def _ffn_kernel(n_ref, h_ref, w1_ref, w3_ref, w2_ref, g_ref, *refs, emit_h):
    if emit_h:
        ho_ref, no_ref, acc_ref = refs
    else:
        no_ref, acc_ref = refs
    f = pl.program_id(1)

    @pl.when(f == 0)
    def _():
        acc_ref[...] = jnp.zeros_like(acc_ref)

    n = n_ref[...]
    act = (jax.nn.silu(_dot(n, w1_ref[...])) * _dot(n, w3_ref[...])).astype(BF16)
    acc_ref[...] += _dot(act, w2_ref[...])

    @pl.when(f == pl.num_programs(1) - 1)
    def _():
        hn = h_ref[...] + acc_ref[...]
        if emit_h:
            ho_ref[...] = hn
        no_ref[...] = _rms(hn, g_ref[...]).astype(no_ref.dtype)


def ffn_call(n, h, w1, w3, w2, g_next, n_dtype, emit_h, tm=512, tf=512):
    M, D = h.shape
    F = w1.shape[1]
    tm = min(tm, M)
    rowblk = pl.BlockSpec((tm, D), lambda i, f: (i, 0))
    out_specs = [rowblk]
    out_shape = [jax.ShapeDtypeStruct((M, D), n_dtype)]
    if emit_h:
        out_specs = [rowblk, rowblk]
        out_shape = [jax.ShapeDtypeStruct((M, D), F32)] + out_shape
    need = 2 * tm * D * (2 + 4 + 4 * len(out_shape)) + tm * D * 4 + 2 * 3 * D * tf * 2 + 4 * tm * tf * 4
    return pl.pallas_call(
        functools.partial(_ffn_kernel, emit_h=emit_h),
        grid=(M // tm, F // tf),
        in_specs=[rowblk, rowblk,
                  pl.BlockSpec((D, tf), lambda i, f: (0, f)),
                  pl.BlockSpec((D, tf), lambda i, f: (0, f)),
                  pl.BlockSpec((tf, D), lambda i, f: (f, 0)),
                  pl.BlockSpec((1, D), lambda i, f: (0, 0))],
        out_specs=out_specs,
        out_shape=out_shape,
        scratch_shapes=[pltpu.VMEM((tm, D), F32)],
        compiler_params=pltpu.CompilerParams(dimension_semantics=("parallel", "arbitrary"),
                                             vmem_limit_bytes=_vmem_limit(need + (8 << 20))),
        name="swiglu_ffn",
    )(n, h, w1, w3, w2, g_next.reshape(1, D))


def _router_kernel(h_ref, g_ref, wh_ref, wl_ref, b_ref, n_ref, id_ref, p_ref):
    n = _rms(h_ref[...], g_ref[...])
    n_ref[...] = n
    nh = n.astype(BF16)
    nl = (n - nh.astype(F32)).astype(BF16)
    logits = _dot(nh, wh_ref[...]) + _dot(nl, wh_ref[...]) + _dot(nh, wl_ref[...]) + b_ref[...]
    lane = lax.broadcasted_iota(jnp.int32, logits.shape, 1)
    neg = jnp.float32(-jnp.inf)
    logits = jnp.where(lane < N_EXPERTS, logits, neg)
    m1 = jnp.max(logits, axis=-1, keepdims=True)
    i1 = jnp.min(jnp.where(logits == m1, lane, ROUTER_LANES), axis=-1, keepdims=True)
    rest = jnp.where(lane == i1, neg, logits)
    m2 = jnp.max(rest, axis=-1, keepdims=True)
    i2 = jnp.min(jnp.where(rest == m2, lane, ROUTER_LANES), axis=-1, keepdims=True)
    e2 = jnp.exp(m2 - m1)
    p1 = 1.0 / (1.0 + e2)
    p2 = e2 / (1.0 + e2)
    id_ref[...] = jnp.where(lane == 0, i1, jnp.where(lane == 1, i2, 0))
    p_ref[...] = jnp.where(lane == 0, p1, jnp.where(lane == 1, p2, 0.0))


def router_call(h, g, router_w, router_b, tm=512):
    M, D = h.shape
    tm = min(tm, M)
    wpad = jnp.zeros((D, ROUTER_LANES), F32).at[:, :N_EXPERTS].set(router_w)
    w_hi = wpad.astype(BF16)
    w_lo = (wpad - w_hi.astype(F32)).astype(BF16)
    bpad = jnp.zeros((1, ROUTER_LANES), F32).at[0, :N_EXPERTS].set(router_b)
    const = lambda shape: pl.BlockSpec(shape, lambda i: (0, 0))
    return pl.pallas_call(
        _router_kernel,
        grid=(M // tm,),
        in_specs=[pl.BlockSpec((tm, D), lambda i: (i, 0)), const((1, D)),
                  const((D, ROUTER_LANES)), const((D, ROUTER_LANES)), const((1, ROUTER_LANES))],
        out_specs=[pl.BlockSpec((tm, D), lambda i: (i, 0)),
                   pl.BlockSpec((tm, ROUTER_LANES), lambda i: (i, 0)),
                   pl.BlockSpec((tm, ROUTER_LANES), lambda i: (i, 0))],
        out_shape=[jax.ShapeDtypeStruct((M, D), F32),
                   jax.ShapeDtypeStruct((M, ROUTER_LANES), jnp.int32),
                   jax.ShapeDtypeStruct((M, ROUTER_LANES), F32)],
        compiler_params=pltpu.CompilerParams(dimension_semantics=("parallel",)),
        name="moe_router",
    )(h, g.reshape(1, D), w_hi, w_lo, bpad)


def moe_plan(ids, M, tm):
    E = N_EXPERTS
    n_tiles = -(-(TOP_K * M) // tm) + E + 1
    e_flat = ids[:, :TOP_K].reshape(-1)
    onehot = (e_flat[:, None] == jnp.arange(E, dtype=jnp.int32)[None, :]).astype(jnp.int32)
    csum = jnp.cumsum(onehot, axis=0)
    rank = jnp.sum((csum - onehot) * onehot, axis=1)
    counts = csum[-1]
    tiles_per_e = (counts + tm - 1) // tm
    tile_end = jnp.cumsum(tiles_per_e)
    offs = (tile_end - tiles_per_e) * tm
    dest = offs[e_flat] + rank
    src_token = jnp.zeros((n_tiles * tm,), jnp.int32).at[dest].set(
        jnp.arange(TOP_K * M, dtype=jnp.int32) // TOP_K, unique_indices=True)
    n_used = tile_end[-1]
    t = jnp.arange(n_tiles, dtype=jnp.int32)
    tile_e = jnp.searchsorted(tile_end, jnp.minimum(t, n_used - 1), side="right").astype(jnp.int32)
    return (src_token.reshape(n_tiles, 1, tm), tile_e, n_used.astype(jnp.int32).reshape(1),
            dest.reshape(M // MOE_COMBINE_TM, 1, TOP_K * MOE_COMBINE_TM))


def _moe_kernel(tile_e_ref, n_used_ref, src_ref, nxt_ref, x_hbm, w1_ref, w3_ref, w2_ref, o_ref,
                xbuf, xb, acc, sem):
    t = pl.program_id(0)
    f = pl.program_id(1)
    nf = pl.num_programs(1)
    tm = xb.shape[0]
    per_step = tm // nf
    n_used = n_used_ref[0]
    used = t < n_used
    cur = t % 2

    def row_copy(idx_ref, r, slot):
        return pltpu.make_async_copy(x_hbm.at[pl.ds(idx_ref[0, r], 1)], xbuf.at[slot, pl.ds(r, 1)],
                                     sem.at[slot])

    @pl.when((f == 0) & (t == 0))
    def _():
        def issue(r, c):
            row_copy(src_ref, r, 0).start()
            return c
        lax.fori_loop(0, tm, issue, 0)

    @pl.when((f == 0) & (t <= n_used))
    def _():
        pltpu.make_async_copy(x_hbm.at[pl.ds(0, tm)], xbuf.at[cur], sem.at[cur]).wait()

    @pl.when((f == 0) & used)
    def _():
        xb[...] = xbuf[cur].astype(BF16)
        acc[...] = jnp.zeros_like(acc)

    @pl.when(used)
    def _():
        for j in range(per_step):
            row_copy(nxt_ref, f * per_step + j, 1 - cur).start()
        x = xb[...]
        act = jax.nn.silu(_dot(x, w1_ref[...])) * _dot(x, w3_ref[...])
        acc[...] += _dot(act.astype(BF16), w2_ref[...])

    @pl.when(f == nf - 1)
    def _():
        o_ref[...] = jnp.where(used, acc[...], 0.0)


def moe_call(x, src_token, tile_e, n_used, w1, w3, w2, tm, tf):
    M, D = x.shape
    n_tiles = src_token.shape[0]
    E, _, Fe = w1.shape
    nf = Fe // tf
    assert tm % nf == 0 and Fe % tf == 0

    def wmap(t, f, tile_e_ref, n_used_ref):
        ff = jnp.where(t < n_used_ref[0], f, nf - 1)
        return tile_e_ref[t], ff

    grid_spec = pltpu.PrefetchScalarGridSpec(
        num_scalar_prefetch=2,
        grid=(n_tiles, nf),
        in_specs=[pl.BlockSpec((None, 1, tm), lambda t, f, te, nu: (t, 0, 0), memory_space=pltpu.SMEM),
                  pl.BlockSpec((None, 1, tm), lambda t, f, te, nu: (jnp.minimum(t + 1, n_tiles - 1), 0, 0),
                               memory_space=pltpu.SMEM),
                  pl.BlockSpec(memory_space=pl.ANY),
                  pl.BlockSpec((None, D, tf), lambda t, f, te, nu: (wmap(t, f, te, nu)[0], 0,
                                                                     wmap(t, f, te, nu)[1])),
                  pl.BlockSpec((None, D, tf), lambda t, f, te, nu: (wmap(t, f, te, nu)[0], 0,
                                                                     wmap(t, f, te, nu)[1])),
                  pl.BlockSpec((None, tf, D), lambda t, f, te, nu: wmap(t, f, te, nu) + (0,))],
        out_specs=pl.BlockSpec((tm, D), lambda t, f, te, nu: (t, 0)),
        scratch_shapes=[pltpu.VMEM((2, tm, D), F32), pltpu.VMEM((tm, D), BF16), pltpu.VMEM((tm, D), F32),
                        pltpu.SemaphoreType.DMA((2,))])
    need = tm * D * (8 + 2 + 4) + 2 * tm * D * 4 + 2 * 3 * D * tf * 2 + 4 * tm * tf * 4
    return pl.pallas_call(
        _moe_kernel,
        grid_spec=grid_spec,
        out_shape=jax.ShapeDtypeStruct((n_tiles * tm, D), F32),
        compiler_params=pltpu.CompilerParams(dimension_semantics=("arbitrary", "arbitrary"),
                                             vmem_limit_bytes=_vmem_limit(need + (8 << 20))),
        name="moe_experts",
    )(tile_e, n_used, src_token, src_token, x, w1, w3, w2)


def _moe_combine_kernel(slot_ref, h_ref, p_ref, g_ref, y_hbm, *refs, emit_h):
    if emit_h:
        ho_ref, no_ref, ybuf, sem = refs
    else:
        no_ref, ybuf, sem = refs
    tm = h_ref.shape[0]

    def row_copy(r, k):
        return pltpu.make_async_copy(y_hbm.at[pl.ds(slot_ref[0, TOP_K * r + k], 1)],
                                     ybuf.at[pl.ds(k * tm + r, 1)], sem)

    def issue(r, c):
        for k in range(TOP_K):
            row_copy(r, k).start()
        return c

    lax.fori_loop(0, tm, issue, 0)
    pltpu.make_async_copy(y_hbm.at[pl.ds(0, TOP_K * tm)], ybuf, sem).wait()
    hn = h_ref[...]
    for k in range(TOP_K):
        hn = hn + p_ref[:, k:k + 1] * ybuf[k * tm:(k + 1) * tm]
    if emit_h:
        ho_ref[...] = hn
    no_ref[...] = _rms(hn, g_ref[...]).astype(no_ref.dtype)


def moe_combine_call(h, y_slots, slots, probs, g_next, n_dtype, emit_h):
    M, D = h.shape
    tm = MOE_COMBINE_TM
    rowblk = pl.BlockSpec((tm, D), lambda i: (i, 0))
    out_specs = [rowblk]
    out_shape = [jax.ShapeDtypeStruct((M, D), n_dtype)]
    if emit_h:
        out_specs = [rowblk, rowblk]
        out_shape = [jax.ShapeDtypeStruct((M, D), F32)] + out_shape
    return pl.pallas_call(
        functools.partial(_moe_combine_kernel, emit_h=emit_h),
        grid=(M // tm,),
        in_specs=[pl.BlockSpec((None, 1, TOP_K * tm), lambda i: (i, 0, 0), memory_space=pltpu.SMEM),
                  rowblk, pl.BlockSpec((tm, ROUTER_LANES), lambda i: (i, 0)),
                  pl.BlockSpec((1, D), lambda i: (0, 0)),
                  pl.BlockSpec(memory_space=pl.ANY)],
        out_specs=out_specs,
        out_shape=out_shape,
        scratch_shapes=[pltpu.VMEM((TOP_K * tm, D), F32), pltpu.SemaphoreType.DMA(())],
        compiler_params=pltpu.CompilerParams(dimension_semantics=("arbitrary",),
                                             vmem_limit_bytes=_vmem_limit(12 * tm * D * 4)),
        name="moe_combine",
    )(slots, h, probs, g_next.reshape(1, D), y_slots)


def kernel(x, norm_mix, w_in, gate_b, gm_ln_g, gm_ln_b, gm_ws, gm_bs, rw_mu, rw_w0, rw_w2, rw_a0, rw_a2,
           rw_g2, rw_kk, rw_ka, rw_rk, rw_ln_g, rw_ln_b, p_gm, p_rw, p_sb, w_o, norm_ffn, ffn_w1, ffn_w3,
           ffn_w2, router_w, router_b, moe_w1, moe_w3, moe_w2, norm_out):
    B, T, D = x.shape
    M = B * T
    bf = lambda a: a.astype(BF16)
    h = x.reshape(M, D)
    n = rmsnorm_call(h, norm_mix[0], BF16)
    for l in range(DEPTH):
        last = l == DEPTH - 1
        g_next = norm_out if last else norm_mix[l + 1]
        n_dtype = x.dtype if last else BF16
        w = w_in[l]
        uv = matmul_call(n, bf(w[:, GM_OFF:RW_OFF]), BF16, tn=512, name="in_proj_gm")
        p = matmul_call(n, bf(w[:, RW_OFF:SB_OFF]), BF16, tn=256, name="in_proj_rw")
        qkv = matmul_call(n, bf(w[:, SB_OFF:GATE_OFF]), BF16, tn=512, name="in_proj_sb")
        gates = matmul_call(n, bf(w[:, GATE_OFF:]), BF16, tn=512, bias=gate_b[l].reshape(-1),
                            name="in_proj_gate")
        y_gm = gm_call(uv, gm_ln_g[l], gm_ln_b[l], gm_ws[l], gm_bs[l])
        y_rw = rw_call(p.reshape(B, T, RW_COLS), B, T, rw_mu[l], rw_w0[l], rw_w2[l], rw_a0[l], rw_a2[l],
                       rw_g2[l], rw_kk[l], rw_ka[l], rw_rk[l], rw_ln_g[l], rw_ln_b[l]).reshape(M, RW_WIDTH)
        y_sb = sb_call(qkv.reshape(B, T, SB_COLS), B, T).reshape(M, SB_WIDTH)
        h, n2 = merge_call(y_gm, y_rw, y_sb, gates, h, bf(p_gm[l]), bf(p_rw[l]), bf(p_sb[l]), bf(w_o[l]),
                           norm_ffn[l])
        j = l // 2
        if l % 2 == 0:
            outs = ffn_call(n2, h, bf(ffn_w1[j]), bf(ffn_w3[j]), bf(ffn_w2[j]), g_next, n_dtype,
                            emit_h=not last)
        else:
            nf32, ids, probs = router_call(h, norm_ffn[l], router_w[j], router_b[j])
            src_token, tile_e, n_used, slots = moe_plan(ids, M, MOE_TM)
            y_slots = moe_call(nf32, src_token, tile_e, n_used, bf(moe_w1[j]), bf(moe_w3[j]),
                               bf(moe_w2[j]), MOE_TM, MOE_TF)
            outs = moe_combine_call(h, y_slots, slots, probs, g_next, n_dtype, emit_h=not last)
        if last:
            n = outs[0]
        else:
            h, n = outs
    return n.reshape(B, T, D)
```

```python
import functools

import jax
import jax.numpy as jnp
from jax import lax
from jax.experimental import pallas as pl
from jax.experimental.pallas import tpu as pltpu

F32 = jnp.float32
BF16 = jnp.bfloat16

D_MODEL = 2048
DEPTH = 2
HEAD_DIM = 64
GM_HEADS = 8
GM_WIDTH = GM_HEADS * HEAD_DIM
CHUNK = 128
RW_HEADS = 8
RW_WIDTH = RW_HEADS * HEAD_DIM
DECAY_LORA = 64
AAA_LORA = 64
GATE_LORA = 128
SB_HEADS = 16
SB_WIDTH = SB_HEADS * HEAD_DIM
N_BRANCH = 3
D_FF = 5632
N_EXPERTS = 8
TOP_K = 2
D_FF_EXPERT = D_FF // TOP_K
RMS_EPS = 1e-6
LN_EPS = 1e-5
RW_GN_EPS = 64e-5

GM_COLS = 2 * GM_WIDTH
RW_COLS = 3 * RW_WIDTH + DECAY_LORA + AAA_LORA + GATE_LORA
SB_COLS = 3 * SB_WIDTH
GATE_COLS = N_BRANCH * D_MODEL
GM_OFF = 0
RW_OFF = GM_OFF + GM_COLS
SB_OFF = RW_OFF + RW_COLS
GATE_OFF = SB_OFF + SB_COLS
N_IN = GATE_OFF + GATE_COLS

V7X_VMEM_BYTES = 64 * 1024 * 1024
LANES = 128
BF16_SUBLANES = 16
RW_CHUNK = 64
RW_TILE = 256
RW_GROUP = 4
SB_TILE = 256
LOG2_E = 1.4426950408889634
SB_EXP2_UNDERFLOW = -151.0
ROUTER_LANES = 128
MOE_TF = 256
MOE_TM = 528
MOE_COMBINE_TM = 256


def _vmem_limit(nbytes):
    return int(min(max(nbytes, 32 * 1024 * 1024), V7X_VMEM_BYTES - 6 * 1024 * 1024))


def _dot(a, b):
    return jnp.dot(a, b, preferred_element_type=F32)


def _dot_nt(a, b):
    return lax.dot_general(a, b, (((1,), (1,)), ((), ())), preferred_element_type=F32)


def _dot_tn(a, b):
    return lax.dot_general(a, b, (((0,), (0,)), ((), ())), preferred_element_type=F32)


def _dot_split(x, ones_bf16):
    hi = x.astype(BF16)
    lo = (x - hi.astype(F32)).astype(BF16)
    return _dot(hi, ones_bf16) + _dot(lo, ones_bf16)


def _rms(x, g):
    ms = jnp.mean(x * x, axis=-1, keepdims=True)
    return x * lax.rsqrt(ms + RMS_EPS) * g


def _rmsnorm_kernel(x_ref, g_ref, o_ref):
    o_ref[...] = _rms(x_ref[...], g_ref[...]).astype(o_ref.dtype)


def rmsnorm_call(x, g, out_dtype, tm=512):
    M, D = x.shape
    tm = min(tm, M)
    return pl.pallas_call(
        _rmsnorm_kernel,
        grid=(M // tm,),
        in_specs=[pl.BlockSpec((tm, D), lambda i: (i, 0)),
                  pl.BlockSpec((1, D), lambda i: (0, 0))],
        out_specs=pl.BlockSpec((tm, D), lambda i: (i, 0)),
        out_shape=jax.ShapeDtypeStruct((M, D), out_dtype),
        compiler_params=pltpu.CompilerParams(dimension_semantics=("parallel",)),
        name="rmsnorm",
    )(x, g.reshape(1, D))


def _mm_kernel(a_ref, w_ref, o_ref):
    o_ref[...] = _dot(a_ref[...], w_ref[...]).astype(o_ref.dtype)


def _mm_gate_kernel(a_ref, w_ref, b_ref, o_ref):
    o_ref[...] = jax.nn.sigmoid(_dot(a_ref[...], w_ref[...]) + b_ref[...]).astype(o_ref.dtype)


def matmul_call(a, w, out_dtype, tn, bias=None, tm=1024, name="matmul"):
    M, K = a.shape
    N = w.shape[1]
    tm = min(tm, M)
    in_specs = [pl.BlockSpec((tm, K), lambda i, j: (i, 0)),
                pl.BlockSpec((K, tn), lambda i, j: (0, j))]
    args = [a, w]
    kern = _mm_kernel
    if bias is not None:
        in_specs.append(pl.BlockSpec((1, tn), lambda i, j: (0, j)))
        args.append(bias.reshape(1, N))
        kern = _mm_gate_kernel
    osz = jnp.dtype(out_dtype).itemsize
    need = 2 * (tm * K * 2 + K * tn * 2 + tm * tn * osz) + 2 * tm * tn * 4
    return pl.pallas_call(
        kern,
        grid=(M // tm, N // tn),
        in_specs=in_specs,
        out_specs=pl.BlockSpec((tm, tn), lambda i, j: (i, j)),
        out_shape=jax.ShapeDtypeStruct((M, N), out_dtype),
        compiler_params=pltpu.CompilerParams(
            dimension_semantics=("parallel", "arbitrary"),
            vmem_limit_bytes=_vmem_limit(need + (8 << 20))),
        name=name,
    )(*args)


def _gm_kernel(uv_ref, lng_ref, lnb_ref, ws_ref, bias_ref, o_ref, *, n_chunks):
    ii = lax.broadcasted_iota(jnp.int32, (CHUNK, CHUNK), 0)
    jj = lax.broadcasted_iota(jnp.int32, (CHUNK, CHUNK), 1)
    causal = jj <= ii
    ws = [jnp.where(causal, ws_ref[h], 0.0).astype(BF16) for h in range(GM_HEADS)]
    for c in range(n_chunks):
        rows = slice(c * CHUNK, (c + 1) * CHUNK)
        u = jax.nn.gelu(uv_ref[rows, 0:GM_WIDTH].astype(F32))
        v = jax.nn.gelu(uv_ref[rows, GM_WIDTH:2 * GM_WIDTH].astype(F32))
        mu = jnp.mean(v, axis=-1, keepdims=True)
        vc = v - mu
        var = jnp.mean(vc * vc, axis=-1, keepdims=True)
        vn = (vc * lax.rsqrt(var + LN_EPS) * lng_ref[...] + lnb_ref[...]).astype(BF16)
        mixed = jnp.concatenate(
            [_dot(ws[h], vn[:, h * HEAD_DIM:(h + 1) * HEAD_DIM]) for h in range(GM_HEADS)], axis=-1)
        o_ref[rows, :] = (u * (mixed + bias_ref[...])).astype(o_ref.dtype)


def gm_call(uv, ln_g, ln_b, w_s, b_s, n_chunks=2):
    M = uv.shape[0]
    tt = CHUNK * n_chunks
    bias = jnp.repeat(b_s.T, HEAD_DIM, axis=1)
    return pl.pallas_call(
        functools.partial(_gm_kernel, n_chunks=n_chunks),
        grid=(M // tt,),
        in_specs=[pl.BlockSpec((tt, GM_COLS), lambda i: (i, 0)),
                  pl.BlockSpec((1, GM_WIDTH), lambda i: (0, 0)),
                  pl.BlockSpec((1, GM_WIDTH), lambda i: (0, 0)),
                  pl.BlockSpec((GM_HEADS, CHUNK, CHUNK), lambda i: (0, 0, 0)),
                  pl.BlockSpec((CHUNK, GM_WIDTH), lambda i: (0, 0))],
        out_specs=pl.BlockSpec((tt, GM_WIDTH), lambda i: (i, 0)),
        out_shape=jax.ShapeDtypeStruct((M, GM_WIDTH), BF16),
        compiler_params=pltpu.CompilerParams(dimension_semantics=("parallel",)),
        name="gmlp_gating",
    )(uv, ln_g.reshape(1, -1), ln_b.reshape(1, -1), w_s, bias)


def _rw_kernel(p_ref, prev_ref, mu_ref, w0_ref, w2_ref, a0_ref, a2_ref, g2_ref, kk_ref, ka_ref,
               rk_ref, lng_ref, lnb_ref, hsum_ref, o_ref, st_ref, y_ref):
    C = RW_CHUNK
    i = pl.program_id(1)

    @pl.when(i == 0)
    def _():
        st_ref[...] = jnp.zeros_like(st_ref)

    hsum = hsum_ref[...]
    gw = hsum.shape[0]

    def head_sums(x):
        return jnp.concatenate([_dot_split(x[:, q * gw:(q + 1) * gw], hsum)
                                for q in range(x.shape[1] // gw)], axis=1)

    p = p_ref[...].astype(F32)
    prev_row = prev_ref[BF16_SUBLANES - 1:BF16_SUBLANES, :].astype(F32)
    prev_row = jnp.where(i == 0, 0.0, prev_row)
    row = lax.broadcasted_iota(jnp.int32, p.shape, 0)
    p_prev = jnp.where(row == 0, prev_row, pltpu.roll(p, 1, axis=0))
    ps = p + (p_prev - p) * mu_ref[...]
    W = RW_WIDTH
    r = ps[:, 0:W]
    k = ps[:, W:2 * W]
    v = ps[:, 2 * W:3 * W]
    wl = ps[:, 3 * W:3 * W + DECAY_LORA]
    al = ps[:, 3 * W + DECAY_LORA:3 * W + DECAY_LORA + AAA_LORA]
    gl = ps[:, 3 * W + DECAY_LORA + AAA_LORA:RW_COLS]

    d = w0_ref[...] + _dot(jnp.tanh(wl).astype(BF16), w2_ref[...])
    log_w = -jnp.exp(-jax.nn.softplus(-d) - 0.5)
    a = jax.nn.sigmoid(a0_ref[...] + _dot(al.astype(BF16), a2_ref[...]))
    g = _dot(jax.nn.sigmoid(gl).astype(BF16), g2_ref[...])
    kk = k * kk_ref[...]
    kk = kk / jnp.maximum(jnp.sqrt(head_sums(kk * kk)), 1e-12)
    k = k * (1.0 + (a - 1.0) * ka_ref[...])
    bonus = head_sums(r * k * rk_ref[...]) * v

    HD = HEAD_DIM
    GW = RW_GROUP * HD
    ci = lax.broadcasted_iota(jnp.int32, (C, C), 0)
    cj = lax.broadcasted_iota(jnp.int32, (C, C), 1)
    tri_incl = (cj <= ci).astype(BF16)
    ti = lax.broadcasted_iota(jnp.int32, (C, GW), 0)
    tj = lax.broadcasted_iota(jnp.int32, (C, GW), 1) % HD
    strict = tj < ti
    incl = tj <= ti
    eye = ti == tj
    blk_masks = []
    s = 2
    while s < C:
        blk_masks.append((ti // (2 * s) == tj // (2 * s)) & (ti // s != tj // s))
        s *= 2
    pair_mask = (ti // 2 == tj // 2) & strict
    gi = lax.broadcasted_iota(jnp.int32, (GW, GW), 0)
    gj = lax.broadcasted_iota(jnp.int32, (GW, GW), 1)
    bd_mask = gi // HD == gj // HD
    bd_mask_b = jnp.where(bd_mask, 1.0, 0.0).astype(BF16)
    gw_eye = gi == gj

    def bd(x):
        xb = x.astype(BF16)
        return jnp.where(bd_mask_b > 0, jnp.concatenate([xb] * RW_GROUP, axis=0), 0)

    def bd_of_full(full):
        return jnp.where(bd_mask, full, 0.0)

    n_chunks = RW_TILE // C
    n_groups = RW_WIDTH // GW
    systems = [(c, q) for c in range(n_chunks) for q in range(n_groups)]

    pre = []
    for c in range(n_chunks):
        rows = slice(c * C, (c + 1) * C)
        lw = log_w[rows]
        cum = _cumsum_rows(lw, tri_incl)
        gam = jnp.exp(cum)
        gam_prev = jnp.exp(cum - lw)
        gam_inv = jnp.exp(-cum)
        g_end = gam[C - 1:C, :]
        bh_f = kk[rows] * a[rows] * gam_inv
        kh_f = k[rows] * gam_inv
        pre.append(dict(
            rh=(r[rows] * gam).astype(BF16), ah=(-kk[rows] * gam_prev).astype(BF16),
            bh=bh_f.astype(BF16), kh=kh_f.astype(BF16),
            bt=(bh_f * g_end).astype(BF16), kt=(kh_f * g_end).astype(BF16),
            v=v[rows].astype(BF16), g_end=g_end))

    def grp(c, q, name):
        return pre[c][name][:, q * GW:(q + 1) * GW]

    st = {}
    for (c, q) in systems:
        ar = jnp.concatenate([grp(c, q, "ah"), grp(c, q, "rh")], axis=0)
        pb = _dot_nt(ar, bd(grp(c, q, "bh")))
        pk = _dot_nt(ar, bd(grp(c, q, "kh")))
        lab = jnp.where(strict, pb[0:C], 0.0)
        st[(c, q)] = dict(
            lab=lab, mrb=jnp.where(incl, pb[C:2 * C], 0.0).astype(BF16),
            lm=jnp.concatenate([jnp.where(strict, pk[0:C], 0.0), jnp.where(incl, pk[C:2 * C], 0.0)],
                               axis=0).astype(BF16),
            tinv=jnp.where(eye, 1.0, jnp.where(pair_mask, lab, 0.0)))

    for bm in blk_masks:
        xs = {}
        for key in systems:
            d_ = st[key]
            xs[key] = _dot(d_["tinv"].astype(BF16), bd(jnp.where(bm, d_["lab"], 0.0))).astype(BF16)
        for key in systems:
            d_ = st[key]
            d_["tinv"] = d_["tinv"] + _dot(xs[key], bd(d_["tinv"]))

    for (c, q) in systems:
        d_ = st[(c, q)]
        lv = _dot(d_["lm"], bd(grp(c, q, "v")))
        tb = d_["tinv"].astype(BF16)
        ap = _dot(tb, bd(grp(c, q, "ah"))).astype(BF16)
        up = _dot(tb, bd(lv[0:C])).astype(BF16)
        d_["rp"] = (grp(c, q, "rh").astype(F32) + _dot(d_["mrb"], bd(ap))).astype(BF16)
        d_["ypp"] = _dot(d_["mrb"], bd(up)) + lv[C:2 * C]
        g_end = pre[c]["g_end"][:, q * GW:(q + 1) * GW]
        gt = _dot_tn(ap, grp(c, q, "bt"))
        d_["gt"] = jnp.where(gw_eye, gt + g_end, bd_of_full(gt)).astype(BF16)
        ht = bd_of_full(_dot_tn(up, grp(c, q, "bt")) + _dot_tn(grp(c, q, "v"), grp(c, q, "kt")))
        d_["ht"] = sum(ht[hh * HD:(hh + 1) * HD] for hh in range(1, RW_GROUP)) + ht[0:HD]

    for q in range(n_groups):
        s_cur = st_ref[q]
        for c in range(n_chunks):
            d_ = st[(c, q)]
            y_ref[c * C:(c + 1) * C, q * GW:(q + 1) * GW] = _dot_nt(d_["rp"], bd(s_cur)) + d_["ypp"]
            s_cur = _dot(s_cur.astype(BF16), d_["gt"]) + d_["ht"]
        st_ref[q] = s_cur

    y = y_ref[...]
    inv_n = 1.0 / HEAD_DIM
    m = head_sums(y) * inv_n
    yc = y - m
    var = head_sums(yc * yc) * inv_n
    yn = yc * lax.rsqrt(var + RW_GN_EPS) * lng_ref[...] + lnb_ref[...]
    o_ref[...] = ((yn + bonus) * g).astype(o_ref.dtype)


def _cumsum_rows(x, tri_incl_bf16):
    hi = x.astype(BF16)
    lo = (x - hi.astype(F32)).astype(BF16)
    return _dot(tri_incl_bf16, hi) + _dot(tri_incl_bf16, lo)


def rw_call(p, B, T, mu, w0, w2, a0, a2, g2, k_k, k_a, r_k, ln_g, ln_b):
    nt = T // RW_TILE
    sub = RW_TILE // BF16_SUBLANES
    hid = jnp.arange(RW_GROUP * HEAD_DIM) // HEAD_DIM
    hsum = (hid[:, None] == hid[None, :]).astype(BF16)
    row = lambda a: a.reshape(1, -1).astype(F32)
    vec = lambda n: pl.BlockSpec((1, n), lambda b, i: (0, 0))
    return pl.pallas_call(
        _rw_kernel,
        grid=(B, nt),
        in_specs=[pl.BlockSpec((None, RW_TILE, RW_COLS), lambda b, i: (b, i, 0)),
                  pl.BlockSpec((None, BF16_SUBLANES, RW_COLS),
                               lambda b, i: (b, jnp.maximum(i * sub - 1, 0), 0)),
                  vec(RW_COLS), vec(RW_WIDTH),
                  pl.BlockSpec((DECAY_LORA, RW_WIDTH), lambda b, i: (0, 0)),
                  vec(RW_WIDTH),
                  pl.BlockSpec((AAA_LORA, RW_WIDTH), lambda b, i: (0, 0)),
                  pl.BlockSpec((GATE_LORA, RW_WIDTH), lambda b, i: (0, 0)),
                  vec(RW_WIDTH), vec(RW_WIDTH), vec(RW_WIDTH), vec(RW_WIDTH), vec(RW_WIDTH),
                  pl.BlockSpec((RW_GROUP * HEAD_DIM, RW_GROUP * HEAD_DIM), lambda b, i: (0, 0))],
        out_specs=pl.BlockSpec((None, RW_TILE, RW_WIDTH), lambda b, i: (b, i, 0)),
        out_shape=jax.ShapeDtypeStruct((B, T, RW_WIDTH), BF16),
        scratch_shapes=[pltpu.VMEM((RW_HEADS // RW_GROUP, HEAD_DIM, RW_GROUP * HEAD_DIM), F32),
                        pltpu.VMEM((RW_TILE, RW_WIDTH), F32)],
        compiler_params=pltpu.CompilerParams(dimension_semantics=("parallel", "arbitrary")),
        name="rwkv7_time_mix",
    )(p, p, row(mu), row(w0), w2.astype(BF16), row(a0), a2.astype(BF16), g2.astype(BF16),
      row(k_k), row(k_a), row(r_k), row(ln_g), row(ln_b), hsum)


def _sb_scores(qs, ks, masks, suffix_ones):
    n = range(len(qs))
    zs = [_dot_nt(qs[s], ks[s]) * LOG2_E for s in n]
    sps = [jnp.maximum(z, 0.0) + jnp.log2(1.0 + jnp.exp2(-jnp.abs(z))) for z in zs]
    log_nots = [-sp for sp in sps]
    log_nots = [ln if masks[s] is None else jnp.where(masks[s], ln, 0.0) for s, ln in enumerate(log_nots)]
    his = [ln.astype(BF16) for ln in log_nots]
    los = [(log_nots[s] - his[s].astype(F32)).astype(BF16) for s in n]
    sufs = [_dot(his[s], suffix_ones) + _dot(los[s], suffix_ones) for s in n]
    rowsums = [jnp.sum(ln, axis=-1, keepdims=True) for ln in log_nots]
    return zs, sps, sufs, rowsums


def _sb_weighted(scores, s, carry, mask, v):
    zs, sps, sufs, _ = scores
    att = jnp.exp2(zs[s] - sps[s] + (sufs[s] + carry))
    if mask is not None:
        att = jnp.where(mask, att, 0.0)
    return _dot(att.astype(BF16), v)


def _sb_kernel(q_ref, k_ref, v_ref, o_ref, acc_ref, carry_ref, *, T):
    tq = SB_TILE
    nq = T // tq
    ti = lax.broadcasted_iota(jnp.int32, (tq, tq), 0)
    tj = lax.broadcasted_iota(jnp.int32, (tq, tq), 1)
    suffix_ones = (ti > tj).astype(BF16)
    causal = tj < ti
    scale = HEAD_DIM ** -0.5
    heads = [slice(hh * HEAD_DIM, (hh + 1) * HEAD_DIM) for hh in range(LANES // HEAD_DIM)]
    nh = len(heads)

    def q_body(i, _):
        r0 = pl.multiple_of(i * tq, tq)
        has_prev = jnp.where(i > 0, 1.0, 0.0)
        p0 = pl.multiple_of(jnp.maximum(i - 1, 0) * tq, tq)
        qs = [q_ref[pl.ds(r0, tq), cs] * scale for cs in heads]
        ks = [k_ref[pl.ds(r0, tq), cs] for cs in heads] + [k_ref[pl.ds(p0, tq), cs] for cs in heads]
        vs = [v_ref[pl.ds(r0, tq), cs] for cs in heads] + [v_ref[pl.ds(p0, tq), cs] for cs in heads]
        masks = [causal] * nh + [None] * nh
        scores = _sb_scores(qs + qs, ks, masks, suffix_ones)
        rss = scores[3]
        live = jnp.float32(-jnp.inf)
        for hh, cs in enumerate(heads):
            out = _sb_weighted(scores, hh, 0.0, causal, vs[hh])
            out_prev = _sb_weighted(scores, nh + hh, rss[hh], None, vs[nh + hh])
            acc_ref[:, cs] = out + has_prev * out_prev
            carry = rss[hh] + has_prev * rss[nh + hh]
            carry_ref[hh] = carry
            live = jnp.maximum(live, jnp.max(carry))

        def k_cond(state):
            jr, live = state
            return (jr < i) & (live > SB_EXP2_UNDERFLOW)

        def k_body(state):
            jr, _ = state
            c0 = pl.multiple_of((i - 1 - jr) * tq, tq)
            kt = [k_ref[pl.ds(c0, tq), cs] for cs in heads]
            vt = [v_ref[pl.ds(c0, tq), cs] for cs in heads]
            scores = _sb_scores(qs, kt, [None] * nh, suffix_ones)
            live = jnp.float32(-jnp.inf)
            for hh, cs in enumerate(heads):
                carry = carry_ref[hh]
                acc_ref[:, cs] += _sb_weighted(scores, hh, carry, None, vt[hh])
                carry = carry + scores[3][hh]
                carry_ref[hh] = carry
                live = jnp.maximum(live, jnp.max(carry))
            return jr + 1, live

        lax.while_loop(k_cond, k_body, (jnp.int32(1), live))
        o_ref[pl.ds(r0, tq), :] = acc_ref[...].astype(o_ref.dtype)
        return 0

    lax.fori_loop(0, nq, q_body, 0)


def sb_call(qkv, B, T):
    npair = SB_WIDTH // LANES
    blk = lambda off: pl.BlockSpec((None, T, LANES), lambda b, hp: (b, 0, off + hp))
    return pl.pallas_call(
        functools.partial(_sb_kernel, T=T),
        grid=(B, npair),
        in_specs=[blk(0), blk(npair), blk(2 * npair)],
        out_specs=pl.BlockSpec((None, T, LANES), lambda b, hp: (b, 0, hp)),
        out_shape=jax.ShapeDtypeStruct((B, T, SB_WIDTH), BF16),
        scratch_shapes=[pltpu.VMEM((SB_TILE, LANES), F32),
                        pltpu.VMEM((LANES // HEAD_DIM, SB_TILE, 1), F32)],
        compiler_params=pltpu.CompilerParams(dimension_semantics=("parallel", "parallel")),
        name="stick_breaking_attention",
    )(qkv, qkv, qkv)


def _route(n, w_ref, b_ref, id_ref, p_ref):
    nh = n.astype(BF16)
    nl = (n - nh.astype(F32)).astype(BF16)
    a = _dot(nh, w_ref[...])
    b = _dot(nl, w_ref[...])
    L = ROUTER_LANES
    logits = a[:, :L] + (a[:, L:] + b[:, :L] + b[:, L:]) + b_ref[...]
    lane = lax.broadcasted_iota(jnp.int32, logits.shape, 1)
    neg = jnp.float32(-jnp.inf)
    logits = jnp.where(lane < N_EXPERTS, logits, neg)
    m1 = jnp.max(logits, axis=-1, keepdims=True)
    i1 = jnp.min(jnp.where(logits == m1, lane, ROUTER_LANES), axis=-1, keepdims=True)
    rest = jnp.where(lane == i1, neg, logits)
    m2 = jnp.max(rest, axis=-1, keepdims=True)
    i2 = jnp.min(jnp.where(rest == m2, lane, ROUTER_LANES), axis=-1, keepdims=True)
    e2 = jnp.exp(m2 - m1)
    p1 = 1.0 / (1.0 + e2)
    p2 = e2 / (1.0 + e2)
    id_ref[...] = jnp.where(lane == 0, i1, jnp.where(lane == 1, i2, 0))
    p_ref[...] = jnp.where(lane == 0, p1, jnp.where(lane == 1, p2, 0.0))


def _merge_kernel(ygm_ref, yrw_ref, ysb_ref, gate_ref, h_ref, pgm_ref, prw_ref, psb_ref, wo_ref,
                  g_ref, *refs, route):
    D = D_MODEL
    merged = gate_ref[:, 0:D].astype(F32) * _dot(ygm_ref[...], pgm_ref[...])
    merged += gate_ref[:, D:2 * D].astype(F32) * _dot(yrw_ref[...], prw_ref[...])
    merged += gate_ref[:, 2 * D:3 * D].astype(F32) * _dot(ysb_ref[...], psb_ref[...])
    hn = h_ref[...] + _dot(merged.astype(BF16), wo_ref[...])
    n = _rms(hn, g_ref[...])
    if route:
        w_ref, b_ref, ho_ref, id_ref, p_ref = refs
        _route(n, w_ref, b_ref, id_ref, p_ref)
    else:
        ho_ref, no_ref = refs
        no_ref[...] = n.astype(no_ref.dtype)
    ho_ref[...] = hn


def _resident(shape):
    return pl.BlockSpec(shape, lambda i: (0,) * len(shape), pipeline_mode=pl.Buffered(1))


def merge_call(y_gm, y_rw, y_sb, gates, h, p_gm, p_rw, p_sb, w_o, g_next, router=None, tm=256):
    M, D = h.shape
    tm = min(tm, M)
    rowblk = lambda n: pl.BlockSpec((tm, n), lambda i: (i, 0))
    wbytes = 2 * (GM_WIDTH + RW_WIDTH + SB_WIDTH + D) * D
    need = wbytes + 2 * tm * (2 * (GM_WIDTH + RW_WIDTH + SB_WIDTH + GATE_COLS) + 4 * D + 4 * D + 2 * D) \
        + 6 * tm * D * 4
    in_specs = [rowblk(GM_WIDTH), rowblk(RW_WIDTH), rowblk(SB_WIDTH), rowblk(GATE_COLS), rowblk(D),
                _resident((GM_WIDTH, D)), _resident((RW_WIDTH, D)), _resident((SB_WIDTH, D)),
                _resident((D, D)), _resident((1, D))]
    args = [y_gm, y_rw, y_sb, gates, h, p_gm, p_rw, p_sb, w_o, g_next.reshape(1, D)]
    if router is None:
        out_specs = [rowblk(D), rowblk(D)]
        out_shape = [jax.ShapeDtypeStruct((M, D), F32), jax.ShapeDtypeStruct((M, D), BF16)]
    else:
        router_w, router_b = router
        wpad = jnp.zeros((D, ROUTER_LANES), F32).at[:, :N_EXPERTS].set(router_w)
        w_hi = wpad.astype(BF16)
        w_lo = (wpad - w_hi.astype(F32)).astype(BF16)
        bpad = jnp.zeros((1, ROUTER_LANES), F32).at[0, :N_EXPERTS].set(router_b)
        in_specs += [_resident((D, 2 * ROUTER_LANES)), _resident((1, ROUTER_LANES))]
        args += [jnp.concatenate([w_hi, w_lo], axis=1), bpad]
        out_specs = [rowblk(D), rowblk(ROUTER_LANES), rowblk(ROUTER_LANES)]
        out_shape = [jax.ShapeDtypeStruct((M, D), F32), jax.ShapeDtypeStruct((M, ROUTER_LANES), jnp.int32),
                     jax.ShapeDtypeStruct((M, ROUTER_LANES), F32)]
    return pl.pallas_call(
        functools.partial(_merge_kernel, route=router is not None),
        grid=(M // tm,),
        in_specs=in_specs,
        out_specs=out_specs,
        out_shape=out_shape,
        compiler_params=pltpu.CompilerParams(dimension_semantics=("parallel",),
                                             vmem_limit_bytes=_vmem_limit(need)),
        name="merge_out_proj",
    )(*args)


def _ffn_kernel(n_ref, h_ref, w1_ref, w3_ref, w2_ref, g_ref, *refs, emit_h):
    if emit_h:
        ho_ref, no_ref, acc_ref = refs
    else:
        no_ref, acc_ref = refs
    f = pl.program_id(1)

    @pl.when(f == 0)
    def _():
        acc_ref[...] = jnp.zeros_like(acc_ref)

    n = n_ref[...]
    act = (jax.nn.silu(_dot(n, w1_ref[...])) * _dot(n, w3_ref[...])).astype(BF16)
    acc_ref[...] += _dot(act, w2_ref[...])

    @pl.when(f == pl.num_programs(1) - 1)
    def _():
        hn = h_ref[...] + acc_ref[...]
        if emit_h:
            ho_ref[...] = hn
        no_ref[...] = _rms(hn, g_ref[...]).astype(no_ref.dtype)


def ffn_call(n, h, w1, w3, w2, g_next, n_dtype, emit_h, tm=512, tf=512):
    M, D = h.shape
    F = w1.shape[1]
    tm = min(tm, M)
    rowblk = pl.BlockSpec((tm, D), lambda i, f: (i, 0))
    out_specs = [rowblk]
    out_shape = [jax.ShapeDtypeStruct((M, D), n_dtype)]
    if emit_h:
        out_specs = [rowblk, rowblk]
        out_shape = [jax.ShapeDtypeStruct((M, D), F32)] + out_shape
    need = 2 * tm * D * (2 + 4 + 4 * len(out_shape)) + tm * D * 4 + 2 * 3 * D * tf * 2 + 4 * tm * tf * 4
    return pl.pallas_call(
        functools.partial(_ffn_kernel, emit_h=emit_h),
        grid=(M // tm, F // tf),
        in_specs=[rowblk, rowblk,
                  pl.BlockSpec((D, tf), lambda i, f: (0, f)),
                  pl.BlockSpec((D, tf), lambda i, f: (0, f)),
                  pl.BlockSpec((tf, D), lambda i, f: (f, 0)),
                  pl.BlockSpec((1, D), lambda i, f: (0, 0))],
        out_specs=out_specs,
        out_shape=out_shape,
        scratch_shapes=[pltpu.VMEM((tm, D), F32)],
        compiler_params=pltpu.CompilerParams(dimension_semantics=("parallel", "arbitrary"),
                                             vmem_limit_bytes=_vmem_limit(need + (8 << 20))),
        name="swiglu_ffn",
    )(n, h, w1, w3, w2, g_next.reshape(1, D))


def moe_plan(ids, M, tm):
    E = N_EXPERTS
    n_tiles = -(-(TOP_K * M) // tm) + E + 2
    e_flat = ids[:, :TOP_K].reshape(-1)
    onehot = (e_flat[:, None] == jnp.arange(E, dtype=jnp.int32)[None, :]).astype(jnp.int32)
    csum = jnp.cumsum(onehot, axis=0)
    rank = jnp.sum((csum - onehot) * onehot, axis=1)
    counts = csum[-1]
    tiles_per_e = (counts + tm - 1) // tm
    tile_end = jnp.cumsum(tiles_per_e)
    offs = (tile_end - tiles_per_e) * tm
    dest = offs[e_flat] + rank
    pair = jnp.full((n_tiles * tm,), -1, jnp.int32).at[dest].set(
        jnp.arange(TOP_K * M, dtype=jnp.int32), unique_indices=True)
    spare = TOP_K * M + jnp.arange(n_tiles * tm, dtype=jnp.int32) % tm
    src_token = jnp.where(pair >= 0, pair // TOP_K, 0)
    out_row = jnp.where(pair >= 0, (pair % TOP_K) * M + pair // TOP_K, spare)
    prev_out_row = jnp.concatenate([spare[:tm], out_row[:-tm]])
    n_used = tile_end[-1]
    t = jnp.arange(n_tiles, dtype=jnp.int32)
    tile_e = jnp.searchsorted(tile_end, jnp.minimum(t, n_used - 1), side="right").astype(jnp.int32)
    return (src_token.reshape(n_tiles, 1, tm), prev_out_row.reshape(n_tiles, 1, tm), tile_e,
            n_used.astype(jnp.int32).reshape(1))


def _moe_kernel(tile_e_ref, n_used_ref, src_ref, nxt_ref, dst_ref, h_hbm, g_ref, w1_ref, w3_ref, w2_ref,
                y_hbm, xbuf, xb, acc, ostage, gsem, ssem):
    t = pl.program_id(0)
    f = pl.program_id(1)
    nf = pl.num_programs(1)
    tm = xb.shape[0]
    per_step = tm // nf
    n_used = n_used_ref[0]
    used = t < n_used
    cur = t % 2

    def fetch(idx_ref, r, slot):
        return pltpu.make_async_copy(h_hbm.at[pl.ds(idx_ref[0, r], 1)], xbuf.at[slot, pl.ds(r, 1)],
                                     gsem.at[slot])

    def put(r):
        return pltpu.make_async_copy(ostage.at[1 - cur, pl.ds(r, 1)], y_hbm.at[pl.ds(dst_ref[0, r], 1)],
                                     ssem.at[1 - cur])

    @pl.when((f == 0) & (t == 0))
    def _():
        def issue(r, c):
            fetch(src_ref, r, 0).start()
            return c
        lax.fori_loop(0, tm, issue, 0)
        ostage[1] = jnp.zeros((tm, ostage.shape[2]), F32)

    @pl.when((f == 0) & (t <= n_used))
    def _():
        pltpu.make_async_copy(h_hbm.at[pl.ds(0, tm)], xbuf.at[cur], gsem.at[cur]).wait()

    @pl.when((f == 0) & (t >= 1) & (t - 1 <= n_used))
    def _():
        pltpu.make_async_copy(ostage.at[cur], y_hbm.at[pl.ds(0, tm)], ssem.at[cur]).wait()

    @pl.when((f == 0) & used)
    def _():
        xb[...] = _rms(xbuf[cur], g_ref[...]).astype(BF16)
        acc[...] = jnp.zeros_like(acc)

    @pl.when(used)
    def _():
        for j in range(per_step):
            fetch(nxt_ref, f * per_step + j, 1 - cur).start()
            put(f * per_step + j).start()
        x = xb[...]
        act = jax.nn.silu(_dot(x, w1_ref[...])) * _dot(x, w3_ref[...])
        acc[...] += _dot(act.astype(BF16), w2_ref[...])

    @pl.when(used & (f == nf - 1))
    def _():
        ostage[cur] = acc[...]

    @pl.when((f == 0) & (t == n_used))
    def _():
        def issue(r, c):
            put(r).start()
            return c
        lax.fori_loop(0, tm, issue, 0)


def moe_call(h, g, src_token, prev_out_row, tile_e, n_used, w1, w3, w2, tm, tf):
    M, D = h.shape
    n_tiles = src_token.shape[0]
    E, _, Fe = w1.shape
    nf = Fe // tf
    assert tm % nf == 0 and Fe % tf == 0

    def wmap(t, f, tile_e_ref, n_used_ref):
        ff = jnp.where(t < n_used_ref[0], f, nf - 1)
        return tile_e_ref[t], ff

    smem_tile = lambda fn: pl.BlockSpec((None, 1, tm), fn, memory_space=pltpu.SMEM)
    grid_spec = pltpu.PrefetchScalarGridSpec(
        num_scalar_prefetch=2,
        grid=(n_tiles, nf),
        in_specs=[smem_tile(lambda t, f, te, nu: (t, 0, 0)),
                  smem_tile(lambda t, f, te, nu: (jnp.minimum(t + 1, n_tiles - 1), 0, 0)),
                  smem_tile(lambda t, f, te, nu: (t, 0, 0)),
                  pl.BlockSpec(memory_space=pl.ANY),
                  pl.BlockSpec((1, D), lambda t, f, te, nu: (0, 0)),
                  pl.BlockSpec((None, D, tf), lambda t, f, te, nu: (wmap(t, f, te, nu)[0], 0,
                                                                     wmap(t, f, te, nu)[1])),
                  pl.BlockSpec((None, D, tf), lambda t, f, te, nu: (wmap(t, f, te, nu)[0], 0,
                                                                     wmap(t, f, te, nu)[1])),
                  pl.BlockSpec((None, tf, D), lambda t, f, te, nu: wmap(t, f, te, nu) + (0,))],
        out_specs=pl.BlockSpec(memory_space=pl.ANY),
        scratch_shapes=[pltpu.VMEM((2, tm, D), F32), pltpu.VMEM((tm, D), BF16), pltpu.VMEM((tm, D), F32),
                        pltpu.VMEM((2, tm, D), F32),
                        pltpu.SemaphoreType.DMA((2,)), pltpu.SemaphoreType.DMA((2,))])
    need = tm * D * (8 + 2 + 4 + 8) + 2 * 3 * D * tf * 2 + 4 * tm * tf * 4
    return pl.pallas_call(
        _moe_kernel,
        grid_spec=grid_spec,
        out_shape=jax.ShapeDtypeStruct((TOP_K * M + tm, D), F32),
        compiler_params=pltpu.CompilerParams(dimension_semantics=("arbitrary", "arbitrary"),
                                             vmem_limit_bytes=_vmem_limit(need + (8 << 20))),
        name="moe_experts",
    )(tile_e, n_used, src_token, src_token, prev_out_row, h, g.reshape(1, D), w1, w3, w2)


def _moe_combine_kernel(h_ref, p_ref, g_ref, *refs, emit_h):
    y_refs, out_refs = refs[:TOP_K], refs[TOP_K:]
    hn = h_ref[...]
    for k in range(TOP_K):
        hn = hn + p_ref[:, k:k + 1] * y_refs[k][...]
    if emit_h:
        out_refs[0][...] = hn
    out_refs[-1][...] = _rms(hn, g_ref[...]).astype(out_refs[-1].dtype)


def moe_combine_call(h, y_tok, probs, g_next, n_dtype, emit_h, tm=MOE_COMBINE_TM):
    M, D = h.shape
    tm = min(tm, M)
    rowblk = pl.BlockSpec((tm, D), lambda i: (i, 0))
    out_specs = [rowblk]
    out_shape = [jax.ShapeDtypeStruct((M, D), n_dtype)]
    if emit_h:
        out_specs = [rowblk, rowblk]
        out_shape = [jax.ShapeDtypeStruct((M, D), F32)] + out_shape
    y_specs = [pl.BlockSpec((tm, D), functools.partial(lambda k, i: (k * (M // tm) + i, 0), k))
               for k in range(TOP_K)]
    return pl.pallas_call(
        functools.partial(_moe_combine_kernel, emit_h=emit_h),
        grid=(M // tm,),
        in_specs=[rowblk, pl.BlockSpec((tm, ROUTER_LANES), lambda i: (i, 0)),
                  pl.BlockSpec((1, D), lambda i: (0, 0))] + y_specs,
        out_specs=out_specs,
        out_shape=out_shape,
        compiler_params=pltpu.CompilerParams(dimension_semantics=("parallel",),
                                             vmem_limit_bytes=_vmem_limit(14 * tm * D * 4)),
        name="moe_combine",
    )(h, probs, g_next.reshape(1, D), *([y_tok] * TOP_K))


def kernel(x, norm_mix, w_in, gate_b, gm_ln_g, gm_ln_b, gm_ws, gm_bs, rw_mu, rw_w0, rw_w2, rw_a0, rw_a2,
           rw_g2, rw_kk, rw_ka, rw_rk, rw_ln_g, rw_ln_b, p_gm, p_rw, p_sb, w_o, norm_ffn, ffn_w1, ffn_w3,
           ffn_w2, router_w, router_b, moe_w1, moe_w3, moe_w2, norm_out):
    B, T, D = x.shape
    M = B * T
    bf = lambda a: a.astype(BF16)
    h = x.reshape(M, D)
    n = rmsnorm_call(h, norm_mix[0], BF16)
    for l in range(DEPTH):
        last = l == DEPTH - 1
        g_next = norm_out if last else norm_mix[l + 1]
        n_dtype = x.dtype if last else BF16
        w = w_in[l]
        uv = matmul_call(n, bf(w[:, GM_OFF:RW_OFF]), BF16, tn=GM_COLS, name="in_proj_gm")
        p = matmul_call(n, bf(w[:, RW_OFF:SB_OFF]), BF16, tn=RW_COLS, name="in_proj_rw")
        qkv = matmul_call(n, bf(w[:, SB_OFF:GATE_OFF]), BF16, tn=1024, name="in_proj_sb")
        gates = matmul_call(n, bf(w[:, GATE_OFF:]), BF16, tn=1024, bias=gate_b[l].reshape(-1),
                            name="in_proj_gate")
        y_gm = gm_call(uv, gm_ln_g[l], gm_ln_b[l], gm_ws[l], gm_bs[l])
        y_rw = rw_call(p.reshape(B, T, RW_COLS), B, T, rw_mu[l], rw_w0[l], rw_w2[l], rw_a0[l], rw_a2[l],
                       rw_g2[l], rw_kk[l], rw_ka[l], rw_rk[l], rw_ln_g[l], rw_ln_b[l]).reshape(M, RW_WIDTH)
        y_sb = sb_call(qkv.reshape(B, T, SB_COLS), B, T).reshape(M, SB_WIDTH)
        merge_args = (y_gm, y_rw, y_sb, gates, h, bf(p_gm[l]), bf(p_rw[l]), bf(p_sb[l]), bf(w_o[l]), norm_ffn[l])
        j = l // 2
        if l % 2 == 0:
            h, n2 = merge_call(*merge_args)
            outs = ffn_call(n2, h, bf(ffn_w1[j]), bf(ffn_w3[j]), bf(ffn_w2[j]), g_next, n_dtype,
                            emit_h=not last)
        else:
            h, ids, probs = merge_call(*merge_args, router=(router_w[j], router_b[j]))
            src_token, prev_out_row, tile_e, n_used = moe_plan(ids, M, MOE_TM)
            y_tok = moe_call(h, norm_ffn[l], src_token, prev_out_row, tile_e, n_used, bf(moe_w1[j]),
                             bf(moe_w3[j]), bf(moe_w2[j]), MOE_TM, MOE_TF)
            outs = moe_combine_call(h, y_tok, probs, g_next, n_dtype, emit_h=not last)
        if last:
            n = outs[0]
        else:
            h, n = outs
    return n.reshape(B, T, D)
```

```python
import functools

import jax
import jax.numpy as jnp
from jax import lax
from jax.experimental import pallas as pl
from jax.experimental.pallas import tpu as pltpu

F32 = jnp.float32
BF16 = jnp.bfloat16

D_MODEL = 2048
DEPTH = 2
HEAD_DIM = 64
GM_HEADS = 8
GM_WIDTH = GM_HEADS * HEAD_DIM
CHUNK = 128
RW_HEADS = 8
RW_WIDTH = RW_HEADS * HEAD_DIM
DECAY_LORA = 64
AAA_LORA = 64
GATE_LORA = 128
SB_HEADS = 16
SB_WIDTH = SB_HEADS * HEAD_DIM
N_BRANCH = 3
D_FF = 5632
N_EXPERTS = 8
TOP_K = 2
D_FF_EXPERT = D_FF // TOP_K
RMS_EPS = 1e-6
LN_EPS = 1e-5
RW_GN_EPS = 64e-5

GM_COLS = 2 * GM_WIDTH
RW_COLS = 3 * RW_WIDTH + DECAY_LORA + AAA_LORA + GATE_LORA
SB_COLS = 3 * SB_WIDTH
GATE_COLS = N_BRANCH * D_MODEL
GM_OFF = 0
RW_OFF = GM_OFF + GM_COLS
SB_OFF = RW_OFF + RW_COLS
GATE_OFF = SB_OFF + SB_COLS
N_IN = GATE_OFF + GATE_COLS

V7X_VMEM_BYTES = 64 * 1024 * 1024
LANES = 128
BF16_SUBLANES = 16
RW_CHUNK = 64
RW_TILE = 256
RW_GROUP = 4
SB_TILE = 256
LOG2_E = 1.4426950408889634
SB_EXP2_UNDERFLOW = -151.0
ROUTER_LANES = 128
MOE_TF = 256
MOE_TM = 528
MOE_COMBINE_TM = 256


def _vmem_limit(nbytes):
    return int(min(max(nbytes, 32 * 1024 * 1024), V7X_VMEM_BYTES - 6 * 1024 * 1024))


def _dot(a, b):
    return jnp.dot(a, b, preferred_element_type=F32)


def _dot_nt(a, b):
    return lax.dot_general(a, b, (((1,), (1,)), ((), ())), preferred_element_type=F32)


def _dot_tn(a, b):
    return lax.dot_general(a, b, (((0,), (0,)), ((), ())), preferred_element_type=F32)


def _dot_split(x, ones_bf16):
    hi = x.astype(BF16)
    lo = (x - hi.astype(F32)).astype(BF16)
    return _dot(hi, ones_bf16) + _dot(lo, ones_bf16)


def _rms(x, g):
    ms = jnp.mean(x * x, axis=-1, keepdims=True)
    return x * lax.rsqrt(ms + RMS_EPS) * g


def _rmsnorm_kernel(x_ref, g_ref, o_ref):
    o_ref[...] = _rms(x_ref[...], g_ref[...]).astype(o_ref.dtype)


def rmsnorm_call(x, g, out_dtype, tm=512):
    M, D = x.shape
    tm = min(tm, M)
    return pl.pallas_call(
        _rmsnorm_kernel,
        grid=(M // tm,),
        in_specs=[pl.BlockSpec((tm, D), lambda i: (i, 0)),
                  pl.BlockSpec((1, D), lambda i: (0, 0))],
        out_specs=pl.BlockSpec((tm, D), lambda i: (i, 0)),
        out_shape=jax.ShapeDtypeStruct((M, D), out_dtype),
        compiler_params=pltpu.CompilerParams(dimension_semantics=("parallel",)),
        name="rmsnorm",
    )(x, g.reshape(1, D))


def _mm_kernel(a_ref, w_ref, *refs, gate, n_cast):
    if gate:
        b_ref, refs = refs[0], refs[1:]
    cast_in, o_ref, cast_out = refs[:n_cast], refs[n_cast], refs[n_cast + 1:]
    if n_cast:
        @pl.when(pl.program_id(1) == 0)
        def _():
            for src, dst in zip(cast_in, cast_out):
                dst[...] = src[...].astype(dst.dtype)
    z = _dot(a_ref[...], w_ref[...])
    if gate:
        z = jax.nn.sigmoid(z + b_ref[...])
    o_ref[...] = z.astype(o_ref.dtype)


def row_tiles(M, tm=1024):
    return M // min(tm, M)


def can_ride_rows(w, n_i):
    return w.shape[0] % n_i == 0 and (w.shape[0] // n_i) % BF16_SUBLANES == 0


def matmul_call(a, w, out_dtype, tn, bias=None, riders=(), tm=1024, name="matmul"):
    M, K = a.shape
    N = w.shape[1]
    tm = min(tm, M)
    n_i = M // tm
    in_specs = [pl.BlockSpec((tm, K), lambda i, j: (i, 0)),
                pl.BlockSpec((K, tn), lambda i, j: (0, j))]
    args = [a, w]
    if bias is not None:
        in_specs.append(pl.BlockSpec((1, tn), lambda i, j: (0, j)))
        args.append(bias.reshape(1, N))
    ride_specs = [pl.BlockSpec((r.shape[0] // n_i, r.shape[1]), lambda i, j: (i, 0)) for r in riders]
    osz = jnp.dtype(out_dtype).itemsize
    need = 2 * (tm * K * 2 + K * tn * 2 + tm * tn * osz) + 2 * tm * tn * 4 \
        + sum(2 * (r.size // n_i) * 6 for r in riders)
    outs = pl.pallas_call(
        functools.partial(_mm_kernel, gate=bias is not None, n_cast=len(riders)),
        grid=(n_i, N // tn),
        in_specs=in_specs + ride_specs,
        out_specs=[pl.BlockSpec((tm, tn), lambda i, j: (i, j))] + ride_specs,
        out_shape=[jax.ShapeDtypeStruct((M, N), out_dtype)]
        + [jax.ShapeDtypeStruct(r.shape, BF16) for r in riders],
        compiler_params=pltpu.CompilerParams(
            dimension_semantics=("parallel", "arbitrary"),
            vmem_limit_bytes=_vmem_limit(need + (8 << 20))),
        name=name,
    )(*args, *riders)
    return outs[0], outs[1:]


def _gm_kernel(uv_ref, lng_ref, lnb_ref, ws_ref, bias_ref, o_ref, *, n_chunks):
    ii = lax.broadcasted_iota(jnp.int32, (CHUNK, CHUNK), 0)
    jj = lax.broadcasted_iota(jnp.int32, (CHUNK, CHUNK), 1)
    causal = jj <= ii
    ws = [jnp.where(causal, ws_ref[h], 0.0).astype(BF16) for h in range(GM_HEADS)]
    for c in range(n_chunks):
        rows = slice(c * CHUNK, (c + 1) * CHUNK)
        u = jax.nn.gelu(uv_ref[rows, 0:GM_WIDTH].astype(F32))
        v = jax.nn.gelu(uv_ref[rows, GM_WIDTH:2 * GM_WIDTH].astype(F32))
        mu = jnp.mean(v, axis=-1, keepdims=True)
        vc = v - mu
        var = jnp.mean(vc * vc, axis=-1, keepdims=True)
        vn = (vc * lax.rsqrt(var + LN_EPS) * lng_ref[...] + lnb_ref[...]).astype(BF16)
        mixed = jnp.concatenate(
            [_dot(ws[h], vn[:, h * HEAD_DIM:(h + 1) * HEAD_DIM]) for h in range(GM_HEADS)], axis=-1)
        o_ref[rows, :] = (u * (mixed + bias_ref[...])).astype(o_ref.dtype)


def gm_call(uv, ln_g, ln_b, w_s, b_s, n_chunks=2):
    M = uv.shape[0]
    tt = CHUNK * n_chunks
    bias = jnp.repeat(b_s.T, HEAD_DIM, axis=1)
    return pl.pallas_call(
        functools.partial(_gm_kernel, n_chunks=n_chunks),
        grid=(M // tt,),
        in_specs=[pl.BlockSpec((tt, GM_COLS), lambda i: (i, 0)),
                  pl.BlockSpec((1, GM_WIDTH), lambda i: (0, 0)),
                  pl.BlockSpec((1, GM_WIDTH), lambda i: (0, 0)),
                  pl.BlockSpec((GM_HEADS, CHUNK, CHUNK), lambda i: (0, 0, 0)),
                  pl.BlockSpec((CHUNK, GM_WIDTH), lambda i: (0, 0))],
        out_specs=pl.BlockSpec((tt, GM_WIDTH), lambda i: (i, 0)),
        out_shape=jax.ShapeDtypeStruct((M, GM_WIDTH), BF16),
        compiler_params=pltpu.CompilerParams(dimension_semantics=("parallel",)),
        name="gmlp_gating",
    )(uv, ln_g.reshape(1, -1), ln_b.reshape(1, -1), w_s, bias)


def _rw_kernel(p_ref, prev_ref, mu_ref, w0_ref, w2_ref, a0_ref, a2_ref, g2_ref, kk_ref, ka_ref,
               rk_ref, lng_ref, lnb_ref, hsum_ref, o_ref, st_ref, y_ref):
    C = RW_CHUNK
    i = pl.program_id(1)

    @pl.when(i == 0)
    def _():
        st_ref[...] = jnp.zeros_like(st_ref)

    hsum = hsum_ref[...]
    gw = hsum.shape[0]

    def head_sums(x):
        return jnp.concatenate([_dot_split(x[:, q * gw:(q + 1) * gw], hsum)
                                for q in range(x.shape[1] // gw)], axis=1)

    p = p_ref[...].astype(F32)
    prev_row = prev_ref[BF16_SUBLANES - 1:BF16_SUBLANES, :].astype(F32)
    prev_row = jnp.where(i == 0, 0.0, prev_row)
    row = lax.broadcasted_iota(jnp.int32, p.shape, 0)
    p_prev = jnp.where(row == 0, prev_row, pltpu.roll(p, 1, axis=0))
    ps = p + (p_prev - p) * mu_ref[...]
    W = RW_WIDTH
    r = ps[:, 0:W]
    k = ps[:, W:2 * W]
    v = ps[:, 2 * W:3 * W]
    wl = ps[:, 3 * W:3 * W + DECAY_LORA]
    al = ps[:, 3 * W + DECAY_LORA:3 * W + DECAY_LORA + AAA_LORA]
    gl = ps[:, 3 * W + DECAY_LORA + AAA_LORA:RW_COLS]

    d = w0_ref[...] + _dot(jnp.tanh(wl).astype(BF16), w2_ref[...])
    log_w = -jnp.exp(-jax.nn.softplus(-d) - 0.5)
    a = jax.nn.sigmoid(a0_ref[...] + _dot(al.astype(BF16), a2_ref[...]))
    g = _dot(jax.nn.sigmoid(gl).astype(BF16), g2_ref[...])
    kk = k * kk_ref[...]
    kk = kk / jnp.maximum(jnp.sqrt(head_sums(kk * kk)), 1e-12)
    k = k * (1.0 + (a - 1.0) * ka_ref[...])
    bonus = head_sums(r * k * rk_ref[...]) * v

    HD = HEAD_DIM
    GW = RW_GROUP * HD
    ci = lax.broadcasted_iota(jnp.int32, (C, C), 0)
    cj = lax.broadcasted_iota(jnp.int32, (C, C), 1)
    tri_incl = (cj <= ci).astype(BF16)
    ti = lax.broadcasted_iota(jnp.int32, (C, GW), 0)
    tj = lax.broadcasted_iota(jnp.int32, (C, GW), 1) % HD
    strict = tj < ti
    incl = tj <= ti
    eye = ti == tj
    blk_masks = []
    s = 2
    while s < C:
        blk_masks.append((ti // (2 * s) == tj // (2 * s)) & (ti // s != tj // s))
        s *= 2
    pair_mask = (ti // 2 == tj // 2) & strict
    gi = lax.broadcasted_iota(jnp.int32, (GW, GW), 0)
    gj = lax.broadcasted_iota(jnp.int32, (GW, GW), 1)
    bd_mask = gi // HD == gj // HD
    bd_mask_b = jnp.where(bd_mask, 1.0, 0.0).astype(BF16)
    gw_eye = gi == gj

    def bd(x):
        xb = x.astype(BF16)
        return jnp.where(bd_mask_b > 0, jnp.concatenate([xb] * RW_GROUP, axis=0), 0)

    def bd_of_full(full):
        return jnp.where(bd_mask, full, 0.0)

    n_chunks = RW_TILE // C
    n_groups = RW_WIDTH // GW
    systems = [(c, q) for c in range(n_chunks) for q in range(n_groups)]

    pre = []
    for c in range(n_chunks):
        rows = slice(c * C, (c + 1) * C)
        lw = log_w[rows]
        cum = _cumsum_rows(lw, tri_incl)
        gam = jnp.exp(cum)
        gam_prev = jnp.exp(cum - lw)
        gam_inv = jnp.exp(-cum)
        g_end = gam[C - 1:C, :]
        bh_f = kk[rows] * a[rows] * gam_inv
        kh_f = k[rows] * gam_inv
        pre.append(dict(
            rh=(r[rows] * gam).astype(BF16), ah=(-kk[rows] * gam_prev).astype(BF16),
            bh=bh_f.astype(BF16), kh=kh_f.astype(BF16),
            bt=(bh_f * g_end).astype(BF16), kt=(kh_f * g_end).astype(BF16),
            v=v[rows].astype(BF16), g_end=g_end))

    def grp(c, q, name):
        return pre[c][name][:, q * GW:(q + 1) * GW]

    st = {}
    for (c, q) in systems:
        ar = jnp.concatenate([grp(c, q, "ah"), grp(c, q, "rh")], axis=0)
        pb = _dot_nt(ar, bd(grp(c, q, "bh")))
        pk = _dot_nt(ar, bd(grp(c, q, "kh")))
        lab = jnp.where(strict, pb[0:C], 0.0)
        st[(c, q)] = dict(
            lab=lab, mrb=jnp.where(incl, pb[C:2 * C], 0.0).astype(BF16),
            lm=jnp.concatenate([jnp.where(strict, pk[0:C], 0.0), jnp.where(incl, pk[C:2 * C], 0.0)],
                               axis=0).astype(BF16),
            tinv=jnp.where(eye, 1.0, jnp.where(pair_mask, lab, 0.0)))

    for bm in blk_masks:
        xs = {}
        for key in systems:
            d_ = st[key]
            xs[key] = _dot(d_["tinv"].astype(BF16), bd(jnp.where(bm, d_["lab"], 0.0))).astype(BF16)
        for key in systems:
            d_ = st[key]
            d_["tinv"] = d_["tinv"] + _dot(xs[key], bd(d_["tinv"]))

    for (c, q) in systems:
        d_ = st[(c, q)]
        lv = _dot(d_["lm"], bd(grp(c, q, "v")))
        tb = d_["tinv"].astype(BF16)
        ap = _dot(tb, bd(grp(c, q, "ah"))).astype(BF16)
        up = _dot(tb, bd(lv[0:C])).astype(BF16)
        d_["rp"] = (grp(c, q, "rh").astype(F32) + _dot(d_["mrb"], bd(ap))).astype(BF16)
        d_["ypp"] = _dot(d_["mrb"], bd(up)) + lv[C:2 * C]
        g_end = pre[c]["g_end"][:, q * GW:(q + 1) * GW]
        gt = _dot_tn(ap, grp(c, q, "bt"))
        d_["gt"] = jnp.where(gw_eye, gt + g_end, bd_of_full(gt)).astype(BF16)
        ht = bd_of_full(_dot_tn(up, grp(c, q, "bt")) + _dot_tn(grp(c, q, "v"), grp(c, q, "kt")))
        d_["ht"] = sum(ht[hh * HD:(hh + 1) * HD] for hh in range(1, RW_GROUP)) + ht[0:HD]

    for q in range(n_groups):
        s_cur = st_ref[q]
        for c in range(n_chunks):
            d_ = st[(c, q)]
            y_ref[c * C:(c + 1) * C, q * GW:(q + 1) * GW] = _dot_nt(d_["rp"], bd(s_cur)) + d_["ypp"]
            s_cur = _dot(s_cur.astype(BF16), d_["gt"]) + d_["ht"]
        st_ref[q] = s_cur

    y = y_ref[...]
    inv_n = 1.0 / HEAD_DIM
    m = head_sums(y) * inv_n
    yc = y - m
    var = head_sums(yc * yc) * inv_n
    yn = yc * lax.rsqrt(var + RW_GN_EPS) * lng_ref[...] + lnb_ref[...]
    o_ref[...] = ((yn + bonus) * g).astype(o_ref.dtype)


def _cumsum_rows(x, tri_incl_bf16):
    hi = x.astype(BF16)
    lo = (x - hi.astype(F32)).astype(BF16)
    return _dot(tri_incl_bf16, hi) + _dot(tri_incl_bf16, lo)


def rw_call(p, B, T, mu, w0, w2, a0, a2, g2, k_k, k_a, r_k, ln_g, ln_b):
    nt = T // RW_TILE
    sub = RW_TILE // BF16_SUBLANES
    hid = jnp.arange(RW_GROUP * HEAD_DIM) // HEAD_DIM
    hsum = (hid[:, None] == hid[None, :]).astype(BF16)
    row = lambda a: a.reshape(1, -1).astype(F32)
    vec = lambda n: pl.BlockSpec((1, n), lambda b, i: (0, 0))
    return pl.pallas_call(
        _rw_kernel,
        grid=(B, nt),
        in_specs=[pl.BlockSpec((None, RW_TILE, RW_COLS), lambda b, i: (b, i, 0)),
                  pl.BlockSpec((None, BF16_SUBLANES, RW_COLS),
                               lambda b, i: (b, jnp.maximum(i * sub - 1, 0), 0)),
                  vec(RW_COLS), vec(RW_WIDTH),
                  pl.BlockSpec((DECAY_LORA, RW_WIDTH), lambda b, i: (0, 0)),
                  vec(RW_WIDTH),
                  pl.BlockSpec((AAA_LORA, RW_WIDTH), lambda b, i: (0, 0)),
                  pl.BlockSpec((GATE_LORA, RW_WIDTH), lambda b, i: (0, 0)),
                  vec(RW_WIDTH), vec(RW_WIDTH), vec(RW_WIDTH), vec(RW_WIDTH), vec(RW_WIDTH),
                  pl.BlockSpec((RW_GROUP * HEAD_DIM, RW_GROUP * HEAD_DIM), lambda b, i: (0, 0))],
        out_specs=pl.BlockSpec((None, RW_TILE, RW_WIDTH), lambda b, i: (b, i, 0)),
        out_shape=jax.ShapeDtypeStruct((B, T, RW_WIDTH), BF16),
        scratch_shapes=[pltpu.VMEM((RW_HEADS // RW_GROUP, HEAD_DIM, RW_GROUP * HEAD_DIM), F32),
                        pltpu.VMEM((RW_TILE, RW_WIDTH), F32)],
        compiler_params=pltpu.CompilerParams(dimension_semantics=("parallel", "arbitrary")),
        name="rwkv7_time_mix",
    )(p, p, row(mu), row(w0), w2.astype(BF16), row(a0), a2.astype(BF16), g2.astype(BF16),
      row(k_k), row(k_a), row(r_k), row(ln_g), row(ln_b), hsum)


def _sb_scores(qs, ks, masks, suffix_ones):
    n = range(len(qs))
    zs = [_dot_nt(qs[s], ks[s]) * LOG2_E for s in n]
    sps = [jnp.maximum(z, 0.0) + jnp.log2(1.0 + jnp.exp2(-jnp.abs(z))) for z in zs]
    log_nots = [-sp for sp in sps]
    log_nots = [ln if masks[s] is None else jnp.where(masks[s], ln, 0.0) for s, ln in enumerate(log_nots)]
    his = [ln.astype(BF16) for ln in log_nots]
    los = [(log_nots[s] - his[s].astype(F32)).astype(BF16) for s in n]
    sufs = [_dot(his[s], suffix_ones) + _dot(los[s], suffix_ones) for s in n]
    rowsums = [jnp.sum(ln, axis=-1, keepdims=True) for ln in log_nots]
    return zs, sps, sufs, rowsums


def _sb_weighted(scores, s, carry, mask, v):
    zs, sps, sufs, _ = scores
    att = jnp.exp2(zs[s] - sps[s] + (sufs[s] + carry))
    if mask is not None:
        att = jnp.where(mask, att, 0.0)
    return _dot(att.astype(BF16), v)


def _sb_kernel(q_ref, k_ref, v_ref, o_ref, acc_ref, carry_ref, *, T):
    tq = SB_TILE
    nq = T // tq
    ti = lax.broadcasted_iota(jnp.int32, (tq, tq), 0)
    tj = lax.broadcasted_iota(jnp.int32, (tq, tq), 1)
    suffix_ones = (ti > tj).astype(BF16)
    causal = tj < ti
    scale = HEAD_DIM ** -0.5
    heads = [slice(hh * HEAD_DIM, (hh + 1) * HEAD_DIM) for hh in range(LANES // HEAD_DIM)]
    nh = len(heads)

    def q_body(i, _):
        r0 = pl.multiple_of(i * tq, tq)
        has_prev = jnp.where(i > 0, 1.0, 0.0)
        p0 = pl.multiple_of(jnp.maximum(i - 1, 0) * tq, tq)
        qs = [q_ref[pl.ds(r0, tq), cs] * scale for cs in heads]
        ks = [k_ref[pl.ds(r0, tq), cs] for cs in heads] + [k_ref[pl.ds(p0, tq), cs] for cs in heads]
        vs = [v_ref[pl.ds(r0, tq), cs] for cs in heads] + [v_ref[pl.ds(p0, tq), cs] for cs in heads]
        masks = [causal] * nh + [None] * nh
        scores = _sb_scores(qs + qs, ks, masks, suffix_ones)
        rss = scores[3]
        live = jnp.float32(-jnp.inf)
        for hh, cs in enumerate(heads):
            out = _sb_weighted(scores, hh, 0.0, causal, vs[hh])
            out_prev = _sb_weighted(scores, nh + hh, rss[hh], None, vs[nh + hh])
            acc_ref[:, cs] = out + has_prev * out_prev
            carry = rss[hh] + has_prev * rss[nh + hh]
            carry_ref[hh] = carry
            live = jnp.maximum(live, jnp.max(carry))

        def k_cond(state):
            jr, live = state
            return (jr < i) & (live > SB_EXP2_UNDERFLOW)

        def k_body(state):
            jr, _ = state
            c0 = pl.multiple_of((i - 1 - jr) * tq, tq)
            kt = [k_ref[pl.ds(c0, tq), cs] for cs in heads]
            vt = [v_ref[pl.ds(c0, tq), cs] for cs in heads]
            scores = _sb_scores(qs, kt, [None] * nh, suffix_ones)
            live = jnp.float32(-jnp.inf)
            for hh, cs in enumerate(heads):
                carry = carry_ref[hh]
                acc_ref[:, cs] += _sb_weighted(scores, hh, carry, None, vt[hh])
                carry = carry + scores[3][hh]
                carry_ref[hh] = carry
                live = jnp.maximum(live, jnp.max(carry))
            return jr + 1, live

        lax.while_loop(k_cond, k_body, (jnp.int32(1), live))
        o_ref[pl.ds(r0, tq), :] = acc_ref[...].astype(o_ref.dtype)
        return 0

    lax.fori_loop(0, nq, q_body, 0)


def sb_call(qkv, B, T):
    npair = SB_WIDTH // LANES
    blk = lambda off: pl.BlockSpec((None, T, LANES), lambda b, hp: (b, 0, off + hp))
    return pl.pallas_call(
        functools.partial(_sb_kernel, T=T),
        grid=(B, npair),
        in_specs=[blk(0), blk(npair), blk(2 * npair)],
        out_specs=pl.BlockSpec((None, T, LANES), lambda b, hp: (b, 0, hp)),
        out_shape=jax.ShapeDtypeStruct((B, T, SB_WIDTH), BF16),
        scratch_shapes=[pltpu.VMEM((SB_TILE, LANES), F32),
                        pltpu.VMEM((LANES // HEAD_DIM, SB_TILE, 1), F32)],
        compiler_params=pltpu.CompilerParams(dimension_semantics=("parallel", "parallel")),
        name="stick_breaking_attention",
    )(qkv, qkv, qkv)


def _route(n, w_ref, b_ref, id_ref, p_ref):
    nh = n.astype(BF16)
    nl = (n - nh.astype(F32)).astype(BF16)
    a = _dot(nh, w_ref[...])
    b = _dot(nl, w_ref[...])
    L = ROUTER_LANES
    logits = a[:, :L] + (a[:, L:] + b[:, :L] + b[:, L:]) + b_ref[...]
    lane = lax.broadcasted_iota(jnp.int32, logits.shape, 1)
    neg = jnp.float32(-jnp.inf)
    logits = jnp.where(lane < N_EXPERTS, logits, neg)
    m1 = jnp.max(logits, axis=-1, keepdims=True)
    i1 = jnp.min(jnp.where(logits == m1, lane, ROUTER_LANES), axis=-1, keepdims=True)
    rest = jnp.where(lane == i1, neg, logits)
    m2 = jnp.max(rest, axis=-1, keepdims=True)
    i2 = jnp.min(jnp.where(rest == m2, lane, ROUTER_LANES), axis=-1, keepdims=True)
    e2 = jnp.exp(m2 - m1)
    p1 = 1.0 / (1.0 + e2)
    p2 = e2 / (1.0 + e2)
    id_ref[...] = jnp.where(lane == 0, i1, jnp.where(lane == 1, i2, 0))
    p_ref[...] = jnp.where(lane == 0, p1, jnp.where(lane == 1, p2, 0.0))


def _merge_kernel(ygm_ref, yrw_ref, ysb_ref, gate_ref, h_ref, pgm_ref, prw_ref, psb_ref, wo_ref,
                  g_ref, *refs, route):
    D = D_MODEL
    merged = gate_ref[:, 0:D].astype(F32) * _dot(ygm_ref[...], pgm_ref[...])
    merged += gate_ref[:, D:2 * D].astype(F32) * _dot(yrw_ref[...], prw_ref[...])
    merged += gate_ref[:, 2 * D:3 * D].astype(F32) * _dot(ysb_ref[...], psb_ref[...])
    hn = h_ref[...] + _dot(merged.astype(BF16), wo_ref[...])
    n = _rms(hn, g_ref[...])
    if route:
        w_ref, b_ref, ho_ref, id_ref, p_ref = refs
        _route(n, w_ref, b_ref, id_ref, p_ref)
    else:
        ho_ref, no_ref = refs
        no_ref[...] = n.astype(no_ref.dtype)
    ho_ref[...] = hn


def _resident(shape):
    return pl.BlockSpec(shape, lambda i: (0,) * len(shape), pipeline_mode=pl.Buffered(1))


def merge_call(y_gm, y_rw, y_sb, gates, h, p_gm, p_rw, p_sb, w_o, g_next, router=None, tm=256):
    M, D = h.shape
    tm = min(tm, M)
    rowblk = lambda n: pl.BlockSpec((tm, n), lambda i: (i, 0))
    wbytes = 2 * (GM_WIDTH + RW_WIDTH + SB_WIDTH + D) * D
    need = wbytes + 2 * tm * (2 * (GM_WIDTH + RW_WIDTH + SB_WIDTH + GATE_COLS) + 4 * D + 4 * D + 2 * D) \
        + 6 * tm * D * 4
    in_specs = [rowblk(GM_WIDTH), rowblk(RW_WIDTH), rowblk(SB_WIDTH), rowblk(GATE_COLS), rowblk(D),
                _resident((GM_WIDTH, D)), _resident((RW_WIDTH, D)), _resident((SB_WIDTH, D)),
                _resident((D, D)), _resident((1, D))]
    args = [y_gm, y_rw, y_sb, gates, h, p_gm, p_rw, p_sb, w_o, g_next.reshape(1, D)]
    if router is None:
        out_specs = [rowblk(D), rowblk(D)]
        out_shape = [jax.ShapeDtypeStruct((M, D), F32), jax.ShapeDtypeStruct((M, D), BF16)]
    else:
        router_w, router_b = router
        wpad = jnp.zeros((D, ROUTER_LANES), F32).at[:, :N_EXPERTS].set(router_w)
        w_hi = wpad.astype(BF16)
        w_lo = (wpad - w_hi.astype(F32)).astype(BF16)
        bpad = jnp.zeros((1, ROUTER_LANES), F32).at[0, :N_EXPERTS].set(router_b)
        in_specs += [_resident((D, 2 * ROUTER_LANES)), _resident((1, ROUTER_LANES))]
        args += [jnp.concatenate([w_hi, w_lo], axis=1), bpad]
        out_specs = [rowblk(D), rowblk(ROUTER_LANES), rowblk(ROUTER_LANES)]
        out_shape = [jax.ShapeDtypeStruct((M, D), F32), jax.ShapeDtypeStruct((M, ROUTER_LANES), jnp.int32),
                     jax.ShapeDtypeStruct((M, ROUTER_LANES), F32)]
    return pl.pallas_call(
        functools.partial(_merge_kernel, route=router is not None),
        grid=(M // tm,),
        in_specs=in_specs,
        out_specs=out_specs,
        out_shape=out_shape,
        compiler_params=pltpu.CompilerParams(dimension_semantics=("parallel",),
                                             vmem_limit_bytes=_vmem_limit(need)),
        name="merge_out_proj",
    )(*args)


def _ffn_kernel(n_ref, h_ref, w1_ref, w3_ref, w2_ref, g_ref, *refs, emit_h, n_cast):
    cast_in, refs = refs[:n_cast], refs[n_cast:]
    if emit_h:
        ho_ref, no_ref = refs[:2]
        refs = refs[2:]
    else:
        no_ref = refs[0]
        refs = refs[1:]
    cast_out, acc_ref = refs[:n_cast], refs[n_cast]
    f = pl.program_id(1)

    @pl.when(f == 0)
    def _():
        acc_ref[...] = jnp.zeros_like(acc_ref)

    for src, dst in zip(cast_in, cast_out):
        dst[...] = src[...].astype(dst.dtype)

    n = n_ref[...]
    act = (jax.nn.silu(_dot(n, w1_ref[...])) * _dot(n, w3_ref[...])).astype(BF16)
    acc_ref[...] += _dot(act, w2_ref[...])

    @pl.when(f == pl.num_programs(1) - 1)
    def _():
        hn = h_ref[...] + acc_ref[...]
        if emit_h:
            ho_ref[...] = hn
        no_ref[...] = _rms(hn, g_ref[...]).astype(no_ref.dtype)


def ffn_steps(M, F, tm=512, tf=512):
    return (M // min(tm, M)) * (F // tf)


def can_ride(w, steps, D):
    return w.size % (steps * D) == 0 and (w.size // (steps * D)) % BF16_SUBLANES == 0


def ffn_call(n, h, w1, w3, w2, g_next, n_dtype, emit_h, riders=(), tm=512, tf=512):
    M, D = h.shape
    F = w1.shape[1]
    tm = min(tm, M)
    nf = F // tf
    steps = (M // tm) * nf
    rowblk = pl.BlockSpec((tm, D), lambda i, f: (i, 0))
    out_specs = [rowblk]
    out_shape = [jax.ShapeDtypeStruct((M, D), n_dtype)]
    if emit_h:
        out_specs = [rowblk, rowblk]
        out_shape = [jax.ShapeDtypeStruct((M, D), F32)] + out_shape
    n_main = len(out_shape)
    ride_in = [w.reshape(steps, w.size // (steps * D), D) for w in riders]
    ride_specs = [pl.BlockSpec((None, r.shape[1], D), lambda i, f: (i * nf + f, 0, 0)) for r in ride_in]
    need = 2 * tm * D * (2 + 4 + 4 * n_main) + tm * D * 4 + 2 * 3 * D * tf * 2 + 4 * tm * tf * 4 \
        + sum(2 * r.shape[1] * D * 6 for r in ride_in)
    outs = pl.pallas_call(
        functools.partial(_ffn_kernel, emit_h=emit_h, n_cast=len(riders)),
        grid=(M // tm, nf),
        in_specs=[rowblk, rowblk,
                  pl.BlockSpec((D, tf), lambda i, f: (0, f)),
                  pl.BlockSpec((D, tf), lambda i, f: (0, f)),
                  pl.BlockSpec((tf, D), lambda i, f: (f, 0)),
                  pl.BlockSpec((1, D), lambda i, f: (0, 0))] + ride_specs,
        out_specs=out_specs + ride_specs,
        out_shape=out_shape + [jax.ShapeDtypeStruct(r.shape, BF16) for r in ride_in],
        scratch_shapes=[pltpu.VMEM((tm, D), F32)],
        compiler_params=pltpu.CompilerParams(dimension_semantics=("parallel", "arbitrary"),
                                             vmem_limit_bytes=_vmem_limit(need + (8 << 20))),
        name="swiglu_ffn",
    )(n, h, w1, w3, w2, g_next.reshape(1, D), *ride_in)
    return outs[:n_main], [o.reshape(w.shape) for o, w in zip(outs[n_main:], riders)]


def moe_plan(ids, M, tm):
    E = N_EXPERTS
    n_tiles = -(-(TOP_K * M) // tm) + E + 2
    e_flat = ids[:, :TOP_K].reshape(-1)
    onehot = (e_flat[:, None] == jnp.arange(E, dtype=jnp.int32)[None, :]).astype(jnp.int32)
    csum = jnp.cumsum(onehot, axis=0)
    rank = jnp.sum((csum - onehot) * onehot, axis=1)
    counts = csum[-1]
    tiles_per_e = (counts + tm - 1) // tm
    tile_end = jnp.cumsum(tiles_per_e)
    offs = (tile_end - tiles_per_e) * tm
    dest = offs[e_flat] + rank
    pair = jnp.full((n_tiles * tm,), -1, jnp.int32).at[dest].set(
        jnp.arange(TOP_K * M, dtype=jnp.int32), unique_indices=True)
    spare = TOP_K * M + jnp.arange(n_tiles * tm, dtype=jnp.int32) % tm
    src_token = jnp.where(pair >= 0, pair // TOP_K, 0)
    out_row = jnp.where(pair >= 0, (pair % TOP_K) * M + pair // TOP_K, spare)
    prev_out_row = jnp.concatenate([spare[:tm], out_row[:-tm]])
    n_used = tile_end[-1]
    t = jnp.arange(n_tiles, dtype=jnp.int32)
    tile_e = jnp.searchsorted(tile_end, jnp.minimum(t, n_used - 1), side="right").astype(jnp.int32)
    return (src_token.reshape(n_tiles, 1, tm), prev_out_row.reshape(n_tiles, 1, tm), tile_e,
            n_used.astype(jnp.int32).reshape(1))


def _moe_kernel(tile_e_ref, n_used_ref, src_ref, nxt_ref, dst_ref, h_hbm, g_ref, w1_ref, w3_ref, w2_ref,
                y_hbm, xbuf, xb, acc, ostage, gsem, ssem):
    t = pl.program_id(0)
    f = pl.program_id(1)
    nf = pl.num_programs(1)
    tm = xb.shape[0]
    per_step = tm // nf
    n_used = n_used_ref[0]
    used = t < n_used
    cur = t % 2

    def fetch(idx_ref, r, slot):
        return pltpu.make_async_copy(h_hbm.at[pl.ds(idx_ref[0, r], 1)], xbuf.at[slot, pl.ds(r, 1)],
                                     gsem.at[slot])

    def put(r):
        return pltpu.make_async_copy(ostage.at[1 - cur, pl.ds(r, 1)], y_hbm.at[pl.ds(dst_ref[0, r], 1)],
                                     ssem.at[1 - cur])

    @pl.when((f == 0) & (t == 0))
    def _():
        def issue(r, c):
            fetch(src_ref, r, 0).start()
            return c
        lax.fori_loop(0, tm, issue, 0)
        ostage[1] = jnp.zeros((tm, ostage.shape[2]), F32)

    @pl.when((f == 0) & (t <= n_used))
    def _():
        pltpu.make_async_copy(h_hbm.at[pl.ds(0, tm)], xbuf.at[cur], gsem.at[cur]).wait()

    @pl.when((f == 0) & (t >= 1) & (t - 1 <= n_used))
    def _():
        pltpu.make_async_copy(ostage.at[cur], y_hbm.at[pl.ds(0, tm)], ssem.at[cur]).wait()

    @pl.when((f == 0) & used)
    def _():
        xb[...] = _rms(xbuf[cur], g_ref[...]).astype(BF16)
        acc[...] = jnp.zeros_like(acc)

    @pl.when(used)
    def _():
        for j in range(per_step):
            fetch(nxt_ref, f * per_step + j, 1 - cur).start()
            put(f * per_step + j).start()
        x = xb[...]
        act = jax.nn.silu(_dot(x, w1_ref[...])) * _dot(x, w3_ref[...])
        acc[...] += _dot(act.astype(BF16), w2_ref[...])

    @pl.when(used & (f == nf - 1))
    def _():
        ostage[cur] = acc[...]

    @pl.when((f == 0) & (t == n_used))
    def _():
        def issue(r, c):
            put(r).start()
            return c
        lax.fori_loop(0, tm, issue, 0)


def moe_call(h, g, src_token, prev_out_row, tile_e, n_used, w1, w3, w2, tm, tf):
    M, D = h.shape
    n_tiles = src_token.shape[0]
    E, _, Fe = w1.shape
    nf = Fe // tf
    assert tm % nf == 0 and Fe % tf == 0

    def wmap(t, f, tile_e_ref, n_used_ref):
        ff = jnp.where(t < n_used_ref[0], f, nf - 1)
        return tile_e_ref[t], ff

    smem_tile = lambda fn: pl.BlockSpec((None, 1, tm), fn, memory_space=pltpu.SMEM)
    grid_spec = pltpu.PrefetchScalarGridSpec(
        num_scalar_prefetch=2,
        grid=(n_tiles, nf),
        in_specs=[smem_tile(lambda t, f, te, nu: (t, 0, 0)),
                  smem_tile(lambda t, f, te, nu: (jnp.minimum(t + 1, n_tiles - 1), 0, 0)),
                  smem_tile(lambda t, f, te, nu: (t, 0, 0)),
                  pl.BlockSpec(memory_space=pl.ANY),
                  pl.BlockSpec((1, D), lambda t, f, te, nu: (0, 0)),
                  pl.BlockSpec((None, D, tf), lambda t, f, te, nu: (wmap(t, f, te, nu)[0], 0,
                                                                     wmap(t, f, te, nu)[1])),
                  pl.BlockSpec((None, D, tf), lambda t, f, te, nu: (wmap(t, f, te, nu)[0], 0,
                                                                     wmap(t, f, te, nu)[1])),
                  pl.BlockSpec((None, tf, D), lambda t, f, te, nu: wmap(t, f, te, nu) + (0,))],
        out_specs=pl.BlockSpec(memory_space=pl.ANY),
        scratch_shapes=[pltpu.VMEM((2, tm, D), F32), pltpu.VMEM((tm, D), BF16), pltpu.VMEM((tm, D), F32),
                        pltpu.VMEM((2, tm, D), F32),
                        pltpu.SemaphoreType.DMA((2,)), pltpu.SemaphoreType.DMA((2,))])
    need = tm * D * (8 + 2 + 4 + 8) + 2 * 3 * D * tf * 2 + 4 * tm * tf * 4
    return pl.pallas_call(
        _moe_kernel,
        grid_spec=grid_spec,
        out_shape=jax.ShapeDtypeStruct((TOP_K * M + tm, D), F32),
        compiler_params=pltpu.CompilerParams(dimension_semantics=("arbitrary", "arbitrary"),
                                             vmem_limit_bytes=_vmem_limit(need + (8 << 20))),
        name="moe_experts",
    )(tile_e, n_used, src_token, src_token, prev_out_row, h, g.reshape(1, D), w1, w3, w2)


def _moe_combine_kernel(h_ref, p_ref, g_ref, *refs, emit_h):
    y_refs, out_refs = refs[:TOP_K], refs[TOP_K:]
    hn = h_ref[...]
    for k in range(TOP_K):
        hn = hn + p_ref[:, k:k + 1] * y_refs[k][...]
    if emit_h:
        out_refs[0][...] = hn
    out_refs[-1][...] = _rms(hn, g_ref[...]).astype(out_refs[-1].dtype)


def moe_combine_call(h, y_tok, probs, g_next, n_dtype, emit_h, tm=MOE_COMBINE_TM):
    M, D = h.shape
    tm = min(tm, M)
    rowblk = pl.BlockSpec((tm, D), lambda i: (i, 0))
    out_specs = [rowblk]
    out_shape = [jax.ShapeDtypeStruct((M, D), n_dtype)]
    if emit_h:
        out_specs = [rowblk, rowblk]
        out_shape = [jax.ShapeDtypeStruct((M, D), F32)] + out_shape
    y_specs = [pl.BlockSpec((tm, D), functools.partial(lambda k, i: (k * (M // tm) + i, 0), k))
               for k in range(TOP_K)]
    return pl.pallas_call(
        functools.partial(_moe_combine_kernel, emit_h=emit_h),
        grid=(M // tm,),
        in_specs=[rowblk, pl.BlockSpec((tm, ROUTER_LANES), lambda i: (i, 0)),
                  pl.BlockSpec((1, D), lambda i: (0, 0))] + y_specs,
        out_specs=out_specs,
        out_shape=out_shape,
        compiler_params=pltpu.CompilerParams(dimension_semantics=("parallel",),
                                             vmem_limit_bytes=_vmem_limit(14 * tm * D * 4)),
        name="moe_combine",
    )(h, probs, g_next.reshape(1, D), *([y_tok] * TOP_K))


def kernel(x, norm_mix, w_in, gate_b, gm_ln_g, gm_ln_b, gm_ws, gm_bs, rw_mu, rw_w0, rw_w2, rw_a0, rw_a2,
           rw_g2, rw_kk, rw_ka, rw_rk, rw_ln_g, rw_ln_b, p_gm, p_rw, p_sb, w_o, norm_ffn, ffn_w1, ffn_w3,
           ffn_w2, router_w, router_b, moe_w1, moe_w3, moe_w2, norm_out):
    B, T, D = x.shape
    M = B * T
    bf = lambda a: a.astype(BF16)
    h = x.reshape(M, D)
    n = rmsnorm_call(h, norm_mix[0], BF16)
    moe_bf16 = {}
    for l in range(DEPTH):
        last = l == DEPTH - 1
        g_next = norm_out if last else norm_mix[l + 1]
        n_dtype = x.dtype if last else BF16
        w = w_in[l]
        j = l // 2
        uv, _ = matmul_call(n, bf(w[:, GM_OFF:RW_OFF]), BF16, tn=GM_COLS, name="in_proj_gm")
        p, _ = matmul_call(n, bf(w[:, RW_OFF:SB_OFF]), BF16, tn=RW_COLS, name="in_proj_rw")
        qkv, _ = matmul_call(n, bf(w[:, SB_OFF:GATE_OFF]), BF16, tn=1024, name="in_proj_sb")
        ffn_f32 = [wt[j] for wt in (ffn_w1, ffn_w3, ffn_w2)] if l % 2 == 0 else []
        ride = all(can_ride_rows(wt, row_tiles(M)) for wt in ffn_f32)
        gates, ffn_bf16 = matmul_call(n, bf(w[:, GATE_OFF:]), BF16, tn=1024, bias=gate_b[l].reshape(-1),
                                      riders=ffn_f32 if ride else (), name="in_proj_gate")
        if not ride:
            ffn_bf16 = [bf(wt) for wt in ffn_f32]
        y_gm = gm_call(uv, gm_ln_g[l], gm_ln_b[l], gm_ws[l], gm_bs[l])
        y_rw = rw_call(p.reshape(B, T, RW_COLS), B, T, rw_mu[l], rw_w0[l], rw_w2[l], rw_a0[l], rw_a2[l],
                       rw_g2[l], rw_kk[l], rw_ka[l], rw_rk[l], rw_ln_g[l], rw_ln_b[l]).reshape(M, RW_WIDTH)
        y_sb = sb_call(qkv.reshape(B, T, SB_COLS), B, T).reshape(M, SB_WIDTH)
        merge_args = (y_gm, y_rw, y_sb, gates, h, bf(p_gm[l]), bf(p_rw[l]), bf(p_sb[l]), bf(w_o[l]), norm_ffn[l])
        if l % 2 == 0:
            h, n2 = merge_call(*merge_args)
            ahead = []
            if not last:
                ahead = [(k, wt[(l + 1) // 2]) for k, wt in (("w1", moe_w1), ("w3", moe_w3), ("w2", moe_w2))
                         if can_ride(wt[(l + 1) // 2], ffn_steps(M, D_FF), D)]
            outs, casts = ffn_call(n2, h, *ffn_bf16, g_next, n_dtype,
                                   emit_h=not last, riders=[wt for _, wt in ahead])
            moe_bf16 = {k: c for (k, _), c in zip(ahead, casts)}
        else:
            h, ids, probs = merge_call(*merge_args, router=(router_w[j], router_b[j]))
            src_token, prev_out_row, tile_e, n_used = moe_plan(ids, M, MOE_TM)
            ew = {k: moe_bf16[k] if k in moe_bf16 else bf(wt[j])
                  for k, wt in (("w1", moe_w1), ("w3", moe_w3), ("w2", moe_w2))}
            y_tok = moe_call(h, norm_ffn[l], src_token, prev_out_row, tile_e, n_used, ew["w1"], ew["w3"],
                             ew["w2"], MOE_TM, MOE_TF)
            outs = moe_combine_call(h, y_tok, probs, g_next, n_dtype, emit_h=not last)
            moe_bf16 = {}
        if last:
            n = outs[0]
        else:
            h, n = outs
    return n.reshape(B, T, D)
```

```python
import functools

import jax
import jax.numpy as jnp
from jax import lax
from jax.experimental import pallas as pl
from jax.experimental.pallas import tpu as pltpu

F32 = jnp.float32
BF16 = jnp.bfloat16

D_MODEL = 2048
DEPTH = 2
HEAD_DIM = 64
GM_HEADS = 8
GM_WIDTH = GM_HEADS * HEAD_DIM
CHUNK = 128
RW_HEADS = 8
RW_WIDTH = RW_HEADS * HEAD_DIM
DECAY_LORA = 64
AAA_LORA = 64
GATE_LORA = 128
SB_HEADS = 16
SB_WIDTH = SB_HEADS * HEAD_DIM
N_BRANCH = 3
D_FF = 5632
N_EXPERTS = 8
TOP_K = 2
D_FF_EXPERT = D_FF // TOP_K
RMS_EPS = 1e-6
LN_EPS = 1e-5
RW_GN_EPS = 64e-5

GM_COLS = 2 * GM_WIDTH
RW_COLS = 3 * RW_WIDTH + DECAY_LORA + AAA_LORA + GATE_LORA
SB_COLS = 3 * SB_WIDTH
GATE_COLS = N_BRANCH * D_MODEL
GM_OFF = 0
RW_OFF = GM_OFF + GM_COLS
SB_OFF = RW_OFF + RW_COLS
GATE_OFF = SB_OFF + SB_COLS
N_IN = GATE_OFF + GATE_COLS

V7X_VMEM_BYTES = 64 * 1024 * 1024
LANES = 128
BF16_SUBLANES = 16
RW_CHUNK = 64
RW_TILE = 256
RW_GROUP = 4
SB_TILE = 256
LOG2_E = 1.4426950408889634
SB_EXP2_UNDERFLOW = -151.0
ROUTER_LANES = 128
MOE_TF = 256
MOE_TM = 528
MOE_COMBINE_TM = 256


def _vmem_limit(nbytes):
    return int(min(max(nbytes, 32 * 1024 * 1024), V7X_VMEM_BYTES - 6 * 1024 * 1024))


def _dot(a, b):
    return jnp.dot(a, b, preferred_element_type=F32)


def _dot_nt(a, b):
    return lax.dot_general(a, b, (((1,), (1,)), ((), ())), preferred_element_type=F32)


def _dot_tn(a, b):
    return lax.dot_general(a, b, (((0,), (0,)), ((), ())), preferred_element_type=F32)


def _dot_split(x, ones_bf16):
    hi = x.astype(BF16)
    lo = (x - hi.astype(F32)).astype(BF16)
    return _dot(hi, ones_bf16) + _dot(lo, ones_bf16)


def _rms(x, g):
    ms = jnp.mean(x * x, axis=-1, keepdims=True)
    return x * lax.rsqrt(ms + RMS_EPS) * g


def _rmsnorm_kernel(x_ref, g_ref, o_ref):
    o_ref[...] = _rms(x_ref[...], g_ref[...]).astype(o_ref.dtype)


def rmsnorm_call(x, g, out_dtype, tm=512):
    M, D = x.shape
    tm = min(tm, M)
    return pl.pallas_call(
        _rmsnorm_kernel,
        grid=(M // tm,),
        in_specs=[pl.BlockSpec((tm, D), lambda i: (i, 0)),
                  pl.BlockSpec((1, D), lambda i: (0, 0))],
        out_specs=pl.BlockSpec((tm, D), lambda i: (i, 0)),
        out_shape=jax.ShapeDtypeStruct((M, D), out_dtype),
        compiler_params=pltpu.CompilerParams(dimension_semantics=("parallel",)),
        name="rmsnorm",
    )(x, g.reshape(1, D))


def _mm_kernel(a_ref, w_ref, o_ref):
    o_ref[...] = _dot(a_ref[...], w_ref[...]).astype(o_ref.dtype)


def _mm_gate_kernel(a_ref, w_ref, b_ref, o_ref):
    o_ref[...] = jax.nn.sigmoid(_dot(a_ref[...], w_ref[...]) + b_ref[...]).astype(o_ref.dtype)


def matmul_call(a, w, out_dtype, tn, bias=None, tm=1024, name="matmul"):
    M, K = a.shape
    N = w.shape[1]
    tm = min(tm, M)
    in_specs = [pl.BlockSpec((tm, K), lambda i, j: (i, 0)),
                pl.BlockSpec((K, tn), lambda i, j: (0, j))]
    args = [a, w]
    kern = _mm_kernel
    if bias is not None:
        in_specs.append(pl.BlockSpec((1, tn), lambda i, j: (0, j)))
        args.append(bias.reshape(1, N))
        kern = _mm_gate_kernel
    osz = jnp.dtype(out_dtype).itemsize
    need = 2 * (tm * K * 2 + K * tn * 2 + tm * tn * osz) + 2 * tm * tn * 4
    return pl.pallas_call(
        kern,
        grid=(M // tm, N // tn),
        in_specs=in_specs,
        out_specs=pl.BlockSpec((tm, tn), lambda i, j: (i, j)),
        out_shape=jax.ShapeDtypeStruct((M, N), out_dtype),
        compiler_params=pltpu.CompilerParams(
            dimension_semantics=("parallel", "arbitrary"),
            vmem_limit_bytes=_vmem_limit(need + (8 << 20))),
        name=name,
    )(*args)


def _gm_kernel(uv_ref, lng_ref, lnb_ref, ws_ref, bias_ref, o_ref, *, n_chunks):
    ii = lax.broadcasted_iota(jnp.int32, (CHUNK, CHUNK), 0)
    jj = lax.broadcasted_iota(jnp.int32, (CHUNK, CHUNK), 1)
    causal = jj <= ii
    ws = [jnp.where(causal, ws_ref[h], 0.0).astype(BF16) for h in range(GM_HEADS)]
    for c in range(n_chunks):
        rows = slice(c * CHUNK, (c + 1) * CHUNK)
        u = jax.nn.gelu(uv_ref[rows, 0:GM_WIDTH].astype(F32))
        v = jax.nn.gelu(uv_ref[rows, GM_WIDTH:2 * GM_WIDTH].astype(F32))
        mu = jnp.mean(v, axis=-1, keepdims=True)
        vc = v - mu
        var = jnp.mean(vc * vc, axis=-1, keepdims=True)
        vn = (vc * lax.rsqrt(var + LN_EPS) * lng_ref[...] + lnb_ref[...]).astype(BF16)
        mixed = jnp.concatenate(
            [_dot(ws[h], vn[:, h * HEAD_DIM:(h + 1) * HEAD_DIM]) for h in range(GM_HEADS)], axis=-1)
        o_ref[rows, :] = (u * (mixed + bias_ref[...])).astype(o_ref.dtype)


def gm_call(uv, ln_g, ln_b, w_s, b_s, n_chunks=2):
    M = uv.shape[0]
    tt = CHUNK * n_chunks
    bias = jnp.repeat(b_s.T, HEAD_DIM, axis=1)
    return pl.pallas_call(
        functools.partial(_gm_kernel, n_chunks=n_chunks),
        grid=(M // tt,),
        in_specs=[pl.BlockSpec((tt, GM_COLS), lambda i: (i, 0)),
                  pl.BlockSpec((1, GM_WIDTH), lambda i: (0, 0)),
                  pl.BlockSpec((1, GM_WIDTH), lambda i: (0, 0)),
                  pl.BlockSpec((GM_HEADS, CHUNK, CHUNK), lambda i: (0, 0, 0)),
                  pl.BlockSpec((CHUNK, GM_WIDTH), lambda i: (0, 0))],
        out_specs=pl.BlockSpec((tt, GM_WIDTH), lambda i: (i, 0)),
        out_shape=jax.ShapeDtypeStruct((M, GM_WIDTH), BF16),
        compiler_params=pltpu.CompilerParams(dimension_semantics=("parallel",)),
        name="gmlp_gating",
    )(uv, ln_g.reshape(1, -1), ln_b.reshape(1, -1), w_s, bias)


def _rw_kernel(p_ref, prev_ref, mu_ref, w0_ref, w2_ref, a0_ref, a2_ref, g2_ref, kk_ref, ka_ref,
               rk_ref, lng_ref, lnb_ref, hsum_ref, o_ref, st_ref, y_ref):
    C = RW_CHUNK
    i = pl.program_id(1)

    @pl.when(i == 0)
    def _():
        st_ref[...] = jnp.zeros_like(st_ref)

    hsum = hsum_ref[...]
    gw = hsum.shape[0]

    def head_sums(x):
        return jnp.concatenate([_dot_split(x[:, q * gw:(q + 1) * gw], hsum)
                                for q in range(x.shape[1] // gw)], axis=1)

    p = p_ref[...].astype(F32)
    prev_row = prev_ref[BF16_SUBLANES - 1:BF16_SUBLANES, :].astype(F32)
    prev_row = jnp.where(i == 0, 0.0, prev_row)
    row = lax.broadcasted_iota(jnp.int32, p.shape, 0)
    p_prev = jnp.where(row == 0, prev_row, pltpu.roll(p, 1, axis=0))
    ps = p + (p_prev - p) * mu_ref[...]
    W = RW_WIDTH
    r = ps[:, 0:W]
    k = ps[:, W:2 * W]
    v = ps[:, 2 * W:3 * W]
    wl = ps[:, 3 * W:3 * W + DECAY_LORA]
    al = ps[:, 3 * W + DECAY_LORA:3 * W + DECAY_LORA + AAA_LORA]
    gl = ps[:, 3 * W + DECAY_LORA + AAA_LORA:RW_COLS]

    d = w0_ref[...] + _dot(jnp.tanh(wl).astype(BF16), w2_ref[...])
    log_w = -jnp.exp(-jax.nn.softplus(-d) - 0.5)
    a = jax.nn.sigmoid(a0_ref[...] + _dot(al.astype(BF16), a2_ref[...]))
    g = _dot(jax.nn.sigmoid(gl).astype(BF16), g2_ref[...])
    kk = k * kk_ref[...]
    kk = kk / jnp.maximum(jnp.sqrt(head_sums(kk * kk)), 1e-12)
    k = k * (1.0 + (a - 1.0) * ka_ref[...])
    bonus = head_sums(r * k * rk_ref[...]) * v

    HD = HEAD_DIM
    GW = RW_GROUP * HD
    ci = lax.broadcasted_iota(jnp.int32, (C, C), 0)
    cj = lax.broadcasted_iota(jnp.int32, (C, C), 1)
    tri_incl = (cj <= ci).astype(BF16)
    ti = lax.broadcasted_iota(jnp.int32, (C, GW), 0)
    tj = lax.broadcasted_iota(jnp.int32, (C, GW), 1) % HD
    strict = tj < ti
    incl = tj <= ti
    eye = ti == tj
    blk_masks = []
    s = 2
    while s < C:
        blk_masks.append((ti // (2 * s) == tj // (2 * s)) & (ti // s != tj // s))
        s *= 2
    pair_mask = (ti // 2 == tj // 2) & strict
    gi = lax.broadcasted_iota(jnp.int32, (GW, GW), 0)
    gj = lax.broadcasted_iota(jnp.int32, (GW, GW), 1)
    bd_mask = gi // HD == gj // HD
    bd_mask_b = jnp.where(bd_mask, 1.0, 0.0).astype(BF16)
    gw_eye = gi == gj

    def bd(x):
        xb = x.astype(BF16)
        return jnp.where(bd_mask_b > 0, jnp.concatenate([xb] * RW_GROUP, axis=0), 0)

    def bd_of_full(full):
        return jnp.where(bd_mask, full, 0.0)

    n_chunks = RW_TILE // C
    n_groups = RW_WIDTH // GW
    systems = [(c, q) for c in range(n_chunks) for q in range(n_groups)]

    pre = []
    for c in range(n_chunks):
        rows = slice(c * C, (c + 1) * C)
        lw = log_w[rows]
        cum = _cumsum_rows(lw, tri_incl)
        gam = jnp.exp(cum)
        gam_prev = jnp.exp(cum - lw)
        gam_inv = jnp.exp(-cum)
        g_end = gam[C - 1:C, :]
        bh_f = kk[rows] * a[rows] * gam_inv
        kh_f = k[rows] * gam_inv
        pre.append(dict(
            rh=(r[rows] * gam).astype(BF16), ah=(-kk[rows] * gam_prev).astype(BF16),
            bh=bh_f.astype(BF16), kh=kh_f.astype(BF16),
            bt=(bh_f * g_end).astype(BF16), kt=(kh_f * g_end).astype(BF16),
            v=v[rows].astype(BF16), g_end=g_end))

    def grp(c, q, name):
        return pre[c][name][:, q * GW:(q + 1) * GW]

    st = {}
    for (c, q) in systems:
        ar = jnp.concatenate([grp(c, q, "ah"), grp(c, q, "rh")], axis=0)
        pb = _dot_nt(ar, bd(grp(c, q, "bh")))
        pk = _dot_nt(ar, bd(grp(c, q, "kh")))
        lab = jnp.where(strict, pb[0:C], 0.0)
        st[(c, q)] = dict(
            lab=lab, mrb=jnp.where(incl, pb[C:2 * C], 0.0).astype(BF16),
            lm=jnp.concatenate([jnp.where(strict, pk[0:C], 0.0), jnp.where(incl, pk[C:2 * C], 0.0)],
                               axis=0).astype(BF16),
            tinv=jnp.where(eye, 1.0, jnp.where(pair_mask, lab, 0.0)))

    for bm in blk_masks:
        xs = {}
        for key in systems:
            d_ = st[key]
            xs[key] = _dot(d_["tinv"].astype(BF16), bd(jnp.where(bm, d_["lab"], 0.0))).astype(BF16)
        for key in systems:
            d_ = st[key]
            d_["tinv"] = d_["tinv"] + _dot(xs[key], bd(d_["tinv"]))

    for (c, q) in systems:
        d_ = st[(c, q)]
        lv = _dot(d_["lm"], bd(grp(c, q, "v")))
        tb = d_["tinv"].astype(BF16)
        ap = _dot(tb, bd(grp(c, q, "ah"))).astype(BF16)
        up = _dot(tb, bd(lv[0:C])).astype(BF16)
        d_["rp"] = (grp(c, q, "rh").astype(F32) + _dot(d_["mrb"], bd(ap))).astype(BF16)
        d_["ypp"] = _dot(d_["mrb"], bd(up)) + lv[C:2 * C]
        g_end = pre[c]["g_end"][:, q * GW:(q + 1) * GW]
        gt = _dot_tn(ap, grp(c, q, "bt"))
        d_["gt"] = jnp.where(gw_eye, gt + g_end, bd_of_full(gt)).astype(BF16)
        ht = bd_of_full(_dot_tn(up, grp(c, q, "bt")) + _dot_tn(grp(c, q, "v"), grp(c, q, "kt")))
        d_["ht"] = sum(ht[hh * HD:(hh + 1) * HD] for hh in range(1, RW_GROUP)) + ht[0:HD]

    for q in range(n_groups):
        s_cur = st_ref[q]
        for c in range(n_chunks):
            d_ = st[(c, q)]
            y_ref[c * C:(c + 1) * C, q * GW:(q + 1) * GW] = _dot_nt(d_["rp"], bd(s_cur)) + d_["ypp"]
            s_cur = _dot(s_cur.astype(BF16), d_["gt"]) + d_["ht"]
        st_ref[q] = s_cur

    y = y_ref[...]
    inv_n = 1.0 / HEAD_DIM
    m = head_sums(y) * inv_n
    yc = y - m
    var = head_sums(yc * yc) * inv_n
    yn = yc * lax.rsqrt(var + RW_GN_EPS) * lng_ref[...] + lnb_ref[...]
    o_ref[...] = ((yn + bonus) * g).astype(o_ref.dtype)


def _cumsum_rows(x, tri_incl_bf16):
    hi = x.astype(BF16)
    lo = (x - hi.astype(F32)).astype(BF16)
    return _dot(tri_incl_bf16, hi) + _dot(tri_incl_bf16, lo)


def rw_call(p, B, T, mu, w0, w2, a0, a2, g2, k_k, k_a, r_k, ln_g, ln_b):
    nt = T // RW_TILE
    sub = RW_TILE // BF16_SUBLANES
    hid = jnp.arange(RW_GROUP * HEAD_DIM) // HEAD_DIM
    hsum = (hid[:, None] == hid[None, :]).astype(BF16)
    row = lambda a: a.reshape(1, -1).astype(F32)
    vec = lambda n: pl.BlockSpec((1, n), lambda b, i: (0, 0))
    return pl.pallas_call(
        _rw_kernel,
        grid=(B, nt),
        in_specs=[pl.BlockSpec((None, RW_TILE, RW_COLS), lambda b, i: (b, i, 0)),
                  pl.BlockSpec((None, BF16_SUBLANES, RW_COLS),
                               lambda b, i: (b, jnp.maximum(i * sub - 1, 0), 0)),
                  vec(RW_COLS), vec(RW_WIDTH),
                  pl.BlockSpec((DECAY_LORA, RW_WIDTH), lambda b, i: (0, 0)),
                  vec(RW_WIDTH),
                  pl.BlockSpec((AAA_LORA, RW_WIDTH), lambda b, i: (0, 0)),
                  pl.BlockSpec((GATE_LORA, RW_WIDTH), lambda b, i: (0, 0)),
                  vec(RW_WIDTH), vec(RW_WIDTH), vec(RW_WIDTH), vec(RW_WIDTH), vec(RW_WIDTH),
                  pl.BlockSpec((RW_GROUP * HEAD_DIM, RW_GROUP * HEAD_DIM), lambda b, i: (0, 0))],
        out_specs=pl.BlockSpec((None, RW_TILE, RW_WIDTH), lambda b, i: (b, i, 0)),
        out_shape=jax.ShapeDtypeStruct((B, T, RW_WIDTH), BF16),
        scratch_shapes=[pltpu.VMEM((RW_HEADS // RW_GROUP, HEAD_DIM, RW_GROUP * HEAD_DIM), F32),
                        pltpu.VMEM((RW_TILE, RW_WIDTH), F32)],
        compiler_params=pltpu.CompilerParams(dimension_semantics=("parallel", "arbitrary")),
        name="rwkv7_time_mix",
    )(p, p, row(mu), row(w0), w2.astype(BF16), row(a0), a2.astype(BF16), g2.astype(BF16),
      row(k_k), row(k_a), row(r_k), row(ln_g), row(ln_b), hsum)


def _sb_scores(qs, ks, masks, suffix_ones):
    n = range(len(qs))
    zs = [_dot_nt(qs[s], ks[s]) * LOG2_E for s in n]
    sps = [jnp.maximum(z, 0.0) + jnp.log2(1.0 + jnp.exp2(-jnp.abs(z))) for z in zs]
    log_nots = [-sp for sp in sps]
    log_nots = [ln if masks[s] is None else jnp.where(masks[s], ln, 0.0) for s, ln in enumerate(log_nots)]
    his = [ln.astype(BF16) for ln in log_nots]
    los = [(log_nots[s] - his[s].astype(F32)).astype(BF16) for s in n]
    sufs = [_dot(his[s], suffix_ones) + _dot(los[s], suffix_ones) for s in n]
    rowsums = [jnp.sum(ln, axis=-1, keepdims=True) for ln in log_nots]
    return zs, sps, sufs, rowsums


def _sb_weighted(scores, s, carry, mask, v):
    zs, sps, sufs, _ = scores
    att = jnp.exp2(zs[s] - sps[s] + (sufs[s] + carry))
    if mask is not None:
        att = jnp.where(mask, att, 0.0)
    return _dot(att.astype(BF16), v)


def _sb_kernel(q_ref, k_ref, v_ref, o_ref, acc_ref, carry_ref, *, T):
    tq = SB_TILE
    nq = T // tq
    ti = lax.broadcasted_iota(jnp.int32, (tq, tq), 0)
    tj = lax.broadcasted_iota(jnp.int32, (tq, tq), 1)
    suffix_ones = (ti > tj).astype(BF16)
    causal = tj < ti
    scale = HEAD_DIM ** -0.5
    heads = [slice(hh * HEAD_DIM, (hh + 1) * HEAD_DIM) for hh in range(LANES // HEAD_DIM)]
    nh = len(heads)

    def q_body(i, _):
        r0 = pl.multiple_of(i * tq, tq)
        has_prev = jnp.where(i > 0, 1.0, 0.0)
        p0 = pl.multiple_of(jnp.maximum(i - 1, 0) * tq, tq)
        qs = [q_ref[pl.ds(r0, tq), cs] * scale for cs in heads]
        ks = [k_ref[pl.ds(r0, tq), cs] for cs in heads] + [k_ref[pl.ds(p0, tq), cs] for cs in heads]
        vs = [v_ref[pl.ds(r0, tq), cs] for cs in heads] + [v_ref[pl.ds(p0, tq), cs] for cs in heads]
        masks = [causal] * nh + [None] * nh
        scores = _sb_scores(qs + qs, ks, masks, suffix_ones)
        rss = scores[3]
        live = jnp.float32(-jnp.inf)
        for hh, cs in enumerate(heads):
            out = _sb_weighted(scores, hh, 0.0, causal, vs[hh])
            out_prev = _sb_weighted(scores, nh + hh, rss[hh], None, vs[nh + hh])
            acc_ref[:, cs] = out + has_prev * out_prev
            carry = rss[hh] + has_prev * rss[nh + hh]
            carry_ref[hh] = carry
            live = jnp.maximum(live, jnp.max(carry))

        def k_cond(state):
            jr, live = state
            return (jr < i) & (live > SB_EXP2_UNDERFLOW)

        def k_body(state):
            jr, _ = state
            c0 = pl.multiple_of((i - 1 - jr) * tq, tq)
            kt = [k_ref[pl.ds(c0, tq), cs] for cs in heads]
            vt = [v_ref[pl.ds(c0, tq), cs] for cs in heads]
            scores = _sb_scores(qs, kt, [None] * nh, suffix_ones)
            live = jnp.float32(-jnp.inf)
            for hh, cs in enumerate(heads):
                carry = carry_ref[hh]
                acc_ref[:, cs] += _sb_weighted(scores, hh, carry, None, vt[hh])
                carry = carry + scores[3][hh]
                carry_ref[hh] = carry
                live = jnp.maximum(live, jnp.max(carry))
            return jr + 1, live

        lax.while_loop(k_cond, k_body, (jnp.int32(1), live))
        o_ref[pl.ds(r0, tq), :] = acc_ref[...].astype(o_ref.dtype)
        return 0

    lax.fori_loop(0, nq, q_body, 0)


def sb_call(qkv, B, T):
    npair = SB_WIDTH // LANES
    blk = lambda off: pl.BlockSpec((None, T, LANES), lambda b, hp: (b, 0, off + hp))
    return pl.pallas_call(
        functools.partial(_sb_kernel, T=T),
        grid=(B, npair),
        in_specs=[blk(0), blk(npair), blk(2 * npair)],
        out_specs=pl.BlockSpec((None, T, LANES), lambda b, hp: (b, 0, hp)),
        out_shape=jax.ShapeDtypeStruct((B, T, SB_WIDTH), BF16),
        scratch_shapes=[pltpu.VMEM((SB_TILE, LANES), F32),
                        pltpu.VMEM((LANES // HEAD_DIM, SB_TILE, 1), F32)],
        compiler_params=pltpu.CompilerParams(dimension_semantics=("parallel", "parallel")),
        name="stick_breaking_attention",
    )(qkv, qkv, qkv)


def _route(n, w_ref, b_ref, id_ref, p_ref):
    nh = n.astype(BF16)
    nl = (n - nh.astype(F32)).astype(BF16)
    a = _dot(nh, w_ref[...])
    b = _dot(nl, w_ref[...])
    L = ROUTER_LANES
    logits = a[:, :L] + (a[:, L:] + b[:, :L] + b[:, L:]) + b_ref[...]
    lane = lax.broadcasted_iota(jnp.int32, logits.shape, 1)
    neg = jnp.float32(-jnp.inf)
    logits = jnp.where(lane < N_EXPERTS, logits, neg)
    m1 = jnp.max(logits, axis=-1, keepdims=True)
    i1 = jnp.min(jnp.where(logits == m1, lane, ROUTER_LANES), axis=-1, keepdims=True)
    rest = jnp.where(lane == i1, neg, logits)
    m2 = jnp.max(rest, axis=-1, keepdims=True)
    i2 = jnp.min(jnp.where(rest == m2, lane, ROUTER_LANES), axis=-1, keepdims=True)
    e2 = jnp.exp(m2 - m1)
    p1 = 1.0 / (1.0 + e2)
    p2 = e2 / (1.0 + e2)
    id_ref[...] = jnp.where(lane == 0, i1, jnp.where(lane == 1, i2, 0))
    p_ref[...] = jnp.where(lane == 0, p1, jnp.where(lane == 1, p2, 0.0))


def _merge_kernel(ygm_ref, yrw_ref, ysb_ref, gate_ref, h_ref, pgm_ref, prw_ref, psb_ref, wo_ref,
                  g_ref, *refs, route):
    D = D_MODEL
    merged = gate_ref[:, 0:D].astype(F32) * _dot(ygm_ref[...], pgm_ref[...])
    merged += gate_ref[:, D:2 * D].astype(F32) * _dot(yrw_ref[...], prw_ref[...])
    merged += gate_ref[:, 2 * D:3 * D].astype(F32) * _dot(ysb_ref[...], psb_ref[...])
    hn = h_ref[...] + _dot(merged.astype(BF16), wo_ref[...])
    n = _rms(hn, g_ref[...])
    if route:
        w_ref, b_ref, ho_ref, id_ref, p_ref = refs
        _route(n, w_ref, b_ref, id_ref, p_ref)
    else:
        ho_ref, no_ref = refs
        no_ref[...] = n.astype(no_ref.dtype)
    ho_ref[...] = hn


def _resident(shape):
    return pl.BlockSpec(shape, lambda i: (0,) * len(shape), pipeline_mode=pl.Buffered(1))


def merge_call(y_gm, y_rw, y_sb, gates, h, p_gm, p_rw, p_sb, w_o, g_next, router=None, tm=256):
    M, D = h.shape
    tm = min(tm, M)
    rowblk = lambda n: pl.BlockSpec((tm, n), lambda i: (i, 0))
    wbytes = 2 * (GM_WIDTH + RW_WIDTH + SB_WIDTH + D) * D
    need = wbytes + 2 * tm * (2 * (GM_WIDTH + RW_WIDTH + SB_WIDTH + GATE_COLS) + 4 * D + 4 * D + 2 * D) \
        + 6 * tm * D * 4
    in_specs = [rowblk(GM_WIDTH), rowblk(RW_WIDTH), rowblk(SB_WIDTH), rowblk(GATE_COLS), rowblk(D),
                _resident((GM_WIDTH, D)), _resident((RW_WIDTH, D)), _resident((SB_WIDTH, D)),
                _resident((D, D)), _resident((1, D))]
    args = [y_gm, y_rw, y_sb, gates, h, p_gm, p_rw, p_sb, w_o, g_next.reshape(1, D)]
    if router is None:
        out_specs = [rowblk(D), rowblk(D)]
        out_shape = [jax.ShapeDtypeStruct((M, D), F32), jax.ShapeDtypeStruct((M, D), BF16)]
    else:
        router_w, router_b = router
        wpad = jnp.zeros((D, ROUTER_LANES), F32).at[:, :N_EXPERTS].set(router_w)
        w_hi = wpad.astype(BF16)
        w_lo = (wpad - w_hi.astype(F32)).astype(BF16)
        bpad = jnp.zeros((1, ROUTER_LANES), F32).at[0, :N_EXPERTS].set(router_b)
        in_specs += [_resident((D, 2 * ROUTER_LANES)), _resident((1, ROUTER_LANES))]
        args += [jnp.concatenate([w_hi, w_lo], axis=1), bpad]
        out_specs = [rowblk(D), rowblk(ROUTER_LANES), rowblk(ROUTER_LANES)]
        out_shape = [jax.ShapeDtypeStruct((M, D), F32), jax.ShapeDtypeStruct((M, ROUTER_LANES), jnp.int32),
                     jax.ShapeDtypeStruct((M, ROUTER_LANES), F32)]
    return pl.pallas_call(
        functools.partial(_merge_kernel, route=router is not None),
        grid=(M // tm,),
        in_specs=in_specs,
        out_specs=out_specs,
        out_shape=out_shape,
        compiler_params=pltpu.CompilerParams(dimension_semantics=("parallel",),
                                             vmem_limit_bytes=_vmem_limit(need)),
        name="merge_out_proj",
    )(*args)


def _ffn_kernel(n_ref, h_ref, w1_ref, w3_ref, w2_ref, g_ref, *refs, emit_h, n_cast):
    cast_in, refs = refs[:n_cast], refs[n_cast:]
    if emit_h:
        ho_ref, no_ref = refs[:2]
        refs = refs[2:]
    else:
        no_ref = refs[0]
        refs = refs[1:]
    cast_out, acc_ref = refs[:n_cast], refs[n_cast]
    f = pl.program_id(1)

    @pl.when(f == 0)
    def _():
        acc_ref[...] = jnp.zeros_like(acc_ref)

    for src, dst in zip(cast_in, cast_out):
        dst[...] = src[...].astype(dst.dtype)

    n = n_ref[...]
    act = (jax.nn.silu(_dot(n, w1_ref[...])) * _dot(n, w3_ref[...])).astype(BF16)
    acc_ref[...] += _dot(act, w2_ref[...])

    @pl.when(f == pl.num_programs(1) - 1)
    def _():
        hn = h_ref[...] + acc_ref[...]
        if emit_h:
            ho_ref[...] = hn
        no_ref[...] = _rms(hn, g_ref[...]).astype(no_ref.dtype)


def ffn_steps(M, F, tm=512, tf=512):
    return (M // min(tm, M)) * (F // tf)


RIDER_SLAB_ROWS = 64


def can_ride(w, steps):
    return w.shape[1] % RIDER_SLAB_ROWS == 0 and w.shape[0] * (w.shape[1] // RIDER_SLAB_ROWS) <= steps


def ffn_call(n, h, w1, w3, w2, g_next, n_dtype, emit_h, riders=(), tm=512, tf=512):
    M, D = h.shape
    F = w1.shape[1]
    tm = min(tm, M)
    nf = F // tf
    steps = (M // tm) * nf
    rowblk = pl.BlockSpec((tm, D), lambda i, f: (i, 0))
    out_specs = [rowblk]
    out_shape = [jax.ShapeDtypeStruct((M, D), n_dtype)]
    if emit_h:
        out_specs = [rowblk, rowblk]
        out_shape = [jax.ShapeDtypeStruct((M, D), F32)] + out_shape
    n_main = len(out_shape)

    def slab_spec(w):
        per = w.shape[1] // RIDER_SLAB_ROWS
        n_slabs = w.shape[0] * per
        hold = steps // n_slabs

        def index(i, f):
            k = jnp.minimum((i * nf + f) // hold, n_slabs - 1)
            return k // per, k % per, 0
        return pl.BlockSpec((None, RIDER_SLAB_ROWS, w.shape[2]), index)

    ride_specs = [slab_spec(w) for w in riders]
    need = 2 * tm * D * (2 + 4 + 4 * n_main) + tm * D * 4 + 2 * 3 * D * tf * 2 + 4 * tm * tf * 4 \
        + sum(2 * RIDER_SLAB_ROWS * w.shape[2] * 6 for w in riders)
    outs = pl.pallas_call(
        functools.partial(_ffn_kernel, emit_h=emit_h, n_cast=len(riders)),
        grid=(M // tm, nf),
        in_specs=[rowblk, rowblk,
                  pl.BlockSpec((D, tf), lambda i, f: (0, f)),
                  pl.BlockSpec((D, tf), lambda i, f: (0, f)),
                  pl.BlockSpec((tf, D), lambda i, f: (f, 0)),
                  pl.BlockSpec((1, D), lambda i, f: (0, 0))] + ride_specs,
        out_specs=out_specs + ride_specs,
        out_shape=out_shape + [jax.ShapeDtypeStruct(w.shape, BF16) for w in riders],
        scratch_shapes=[pltpu.VMEM((tm, D), F32)],
        compiler_params=pltpu.CompilerParams(dimension_semantics=("arbitrary", "arbitrary"),
                                             vmem_limit_bytes=_vmem_limit(need + (8 << 20))),
        name="swiglu_ffn",
    )(n, h, w1, w3, w2, g_next.reshape(1, D), *riders)
    return outs[:n_main], outs[n_main:]


def moe_plan(ids, M, tm):
    E = N_EXPERTS
    n_tiles = -(-(TOP_K * M) // tm) + E + 2
    e_flat = ids[:, :TOP_K].reshape(-1)
    onehot = (e_flat[:, None] == jnp.arange(E, dtype=jnp.int32)[None, :]).astype(jnp.int32)
    csum = jnp.cumsum(onehot, axis=0)
    rank = jnp.sum((csum - onehot) * onehot, axis=1)
    counts = csum[-1]
    tiles_per_e = (counts + tm - 1) // tm
    tile_end = jnp.cumsum(tiles_per_e)
    offs = (tile_end - tiles_per_e) * tm
    dest = offs[e_flat] + rank
    pair = jnp.full((n_tiles * tm,), -1, jnp.int32).at[dest].set(
        jnp.arange(TOP_K * M, dtype=jnp.int32), unique_indices=True)
    spare = TOP_K * M + jnp.arange(n_tiles * tm, dtype=jnp.int32) % tm
    src_token = jnp.where(pair >= 0, pair // TOP_K, 0)
    out_row = jnp.where(pair >= 0, (pair % TOP_K) * M + pair // TOP_K, spare)
    prev_out_row = jnp.concatenate([spare[:tm], out_row[:-tm]])
    n_used = tile_end[-1]
    t = jnp.arange(n_tiles, dtype=jnp.int32)
    tile_e = jnp.searchsorted(tile_end, jnp.minimum(t, n_used - 1), side="right").astype(jnp.int32)
    return (src_token.reshape(n_tiles, 1, tm), prev_out_row.reshape(n_tiles, 1, tm), tile_e,
            n_used.astype(jnp.int32).reshape(1))


def _moe_kernel(tile_e_ref, n_used_ref, src_ref, nxt_ref, dst_ref, h_hbm, g_ref, w1_ref, w3_ref, w2_ref,
                y_hbm, xbuf, xb, acc, ostage, gsem, ssem):
    t = pl.program_id(0)
    f = pl.program_id(1)
    nf = pl.num_programs(1)
    tm = xb.shape[0]
    per_step = tm // nf
    n_used = n_used_ref[0]
    used = t < n_used
    cur = t % 2

    def fetch(idx_ref, r, slot):
        return pltpu.make_async_copy(h_hbm.at[pl.ds(idx_ref[0, r], 1)], xbuf.at[slot, pl.ds(r, 1)],
                                     gsem.at[slot])

    def put(r):
        return pltpu.make_async_copy(ostage.at[1 - cur, pl.ds(r, 1)], y_hbm.at[pl.ds(dst_ref[0, r], 1)],
                                     ssem.at[1 - cur])

    @pl.when((f == 0) & (t == 0))
    def _():
        def issue(r, c):
            fetch(src_ref, r, 0).start()
            return c
        lax.fori_loop(0, tm, issue, 0)
        ostage[1] = jnp.zeros((tm, ostage.shape[2]), F32)

    @pl.when((f == 0) & (t <= n_used))
    def _():
        pltpu.make_async_copy(h_hbm.at[pl.ds(0, tm)], xbuf.at[cur], gsem.at[cur]).wait()

    @pl.when((f == 0) & (t >= 1) & (t - 1 <= n_used))
    def _():
        pltpu.make_async_copy(ostage.at[cur], y_hbm.at[pl.ds(0, tm)], ssem.at[cur]).wait()

    @pl.when((f == 0) & used)
    def _():
        xb[...] = _rms(xbuf[cur], g_ref[...]).astype(BF16)
        acc[...] = jnp.zeros_like(acc)

    @pl.when(used)
    def _():
        for j in range(per_step):
            fetch(nxt_ref, f * per_step + j, 1 - cur).start()
            put(f * per_step + j).start()
        x = xb[...]
        act = jax.nn.silu(_dot(x, w1_ref[...])) * _dot(x, w3_ref[...])
        acc[...] += _dot(act.astype(BF16), w2_ref[...])

    @pl.when(used & (f == nf - 1))
    def _():
        ostage[cur] = acc[...]

    @pl.when((f == 0) & (t == n_used))
    def _():
        def issue(r, c):
            put(r).start()
            return c
        lax.fori_loop(0, tm, issue, 0)


def moe_call(h, g, src_token, prev_out_row, tile_e, n_used, w1, w3, w2, tm, tf):
    M, D = h.shape
    n_tiles = src_token.shape[0]
    E, _, Fe = w1.shape
    nf = Fe // tf
    assert tm % nf == 0 and Fe % tf == 0

    def wmap(t, f, tile_e_ref, n_used_ref):
        ff = jnp.where(t < n_used_ref[0], f, nf - 1)
        return tile_e_ref[t], ff

    smem_tile = lambda fn: pl.BlockSpec((None, 1, tm), fn, memory_space=pltpu.SMEM)
    grid_spec = pltpu.PrefetchScalarGridSpec(
        num_scalar_prefetch=2,
        grid=(n_tiles, nf),
        in_specs=[smem_tile(lambda t, f, te, nu: (t, 0, 0)),
                  smem_tile(lambda t, f, te, nu: (jnp.minimum(t + 1, n_tiles - 1), 0, 0)),
                  smem_tile(lambda t, f, te, nu: (t, 0, 0)),
                  pl.BlockSpec(memory_space=pl.ANY),
                  pl.BlockSpec((1, D), lambda t, f, te, nu: (0, 0)),
                  pl.BlockSpec((None, D, tf), lambda t, f, te, nu: (wmap(t, f, te, nu)[0], 0,
                                                                     wmap(t, f, te, nu)[1])),
                  pl.BlockSpec((None, D, tf), lambda t, f, te, nu: (wmap(t, f, te, nu)[0], 0,
                                                                     wmap(t, f, te, nu)[1])),
                  pl.BlockSpec((None, tf, D), lambda t, f, te, nu: wmap(t, f, te, nu) + (0,))],
        out_specs=pl.BlockSpec(memory_space=pl.ANY),
        scratch_shapes=[pltpu.VMEM((2, tm, D), F32), pltpu.VMEM((tm, D), BF16), pltpu.VMEM((tm, D), F32),
                        pltpu.VMEM((2, tm, D), F32),
                        pltpu.SemaphoreType.DMA((2,)), pltpu.SemaphoreType.DMA((2,))])
    need = tm * D * (8 + 2 + 4 + 8) + 2 * 3 * D * tf * 2 + 4 * tm * tf * 4
    return pl.pallas_call(
        _moe_kernel,
        grid_spec=grid_spec,
        out_shape=jax.ShapeDtypeStruct((TOP_K * M + tm, D), F32),
        compiler_params=pltpu.CompilerParams(dimension_semantics=("arbitrary", "arbitrary"),
                                             vmem_limit_bytes=_vmem_limit(need + (8 << 20))),
        name="moe_experts",
    )(tile_e, n_used, src_token, src_token, prev_out_row, h, g.reshape(1, D), w1, w3, w2)


def _moe_combine_kernel(h_ref, p_ref, g_ref, *refs, emit_h):
    y_refs, out_refs = refs[:TOP_K], refs[TOP_K:]
    hn = h_ref[...]
    for k in range(TOP_K):
        hn = hn + p_ref[:, k:k + 1] * y_refs[k][...]
    if emit_h:
        out_refs[0][...] = hn
    out_refs[-1][...] = _rms(hn, g_ref[...]).astype(out_refs[-1].dtype)


def moe_combine_call(h, y_tok, probs, g_next, n_dtype, emit_h, tm=MOE_COMBINE_TM):
    M, D = h.shape
    tm = min(tm, M)
    rowblk = pl.BlockSpec((tm, D), lambda i: (i, 0))
    out_specs = [rowblk]
    out_shape = [jax.ShapeDtypeStruct((M, D), n_dtype)]
    if emit_h:
        out_specs = [rowblk, rowblk]
        out_shape = [jax.ShapeDtypeStruct((M, D), F32)] + out_shape
    y_specs = [pl.BlockSpec((tm, D), functools.partial(lambda k, i: (k * (M // tm) + i, 0), k))
               for k in range(TOP_K)]
    return pl.pallas_call(
        functools.partial(_moe_combine_kernel, emit_h=emit_h),
        grid=(M // tm,),
        in_specs=[rowblk, pl.BlockSpec((tm, ROUTER_LANES), lambda i: (i, 0)),
                  pl.BlockSpec((1, D), lambda i: (0, 0))] + y_specs,
        out_specs=out_specs,
        out_shape=out_shape,
        compiler_params=pltpu.CompilerParams(dimension_semantics=("parallel",),
                                             vmem_limit_bytes=_vmem_limit(14 * tm * D * 4)),
        name="moe_combine",
    )(h, probs, g_next.reshape(1, D), *([y_tok] * TOP_K))


def kernel(x, norm_mix, w_in, gate_b, gm_ln_g, gm_ln_b, gm_ws, gm_bs, rw_mu, rw_w0, rw_w2, rw_a0, rw_a2,
           rw_g2, rw_kk, rw_ka, rw_rk, rw_ln_g, rw_ln_b, p_gm, p_rw, p_sb, w_o, norm_ffn, ffn_w1, ffn_w3,
           ffn_w2, router_w, router_b, moe_w1, moe_w3, moe_w2, norm_out):
    B, T, D = x.shape
    M = B * T
    bf = lambda a: a.astype(BF16)
    h = x.reshape(M, D)
    n = rmsnorm_call(h, norm_mix[0], BF16)
    moe_bf16 = {}
    for l in range(DEPTH):
        last = l == DEPTH - 1
        g_next = norm_out if last else norm_mix[l + 1]
        n_dtype = x.dtype if last else BF16
        w = w_in[l]
        j = l // 2
        uv = matmul_call(n, bf(w[:, GM_OFF:RW_OFF]), BF16, tn=GM_COLS, name="in_proj_gm")
        p = matmul_call(n, bf(w[:, RW_OFF:SB_OFF]), BF16, tn=RW_COLS, name="in_proj_rw")
        qkv = matmul_call(n, bf(w[:, SB_OFF:GATE_OFF]), BF16, tn=1024, name="in_proj_sb")
        gates = matmul_call(n, bf(w[:, GATE_OFF:]), BF16, tn=1024, bias=gate_b[l].reshape(-1),
                            name="in_proj_gate")
        y_gm = gm_call(uv, gm_ln_g[l], gm_ln_b[l], gm_ws[l], gm_bs[l])
        y_rw = rw_call(p.reshape(B, T, RW_COLS), B, T, rw_mu[l], rw_w0[l], rw_w2[l], rw_a0[l], rw_a2[l],
                       rw_g2[l], rw_kk[l], rw_ka[l], rw_rk[l], rw_ln_g[l], rw_ln_b[l]).reshape(M, RW_WIDTH)
        y_sb = sb_call(qkv.reshape(B, T, SB_COLS), B, T).reshape(M, SB_WIDTH)
        merge_args = (y_gm, y_rw, y_sb, gates, h, bf(p_gm[l]), bf(p_rw[l]), bf(p_sb[l]), bf(w_o[l]), norm_ffn[l])
        if l % 2 == 0:
            h, n2 = merge_call(*merge_args)
            ahead = []
            if not last:
                ahead = [(k, wt[(l + 1) // 2]) for k, wt in (("w1", moe_w1), ("w3", moe_w3), ("w2", moe_w2))
                         if can_ride(wt[(l + 1) // 2], ffn_steps(M, D_FF))]
            outs, casts = ffn_call(n2, h, bf(ffn_w1[j]), bf(ffn_w3[j]), bf(ffn_w2[j]), g_next, n_dtype,
                                   emit_h=not last, riders=[wt for _, wt in ahead])
            moe_bf16 = {k: c for (k, _), c in zip(ahead, casts)}
        else:
            h, ids, probs = merge_call(*merge_args, router=(router_w[j], router_b[j]))
            src_token, prev_out_row, tile_e, n_used = moe_plan(ids, M, MOE_TM)
            ew = {k: moe_bf16[k] if k in moe_bf16 else bf(wt[j])
                  for k, wt in (("w1", moe_w1), ("w3", moe_w3), ("w2", moe_w2))}
            y_tok = moe_call(h, norm_ffn[l], src_token, prev_out_row, tile_e, n_used, ew["w1"], ew["w3"],
                             ew["w2"], MOE_TM, MOE_TF)
            outs = moe_combine_call(h, y_tok, probs, g_next, n_dtype, emit_h=not last)
            moe_bf16 = {}
        if last:
            n = outs[0]
        else:
            h, n = outs
    return n.reshape(B, T, D)
```

```python
import functools

import jax
import jax.numpy as jnp
from jax import lax
from jax.experimental import pallas as pl
from jax.experimental.pallas import tpu as pltpu

F32 = jnp.float32
BF16 = jnp.bfloat16

D_MODEL = 2048
DEPTH = 2
HEAD_DIM = 64
GM_HEADS = 8
GM_WIDTH = GM_HEADS * HEAD_DIM
CHUNK = 128
RW_HEADS = 8
RW_WIDTH = RW_HEADS * HEAD_DIM
DECAY_LORA = 64
AAA_LORA = 64
GATE_LORA = 128
SB_HEADS = 16
SB_WIDTH = SB_HEADS * HEAD_DIM
N_BRANCH = 3
D_FF = 5632
N_EXPERTS = 8
TOP_K = 2
D_FF_EXPERT = D_FF // TOP_K
RMS_EPS = 1e-6
LN_EPS = 1e-5
RW_GN_EPS = 64e-5

GM_COLS = 2 * GM_WIDTH
RW_COLS = 3 * RW_WIDTH + DECAY_LORA + AAA_LORA + GATE_LORA
SB_COLS = 3 * SB_WIDTH
GATE_COLS = N_BRANCH * D_MODEL
GM_OFF = 0
RW_OFF = GM_OFF + GM_COLS
SB_OFF = RW_OFF + RW_COLS
GATE_OFF = SB_OFF + SB_COLS
N_IN = GATE_OFF + GATE_COLS

V7X_VMEM_BYTES = 64 * 1024 * 1024
LANES = 128
BF16_SUBLANES = 16
RW_CHUNK = 64
RW_TILE = 256
RW_GROUP = 4
SB_TILE = 256
LOG2_E = 1.4426950408889634
SB_EXP2_UNDERFLOW = -151.0
ROUTER_LANES = 128
MOE_TF = 256
MOE_TM = 528
MOE_COMBINE_TM = 256


def _vmem_limit(nbytes):
    return int(min(max(nbytes, 32 * 1024 * 1024), V7X_VMEM_BYTES - 6 * 1024 * 1024))


def _dot(a, b):
    return jnp.dot(a, b, preferred_element_type=F32)


def _dot_nt(a, b):
    return lax.dot_general(a, b, (((1,), (1,)), ((), ())), preferred_element_type=F32)


def _dot_tn(a, b):
    return lax.dot_general(a, b, (((0,), (0,)), ((), ())), preferred_element_type=F32)


def _dot_split(x, ones_bf16):
    hi = x.astype(BF16)
    lo = (x - hi.astype(F32)).astype(BF16)
    return _dot(hi, ones_bf16) + _dot(lo, ones_bf16)


def _rms(x, g):
    ms = jnp.mean(x * x, axis=-1, keepdims=True)
    return x * lax.rsqrt(ms + RMS_EPS) * g


def _rmsnorm_kernel(x_ref, g_ref, o_ref):
    o_ref[...] = _rms(x_ref[...], g_ref[...]).astype(o_ref.dtype)


def rmsnorm_call(x, g, out_dtype, tm=512):
    M, D = x.shape
    tm = min(tm, M)
    return pl.pallas_call(
        _rmsnorm_kernel,
        grid=(M // tm,),
        in_specs=[pl.BlockSpec((tm, D), lambda i: (i, 0)),
                  pl.BlockSpec((1, D), lambda i: (0, 0))],
        out_specs=pl.BlockSpec((tm, D), lambda i: (i, 0)),
        out_shape=jax.ShapeDtypeStruct((M, D), out_dtype),
        compiler_params=pltpu.CompilerParams(dimension_semantics=("parallel",)),
        name="rmsnorm",
    )(x, g.reshape(1, D))


def _mm_kernel(a_ref, w_ref, o_ref):
    o_ref[...] = _dot(a_ref[...], w_ref[...]).astype(o_ref.dtype)


def _mm_gate_kernel(a_ref, w_ref, b_ref, o_ref):
    o_ref[...] = jax.nn.sigmoid(_dot(a_ref[...], w_ref[...]) + b_ref[...]).astype(o_ref.dtype)


def matmul_call(a, w, out_dtype, tn, bias=None, tm=1024, name="matmul"):
    M, K = a.shape
    N = w.shape[1]
    tm = min(tm, M)
    in_specs = [pl.BlockSpec((tm, K), lambda i, j: (i, 0)),
                pl.BlockSpec((K, tn), lambda i, j: (0, j))]
    args = [a, w]
    kern = _mm_kernel
    if bias is not None:
        in_specs.append(pl.BlockSpec((1, tn), lambda i, j: (0, j)))
        args.append(bias.reshape(1, N))
        kern = _mm_gate_kernel
    osz = jnp.dtype(out_dtype).itemsize
    need = 2 * (tm * K * 2 + K * tn * 2 + tm * tn * osz) + 2 * tm * tn * 4
    return pl.pallas_call(
        kern,
        grid=(M // tm, N // tn),
        in_specs=in_specs,
        out_specs=pl.BlockSpec((tm, tn), lambda i, j: (i, j)),
        out_shape=jax.ShapeDtypeStruct((M, N), out_dtype),
        compiler_params=pltpu.CompilerParams(
            dimension_semantics=("parallel", "arbitrary"),
            vmem_limit_bytes=_vmem_limit(need + (8 << 20))),
        name=name,
    )(*args)


def _gm_kernel(uv_ref, lng_ref, lnb_ref, ws_ref, bias_ref, o_ref, *, n_chunks):
    ii = lax.broadcasted_iota(jnp.int32, (CHUNK, CHUNK), 0)
    jj = lax.broadcasted_iota(jnp.int32, (CHUNK, CHUNK), 1)
    causal = jj <= ii
    ws = [jnp.where(causal, ws_ref[h], 0.0).astype(BF16) for h in range(GM_HEADS)]
    for c in range(n_chunks):
        rows = slice(c * CHUNK, (c + 1) * CHUNK)
        u = jax.nn.gelu(uv_ref[rows, 0:GM_WIDTH].astype(F32))
        v = jax.nn.gelu(uv_ref[rows, GM_WIDTH:2 * GM_WIDTH].astype(F32))
        mu = jnp.mean(v, axis=-1, keepdims=True)
        vc = v - mu
        var = jnp.mean(vc * vc, axis=-1, keepdims=True)
        vn = (vc * lax.rsqrt(var + LN_EPS) * lng_ref[...] + lnb_ref[...]).astype(BF16)
        mixed = jnp.concatenate(
            [_dot(ws[h], vn[:, h * HEAD_DIM:(h + 1) * HEAD_DIM]) for h in range(GM_HEADS)], axis=-1)
        o_ref[rows, :] = (u * (mixed + bias_ref[...])).astype(o_ref.dtype)


def gm_call(uv, ln_g, ln_b, w_s, b_s, n_chunks=2):
    M = uv.shape[0]
    tt = CHUNK * n_chunks
    bias = jnp.repeat(b_s.T, HEAD_DIM, axis=1)
    return pl.pallas_call(
        functools.partial(_gm_kernel, n_chunks=n_chunks),
        grid=(M // tt,),
        in_specs=[pl.BlockSpec((tt, GM_COLS), lambda i: (i, 0)),
                  pl.BlockSpec((1, GM_WIDTH), lambda i: (0, 0)),
                  pl.BlockSpec((1, GM_WIDTH), lambda i: (0, 0)),
                  pl.BlockSpec((GM_HEADS, CHUNK, CHUNK), lambda i: (0, 0, 0)),
                  pl.BlockSpec((CHUNK, GM_WIDTH), lambda i: (0, 0))],
        out_specs=pl.BlockSpec((tt, GM_WIDTH), lambda i: (i, 0)),
        out_shape=jax.ShapeDtypeStruct((M, GM_WIDTH), BF16),
        compiler_params=pltpu.CompilerParams(dimension_semantics=("parallel",)),
        name="gmlp_gating",
    )(uv, ln_g.reshape(1, -1), ln_b.reshape(1, -1), w_s, bias)


def _rw_kernel(p_ref, prev_ref, mu_ref, w0_ref, w2_ref, a0_ref, a2_ref, g2_ref, kk_ref, ka_ref,
               rk_ref, lng_ref, lnb_ref, hsum_ref, o_ref, st_ref, y_ref):
    C = RW_CHUNK
    i = pl.program_id(1)

    @pl.when(i == 0)
    def _():
        st_ref[...] = jnp.zeros_like(st_ref)

    hsum = hsum_ref[...]
    gw = hsum.shape[0]

    def head_sums(x):
        return jnp.concatenate([_dot_split(x[:, q * gw:(q + 1) * gw], hsum)
                                for q in range(x.shape[1] // gw)], axis=1)

    p = p_ref[...].astype(F32)
    prev_row = prev_ref[BF16_SUBLANES - 1:BF16_SUBLANES, :].astype(F32)
    prev_row = jnp.where(i == 0, 0.0, prev_row)
    row = lax.broadcasted_iota(jnp.int32, p.shape, 0)
    p_prev = jnp.where(row == 0, prev_row, pltpu.roll(p, 1, axis=0))
    ps = p + (p_prev - p) * mu_ref[...]
    W = RW_WIDTH
    r = ps[:, 0:W]
    k = ps[:, W:2 * W]
    v = ps[:, 2 * W:3 * W]
    wl = ps[:, 3 * W:3 * W + DECAY_LORA]
    al = ps[:, 3 * W + DECAY_LORA:3 * W + DECAY_LORA + AAA_LORA]
    gl = ps[:, 3 * W + DECAY_LORA + AAA_LORA:RW_COLS]

    d = w0_ref[...] + _dot(jnp.tanh(wl).astype(BF16), w2_ref[...])
    log_w = -jnp.exp(-jax.nn.softplus(-d) - 0.5)
    a = jax.nn.sigmoid(a0_ref[...] + _dot(al.astype(BF16), a2_ref[...]))
    g = _dot(jax.nn.sigmoid(gl).astype(BF16), g2_ref[...])
    kk = k * kk_ref[...]
    kk = kk / jnp.maximum(jnp.sqrt(head_sums(kk * kk)), 1e-12)
    k = k * (1.0 + (a - 1.0) * ka_ref[...])
    bonus = head_sums(r * k * rk_ref[...]) * v

    HD = HEAD_DIM
    GW = RW_GROUP * HD
    ci = lax.broadcasted_iota(jnp.int32, (C, C), 0)
    cj = lax.broadcasted_iota(jnp.int32, (C, C), 1)
    tri_incl = (cj <= ci).astype(BF16)
    ti = lax.broadcasted_iota(jnp.int32, (C, GW), 0)
    tj = lax.broadcasted_iota(jnp.int32, (C, GW), 1) % HD
    strict = tj < ti
    incl = tj <= ti
    eye = ti == tj
    blk_masks = []
    s = 2
    while s < C:
        blk_masks.append((ti // (2 * s) == tj // (2 * s)) & (ti // s != tj // s))
        s *= 2
    pair_mask = (ti // 2 == tj // 2) & strict
    gi = lax.broadcasted_iota(jnp.int32, (GW, GW), 0)
    gj = lax.broadcasted_iota(jnp.int32, (GW, GW), 1)
    bd_mask = gi // HD == gj // HD
    bd_mask_b = jnp.where(bd_mask, 1.0, 0.0).astype(BF16)
    gw_eye = gi == gj

    def bd(x):
        xb = x.astype(BF16)
        return jnp.where(bd_mask_b > 0, jnp.concatenate([xb] * RW_GROUP, axis=0), 0)

    def bd_of_full(full):
        return jnp.where(bd_mask, full, 0.0)

    n_chunks = RW_TILE // C
    n_groups = RW_WIDTH // GW
    systems = [(c, q) for c in range(n_chunks) for q in range(n_groups)]

    pre = []
    for c in range(n_chunks):
        rows = slice(c * C, (c + 1) * C)
        lw = log_w[rows]
        cum = _cumsum_rows(lw, tri_incl)
        gam = jnp.exp(cum)
        gam_prev = jnp.exp(cum - lw)
        gam_inv = jnp.exp(-cum)
        g_end = gam[C - 1:C, :]
        bh_f = kk[rows] * a[rows] * gam_inv
        kh_f = k[rows] * gam_inv
        pre.append(dict(
            rh=(r[rows] * gam).astype(BF16), ah=(-kk[rows] * gam_prev).astype(BF16),
            bh=bh_f.astype(BF16), kh=kh_f.astype(BF16),
            bt=(bh_f * g_end).astype(BF16), kt=(kh_f * g_end).astype(BF16),
            v=v[rows].astype(BF16), g_end=g_end))

    def grp(c, q, name):
        return pre[c][name][:, q * GW:(q + 1) * GW]

    st = {}
    for (c, q) in systems:
        ar = jnp.concatenate([grp(c, q, "ah"), grp(c, q, "rh")], axis=0)
        pb = _dot_nt(ar, bd(grp(c, q, "bh")))
        pk = _dot_nt(ar, bd(grp(c, q, "kh")))
        lab = jnp.where(strict, pb[0:C], 0.0)
        st[(c, q)] = dict(
            lab=lab, mrb=jnp.where(incl, pb[C:2 * C], 0.0).astype(BF16),
            lm=jnp.concatenate([jnp.where(strict, pk[0:C], 0.0), jnp.where(incl, pk[C:2 * C], 0.0)],
                               axis=0).astype(BF16),
            tinv=jnp.where(eye, 1.0, jnp.where(pair_mask, lab, 0.0)))

    for bm in blk_masks:
        xs = {}
        for key in systems:
            d_ = st[key]
            xs[key] = _dot(d_["tinv"].astype(BF16), bd(jnp.where(bm, d_["lab"], 0.0))).astype(BF16)
        for key in systems:
            d_ = st[key]
            d_["tinv"] = d_["tinv"] + _dot(xs[key], bd(d_["tinv"]))

    for (c, q) in systems:
        d_ = st[(c, q)]
        lv = _dot(d_["lm"], bd(grp(c, q, "v")))
        tb = d_["tinv"].astype(BF16)
        ap = _dot(tb, bd(grp(c, q, "ah"))).astype(BF16)
        up = _dot(tb, bd(lv[0:C])).astype(BF16)
        d_["rp"] = (grp(c, q, "rh").astype(F32) + _dot(d_["mrb"], bd(ap))).astype(BF16)
        d_["ypp"] = _dot(d_["mrb"], bd(up)) + lv[C:2 * C]
        g_end = pre[c]["g_end"][:, q * GW:(q + 1) * GW]
        gt = _dot_tn(ap, grp(c, q, "bt"))
        d_["gt"] = jnp.where(gw_eye, gt + g_end, bd_of_full(gt)).astype(BF16)
        ht = bd_of_full(_dot_tn(up, grp(c, q, "bt")) + _dot_tn(grp(c, q, "v"), grp(c, q, "kt")))
        d_["ht"] = sum(ht[hh * HD:(hh + 1) * HD] for hh in range(1, RW_GROUP)) + ht[0:HD]

    for q in range(n_groups):
        s_cur = st_ref[q]
        for c in range(n_chunks):
            d_ = st[(c, q)]
            y_ref[c * C:(c + 1) * C, q * GW:(q + 1) * GW] = _dot_nt(d_["rp"], bd(s_cur)) + d_["ypp"]
            s_cur = _dot(s_cur.astype(BF16), d_["gt"]) + d_["ht"]
        st_ref[q] = s_cur

    y = y_ref[...]
    inv_n = 1.0 / HEAD_DIM
    m = head_sums(y) * inv_n
    yc = y - m
    var = head_sums(yc * yc) * inv_n
    yn = yc * lax.rsqrt(var + RW_GN_EPS) * lng_ref[...] + lnb_ref[...]
    o_ref[...] = ((yn + bonus) * g).astype(o_ref.dtype)


def _cumsum_rows(x, tri_incl_bf16):
    hi = x.astype(BF16)
    lo = (x - hi.astype(F32)).astype(BF16)
    return _dot(tri_incl_bf16, hi) + _dot(tri_incl_bf16, lo)


def rw_call(p, B, T, mu, w0, w2, a0, a2, g2, k_k, k_a, r_k, ln_g, ln_b):
    nt = T // RW_TILE
    sub = RW_TILE // BF16_SUBLANES
    hid = jnp.arange(RW_GROUP * HEAD_DIM) // HEAD_DIM
    hsum = (hid[:, None] == hid[None, :]).astype(BF16)
    row = lambda a: a.reshape(1, -1).astype(F32)
    vec = lambda n: pl.BlockSpec((1, n), lambda b, i: (0, 0))
    return pl.pallas_call(
        _rw_kernel,
        grid=(B, nt),
        in_specs=[pl.BlockSpec((None, RW_TILE, RW_COLS), lambda b, i: (b, i, 0)),
                  pl.BlockSpec((None, BF16_SUBLANES, RW_COLS),
                               lambda b, i: (b, jnp.maximum(i * sub - 1, 0), 0)),
                  vec(RW_COLS), vec(RW_WIDTH),
                  pl.BlockSpec((DECAY_LORA, RW_WIDTH), lambda b, i: (0, 0)),
                  vec(RW_WIDTH),
                  pl.BlockSpec((AAA_LORA, RW_WIDTH), lambda b, i: (0, 0)),
                  pl.BlockSpec((GATE_LORA, RW_WIDTH), lambda b, i: (0, 0)),
                  vec(RW_WIDTH), vec(RW_WIDTH), vec(RW_WIDTH), vec(RW_WIDTH), vec(RW_WIDTH),
                  pl.BlockSpec((RW_GROUP * HEAD_DIM, RW_GROUP * HEAD_DIM), lambda b, i: (0, 0))],
        out_specs=pl.BlockSpec((None, RW_TILE, RW_WIDTH), lambda b, i: (b, i, 0)),
        out_shape=jax.ShapeDtypeStruct((B, T, RW_WIDTH), BF16),
        scratch_shapes=[pltpu.VMEM((RW_HEADS // RW_GROUP, HEAD_DIM, RW_GROUP * HEAD_DIM), F32),
                        pltpu.VMEM((RW_TILE, RW_WIDTH), F32)],
        compiler_params=pltpu.CompilerParams(dimension_semantics=("parallel", "arbitrary")),
        name="rwkv7_time_mix",
    )(p, p, row(mu), row(w0), w2.astype(BF16), row(a0), a2.astype(BF16), g2.astype(BF16),
      row(k_k), row(k_a), row(r_k), row(ln_g), row(ln_b), hsum)


def _sb_scores(qs, ks, masks, suffix_ones):
    n = range(len(qs))
    zs = [_dot_nt(qs[s], ks[s]) * LOG2_E for s in n]
    sps = [jnp.maximum(z, 0.0) + jnp.log2(1.0 + jnp.exp2(-jnp.abs(z))) for z in zs]
    log_nots = [-sp for sp in sps]
    log_nots = [ln if masks[s] is None else jnp.where(masks[s], ln, 0.0) for s, ln in enumerate(log_nots)]
    his = [ln.astype(BF16) for ln in log_nots]
    los = [(log_nots[s] - his[s].astype(F32)).astype(BF16) for s in n]
    sufs = [_dot(his[s], suffix_ones) + _dot(los[s], suffix_ones) for s in n]
    rowsums = [jnp.sum(ln, axis=-1, keepdims=True) for ln in log_nots]
    return zs, sps, sufs, rowsums


def _sb_weighted(scores, s, carry, mask, v):
    zs, sps, sufs, _ = scores
    att = jnp.exp2(zs[s] - sps[s] + (sufs[s] + carry))
    if mask is not None:
        att = jnp.where(mask, att, 0.0)
    return _dot(att.astype(BF16), v)


def _sb_kernel(q_ref, k_ref, v_ref, o_ref, acc_ref, carry_ref, *, T):
    tq = SB_TILE
    nq = T // tq
    ti = lax.broadcasted_iota(jnp.int32, (tq, tq), 0)
    tj = lax.broadcasted_iota(jnp.int32, (tq, tq), 1)
    suffix_ones = (ti > tj).astype(BF16)
    causal = tj < ti
    scale = HEAD_DIM ** -0.5
    heads = [slice(hh * HEAD_DIM, (hh + 1) * HEAD_DIM) for hh in range(LANES // HEAD_DIM)]
    nh = len(heads)

    def q_body(i, _):
        r0 = pl.multiple_of(i * tq, tq)
        has_prev = jnp.where(i > 0, 1.0, 0.0)
        p0 = pl.multiple_of(jnp.maximum(i - 1, 0) * tq, tq)
        qs = [q_ref[pl.ds(r0, tq), cs] * scale for cs in heads]
        ks = [k_ref[pl.ds(r0, tq), cs] for cs in heads] + [k_ref[pl.ds(p0, tq), cs] for cs in heads]
        vs = [v_ref[pl.ds(r0, tq), cs] for cs in heads] + [v_ref[pl.ds(p0, tq), cs] for cs in heads]
        masks = [causal] * nh + [None] * nh
        scores = _sb_scores(qs + qs, ks, masks, suffix_ones)
        rss = scores[3]
        live = jnp.float32(-jnp.inf)
        for hh, cs in enumerate(heads):
            out = _sb_weighted(scores, hh, 0.0, causal, vs[hh])
            out_prev = _sb_weighted(scores, nh + hh, rss[hh], None, vs[nh + hh])
            acc_ref[:, cs] = out + has_prev * out_prev
            carry = rss[hh] + has_prev * rss[nh + hh]
            carry_ref[hh] = carry
            live = jnp.maximum(live, jnp.max(carry))

        def k_cond(state):
            jr, live = state
            return (jr < i) & (live > SB_EXP2_UNDERFLOW)

        def k_body(state):
            jr, _ = state
            c0 = pl.multiple_of((i - 1 - jr) * tq, tq)
            kt = [k_ref[pl.ds(c0, tq), cs] for cs in heads]
            vt = [v_ref[pl.ds(c0, tq), cs] for cs in heads]
            scores = _sb_scores(qs, kt, [None] * nh, suffix_ones)
            live = jnp.float32(-jnp.inf)
            for hh, cs in enumerate(heads):
                carry = carry_ref[hh]
                acc_ref[:, cs] += _sb_weighted(scores, hh, carry, None, vt[hh])
                carry = carry + scores[3][hh]
                carry_ref[hh] = carry
                live = jnp.maximum(live, jnp.max(carry))
            return jr + 1, live

        lax.while_loop(k_cond, k_body, (jnp.int32(1), live))
        o_ref[pl.ds(r0, tq), :] = acc_ref[...].astype(o_ref.dtype)
        return 0

    lax.fori_loop(0, nq, q_body, 0)


def sb_call(qkv, B, T):
    npair = SB_WIDTH // LANES
    blk = lambda off: pl.BlockSpec((None, T, LANES), lambda b, hp: (b, 0, off + hp))
    return pl.pallas_call(
        functools.partial(_sb_kernel, T=T),
        grid=(B, npair),
        in_specs=[blk(0), blk(npair), blk(2 * npair)],
        out_specs=pl.BlockSpec((None, T, LANES), lambda b, hp: (b, 0, hp)),
        out_shape=jax.ShapeDtypeStruct((B, T, SB_WIDTH), BF16),
        scratch_shapes=[pltpu.VMEM((SB_TILE, LANES), F32),
                        pltpu.VMEM((LANES // HEAD_DIM, SB_TILE, 1), F32)],
        compiler_params=pltpu.CompilerParams(dimension_semantics=("parallel", "parallel")),
        name="stick_breaking_attention",
    )(qkv, qkv, qkv)


def _route(n, w_ref, b_ref, id_ref, p_ref):
    nh = n.astype(BF16)
    nl = (n - nh.astype(F32)).astype(BF16)
    a = _dot(nh, w_ref[...])
    b = _dot(nl, w_ref[...])
    L = ROUTER_LANES
    logits = a[:, :L] + (a[:, L:] + b[:, :L] + b[:, L:]) + b_ref[...]
    lane = lax.broadcasted_iota(jnp.int32, logits.shape, 1)
    neg = jnp.float32(-jnp.inf)
    logits = jnp.where(lane < N_EXPERTS, logits, neg)
    m1 = jnp.max(logits, axis=-1, keepdims=True)
    i1 = jnp.min(jnp.where(logits == m1, lane, ROUTER_LANES), axis=-1, keepdims=True)
    rest = jnp.where(lane == i1, neg, logits)
    m2 = jnp.max(rest, axis=-1, keepdims=True)
    i2 = jnp.min(jnp.where(rest == m2, lane, ROUTER_LANES), axis=-1, keepdims=True)
    e2 = jnp.exp(m2 - m1)
    p1 = 1.0 / (1.0 + e2)
    p2 = e2 / (1.0 + e2)
    id_ref[...] = jnp.where(lane == 0, i1, jnp.where(lane == 1, i2, 0))
    p_ref[...] = jnp.where(lane == 0, p1, jnp.where(lane == 1, p2, 0.0))


def _merge_kernel(ygm_ref, yrw_ref, ysb_ref, gate_ref, h_ref, pgm_ref, prw_ref, psb_ref, wo_ref,
                  g_ref, *refs, route):
    D = D_MODEL
    merged = gate_ref[:, 0:D].astype(F32) * _dot(ygm_ref[...], pgm_ref[...])
    merged += gate_ref[:, D:2 * D].astype(F32) * _dot(yrw_ref[...], prw_ref[...])
    merged += gate_ref[:, 2 * D:3 * D].astype(F32) * _dot(ysb_ref[...], psb_ref[...])
    hn = h_ref[...] + _dot(merged.astype(BF16), wo_ref[...])
    n = _rms(hn, g_ref[...])
    if route:
        w_ref, b_ref, ho_ref, id_ref, p_ref = refs
        _route(n, w_ref, b_ref, id_ref, p_ref)
    else:
        ho_ref, no_ref = refs
        no_ref[...] = n.astype(no_ref.dtype)
    ho_ref[...] = hn


def _resident(shape):
    return pl.BlockSpec(shape, lambda i: (0,) * len(shape), pipeline_mode=pl.Buffered(1))


def merge_call(y_gm, y_rw, y_sb, gates, h, p_gm, p_rw, p_sb, w_o, g_next, router=None, tm=256):
    M, D = h.shape
    tm = min(tm, M)
    rowblk = lambda n: pl.BlockSpec((tm, n), lambda i: (i, 0))
    wbytes = 2 * (GM_WIDTH + RW_WIDTH + SB_WIDTH + D) * D
    need = wbytes + 2 * tm * (2 * (GM_WIDTH + RW_WIDTH + SB_WIDTH + GATE_COLS) + 4 * D + 4 * D + 2 * D) \
        + 6 * tm * D * 4
    in_specs = [rowblk(GM_WIDTH), rowblk(RW_WIDTH), rowblk(SB_WIDTH), rowblk(GATE_COLS), rowblk(D),
                _resident((GM_WIDTH, D)), _resident((RW_WIDTH, D)), _resident((SB_WIDTH, D)),
                _resident((D, D)), _resident((1, D))]
    args = [y_gm, y_rw, y_sb, gates, h, p_gm, p_rw, p_sb, w_o, g_next.reshape(1, D)]
    if router is None:
        out_specs = [rowblk(D), rowblk(D)]
        out_shape = [jax.ShapeDtypeStruct((M, D), F32), jax.ShapeDtypeStruct((M, D), BF16)]
    else:
        router_w, router_b = router
        wpad = jnp.zeros((D, ROUTER_LANES), F32).at[:, :N_EXPERTS].set(router_w)
        w_hi = wpad.astype(BF16)
        w_lo = (wpad - w_hi.astype(F32)).astype(BF16)
        bpad = jnp.zeros((1, ROUTER_LANES), F32).at[0, :N_EXPERTS].set(router_b)
        in_specs += [_resident((D, 2 * ROUTER_LANES)), _resident((1, ROUTER_LANES))]
        args += [jnp.concatenate([w_hi, w_lo], axis=1), bpad]
        out_specs = [rowblk(D), rowblk(ROUTER_LANES), rowblk(ROUTER_LANES)]
        out_shape = [jax.ShapeDtypeStruct((M, D), F32), jax.ShapeDtypeStruct((M, ROUTER_LANES), jnp.int32),
                     jax.ShapeDtypeStruct((M, ROUTER_LANES), F32)]
    return pl.pallas_call(
        functools.partial(_merge_kernel, route=router is not None),
        grid=(M // tm,),
        in_specs=in_specs,
        out_specs=out_specs,
        out_shape=out_shape,
        compiler_params=pltpu.CompilerParams(dimension_semantics=("parallel",),
                                             vmem_limit_bytes=_vmem_limit(need)),
        name="merge_out_proj",
    )(*args)


def _ffn_kernel(n_ref, h_ref, w1_ref, w3_ref, w2_ref, g_ref, *refs, emit_h, n_cast):
    cast_in, refs = refs[:n_cast], refs[n_cast:]
    if emit_h:
        ho_ref, no_ref = refs[:2]
        refs = refs[2:]
    else:
        no_ref = refs[0]
        refs = refs[1:]
    cast_out, acc_ref = refs[:n_cast], refs[n_cast]
    f = pl.program_id(1)

    @pl.when(f == 0)
    def _():
        acc_ref[...] = jnp.zeros_like(acc_ref)

    n = n_ref[...]
    a = _dot(n, w1_ref[...])
    b = _dot(n, w3_ref[...])
    for src, dst in zip(cast_in, cast_out):
        if len(dst.shape) == 3:
            ct = dst.shape[2]
            for c in range(dst.shape[0]):
                dst[c] = src[:, c * ct:(c + 1) * ct].astype(dst.dtype)
        else:
            dst[...] = src[...].astype(dst.dtype)
    act = (jax.nn.silu(a) * b).astype(BF16)
    acc_ref[...] += _dot(act, w2_ref[...])

    @pl.when(f == pl.num_programs(1) - 1)
    def _():
        hn = h_ref[...] + acc_ref[...]
        if emit_h:
            ho_ref[...] = hn
        no_ref[...] = _rms(hn, g_ref[...]).astype(no_ref.dtype)


def ffn_steps(M, F, tm=512, tf=512):
    return (M // min(tm, M)) * (F // tf)


RIDER_SLAB_ROWS = 64


def can_ride(w, steps):
    return w.shape[1] % RIDER_SLAB_ROWS == 0 and w.shape[0] * (w.shape[1] // RIDER_SLAB_ROWS) <= steps


def col_tiled(w, ct):
    E, R, C = w.shape
    return w.reshape(E, R, C // ct, ct).transpose(0, 2, 1, 3)


def ffn_call(n, h, w1, w3, w2, g_next, n_dtype, emit_h, riders=(), tm=512, tf=512):
    M, D = h.shape
    F = w1.shape[1]
    tm = min(tm, M)
    nf = F // tf
    steps = (M // tm) * nf
    rowblk = pl.BlockSpec((tm, D), lambda i, f: (i, 0))
    out_specs = [rowblk]
    out_shape = [jax.ShapeDtypeStruct((M, D), n_dtype)]
    if emit_h:
        out_specs = [rowblk, rowblk]
        out_shape = [jax.ShapeDtypeStruct((M, D), F32)] + out_shape
    n_main = len(out_shape)

    def slab_specs(w, ct):
        E, R, C = w.shape
        per = R // RIDER_SLAB_ROWS
        n_slabs = E * per
        hold = steps // n_slabs
        slab = lambda i, f: jnp.minimum((i * nf + f) // hold, n_slabs - 1)
        src = pl.BlockSpec((None, RIDER_SLAB_ROWS, C), lambda i, f: (slab(i, f) // per, slab(i, f) % per, 0))
        if ct is None:
            return src, src, jax.ShapeDtypeStruct(w.shape, BF16)
        dst = pl.BlockSpec((None, C // ct, RIDER_SLAB_ROWS, ct),
                           lambda i, f: (slab(i, f) // per, 0, slab(i, f) % per, 0))
        return src, dst, jax.ShapeDtypeStruct((E, C // ct, R, ct), BF16)

    ride = [slab_specs(w, ct) for w, ct in riders]
    need = 2 * tm * D * (2 + 4 + 4 * n_main) + tm * D * 4 + 2 * 3 * D * tf * 2 + 4 * tm * tf * 4 \
        + sum(2 * RIDER_SLAB_ROWS * w.shape[2] * 6 for w, _ in riders)
    outs = pl.pallas_call(
        functools.partial(_ffn_kernel, emit_h=emit_h, n_cast=len(riders)),
        grid=(M // tm, nf),
        in_specs=[rowblk, rowblk,
                  pl.BlockSpec((D, tf), lambda i, f: (0, f)),
                  pl.BlockSpec((D, tf), lambda i, f: (0, f)),
                  pl.BlockSpec((tf, D), lambda i, f: (f, 0)),
                  pl.BlockSpec((1, D), lambda i, f: (0, 0))] + [r[0] for r in ride],
        out_specs=out_specs + [r[1] for r in ride],
        out_shape=out_shape + [r[2] for r in ride],
        scratch_shapes=[pltpu.VMEM((tm, D), F32)],
        compiler_params=pltpu.CompilerParams(dimension_semantics=("arbitrary", "arbitrary"),
                                             vmem_limit_bytes=_vmem_limit(need + (8 << 20))),
        name="swiglu_ffn",
    )(n, h, w1, w3, w2, g_next.reshape(1, D), *[w for w, _ in riders])
    return outs[:n_main], outs[n_main:]


def moe_plan(ids, M, tm):
    E = N_EXPERTS
    n_tiles = -(-(TOP_K * M) // tm) + E + 2
    e_flat = ids[:, :TOP_K].reshape(-1)
    onehot = (e_flat[:, None] == jnp.arange(E, dtype=jnp.int32)[None, :]).astype(jnp.int32)
    csum = jnp.cumsum(onehot, axis=0)
    rank = jnp.sum((csum - onehot) * onehot, axis=1)
    counts = csum[-1]
    tiles_per_e = (counts + tm - 1) // tm
    tile_end = jnp.cumsum(tiles_per_e)
    offs = (tile_end - tiles_per_e) * tm
    dest = offs[e_flat] + rank
    pair = jnp.full((n_tiles * tm,), -1, jnp.int32).at[dest].set(
        jnp.arange(TOP_K * M, dtype=jnp.int32), unique_indices=True)
    spare = TOP_K * M + jnp.arange(n_tiles * tm, dtype=jnp.int32) % tm
    src_token = jnp.where(pair >= 0, pair // TOP_K, 0)
    out_row = jnp.where(pair >= 0, (pair % TOP_K) * M + pair // TOP_K, spare)
    prev_out_row = jnp.concatenate([spare[:tm], out_row[:-tm]])
    n_used = tile_end[-1]
    t = jnp.arange(n_tiles, dtype=jnp.int32)
    tile_e = jnp.searchsorted(tile_end, jnp.minimum(t, n_used - 1), side="right").astype(jnp.int32)
    return (src_token.reshape(n_tiles, 1, tm), prev_out_row.reshape(n_tiles, 1, tm), tile_e,
            n_used.astype(jnp.int32).reshape(1))


def _moe_kernel(tile_e_ref, n_used_ref, src_ref, nxt_ref, dst_ref, h_hbm, g_ref, w1_ref, w3_ref, w2_ref,
                y_hbm, xbuf, xb, acc, ostage, gsem, ssem):
    t = pl.program_id(0)
    f = pl.program_id(1)
    nf = pl.num_programs(1)
    tm = xb.shape[0]
    per_step = tm // nf
    n_used = n_used_ref[0]
    used = t < n_used
    cur = t % 2

    def fetch(idx_ref, r, slot):
        return pltpu.make_async_copy(h_hbm.at[pl.ds(idx_ref[0, r], 1)], xbuf.at[slot, pl.ds(r, 1)],
                                     gsem.at[slot])

    def put(r):
        return pltpu.make_async_copy(ostage.at[1 - cur, pl.ds(r, 1)], y_hbm.at[pl.ds(dst_ref[0, r], 1)],
                                     ssem.at[1 - cur])

    @pl.when((f == 0) & (t == 0))
    def _():
        def issue(r, c):
            fetch(src_ref, r, 0).start()
            return c
        lax.fori_loop(0, tm, issue, 0)
        ostage[1] = jnp.zeros((tm, ostage.shape[2]), F32)

    @pl.when((f == 0) & (t <= n_used))
    def _():
        pltpu.make_async_copy(h_hbm.at[pl.ds(0, tm)], xbuf.at[cur], gsem.at[cur]).wait()

    @pl.when((f == 0) & (t >= 1) & (t - 1 <= n_used))
    def _():
        pltpu.make_async_copy(ostage.at[cur], y_hbm.at[pl.ds(0, tm)], ssem.at[cur]).wait()

    @pl.when((f == 0) & used)
    def _():
        xb[...] = _rms(xbuf[cur], g_ref[...]).astype(BF16)
        acc[...] = jnp.zeros_like(acc)

    @pl.when(used)
    def _():
        for j in range(per_step):
            fetch(nxt_ref, f * per_step + j, 1 - cur).start()
            put(f * per_step + j).start()
        x = xb[...]
        act = jax.nn.silu(_dot(x, w1_ref[...])) * _dot(x, w3_ref[...])
        acc[...] += _dot(act.astype(BF16), w2_ref[...])

    @pl.when(used & (f == nf - 1))
    def _():
        ostage[cur] = acc[...]

    @pl.when((f == 0) & (t == n_used))
    def _():
        def issue(r, c):
            put(r).start()
            return c
        lax.fori_loop(0, tm, issue, 0)


def moe_call(h, g, src_token, prev_out_row, tile_e, n_used, w1t, w3t, w2, tm):
    M, D = h.shape
    n_tiles = src_token.shape[0]
    E, nf, _, tf = w1t.shape
    assert tm % nf == 0

    def wmap(t, f, tile_e_ref, n_used_ref):
        ff = jnp.where(t < n_used_ref[0], f, nf - 1)
        return tile_e_ref[t], ff

    smem_tile = lambda fn: pl.BlockSpec((None, 1, tm), fn, memory_space=pltpu.SMEM)
    grid_spec = pltpu.PrefetchScalarGridSpec(
        num_scalar_prefetch=2,
        grid=(n_tiles, nf),
        in_specs=[smem_tile(lambda t, f, te, nu: (t, 0, 0)),
                  smem_tile(lambda t, f, te, nu: (jnp.minimum(t + 1, n_tiles - 1), 0, 0)),
                  smem_tile(lambda t, f, te, nu: (t, 0, 0)),
                  pl.BlockSpec(memory_space=pl.ANY),
                  pl.BlockSpec((1, D), lambda t, f, te, nu: (0, 0)),
                  pl.BlockSpec((None, None, D, tf), lambda t, f, te, nu: wmap(t, f, te, nu) + (0, 0)),
                  pl.BlockSpec((None, None, D, tf), lambda t, f, te, nu: wmap(t, f, te, nu) + (0, 0)),
                  pl.BlockSpec((None, tf, D), lambda t, f, te, nu: wmap(t, f, te, nu) + (0,))],
        out_specs=pl.BlockSpec(memory_space=pl.ANY),
        scratch_shapes=[pltpu.VMEM((2, tm, D), F32), pltpu.VMEM((tm, D), BF16), pltpu.VMEM((tm, D), F32),
                        pltpu.VMEM((2, tm, D), F32),
                        pltpu.SemaphoreType.DMA((2,)), pltpu.SemaphoreType.DMA((2,))])
    need = tm * D * (8 + 2 + 4 + 8) + 2 * 3 * D * tf * 2 + 4 * tm * tf * 4
    return pl.pallas_call(
        _moe_kernel,
        grid_spec=grid_spec,
        out_shape=jax.ShapeDtypeStruct((TOP_K * M + tm, D), F32),
        compiler_params=pltpu.CompilerParams(dimension_semantics=("arbitrary", "arbitrary"),
                                             vmem_limit_bytes=_vmem_limit(need + (8 << 20))),
        name="moe_experts",
    )(tile_e, n_used, src_token, src_token, prev_out_row, h, g.reshape(1, D), w1t, w3t, w2)


def _moe_combine_kernel(h_ref, p_ref, g_ref, *refs, emit_h):
    y_refs, out_refs = refs[:TOP_K], refs[TOP_K:]
    hn = h_ref[...]
    for k in range(TOP_K):
        hn = hn + p_ref[:, k:k + 1] * y_refs[k][...]
    if emit_h:
        out_refs[0][...] = hn
    out_refs[-1][...] = _rms(hn, g_ref[...]).astype(out_refs[-1].dtype)


def moe_combine_call(h, y_tok, probs, g_next, n_dtype, emit_h, tm=MOE_COMBINE_TM):
    M, D = h.shape
    tm = min(tm, M)
    rowblk = pl.BlockSpec((tm, D), lambda i: (i, 0))
    out_specs = [rowblk]
    out_shape = [jax.ShapeDtypeStruct((M, D), n_dtype)]
    if emit_h:
        out_specs = [rowblk, rowblk]
        out_shape = [jax.ShapeDtypeStruct((M, D), F32)] + out_shape
    y_specs = [pl.BlockSpec((tm, D), functools.partial(lambda k, i: (k * (M // tm) + i, 0), k))
               for k in range(TOP_K)]
    return pl.pallas_call(
        functools.partial(_moe_combine_kernel, emit_h=emit_h),
        grid=(M // tm,),
        in_specs=[rowblk, pl.BlockSpec((tm, ROUTER_LANES), lambda i: (i, 0)),
                  pl.BlockSpec((1, D), lambda i: (0, 0))] + y_specs,
        out_specs=out_specs,
        out_shape=out_shape,
        compiler_params=pltpu.CompilerParams(dimension_semantics=("parallel",),
                                             vmem_limit_bytes=_vmem_limit(14 * tm * D * 4)),
        name="moe_combine",
    )(h, probs, g_next.reshape(1, D), *([y_tok] * TOP_K))


def moe_weight_tiling(w1, w3, w2):
    return (("w1", w1, MOE_TF), ("w3", w3, MOE_TF), ("w2", w2, None))


def kernel(x, norm_mix, w_in, gate_b, gm_ln_g, gm_ln_b, gm_ws, gm_bs, rw_mu, rw_w0, rw_w2, rw_a0, rw_a2,
           rw_g2, rw_kk, rw_ka, rw_rk, rw_ln_g, rw_ln_b, p_gm, p_rw, p_sb, w_o, norm_ffn, ffn_w1, ffn_w3,
           ffn_w2, router_w, router_b, moe_w1, moe_w3, moe_w2, norm_out):
    B, T, D = x.shape
    M = B * T
    bf = lambda a: a.astype(BF16)
    h = x.reshape(M, D)
    n = rmsnorm_call(h, norm_mix[0], BF16)
    moe_bf16 = {}
    for l in range(DEPTH):
        last = l == DEPTH - 1
        g_next = norm_out if last else norm_mix[l + 1]
        n_dtype = x.dtype if last else BF16
        w = w_in[l]
        j = l // 2
        uv = matmul_call(n, bf(w[:, GM_OFF:RW_OFF]), BF16, tn=GM_COLS, name="in_proj_gm")
        p = matmul_call(n, bf(w[:, RW_OFF:SB_OFF]), BF16, tn=RW_COLS, name="in_proj_rw")
        qkv = matmul_call(n, bf(w[:, SB_OFF:GATE_OFF]), BF16, tn=1024, name="in_proj_sb")
        gates = matmul_call(n, bf(w[:, GATE_OFF:]), BF16, tn=1024, bias=gate_b[l].reshape(-1),
                            name="in_proj_gate")
        y_gm = gm_call(uv, gm_ln_g[l], gm_ln_b[l], gm_ws[l], gm_bs[l])
        y_rw = rw_call(p.reshape(B, T, RW_COLS), B, T, rw_mu[l], rw_w0[l], rw_w2[l], rw_a0[l], rw_a2[l],
                       rw_g2[l], rw_kk[l], rw_ka[l], rw_rk[l], rw_ln_g[l], rw_ln_b[l]).reshape(M, RW_WIDTH)
        y_sb = sb_call(qkv.reshape(B, T, SB_COLS), B, T).reshape(M, SB_WIDTH)
        merge_args = (y_gm, y_rw, y_sb, gates, h, bf(p_gm[l]), bf(p_rw[l]), bf(p_sb[l]), bf(w_o[l]), norm_ffn[l])
        if l % 2 == 0:
            h, n2 = merge_call(*merge_args)
            ahead = []
            if not last:
                ahead = [(k, wt[(l + 1) // 2], ct) for k, wt, ct in moe_weight_tiling(moe_w1, moe_w3, moe_w2)
                         if can_ride(wt[(l + 1) // 2], ffn_steps(M, D_FF))]
            outs, casts = ffn_call(n2, h, bf(ffn_w1[j]), bf(ffn_w3[j]), bf(ffn_w2[j]), g_next, n_dtype,
                                   emit_h=not last, riders=[(wt, ct) for _, wt, ct in ahead])
            moe_bf16 = {k: c for (k, _, _), c in zip(ahead, casts)}
        else:
            h, ids, probs = merge_call(*merge_args, router=(router_w[j], router_b[j]))
            src_token, prev_out_row, tile_e, n_used = moe_plan(ids, M, MOE_TM)
            ew = {k: moe_bf16[k] if k in moe_bf16 else bf(wt[j] if ct is None else col_tiled(wt[j], ct))
                  for k, wt, ct in moe_weight_tiling(moe_w1, moe_w3, moe_w2)}
            y_tok = moe_call(h, norm_ffn[l], src_token, prev_out_row, tile_e, n_used, ew["w1"], ew["w3"],
                             ew["w2"], MOE_TM)
            outs = moe_combine_call(h, y_tok, probs, g_next, n_dtype, emit_h=not last)
            moe_bf16 = {}
        if last:
            n = outs[0]
        else:
            h, n = outs
    return n.reshape(B, T, D)
```

```python
import functools

import jax
import jax.numpy as jnp
from jax import lax
from jax.experimental import pallas as pl
from jax.experimental.pallas import tpu as pltpu

F32 = jnp.float32
BF16 = jnp.bfloat16

D_MODEL = 2048
DEPTH = 2
HEAD_DIM = 64
GM_HEADS = 8
GM_WIDTH = GM_HEADS * HEAD_DIM
CHUNK = 128
RW_HEADS = 8
RW_WIDTH = RW_HEADS * HEAD_DIM
DECAY_LORA = 64
AAA_LORA = 64
GATE_LORA = 128
SB_HEADS = 16
SB_WIDTH = SB_HEADS * HEAD_DIM
N_BRANCH = 3
D_FF = 5632
N_EXPERTS = 8
TOP_K = 2
D_FF_EXPERT = D_FF // TOP_K
RMS_EPS = 1e-6
LN_EPS = 1e-5
RW_GN_EPS = 64e-5

GM_COLS = 2 * GM_WIDTH
RW_COLS = 3 * RW_WIDTH + DECAY_LORA + AAA_LORA + GATE_LORA
SB_COLS = 3 * SB_WIDTH
GATE_COLS = N_BRANCH * D_MODEL
GM_OFF = 0
RW_OFF = GM_OFF + GM_COLS
SB_OFF = RW_OFF + RW_COLS
GATE_OFF = SB_OFF + SB_COLS
N_IN = GATE_OFF + GATE_COLS

V7X_VMEM_BYTES = 64 * 1024 * 1024
LANES = 128
BF16_SUBLANES = 16
RW_CHUNK = 64
RW_TILE = 256
RW_GROUP = 4
SB_TILE = 256
LOG2_E = 1.4426950408889634
SB_EXP2_UNDERFLOW = -151.0
ROUTER_LANES = 128
MOE_TF = 256
MOE_TM = 528
MOE_COMBINE_TM = 256
ROW_DMA_PRIORITY = 1


def _vmem_limit(nbytes):
    return int(min(max(nbytes, 32 * 1024 * 1024), V7X_VMEM_BYTES - 6 * 1024 * 1024))


def _dot(a, b):
    return jnp.dot(a, b, preferred_element_type=F32)


def _dot_nt(a, b):
    return lax.dot_general(a, b, (((1,), (1,)), ((), ())), preferred_element_type=F32)


def _dot_tn(a, b):
    return lax.dot_general(a, b, (((0,), (0,)), ((), ())), preferred_element_type=F32)


def _dot_split(x, ones_bf16):
    hi = x.astype(BF16)
    lo = (x - hi.astype(F32)).astype(BF16)
    return _dot(hi, ones_bf16) + _dot(lo, ones_bf16)


def _rms(x, g):
    ms = jnp.mean(x * x, axis=-1, keepdims=True)
    return x * lax.rsqrt(ms + RMS_EPS) * g


def _rmsnorm_kernel(x_ref, g_ref, o_ref):
    o_ref[...] = _rms(x_ref[...], g_ref[...]).astype(o_ref.dtype)


def rmsnorm_call(x, g, out_dtype, tm=512):
    M, D = x.shape
    tm = min(tm, M)
    return pl.pallas_call(
        _rmsnorm_kernel,
        grid=(M // tm,),
        in_specs=[pl.BlockSpec((tm, D), lambda i: (i, 0)),
                  pl.BlockSpec((1, D), lambda i: (0, 0))],
        out_specs=pl.BlockSpec((tm, D), lambda i: (i, 0)),
        out_shape=jax.ShapeDtypeStruct((M, D), out_dtype),
        compiler_params=pltpu.CompilerParams(dimension_semantics=("parallel",)),
        name="rmsnorm",
    )(x, g.reshape(1, D))


def _mm_kernel(a_ref, w_ref, o_ref):
    o_ref[...] = _dot(a_ref[...], w_ref[...]).astype(o_ref.dtype)


def _mm_gate_kernel(a_ref, w_ref, b_ref, o_ref):
    o_ref[...] = jax.nn.sigmoid(_dot(a_ref[...], w_ref[...]) + b_ref[...]).astype(o_ref.dtype)


def matmul_call(a, w, out_dtype, tn, bias=None, tm=1024, name="matmul"):
    M, K = a.shape
    N = w.shape[1]
    tm = min(tm, M)
    in_specs = [pl.BlockSpec((tm, K), lambda i, j: (i, 0)),
                pl.BlockSpec((K, tn), lambda i, j: (0, j))]
    args = [a, w]
    kern = _mm_kernel
    if bias is not None:
        in_specs.append(pl.BlockSpec((1, tn), lambda i, j: (0, j)))
        args.append(bias.reshape(1, N))
        kern = _mm_gate_kernel
    osz = jnp.dtype(out_dtype).itemsize
    need = 2 * (tm * K * 2 + K * tn * 2 + tm * tn * osz) + 2 * tm * tn * 4
    return pl.pallas_call(
        kern,
        grid=(M // tm, N // tn),
        in_specs=in_specs,
        out_specs=pl.BlockSpec((tm, tn), lambda i, j: (i, j)),
        out_shape=jax.ShapeDtypeStruct((M, N), out_dtype),
        compiler_params=pltpu.CompilerParams(
            dimension_semantics=("parallel", "arbitrary"),
            vmem_limit_bytes=_vmem_limit(need + (8 << 20))),
        name=name,
    )(*args)


def _gm_kernel(uv_ref, lng_ref, lnb_ref, ws_ref, bias_ref, o_ref, *, n_chunks):
    ii = lax.broadcasted_iota(jnp.int32, (CHUNK, CHUNK), 0)
    jj = lax.broadcasted_iota(jnp.int32, (CHUNK, CHUNK), 1)
    causal = jj <= ii
    ws = [jnp.where(causal, ws_ref[h], 0.0).astype(BF16) for h in range(GM_HEADS)]
    for c in range(n_chunks):
        rows = slice(c * CHUNK, (c + 1) * CHUNK)
        u = jax.nn.gelu(uv_ref[rows, 0:GM_WIDTH].astype(F32))
        v = jax.nn.gelu(uv_ref[rows, GM_WIDTH:2 * GM_WIDTH].astype(F32))
        mu = jnp.mean(v, axis=-1, keepdims=True)
        vc = v - mu
        var = jnp.mean(vc * vc, axis=-1, keepdims=True)
        vn = (vc * lax.rsqrt(var + LN_EPS) * lng_ref[...] + lnb_ref[...]).astype(BF16)
        mixed = jnp.concatenate(
            [_dot(ws[h], vn[:, h * HEAD_DIM:(h + 1) * HEAD_DIM]) for h in range(GM_HEADS)], axis=-1)
        o_ref[rows, :] = (u * (mixed + bias_ref[...])).astype(o_ref.dtype)


def gm_call(uv, ln_g, ln_b, w_s, b_s, n_chunks=2):
    M = uv.shape[0]
    tt = CHUNK * n_chunks
    bias = jnp.repeat(b_s.T, HEAD_DIM, axis=1)
    return pl.pallas_call(
        functools.partial(_gm_kernel, n_chunks=n_chunks),
        grid=(M // tt,),
        in_specs=[pl.BlockSpec((tt, GM_COLS), lambda i: (i, 0)),
                  pl.BlockSpec((1, GM_WIDTH), lambda i: (0, 0)),
                  pl.BlockSpec((1, GM_WIDTH), lambda i: (0, 0)),
                  pl.BlockSpec((GM_HEADS, CHUNK, CHUNK), lambda i: (0, 0, 0)),
                  pl.BlockSpec((CHUNK, GM_WIDTH), lambda i: (0, 0))],
        out_specs=pl.BlockSpec((tt, GM_WIDTH), lambda i: (i, 0)),
        out_shape=jax.ShapeDtypeStruct((M, GM_WIDTH), BF16),
        compiler_params=pltpu.CompilerParams(dimension_semantics=("parallel",)),
        name="gmlp_gating",
    )(uv, ln_g.reshape(1, -1), ln_b.reshape(1, -1), w_s, bias)


def _rw_kernel(p_ref, prev_ref, mu_ref, w0_ref, w2_ref, a0_ref, a2_ref, g2_ref, kk_ref, ka_ref,
               rk_ref, lng_ref, lnb_ref, hsum_ref, o_ref, st_ref, y_ref):
    C = RW_CHUNK
    i = pl.program_id(1)

    @pl.when(i == 0)
    def _():
        st_ref[...] = jnp.zeros_like(st_ref)

    hsum = hsum_ref[...]
    gw = hsum.shape[0]

    def head_sums(x):
        return jnp.concatenate([_dot_split(x[:, q * gw:(q + 1) * gw], hsum)
                                for q in range(x.shape[1] // gw)], axis=1)

    p = p_ref[...].astype(F32)
    prev_row = prev_ref[BF16_SUBLANES - 1:BF16_SUBLANES, :].astype(F32)
    prev_row = jnp.where(i == 0, 0.0, prev_row)
    row = lax.broadcasted_iota(jnp.int32, p.shape, 0)
    p_prev = jnp.where(row == 0, prev_row, pltpu.roll(p, 1, axis=0))
    ps = p + (p_prev - p) * mu_ref[...]
    W = RW_WIDTH
    r = ps[:, 0:W]
    k = ps[:, W:2 * W]
    v = ps[:, 2 * W:3 * W]
    wl = ps[:, 3 * W:3 * W + DECAY_LORA]
    al = ps[:, 3 * W + DECAY_LORA:3 * W + DECAY_LORA + AAA_LORA]
    gl = ps[:, 3 * W + DECAY_LORA + AAA_LORA:RW_COLS]

    d = w0_ref[...] + _dot(jnp.tanh(wl).astype(BF16), w2_ref[...])
    log_w = -jnp.exp(-jax.nn.softplus(-d) - 0.5)
    a = jax.nn.sigmoid(a0_ref[...] + _dot(al.astype(BF16), a2_ref[...]))
    g = _dot(jax.nn.sigmoid(gl).astype(BF16), g2_ref[...])
    kk = k * kk_ref[...]
    kk = kk / jnp.maximum(jnp.sqrt(head_sums(kk * kk)), 1e-12)
    k = k * (1.0 + (a - 1.0) * ka_ref[...])
    bonus = head_sums(r * k * rk_ref[...]) * v

    HD = HEAD_DIM
    GW = RW_GROUP * HD
    ci = lax.broadcasted_iota(jnp.int32, (C, C), 0)
    cj = lax.broadcasted_iota(jnp.int32, (C, C), 1)
    tri_incl = (cj <= ci).astype(BF16)
    ti = lax.broadcasted_iota(jnp.int32, (C, GW), 0)
    tj = lax.broadcasted_iota(jnp.int32, (C, GW), 1) % HD
    strict = tj < ti
    incl = tj <= ti
    eye = ti == tj
    blk_masks = []
    s = 2
    while s < C:
        blk_masks.append((ti // (2 * s) == tj // (2 * s)) & (ti // s != tj // s))
        s *= 2
    pair_mask = (ti // 2 == tj // 2) & strict
    gi = lax.broadcasted_iota(jnp.int32, (GW, GW), 0)
    gj = lax.broadcasted_iota(jnp.int32, (GW, GW), 1)
    bd_mask = gi // HD == gj // HD
    bd_mask_b = jnp.where(bd_mask, 1.0, 0.0).astype(BF16)
    gw_eye = gi == gj

    def bd(x):
        xb = x.astype(BF16)
        return jnp.where(bd_mask_b > 0, jnp.concatenate([xb] * RW_GROUP, axis=0), 0)

    def bd_of_full(full):
        return jnp.where(bd_mask, full, 0.0)

    n_chunks = RW_TILE // C
    n_groups = RW_WIDTH // GW
    systems = [(c, q) for c in range(n_chunks) for q in range(n_groups)]

    pre = []
    for c in range(n_chunks):
        rows = slice(c * C, (c + 1) * C)
        lw = log_w[rows]
        cum = _cumsum_rows(lw, tri_incl)
        gam = jnp.exp(cum)
        gam_prev = jnp.exp(cum - lw)
        gam_inv = jnp.exp(-cum)
        g_end = gam[C - 1:C, :]
        bh_f = kk[rows] * a[rows] * gam_inv
        kh_f = k[rows] * gam_inv
        pre.append(dict(
            rh=(r[rows] * gam).astype(BF16), ah=(-kk[rows] * gam_prev).astype(BF16),
            bh=bh_f.astype(BF16), kh=kh_f.astype(BF16),
            bt=(bh_f * g_end).astype(BF16), kt=(kh_f * g_end).astype(BF16),
            v=v[rows].astype(BF16), g_end=g_end))

    def grp(c, q, name):
        return pre[c][name][:, q * GW:(q + 1) * GW]

    st = {}
    for (c, q) in systems:
        ar = jnp.concatenate([grp(c, q, "ah"), grp(c, q, "rh")], axis=0)
        pb = _dot_nt(ar, bd(grp(c, q, "bh")))
        pk = _dot_nt(ar, bd(grp(c, q, "kh")))
        lab = jnp.where(strict, pb[0:C], 0.0)
        st[(c, q)] = dict(
            lab=lab, mrb=jnp.where(incl, pb[C:2 * C], 0.0).astype(BF16),
            lm=jnp.concatenate([jnp.where(strict, pk[0:C], 0.0), jnp.where(incl, pk[C:2 * C], 0.0)],
                               axis=0).astype(BF16),
            tinv=jnp.where(eye, 1.0, jnp.where(pair_mask, lab, 0.0)))

    for bm in blk_masks:
        xs = {}
        for key in systems:
            d_ = st[key]
            xs[key] = _dot(d_["tinv"].astype(BF16), bd(jnp.where(bm, d_["lab"], 0.0))).astype(BF16)
        for key in systems:
            d_ = st[key]
            d_["tinv"] = d_["tinv"] + _dot(xs[key], bd(d_["tinv"]))

    for (c, q) in systems:
        d_ = st[(c, q)]
        lv = _dot(d_["lm"], bd(grp(c, q, "v")))
        tb = d_["tinv"].astype(BF16)
        ap = _dot(tb, bd(grp(c, q, "ah"))).astype(BF16)
        up = _dot(tb, bd(lv[0:C])).astype(BF16)
        d_["rp"] = (grp(c, q, "rh").astype(F32) + _dot(d_["mrb"], bd(ap))).astype(BF16)
        d_["ypp"] = _dot(d_["mrb"], bd(up)) + lv[C:2 * C]
        g_end = pre[c]["g_end"][:, q * GW:(q + 1) * GW]
        gt = _dot_tn(ap, grp(c, q, "bt"))
        d_["gt"] = jnp.where(gw_eye, gt + g_end, bd_of_full(gt)).astype(BF16)
        ht = bd_of_full(_dot_tn(up, grp(c, q, "bt")) + _dot_tn(grp(c, q, "v"), grp(c, q, "kt")))
        d_["ht"] = sum(ht[hh * HD:(hh + 1) * HD] for hh in range(1, RW_GROUP)) + ht[0:HD]

    for q in range(n_groups):
        s_cur = st_ref[q]
        for c in range(n_chunks):
            d_ = st[(c, q)]
            y_ref[c * C:(c + 1) * C, q * GW:(q + 1) * GW] = _dot_nt(d_["rp"], bd(s_cur)) + d_["ypp"]
            s_cur = _dot(s_cur.astype(BF16), d_["gt"]) + d_["ht"]
        st_ref[q] = s_cur

    y = y_ref[...]
    inv_n = 1.0 / HEAD_DIM
    m = head_sums(y) * inv_n
    yc = y - m
    var = head_sums(yc * yc) * inv_n
    yn = yc * lax.rsqrt(var + RW_GN_EPS) * lng_ref[...] + lnb_ref[...]
    o_ref[...] = ((yn + bonus) * g).astype(o_ref.dtype)


def _cumsum_rows(x, tri_incl_bf16):
    hi = x.astype(BF16)
    lo = (x - hi.astype(F32)).astype(BF16)
    return _dot(tri_incl_bf16, hi) + _dot(tri_incl_bf16, lo)


def rw_call(p, B, T, mu, w0, w2, a0, a2, g2, k_k, k_a, r_k, ln_g, ln_b):
    nt = T // RW_TILE
    sub = RW_TILE // BF16_SUBLANES
    hid = jnp.arange(RW_GROUP * HEAD_DIM) // HEAD_DIM
    hsum = (hid[:, None] == hid[None, :]).astype(BF16)
    row = lambda a: a.reshape(1, -1).astype(F32)
    vec = lambda n: pl.BlockSpec((1, n), lambda b, i: (0, 0))
    return pl.pallas_call(
        _rw_kernel,
        grid=(B, nt),
        in_specs=[pl.BlockSpec((None, RW_TILE, RW_COLS), lambda b, i: (b, i, 0)),
                  pl.BlockSpec((None, BF16_SUBLANES, RW_COLS),
                               lambda b, i: (b, jnp.maximum(i * sub - 1, 0), 0)),
                  vec(RW_COLS), vec(RW_WIDTH),
                  pl.BlockSpec((DECAY_LORA, RW_WIDTH), lambda b, i: (0, 0)),
                  vec(RW_WIDTH),
                  pl.BlockSpec((AAA_LORA, RW_WIDTH), lambda b, i: (0, 0)),
                  pl.BlockSpec((GATE_LORA, RW_WIDTH), lambda b, i: (0, 0)),
                  vec(RW_WIDTH), vec(RW_WIDTH), vec(RW_WIDTH), vec(RW_WIDTH), vec(RW_WIDTH),
                  pl.BlockSpec((RW_GROUP * HEAD_DIM, RW_GROUP * HEAD_DIM), lambda b, i: (0, 0))],
        out_specs=pl.BlockSpec((None, RW_TILE, RW_WIDTH), lambda b, i: (b, i, 0)),
        out_shape=jax.ShapeDtypeStruct((B, T, RW_WIDTH), BF16),
        scratch_shapes=[pltpu.VMEM((RW_HEADS // RW_GROUP, HEAD_DIM, RW_GROUP * HEAD_DIM), F32),
                        pltpu.VMEM((RW_TILE, RW_WIDTH), F32)],
        compiler_params=pltpu.CompilerParams(dimension_semantics=("parallel", "arbitrary")),
        name="rwkv7_time_mix",
    )(p, p, row(mu), row(w0), w2.astype(BF16), row(a0), a2.astype(BF16), g2.astype(BF16),
      row(k_k), row(k_a), row(r_k), row(ln_g), row(ln_b), hsum)


def _sb_scores(qs, ks, masks, suffix_ones):
    n = range(len(qs))
    zs = [_dot_nt(qs[s], ks[s]) * LOG2_E for s in n]
    sps = [jnp.maximum(z, 0.0) + jnp.log2(1.0 + jnp.exp2(-jnp.abs(z))) for z in zs]
    log_nots = [-sp for sp in sps]
    log_nots = [ln if masks[s] is None else jnp.where(masks[s], ln, 0.0) for s, ln in enumerate(log_nots)]
    his = [ln.astype(BF16) for ln in log_nots]
    los = [(log_nots[s] - his[s].astype(F32)).astype(BF16) for s in n]
    sufs = [_dot(his[s], suffix_ones) + _dot(los[s], suffix_ones) for s in n]
    rowsums = [jnp.sum(ln, axis=-1, keepdims=True) for ln in log_nots]
    return zs, sps, sufs, rowsums


def _sb_weighted(scores, s, carry, mask, v):
    zs, sps, sufs, _ = scores
    att = jnp.exp2(zs[s] - sps[s] + (sufs[s] + carry))
    if mask is not None:
        att = jnp.where(mask, att, 0.0)
    return _dot(att.astype(BF16), v)


def _sb_kernel(q_ref, k_ref, v_ref, o_ref, acc_ref, carry_ref, *, T):
    tq = SB_TILE
    nq = T // tq
    ti = lax.broadcasted_iota(jnp.int32, (tq, tq), 0)
    tj = lax.broadcasted_iota(jnp.int32, (tq, tq), 1)
    suffix_ones = (ti > tj).astype(BF16)
    causal = tj < ti
    scale = HEAD_DIM ** -0.5
    heads = [slice(hh * HEAD_DIM, (hh + 1) * HEAD_DIM) for hh in range(LANES // HEAD_DIM)]
    nh = len(heads)

    def q_body(i, _):
        r0 = pl.multiple_of(i * tq, tq)
        has_prev = jnp.where(i > 0, 1.0, 0.0)
        p0 = pl.multiple_of(jnp.maximum(i - 1, 0) * tq, tq)
        qs = [q_ref[pl.ds(r0, tq), cs] * scale for cs in heads]
        ks = [k_ref[pl.ds(r0, tq), cs] for cs in heads] + [k_ref[pl.ds(p0, tq), cs] for cs in heads]
        vs = [v_ref[pl.ds(r0, tq), cs] for cs in heads] + [v_ref[pl.ds(p0, tq), cs] for cs in heads]
        masks = [causal] * nh + [None] * nh
        scores = _sb_scores(qs + qs, ks, masks, suffix_ones)
        rss = scores[3]
        live = jnp.float32(-jnp.inf)
        for hh, cs in enumerate(heads):
            out = _sb_weighted(scores, hh, 0.0, causal, vs[hh])
            out_prev = _sb_weighted(scores, nh + hh, rss[hh], None, vs[nh + hh])
            acc_ref[:, cs] = out + has_prev * out_prev
            carry = rss[hh] + has_prev * rss[nh + hh]
            carry_ref[hh] = carry
            live = jnp.maximum(live, jnp.max(carry))

        def k_cond(state):
            jr, live = state
            return (jr < i) & (live > SB_EXP2_UNDERFLOW)

        def k_body(state):
            jr, _ = state
            c0 = pl.multiple_of((i - 1 - jr) * tq, tq)
            kt = [k_ref[pl.ds(c0, tq), cs] for cs in heads]
            vt = [v_ref[pl.ds(c0, tq), cs] for cs in heads]
            scores = _sb_scores(qs, kt, [None] * nh, suffix_ones)
            live = jnp.float32(-jnp.inf)
            for hh, cs in enumerate(heads):
                carry = carry_ref[hh]
                acc_ref[:, cs] += _sb_weighted(scores, hh, carry, None, vt[hh])
                carry = carry + scores[3][hh]
                carry_ref[hh] = carry
                live = jnp.maximum(live, jnp.max(carry))
            return jr + 1, live

        lax.while_loop(k_cond, k_body, (jnp.int32(1), live))
        o_ref[pl.ds(r0, tq), :] = acc_ref[...].astype(o_ref.dtype)
        return 0

    lax.fori_loop(0, nq, q_body, 0)


def sb_call(qkv, B, T):
    npair = SB_WIDTH // LANES
    blk = lambda off: pl.BlockSpec((None, T, LANES), lambda b, hp: (b, 0, off + hp))
    return pl.pallas_call(
        functools.partial(_sb_kernel, T=T),
        grid=(B, npair),
        in_specs=[blk(0), blk(npair), blk(2 * npair)],
        out_specs=pl.BlockSpec((None, T, LANES), lambda b, hp: (b, 0, hp)),
        out_shape=jax.ShapeDtypeStruct((B, T, SB_WIDTH), BF16),
        scratch_shapes=[pltpu.VMEM((SB_TILE, LANES), F32),
                        pltpu.VMEM((LANES // HEAD_DIM, SB_TILE, 1), F32)],
        compiler_params=pltpu.CompilerParams(dimension_semantics=("parallel", "parallel")),
        name="stick_breaking_attention",
    )(qkv, qkv, qkv)


def _route(n, w_ref, b_ref, id_ref, p_ref):
    nh = n.astype(BF16)
    nl = (n - nh.astype(F32)).astype(BF16)
    a = _dot(nh, w_ref[...])
    b = _dot(nl, w_ref[...])
    L = ROUTER_LANES
    logits = a[:, :L] + (a[:, L:] + b[:, :L] + b[:, L:]) + b_ref[...]
    lane = lax.broadcasted_iota(jnp.int32, logits.shape, 1)
    neg = jnp.float32(-jnp.inf)
    logits = jnp.where(lane < N_EXPERTS, logits, neg)
    m1 = jnp.max(logits, axis=-1, keepdims=True)
    i1 = jnp.min(jnp.where(logits == m1, lane, ROUTER_LANES), axis=-1, keepdims=True)
    rest = jnp.where(lane == i1, neg, logits)
    m2 = jnp.max(rest, axis=-1, keepdims=True)
    i2 = jnp.min(jnp.where(rest == m2, lane, ROUTER_LANES), axis=-1, keepdims=True)
    e2 = jnp.exp(m2 - m1)
    p1 = 1.0 / (1.0 + e2)
    p2 = e2 / (1.0 + e2)
    id_ref[...] = jnp.where(lane == 0, i1, jnp.where(lane == 1, i2, 0))
    p_ref[...] = jnp.where(lane == 0, p1, jnp.where(lane == 1, p2, 0.0))


def _merge_kernel(ygm_ref, yrw_ref, ysb_ref, gate_ref, h_ref, pgm_ref, prw_ref, psb_ref, wo_ref,
                  g_ref, *refs, route):
    D = D_MODEL
    merged = gate_ref[:, 0:D].astype(F32) * _dot(ygm_ref[...], pgm_ref[...])
    merged += gate_ref[:, D:2 * D].astype(F32) * _dot(yrw_ref[...], prw_ref[...])
    merged += gate_ref[:, 2 * D:3 * D].astype(F32) * _dot(ysb_ref[...], psb_ref[...])
    hn = h_ref[...] + _dot(merged.astype(BF16), wo_ref[...])
    n = _rms(hn, g_ref[...])
    if route:
        w_ref, b_ref, ho_ref, id_ref, p_ref = refs
        _route(n, w_ref, b_ref, id_ref, p_ref)
    else:
        ho_ref, no_ref = refs
        no_ref[...] = n.astype(no_ref.dtype)
    ho_ref[...] = hn


def _resident(shape):
    return pl.BlockSpec(shape, lambda i: (0,) * len(shape), pipeline_mode=pl.Buffered(1))


def merge_call(y_gm, y_rw, y_sb, gates, h, p_gm, p_rw, p_sb, w_o, g_next, router=None, tm=256):
    M, D = h.shape
    tm = min(tm, M)
    rowblk = lambda n: pl.BlockSpec((tm, n), lambda i: (i, 0))
    wbytes = 2 * (GM_WIDTH + RW_WIDTH + SB_WIDTH + D) * D
    need = wbytes + 2 * tm * (2 * (GM_WIDTH + RW_WIDTH + SB_WIDTH + GATE_COLS) + 4 * D + 4 * D + 2 * D) \
        + 6 * tm * D * 4
    in_specs = [rowblk(GM_WIDTH), rowblk(RW_WIDTH), rowblk(SB_WIDTH), rowblk(GATE_COLS), rowblk(D),
                _resident((GM_WIDTH, D)), _resident((RW_WIDTH, D)), _resident((SB_WIDTH, D)),
                _resident((D, D)), _resident((1, D))]
    args = [y_gm, y_rw, y_sb, gates, h, p_gm, p_rw, p_sb, w_o, g_next.reshape(1, D)]
    if router is None:
        out_specs = [rowblk(D), rowblk(D)]
        out_shape = [jax.ShapeDtypeStruct((M, D), F32), jax.ShapeDtypeStruct((M, D), BF16)]
    else:
        router_w, router_b = router
        wpad = jnp.zeros((D, ROUTER_LANES), F32).at[:, :N_EXPERTS].set(router_w)
        w_hi = wpad.astype(BF16)
        w_lo = (wpad - w_hi.astype(F32)).astype(BF16)
        bpad = jnp.zeros((1, ROUTER_LANES), F32).at[0, :N_EXPERTS].set(router_b)
        in_specs += [_resident((D, 2 * ROUTER_LANES)), _resident((1, ROUTER_LANES))]
        args += [jnp.concatenate([w_hi, w_lo], axis=1), bpad]
        out_specs = [rowblk(D), rowblk(ROUTER_LANES), rowblk(ROUTER_LANES)]
        out_shape = [jax.ShapeDtypeStruct((M, D), F32), jax.ShapeDtypeStruct((M, ROUTER_LANES), jnp.int32),
                     jax.ShapeDtypeStruct((M, ROUTER_LANES), F32)]
    return pl.pallas_call(
        functools.partial(_merge_kernel, route=router is not None),
        grid=(M // tm,),
        in_specs=in_specs,
        out_specs=out_specs,
        out_shape=out_shape,
        compiler_params=pltpu.CompilerParams(dimension_semantics=("parallel",),
                                             vmem_limit_bytes=_vmem_limit(need)),
        name="merge_out_proj",
    )(*args)


def _ffn_kernel(n_ref, h_ref, w1_ref, w3_ref, w2_ref, g_ref, *refs, emit_h, n_cast):
    cast_in, refs = refs[:n_cast], refs[n_cast:]
    if emit_h:
        ho_ref, no_ref = refs[:2]
        refs = refs[2:]
    else:
        no_ref = refs[0]
        refs = refs[1:]
    cast_out, acc_ref = refs[:n_cast], refs[n_cast]
    f = pl.program_id(1)

    @pl.when(f == 0)
    def _():
        acc_ref[...] = jnp.zeros_like(acc_ref)

    n = n_ref[...]
    a = _dot(n, w1_ref[...])
    b = _dot(n, w3_ref[...])
    for src, dst in zip(cast_in, cast_out):
        if len(dst.shape) == 3:
            ct = dst.shape[2]
            for c in range(dst.shape[0]):
                dst[c] = src[:, c * ct:(c + 1) * ct].astype(dst.dtype)
        else:
            dst[...] = src[...].astype(dst.dtype)
    act = (jax.nn.silu(a) * b).astype(BF16)
    acc_ref[...] += _dot(act, w2_ref[...])

    @pl.when(f == pl.num_programs(1) - 1)
    def _():
        hn = h_ref[...] + acc_ref[...]
        if emit_h:
            ho_ref[...] = hn
        no_ref[...] = _rms(hn, g_ref[...]).astype(no_ref.dtype)


def ffn_steps(M, F, tm=512, tf=512):
    return (M // min(tm, M)) * (F // tf)


RIDER_SLAB_ROWS = 64


def can_ride(w, steps):
    return w.shape[1] % RIDER_SLAB_ROWS == 0 and w.shape[0] * (w.shape[1] // RIDER_SLAB_ROWS) <= steps


def col_tiled(w, ct):
    E, R, C = w.shape
    return w.reshape(E, R, C // ct, ct).transpose(0, 2, 1, 3)


def ffn_call(n, h, w1, w3, w2, g_next, n_dtype, emit_h, riders=(), tm=512, tf=512):
    M, D = h.shape
    F = w1.shape[1]
    tm = min(tm, M)
    nf = F // tf
    steps = (M // tm) * nf
    rowblk = pl.BlockSpec((tm, D), lambda i, f: (i, 0))
    out_specs = [rowblk]
    out_shape = [jax.ShapeDtypeStruct((M, D), n_dtype)]
    if emit_h:
        out_specs = [rowblk, rowblk]
        out_shape = [jax.ShapeDtypeStruct((M, D), F32)] + out_shape
    n_main = len(out_shape)

    def slab_specs(w, ct):
        E, R, C = w.shape
        per = R // RIDER_SLAB_ROWS
        n_slabs = E * per
        hold = steps // n_slabs
        slab = lambda i, f: jnp.minimum((i * nf + f) // hold, n_slabs - 1)
        src = pl.BlockSpec((None, RIDER_SLAB_ROWS, C), lambda i, f: (slab(i, f) // per, slab(i, f) % per, 0))
        if ct is None:
            return src, src, jax.ShapeDtypeStruct(w.shape, BF16)
        dst = pl.BlockSpec((None, C // ct, RIDER_SLAB_ROWS, ct),
                           lambda i, f: (slab(i, f) // per, 0, slab(i, f) % per, 0))
        return src, dst, jax.ShapeDtypeStruct((E, C // ct, R, ct), BF16)

    ride = [slab_specs(w, ct) for w, ct in riders]
    need = 2 * tm * D * (2 + 4 + 4 * n_main) + tm * D * 4 + 2 * 3 * D * tf * 2 + 4 * tm * tf * 4 \
        + sum(2 * RIDER_SLAB_ROWS * w.shape[2] * 6 for w, _ in riders)
    outs = pl.pallas_call(
        functools.partial(_ffn_kernel, emit_h=emit_h, n_cast=len(riders)),
        grid=(M // tm, nf),
        in_specs=[rowblk, rowblk,
                  pl.BlockSpec((D, tf), lambda i, f: (0, f)),
                  pl.BlockSpec((D, tf), lambda i, f: (0, f)),
                  pl.BlockSpec((tf, D), lambda i, f: (f, 0)),
                  pl.BlockSpec((1, D), lambda i, f: (0, 0))] + [r[0] for r in ride],
        out_specs=out_specs + [r[1] for r in ride],
        out_shape=out_shape + [r[2] for r in ride],
        scratch_shapes=[pltpu.VMEM((tm, D), F32)],
        compiler_params=pltpu.CompilerParams(dimension_semantics=("arbitrary", "arbitrary"),
                                             vmem_limit_bytes=_vmem_limit(need + (8 << 20))),
        name="swiglu_ffn",
    )(n, h, w1, w3, w2, g_next.reshape(1, D), *[w for w, _ in riders])
    return outs[:n_main], outs[n_main:]


def moe_plan(ids, M, tm):
    E = N_EXPERTS
    n_tiles = -(-(TOP_K * M) // tm) + E + 2
    e_flat = ids[:, :TOP_K].reshape(-1)
    onehot = (e_flat[:, None] == jnp.arange(E, dtype=jnp.int32)[None, :]).astype(jnp.int32)
    csum = jnp.cumsum(onehot, axis=0)
    rank = jnp.sum((csum - onehot) * onehot, axis=1)
    counts = csum[-1]
    tiles_per_e = (counts + tm - 1) // tm
    tile_end = jnp.cumsum(tiles_per_e)
    offs = (tile_end - tiles_per_e) * tm
    dest = offs[e_flat] + rank
    pair = jnp.full((n_tiles * tm,), -1, jnp.int32).at[dest].set(
        jnp.arange(TOP_K * M, dtype=jnp.int32), unique_indices=True)
    spare = TOP_K * M + jnp.arange(n_tiles * tm, dtype=jnp.int32) % tm
    src_token = jnp.where(pair >= 0, pair // TOP_K, 0)
    out_row = jnp.where(pair >= 0, (pair % TOP_K) * M + pair // TOP_K, spare)
    prev_out_row = jnp.concatenate([spare[:tm], out_row[:-tm]])
    n_used = tile_end[-1]
    t = jnp.arange(n_tiles, dtype=jnp.int32)
    tile_e = jnp.searchsorted(tile_end, jnp.minimum(t, n_used - 1), side="right").astype(jnp.int32)
    return (src_token.reshape(n_tiles, 1, tm), prev_out_row.reshape(n_tiles, 1, tm), tile_e,
            n_used.astype(jnp.int32).reshape(1))


def _bf16_bits(x):
    u = lax.bitcast_convert_type(x, jnp.uint32)
    return (u + jnp.uint32(0x7FFF) + ((u >> 16) & jnp.uint32(1))) & jnp.uint32(0xFFFF0000)


def _pack_bf16_pairs(x):
    half = x.shape[1] // 2
    return _bf16_bits(x[:, half:]) | (_bf16_bits(x[:, :half]) >> 16)


def _unpack_bf16_pairs(w):
    lo = lax.bitcast_convert_type(w << 16, F32)
    hi = lax.bitcast_convert_type(w & jnp.uint32(0xFFFF0000), F32)
    return jnp.concatenate([lo, hi], axis=1)


def _moe_kernel(tile_e_ref, n_used_ref, src_ref, nxt_ref, dst_ref, h_hbm, g_ref, w1_ref, w3_ref, w2_ref,
                y_hbm, xbuf, xb, acc, ostage, gsem, ssem):
    t = pl.program_id(0)
    f = pl.program_id(1)
    nf = pl.num_programs(1)
    tm = xb.shape[0]
    per_step = tm // nf
    n_used = n_used_ref[0]
    used = t < n_used
    cur = t % 2

    def fetch(idx_ref, r, slot):
        return pltpu.make_async_copy(h_hbm.at[pl.ds(idx_ref[0, r], 1)], xbuf.at[slot, pl.ds(r, 1)],
                                     gsem.at[slot])

    def put(r):
        return pltpu.make_async_copy(ostage.at[1 - cur, pl.ds(r, 1)], y_hbm.at[pl.ds(dst_ref[0, r], 1)],
                                     ssem.at[1 - cur])

    @pl.when((f == 0) & (t == 0))
    def _():
        def issue(r, c):
            fetch(src_ref, r, 0).start()
            return c
        lax.fori_loop(0, tm, issue, 0)
        ostage[1] = jnp.zeros((tm, ostage.shape[2]), ostage.dtype)

    @pl.when((f == 0) & (t <= n_used))
    def _():
        pltpu.make_async_copy(h_hbm.at[pl.ds(0, tm)], xbuf.at[cur], gsem.at[cur]).wait()

    @pl.when((f == 0) & (t >= 1) & (t - 1 <= n_used))
    def _():
        pltpu.make_async_copy(ostage.at[cur], y_hbm.at[pl.ds(0, tm)], ssem.at[cur]).wait()

    @pl.when((f == 0) & used)
    def _():
        xb[...] = _rms(xbuf[cur], g_ref[...]).astype(BF16)
        acc[...] = jnp.zeros_like(acc)

    @pl.when(used)
    def _():
        for j in range(per_step):
            fetch(nxt_ref, f * per_step + j, 1 - cur).start(priority=ROW_DMA_PRIORITY)
            put(f * per_step + j).start(priority=ROW_DMA_PRIORITY)
        x = xb[...]
        act = jax.nn.silu(_dot(x, w1_ref[...])) * _dot(x, w3_ref[...])
        acc[...] += _dot(act.astype(BF16), w2_ref[...])

    @pl.when(used & (f == nf - 1))
    def _():
        ostage[cur] = _pack_bf16_pairs(acc[...])

    @pl.when((f == 0) & (t == n_used))
    def _():
        def issue(r, c):
            put(r).start()
            return c
        lax.fori_loop(0, tm, issue, 0)


def moe_call(h, g, src_token, prev_out_row, tile_e, n_used, w1t, w3t, w2, tm):
    M, D = h.shape
    n_tiles = src_token.shape[0]
    E, nf, _, tf = w1t.shape
    assert tm % nf == 0

    def wmap(t, f, tile_e_ref, n_used_ref):
        ff = jnp.where(t < n_used_ref[0], f, nf - 1)
        return tile_e_ref[t], ff

    smem_tile = lambda fn: pl.BlockSpec((None, 1, tm), fn, memory_space=pltpu.SMEM)
    grid_spec = pltpu.PrefetchScalarGridSpec(
        num_scalar_prefetch=2,
        grid=(n_tiles, nf),
        in_specs=[smem_tile(lambda t, f, te, nu: (t, 0, 0)),
                  smem_tile(lambda t, f, te, nu: (jnp.minimum(t + 1, n_tiles - 1), 0, 0)),
                  smem_tile(lambda t, f, te, nu: (t, 0, 0)),
                  pl.BlockSpec(memory_space=pl.ANY),
                  pl.BlockSpec((1, D), lambda t, f, te, nu: (0, 0)),
                  pl.BlockSpec((None, None, D, tf), lambda t, f, te, nu: wmap(t, f, te, nu) + (0, 0)),
                  pl.BlockSpec((None, None, D, tf), lambda t, f, te, nu: wmap(t, f, te, nu) + (0, 0)),
                  pl.BlockSpec((None, tf, D), lambda t, f, te, nu: wmap(t, f, te, nu) + (0,))],
        out_specs=pl.BlockSpec(memory_space=pl.ANY),
        scratch_shapes=[pltpu.VMEM((2, tm, D), F32), pltpu.VMEM((tm, D), BF16), pltpu.VMEM((tm, D), F32),
                        pltpu.VMEM((2, tm, D // 2), jnp.uint32),
                        pltpu.SemaphoreType.DMA((2,)), pltpu.SemaphoreType.DMA((2,))])
    need = tm * D * (8 + 2 + 4 + 4) + 2 * 3 * D * tf * 2 + 4 * tm * tf * 4
    return pl.pallas_call(
        _moe_kernel,
        grid_spec=grid_spec,
        out_shape=jax.ShapeDtypeStruct((TOP_K * M + tm, D // 2), jnp.uint32),
        compiler_params=pltpu.CompilerParams(dimension_semantics=("arbitrary", "arbitrary"),
                                             vmem_limit_bytes=_vmem_limit(need + (8 << 20))),
        name="moe_experts",
    )(tile_e, n_used, src_token, src_token, prev_out_row, h, g.reshape(1, D), w1t, w3t, w2)


def _moe_combine_kernel(h_ref, p_ref, g_ref, *refs, emit_h):
    y_refs, out_refs = refs[:TOP_K], refs[TOP_K:]
    hn = h_ref[...]
    for k in range(TOP_K):
        hn = hn + p_ref[:, k:k + 1] * _unpack_bf16_pairs(y_refs[k][...])
    if emit_h:
        out_refs[0][...] = hn
    out_refs[-1][...] = _rms(hn, g_ref[...]).astype(out_refs[-1].dtype)


def moe_combine_call(h, y_tok, probs, g_next, n_dtype, emit_h, tm=MOE_COMBINE_TM):
    M, D = h.shape
    tm = min(tm, M)
    rowblk = pl.BlockSpec((tm, D), lambda i: (i, 0))
    out_specs = [rowblk]
    out_shape = [jax.ShapeDtypeStruct((M, D), n_dtype)]
    if emit_h:
        out_specs = [rowblk, rowblk]
        out_shape = [jax.ShapeDtypeStruct((M, D), F32)] + out_shape
    y_specs = [pl.BlockSpec((tm, D // 2), functools.partial(lambda k, i: (k * (M // tm) + i, 0), k))
               for k in range(TOP_K)]
    return pl.pallas_call(
        functools.partial(_moe_combine_kernel, emit_h=emit_h),
        grid=(M // tm,),
        in_specs=[rowblk, pl.BlockSpec((tm, ROUTER_LANES), lambda i: (i, 0)),
                  pl.BlockSpec((1, D), lambda i: (0, 0))] + y_specs,
        out_specs=out_specs,
        out_shape=out_shape,
        compiler_params=pltpu.CompilerParams(dimension_semantics=("parallel",),
                                             vmem_limit_bytes=_vmem_limit(14 * tm * D * 4)),
        name="moe_combine",
    )(h, probs, g_next.reshape(1, D), *([y_tok] * TOP_K))


def moe_weight_tiling(w1, w3, w2):
    return (("w1", w1, MOE_TF), ("w3", w3, MOE_TF), ("w2", w2, None))


def kernel(x, norm_mix, w_in, gate_b, gm_ln_g, gm_ln_b, gm_ws, gm_bs, rw_mu, rw_w0, rw_w2, rw_a0, rw_a2,
           rw_g2, rw_kk, rw_ka, rw_rk, rw_ln_g, rw_ln_b, p_gm, p_rw, p_sb, w_o, norm_ffn, ffn_w1, ffn_w3,
           ffn_w2, router_w, router_b, moe_w1, moe_w3, moe_w2, norm_out):
    B, T, D = x.shape
    M = B * T
    bf = lambda a: a.astype(BF16)
    h = x.reshape(M, D)
    n = rmsnorm_call(h, norm_mix[0], BF16)
    moe_bf16 = {}
    for l in range(DEPTH):
        last = l == DEPTH - 1
        g_next = norm_out if last else norm_mix[l + 1]
        n_dtype = x.dtype if last else BF16
        w = w_in[l]
        j = l // 2
        uv = matmul_call(n, bf(w[:, GM_OFF:RW_OFF]), BF16, tn=GM_COLS, name="in_proj_gm")
        p = matmul_call(n, bf(w[:, RW_OFF:SB_OFF]), BF16, tn=RW_COLS, name="in_proj_rw")
        qkv = matmul_call(n, bf(w[:, SB_OFF:GATE_OFF]), BF16, tn=1024, name="in_proj_sb")
        gates = matmul_call(n, bf(w[:, GATE_OFF:]), BF16, tn=1024, bias=gate_b[l].reshape(-1),
                            name="in_proj_gate")
        y_gm = gm_call(uv, gm_ln_g[l], gm_ln_b[l], gm_ws[l], gm_bs[l])
        y_rw = rw_call(p.reshape(B, T, RW_COLS), B, T, rw_mu[l], rw_w0[l], rw_w2[l], rw_a0[l], rw_a2[l],
                       rw_g2[l], rw_kk[l], rw_ka[l], rw_rk[l], rw_ln_g[l], rw_ln_b[l]).reshape(M, RW_WIDTH)
        y_sb = sb_call(qkv.reshape(B, T, SB_COLS), B, T).reshape(M, SB_WIDTH)
        merge_args = (y_gm, y_rw, y_sb, gates, h, bf(p_gm[l]), bf(p_rw[l]), bf(p_sb[l]), bf(w_o[l]), norm_ffn[l])
        if l % 2 == 0:
            h, n2 = merge_call(*merge_args)
            ahead = []
            if not last:
                ahead = [(k, wt[(l + 1) // 2], ct) for k, wt, ct in moe_weight_tiling(moe_w1, moe_w3, moe_w2)
                         if can_ride(wt[(l + 1) // 2], ffn_steps(M, D_FF))]
            outs, casts = ffn_call(n2, h, bf(ffn_w1[j]), bf(ffn_w3[j]), bf(ffn_w2[j]), g_next, n_dtype,
                                   emit_h=not last, riders=[(wt, ct) for _, wt, ct in ahead])
            moe_bf16 = {k: c for (k, _, _), c in zip(ahead, casts)}
        else:
            h, ids, probs = merge_call(*merge_args, router=(router_w[j], router_b[j]))
            src_token, prev_out_row, tile_e, n_used = moe_plan(ids, M, MOE_TM)
            ew = {k: moe_bf16[k] if k in moe_bf16 else bf(wt[j] if ct is None else col_tiled(wt[j], ct))
                  for k, wt, ct in moe_weight_tiling(moe_w1, moe_w3, moe_w2)}
            y_tok = moe_call(h, norm_ffn[l], src_token, prev_out_row, tile_e, n_used, ew["w1"], ew["w3"],
                             ew["w2"], MOE_TM)
            outs = moe_combine_call(h, y_tok, probs, g_next, n_dtype, emit_h=not last)
            moe_bf16 = {}
        if last:
            n = outs[0]
        else:
            h, n = outs
    return n.reshape(B, T, D)
```

```python
import functools

import jax
import jax.numpy as jnp
from jax import lax
from jax.experimental import pallas as pl
from jax.experimental.pallas import tpu as pltpu

F32 = jnp.float32
BF16 = jnp.bfloat16

D_MODEL = 2048
DEPTH = 2
HEAD_DIM = 64
GM_HEADS = 8
GM_WIDTH = GM_HEADS * HEAD_DIM
CHUNK = 128
RW_HEADS = 8
RW_WIDTH = RW_HEADS * HEAD_DIM
DECAY_LORA = 64
AAA_LORA = 64
GATE_LORA = 128
SB_HEADS = 16
SB_WIDTH = SB_HEADS * HEAD_DIM
N_BRANCH = 3
D_FF = 5632
N_EXPERTS = 8
TOP_K = 2
D_FF_EXPERT = D_FF // TOP_K
RMS_EPS = 1e-6
LN_EPS = 1e-5
RW_GN_EPS = 64e-5

GM_COLS = 2 * GM_WIDTH
RW_COLS = 3 * RW_WIDTH + DECAY_LORA + AAA_LORA + GATE_LORA
SB_COLS = 3 * SB_WIDTH
GATE_COLS = N_BRANCH * D_MODEL
GM_OFF = 0
RW_OFF = GM_OFF + GM_COLS
SB_OFF = RW_OFF + RW_COLS
GATE_OFF = SB_OFF + SB_COLS
N_IN = GATE_OFF + GATE_COLS

V7X_VMEM_BYTES = 64 * 1024 * 1024
LANES = 128
BF16_SUBLANES = 16
RW_CHUNK = 64
RW_TILE = 256
RW_GROUP = 4
SB_TILE = 256
LOG2_E = 1.4426950408889634
SB_EXP2_UNDERFLOW = -151.0
ROUTER_LANES = 128
MOE_TF = 256
MOE_TM = 528
MOE_COMBINE_TM = 256
ROW_DMA_PRIORITY = 1


def _vmem_limit(nbytes):
    return int(min(max(nbytes, 32 * 1024 * 1024), V7X_VMEM_BYTES - 6 * 1024 * 1024))


def _dot(a, b):
    return jnp.dot(a, b, preferred_element_type=F32)


def _dot_nt(a, b):
    return lax.dot_general(a, b, (((1,), (1,)), ((), ())), preferred_element_type=F32)


def _dot_tn(a, b):
    return lax.dot_general(a, b, (((0,), (0,)), ((), ())), preferred_element_type=F32)


def _dot_split(x, ones_bf16):
    hi = x.astype(BF16)
    lo = (x - hi.astype(F32)).astype(BF16)
    return _dot(hi, ones_bf16) + _dot(lo, ones_bf16)


def _rms(x, g):
    ms = jnp.mean(x * x, axis=-1, keepdims=True)
    return x * lax.rsqrt(ms + RMS_EPS) * g


def _rmsnorm_kernel(x_ref, g_ref, o_ref):
    o_ref[...] = _rms(x_ref[...], g_ref[...]).astype(o_ref.dtype)


def rmsnorm_call(x, g, out_dtype, tm=512):
    M, D = x.shape
    tm = min(tm, M)
    return pl.pallas_call(
        _rmsnorm_kernel,
        grid=(M // tm,),
        in_specs=[pl.BlockSpec((tm, D), lambda i: (i, 0)),
                  pl.BlockSpec((1, D), lambda i: (0, 0))],
        out_specs=pl.BlockSpec((tm, D), lambda i: (i, 0)),
        out_shape=jax.ShapeDtypeStruct((M, D), out_dtype),
        compiler_params=pltpu.CompilerParams(dimension_semantics=("parallel",)),
        name="rmsnorm",
    )(x, g.reshape(1, D))


def _mm_kernel(a_ref, w_ref, o_ref):
    o_ref[...] = _dot(a_ref[...], w_ref[...]).astype(o_ref.dtype)


def _mm_gate_kernel(a_ref, w_ref, b_ref, o_ref):
    o_ref[...] = jax.nn.sigmoid(_dot(a_ref[...], w_ref[...]) + b_ref[...]).astype(o_ref.dtype)


def matmul_call(a, w, out_dtype, tn, bias=None, tm=1024, name="matmul"):
    M, K = a.shape
    N = w.shape[1]
    tm = min(tm, M)
    in_specs = [pl.BlockSpec((tm, K), lambda i, j: (i, 0)),
                pl.BlockSpec((K, tn), lambda i, j: (0, j))]
    args = [a, w]
    kern = _mm_kernel
    if bias is not None:
        in_specs.append(pl.BlockSpec((1, tn), lambda i, j: (0, j)))
        args.append(bias.reshape(1, N))
        kern = _mm_gate_kernel
    osz = jnp.dtype(out_dtype).itemsize
    need = 2 * (tm * K * 2 + K * tn * 2 + tm * tn * osz) + 2 * tm * tn * 4
    return pl.pallas_call(
        kern,
        grid=(M // tm, N // tn),
        in_specs=in_specs,
        out_specs=pl.BlockSpec((tm, tn), lambda i, j: (i, j)),
        out_shape=jax.ShapeDtypeStruct((M, N), out_dtype),
        compiler_params=pltpu.CompilerParams(
            dimension_semantics=("parallel", "arbitrary"),
            vmem_limit_bytes=_vmem_limit(need + (8 << 20))),
        name=name,
    )(*args)


def _gm_kernel(uv_ref, lng_ref, lnb_ref, ws_ref, bias_ref, o_ref, *, n_chunks):
    ii = lax.broadcasted_iota(jnp.int32, (CHUNK, CHUNK), 0)
    jj = lax.broadcasted_iota(jnp.int32, (CHUNK, CHUNK), 1)
    causal = jj <= ii
    ws = [jnp.where(causal, ws_ref[h], 0.0).astype(BF16) for h in range(GM_HEADS)]
    for c in range(n_chunks):
        rows = slice(c * CHUNK, (c + 1) * CHUNK)
        u = jax.nn.gelu(uv_ref[rows, 0:GM_WIDTH].astype(F32))
        v = jax.nn.gelu(uv_ref[rows, GM_WIDTH:2 * GM_WIDTH].astype(F32))
        mu = jnp.mean(v, axis=-1, keepdims=True)
        vc = v - mu
        var = jnp.mean(vc * vc, axis=-1, keepdims=True)
        vn = (vc * lax.rsqrt(var + LN_EPS) * lng_ref[...] + lnb_ref[...]).astype(BF16)
        mixed = jnp.concatenate(
            [_dot(ws[h], vn[:, h * HEAD_DIM:(h + 1) * HEAD_DIM]) for h in range(GM_HEADS)], axis=-1)
        o_ref[rows, :] = (u * (mixed + bias_ref[...])).astype(o_ref.dtype)


def gm_call(uv, ln_g, ln_b, w_s, b_s, n_chunks=2):
    M = uv.shape[0]
    tt = CHUNK * n_chunks
    bias = jnp.repeat(b_s.T, HEAD_DIM, axis=1)
    return pl.pallas_call(
        functools.partial(_gm_kernel, n_chunks=n_chunks),
        grid=(M // tt,),
        in_specs=[pl.BlockSpec((tt, GM_COLS), lambda i: (i, 0)),
                  pl.BlockSpec((1, GM_WIDTH), lambda i: (0, 0)),
                  pl.BlockSpec((1, GM_WIDTH), lambda i: (0, 0)),
                  pl.BlockSpec((GM_HEADS, CHUNK, CHUNK), lambda i: (0, 0, 0)),
                  pl.BlockSpec((CHUNK, GM_WIDTH), lambda i: (0, 0))],
        out_specs=pl.BlockSpec((tt, GM_WIDTH), lambda i: (i, 0)),
        out_shape=jax.ShapeDtypeStruct((M, GM_WIDTH), BF16),
        compiler_params=pltpu.CompilerParams(dimension_semantics=("parallel",)),
        name="gmlp_gating",
    )(uv, ln_g.reshape(1, -1), ln_b.reshape(1, -1), w_s, bias)


def _rw_kernel(p_ref, prev_ref, mu_ref, w0_ref, w2_ref, a0_ref, a2_ref, g2_ref, kk_ref, ka_ref,
               rk_ref, lng_ref, lnb_ref, hsum_ref, o_ref, st_ref, y_ref):
    C = RW_CHUNK
    i = pl.program_id(1)

    @pl.when(i == 0)
    def _():
        st_ref[...] = jnp.zeros_like(st_ref)

    hsum = hsum_ref[...]
    gw = hsum.shape[0]

    def head_sums(x):
        return jnp.concatenate([_dot_split(x[:, q * gw:(q + 1) * gw], hsum)
                                for q in range(x.shape[1] // gw)], axis=1)

    p = p_ref[...].astype(F32)
    prev_row = prev_ref[BF16_SUBLANES - 1:BF16_SUBLANES, :].astype(F32)
    prev_row = jnp.where(i == 0, 0.0, prev_row)
    row = lax.broadcasted_iota(jnp.int32, p.shape, 0)
    p_prev = jnp.where(row == 0, prev_row, pltpu.roll(p, 1, axis=0))
    ps = p + (p_prev - p) * mu_ref[...]
    W = RW_WIDTH
    r = ps[:, 0:W]
    k = ps[:, W:2 * W]
    v = ps[:, 2 * W:3 * W]
    wl = ps[:, 3 * W:3 * W + DECAY_LORA]
    al = ps[:, 3 * W + DECAY_LORA:3 * W + DECAY_LORA + AAA_LORA]
    gl = ps[:, 3 * W + DECAY_LORA + AAA_LORA:RW_COLS]

    d = w0_ref[...] + _dot(jnp.tanh(wl).astype(BF16), w2_ref[...])
    log_w = -jnp.exp(-jax.nn.softplus(-d) - 0.5)
    a = jax.nn.sigmoid(a0_ref[...] + _dot(al.astype(BF16), a2_ref[...]))
    g = _dot(jax.nn.sigmoid(gl).astype(BF16), g2_ref[...])
    kk = k * kk_ref[...]
    kk = kk / jnp.maximum(jnp.sqrt(head_sums(kk * kk)), 1e-12)
    k = k * (1.0 + (a - 1.0) * ka_ref[...])
    bonus = head_sums(r * k * rk_ref[...]) * v

    HD = HEAD_DIM
    GW = RW_GROUP * HD
    ci = lax.broadcasted_iota(jnp.int32, (C, C), 0)
    cj = lax.broadcasted_iota(jnp.int32, (C, C), 1)
    tri_incl = (cj <= ci).astype(BF16)
    ti = lax.broadcasted_iota(jnp.int32, (C, GW), 0)
    tj = lax.broadcasted_iota(jnp.int32, (C, GW), 1) % HD
    strict = tj < ti
    incl = tj <= ti
    eye = ti == tj
    blk_masks = []
    s = 2
    while s < C:
        blk_masks.append((ti // (2 * s) == tj // (2 * s)) & (ti // s != tj // s))
        s *= 2
    pair_mask = (ti // 2 == tj // 2) & strict
    gi = lax.broadcasted_iota(jnp.int32, (GW, GW), 0)
    gj = lax.broadcasted_iota(jnp.int32, (GW, GW), 1)
    bd_mask = gi // HD == gj // HD
    bd_mask_b = jnp.where(bd_mask, 1.0, 0.0).astype(BF16)
    gw_eye = gi == gj

    def bd(x):
        xb = x.astype(BF16)
        return jnp.where(bd_mask_b > 0, jnp.concatenate([xb] * RW_GROUP, axis=0), 0)

    def bd_of_full(full):
        return jnp.where(bd_mask, full, 0.0)

    n_chunks = RW_TILE // C
    n_groups = RW_WIDTH // GW
    systems = [(c, q) for c in range(n_chunks) for q in range(n_groups)]

    pre = []
    for c in range(n_chunks):
        rows = slice(c * C, (c + 1) * C)
        lw = log_w[rows]
        cum = _cumsum_rows(lw, tri_incl)
        gam = jnp.exp(cum)
        gam_prev = jnp.exp(cum - lw)
        gam_inv = jnp.exp(-cum)
        g_end = gam[C - 1:C, :]
        bh_f = kk[rows] * a[rows] * gam_inv
        kh_f = k[rows] * gam_inv
        pre.append(dict(
            rh=(r[rows] * gam).astype(BF16), ah=(-kk[rows] * gam_prev).astype(BF16),
            bh=bh_f.astype(BF16), kh=kh_f.astype(BF16),
            bt=(bh_f * g_end).astype(BF16), kt=(kh_f * g_end).astype(BF16),
            v=v[rows].astype(BF16), g_end=g_end))

    def grp(c, q, name):
        return pre[c][name][:, q * GW:(q + 1) * GW]

    st = {}
    for (c, q) in systems:
        ar = jnp.concatenate([grp(c, q, "ah"), grp(c, q, "rh")], axis=0)
        pb = _dot_nt(ar, bd(grp(c, q, "bh")))
        pk = _dot_nt(ar, bd(grp(c, q, "kh")))
        lab = jnp.where(strict, pb[0:C], 0.0)
        st[(c, q)] = dict(
            lab=lab, mrb=jnp.where(incl, pb[C:2 * C], 0.0).astype(BF16),
            lm=jnp.concatenate([jnp.where(strict, pk[0:C], 0.0), jnp.where(incl, pk[C:2 * C], 0.0)],
                               axis=0).astype(BF16),
            tinv=jnp.where(eye, 1.0, jnp.where(pair_mask, lab, 0.0)))

    for bm in blk_masks:
        xs = {}
        for key in systems:
            d_ = st[key]
            xs[key] = _dot(d_["tinv"].astype(BF16), bd(jnp.where(bm, d_["lab"], 0.0))).astype(BF16)
        for key in systems:
            d_ = st[key]
            d_["tinv"] = d_["tinv"] + _dot(xs[key], bd(d_["tinv"]))

    for (c, q) in systems:
        d_ = st[(c, q)]
        lv = _dot(d_["lm"], bd(grp(c, q, "v")))
        tb = d_["tinv"].astype(BF16)
        ap = _dot(tb, bd(grp(c, q, "ah"))).astype(BF16)
        up = _dot(tb, bd(lv[0:C])).astype(BF16)
        d_["rp"] = (grp(c, q, "rh").astype(F32) + _dot(d_["mrb"], bd(ap))).astype(BF16)
        d_["ypp"] = _dot(d_["mrb"], bd(up)) + lv[C:2 * C]
        g_end = pre[c]["g_end"][:, q * GW:(q + 1) * GW]
        gt = _dot_tn(ap, grp(c, q, "bt"))
        d_["gt"] = jnp.where(gw_eye, gt + g_end, bd_of_full(gt)).astype(BF16)
        ht = bd_of_full(_dot_tn(up, grp(c, q, "bt")) + _dot_tn(grp(c, q, "v"), grp(c, q, "kt")))
        d_["ht"] = sum(ht[hh * HD:(hh + 1) * HD] for hh in range(1, RW_GROUP)) + ht[0:HD]

    for q in range(n_groups):
        s_cur = st_ref[q]
        for c in range(n_chunks):
            d_ = st[(c, q)]
            y_ref[c * C:(c + 1) * C, q * GW:(q + 1) * GW] = _dot_nt(d_["rp"], bd(s_cur)) + d_["ypp"]
            s_cur = _dot(s_cur.astype(BF16), d_["gt"]) + d_["ht"]
        st_ref[q] = s_cur

    y = y_ref[...]
    inv_n = 1.0 / HEAD_DIM
    m = head_sums(y) * inv_n
    yc = y - m
    var = head_sums(yc * yc) * inv_n
    yn = yc * lax.rsqrt(var + RW_GN_EPS) * lng_ref[...] + lnb_ref[...]
    o_ref[...] = ((yn + bonus) * g).astype(o_ref.dtype)


def _cumsum_rows(x, tri_incl_bf16):
    hi = x.astype(BF16)
    lo = (x - hi.astype(F32)).astype(BF16)
    return _dot(tri_incl_bf16, hi) + _dot(tri_incl_bf16, lo)


def rw_call(p, B, T, mu, w0, w2, a0, a2, g2, k_k, k_a, r_k, ln_g, ln_b):
    nt = T // RW_TILE
    sub = RW_TILE // BF16_SUBLANES
    hid = jnp.arange(RW_GROUP * HEAD_DIM) // HEAD_DIM
    hsum = (hid[:, None] == hid[None, :]).astype(BF16)
    row = lambda a: a.reshape(1, -1).astype(F32)
    vec = lambda n: pl.BlockSpec((1, n), lambda b, i: (0, 0))
    return pl.pallas_call(
        _rw_kernel,
        grid=(B, nt),
        in_specs=[pl.BlockSpec((None, RW_TILE, RW_COLS), lambda b, i: (b, i, 0)),
                  pl.BlockSpec((None, BF16_SUBLANES, RW_COLS),
                               lambda b, i: (b, jnp.maximum(i * sub - 1, 0), 0)),
                  vec(RW_COLS), vec(RW_WIDTH),
                  pl.BlockSpec((DECAY_LORA, RW_WIDTH), lambda b, i: (0, 0)),
                  vec(RW_WIDTH),
                  pl.BlockSpec((AAA_LORA, RW_WIDTH), lambda b, i: (0, 0)),
                  pl.BlockSpec((GATE_LORA, RW_WIDTH), lambda b, i: (0, 0)),
                  vec(RW_WIDTH), vec(RW_WIDTH), vec(RW_WIDTH), vec(RW_WIDTH), vec(RW_WIDTH),
                  pl.BlockSpec((RW_GROUP * HEAD_DIM, RW_GROUP * HEAD_DIM), lambda b, i: (0, 0))],
        out_specs=pl.BlockSpec((None, RW_TILE, RW_WIDTH), lambda b, i: (b, i, 0)),
        out_shape=jax.ShapeDtypeStruct((B, T, RW_WIDTH), BF16),
        scratch_shapes=[pltpu.VMEM((RW_HEADS // RW_GROUP, HEAD_DIM, RW_GROUP * HEAD_DIM), F32),
                        pltpu.VMEM((RW_TILE, RW_WIDTH), F32)],
        compiler_params=pltpu.CompilerParams(dimension_semantics=("parallel", "arbitrary")),
        name="rwkv7_time_mix",
    )(p, p, row(mu), row(w0), w2.astype(BF16), row(a0), a2.astype(BF16), g2.astype(BF16),
      row(k_k), row(k_a), row(r_k), row(ln_g), row(ln_b), hsum)


def _sb_scores(qs, ks, masks, neg_suffix):
    n = range(len(qs))
    zs = [_dot_nt(qs[s], ks[s]) * LOG2_E for s in n]
    sps = [jnp.maximum(z, 0.0) + jnp.log2(1.0 + jnp.exp2(-jnp.abs(z))) for z in zs]
    vis = [sp if masks[s] is None else jnp.where(masks[s], sp, 0.0) for s, sp in enumerate(sps)]
    his = [sp.astype(BF16) for sp in vis]
    los = [(vis[s] - his[s].astype(F32)).astype(BF16) for s in n]
    sufs = [_dot(his[s], neg_suffix[s]) + _dot(los[s], neg_suffix[s]) for s in n]
    rowsums = [-jnp.sum(sp, axis=-1, keepdims=True) for sp in vis]
    return zs, sps, sufs, rowsums


def _sb_weighted(scores, s, carry, mask, v):
    zs, sps, sufs, _ = scores
    att = jnp.exp2(zs[s] - sps[s] + (sufs[s] + carry))
    if mask is not None:
        att = jnp.where(mask, att, 0.0)
    return _dot(att.astype(BF16), v)


def _sb_kernel(q_ref, k_ref, v_ref, o_ref, acc_ref, carry_ref, *, T):
    tq = SB_TILE
    nq = T // tq
    th = tq // 2
    ti = lax.broadcasted_iota(jnp.int32, (tq, tq), 0)
    tj = lax.broadcasted_iota(jnp.int32, (tq, tq), 1)
    neg_full = jnp.where(ti > tj, -1.0, 0.0).astype(BF16)
    neg_half = neg_full[:th, :th]
    causal_top = (tj < ti)[:th, :th]
    causal_bot = (tj < ti)[th:, :]
    scale = HEAD_DIM ** -0.5
    heads = [slice(hh * HEAD_DIM, (hh + 1) * HEAD_DIM) for hh in range(LANES // HEAD_DIM)]
    nh = len(heads)

    def q_body(i, _):
        r0 = pl.multiple_of(i * tq, tq)
        has_prev = jnp.where(i > 0, 1.0, 0.0)
        p0 = pl.multiple_of(jnp.maximum(i - 1, 0) * tq, tq)
        qs = [q_ref[pl.ds(r0, tq), cs] * scale for cs in heads]
        kd = [k_ref[pl.ds(r0, tq), cs] for cs in heads]
        vd = [v_ref[pl.ds(r0, tq), cs] for cs in heads]
        kp = [k_ref[pl.ds(p0, tq), cs] for cs in heads]
        vp = [v_ref[pl.ds(p0, tq), cs] for cs in heads]
        scores = _sb_scores([q[:th] for q in qs] + [q[th:] for q in qs] + qs,
                            [k[:th] for k in kd] + kd + kp,
                            [causal_top] * nh + [causal_bot] * nh + [None] * nh,
                            [neg_half] * nh + [neg_full] * (2 * nh))
        rss = scores[3]
        live = jnp.float32(-jnp.inf)
        for hh, cs in enumerate(heads):
            out = jnp.concatenate([_sb_weighted(scores, hh, 0.0, causal_top, vd[hh][:th]),
                                   _sb_weighted(scores, nh + hh, 0.0, causal_bot, vd[hh])], axis=0)
            rs_diag = jnp.concatenate([rss[hh], rss[nh + hh]], axis=0)
            out_prev = _sb_weighted(scores, 2 * nh + hh, rs_diag, None, vp[hh])
            acc_ref[:, cs] = out + has_prev * out_prev
            carry = rs_diag + has_prev * rss[2 * nh + hh]
            carry_ref[hh] = carry
            live = jnp.maximum(live, jnp.max(carry))

        def k_cond(state):
            jr, live = state
            return (jr < i) & (live > SB_EXP2_UNDERFLOW)

        def k_body(state):
            jr, _ = state
            c0 = pl.multiple_of((i - 1 - jr) * tq, tq)
            kt = [k_ref[pl.ds(c0, tq), cs] for cs in heads]
            vt = [v_ref[pl.ds(c0, tq), cs] for cs in heads]
            scores = _sb_scores(qs, kt, [None] * nh, [neg_full] * nh)
            live = jnp.float32(-jnp.inf)
            for hh, cs in enumerate(heads):
                carry = carry_ref[hh]
                acc_ref[:, cs] += _sb_weighted(scores, hh, carry, None, vt[hh])
                carry = carry + scores[3][hh]
                carry_ref[hh] = carry
                live = jnp.maximum(live, jnp.max(carry))
            return jr + 1, live

        lax.while_loop(k_cond, k_body, (jnp.int32(1), live))
        o_ref[pl.ds(r0, tq), :] = acc_ref[...].astype(o_ref.dtype)
        return 0

    lax.fori_loop(0, nq, q_body, 0)


def sb_call(qkv, B, T):
    npair = SB_WIDTH // LANES
    blk = lambda off: pl.BlockSpec((None, T, LANES), lambda b, hp: (b, 0, off + hp))
    return pl.pallas_call(
        functools.partial(_sb_kernel, T=T),
        grid=(B, npair),
        in_specs=[blk(0), blk(npair), blk(2 * npair)],
        out_specs=pl.BlockSpec((None, T, LANES), lambda b, hp: (b, 0, hp)),
        out_shape=jax.ShapeDtypeStruct((B, T, SB_WIDTH), BF16),
        scratch_shapes=[pltpu.VMEM((SB_TILE, LANES), F32),
                        pltpu.VMEM((LANES // HEAD_DIM, SB_TILE, 1), F32)],
        compiler_params=pltpu.CompilerParams(dimension_semantics=("parallel", "parallel")),
        name="stick_breaking_attention",
    )(qkv, qkv, qkv)


def _route(n, w_ref, b_ref, id_ref, p_ref):
    nh = n.astype(BF16)
    nl = (n - nh.astype(F32)).astype(BF16)
    a = _dot(nh, w_ref[...])
    b = _dot(nl, w_ref[...])
    L = ROUTER_LANES
    logits = a[:, :L] + (a[:, L:] + b[:, :L] + b[:, L:]) + b_ref[...]
    lane = lax.broadcasted_iota(jnp.int32, logits.shape, 1)
    neg = jnp.float32(-jnp.inf)
    logits = jnp.where(lane < N_EXPERTS, logits, neg)
    m1 = jnp.max(logits, axis=-1, keepdims=True)
    i1 = jnp.min(jnp.where(logits == m1, lane, ROUTER_LANES), axis=-1, keepdims=True)
    rest = jnp.where(lane == i1, neg, logits)
    m2 = jnp.max(rest, axis=-1, keepdims=True)
    i2 = jnp.min(jnp.where(rest == m2, lane, ROUTER_LANES), axis=-1, keepdims=True)
    e2 = jnp.exp(m2 - m1)
    p1 = 1.0 / (1.0 + e2)
    p2 = e2 / (1.0 + e2)
    id_ref[...] = jnp.where(lane == 0, i1, jnp.where(lane == 1, i2, 0))
    p_ref[...] = jnp.where(lane == 0, p1, jnp.where(lane == 1, p2, 0.0))


def _merge_kernel(ygm_ref, yrw_ref, ysb_ref, gate_ref, h_ref, pgm_ref, prw_ref, psb_ref, wo_ref,
                  g_ref, *refs, route):
    D = D_MODEL
    merged = gate_ref[:, 0:D].astype(F32) * _dot(ygm_ref[...], pgm_ref[...])
    merged += gate_ref[:, D:2 * D].astype(F32) * _dot(yrw_ref[...], prw_ref[...])
    merged += gate_ref[:, 2 * D:3 * D].astype(F32) * _dot(ysb_ref[...], psb_ref[...])
    hn = h_ref[...] + _dot(merged.astype(BF16), wo_ref[...])
    n = _rms(hn, g_ref[...])
    if route:
        w_ref, b_ref, ho_ref, id_ref, p_ref = refs
        _route(n, w_ref, b_ref, id_ref, p_ref)
    else:
        ho_ref, no_ref = refs
        no_ref[...] = n.astype(no_ref.dtype)
    ho_ref[...] = hn


def _resident(shape):
    return pl.BlockSpec(shape, lambda i: (0,) * len(shape), pipeline_mode=pl.Buffered(1))


def merge_call(y_gm, y_rw, y_sb, gates, h, p_gm, p_rw, p_sb, w_o, g_next, router=None, tm=256):
    M, D = h.shape
    tm = min(tm, M)
    rowblk = lambda n: pl.BlockSpec((tm, n), lambda i: (i, 0))
    wbytes = 2 * (GM_WIDTH + RW_WIDTH + SB_WIDTH + D) * D
    need = wbytes + 2 * tm * (2 * (GM_WIDTH + RW_WIDTH + SB_WIDTH + GATE_COLS) + 4 * D + 4 * D + 2 * D) \
        + 6 * tm * D * 4
    in_specs = [rowblk(GM_WIDTH), rowblk(RW_WIDTH), rowblk(SB_WIDTH), rowblk(GATE_COLS), rowblk(D),
                _resident((GM_WIDTH, D)), _resident((RW_WIDTH, D)), _resident((SB_WIDTH, D)),
                _resident((D, D)), _resident((1, D))]
    args = [y_gm, y_rw, y_sb, gates, h, p_gm, p_rw, p_sb, w_o, g_next.reshape(1, D)]
    if router is None:
        out_specs = [rowblk(D), rowblk(D)]
        out_shape = [jax.ShapeDtypeStruct((M, D), F32), jax.ShapeDtypeStruct((M, D), BF16)]
    else:
        router_w, router_b = router
        wpad = jnp.zeros((D, ROUTER_LANES), F32).at[:, :N_EXPERTS].set(router_w)
        w_hi = wpad.astype(BF16)
        w_lo = (wpad - w_hi.astype(F32)).astype(BF16)
        bpad = jnp.zeros((1, ROUTER_LANES), F32).at[0, :N_EXPERTS].set(router_b)
        in_specs += [_resident((D, 2 * ROUTER_LANES)), _resident((1, ROUTER_LANES))]
        args += [jnp.concatenate([w_hi, w_lo], axis=1), bpad]
        out_specs = [rowblk(D), rowblk(ROUTER_LANES), rowblk(ROUTER_LANES)]
        out_shape = [jax.ShapeDtypeStruct((M, D), F32), jax.ShapeDtypeStruct((M, ROUTER_LANES), jnp.int32),
                     jax.ShapeDtypeStruct((M, ROUTER_LANES), F32)]
    return pl.pallas_call(
        functools.partial(_merge_kernel, route=router is not None),
        grid=(M // tm,),
        in_specs=in_specs,
        out_specs=out_specs,
        out_shape=out_shape,
        compiler_params=pltpu.CompilerParams(dimension_semantics=("parallel",),
                                             vmem_limit_bytes=_vmem_limit(need)),
        name="merge_out_proj",
    )(*args)


def _ffn_kernel(n_ref, h_ref, w1_ref, w3_ref, w2_ref, g_ref, *refs, emit_h, n_cast):
    cast_in, refs = refs[:n_cast], refs[n_cast:]
    if emit_h:
        ho_ref, no_ref = refs[:2]
        refs = refs[2:]
    else:
        no_ref = refs[0]
        refs = refs[1:]
    cast_out, acc_ref = refs[:n_cast], refs[n_cast]
    f = pl.program_id(1)

    @pl.when(f == 0)
    def _():
        acc_ref[...] = jnp.zeros_like(acc_ref)

    n = n_ref[...]
    a = _dot(n, w1_ref[...])
    b = _dot(n, w3_ref[...])
    for src, dst in zip(cast_in, cast_out):
        if len(dst.shape) == 3:
            ct = dst.shape[2]
            for c in range(dst.shape[0]):
                dst[c] = src[:, c * ct:(c + 1) * ct].astype(dst.dtype)
        else:
            dst[...] = src[...].astype(dst.dtype)
    act = (jax.nn.silu(a) * b).astype(BF16)
    acc_ref[...] += _dot(act, w2_ref[...])

    @pl.when(f == pl.num_programs(1) - 1)
    def _():
        hn = h_ref[...] + acc_ref[...]
        if emit_h:
            ho_ref[...] = hn
        no_ref[...] = _rms(hn, g_ref[...]).astype(no_ref.dtype)


def ffn_steps(M, F, tm=512, tf=512):
    return (M // min(tm, M)) * (F // tf)


RIDER_SLAB_ROWS = 64


def can_ride(w, steps):
    return w.shape[1] % RIDER_SLAB_ROWS == 0 and w.shape[0] * (w.shape[1] // RIDER_SLAB_ROWS) <= steps


def col_tiled(w, ct):
    E, R, C = w.shape
    return w.reshape(E, R, C // ct, ct).transpose(0, 2, 1, 3)


def ffn_call(n, h, w1, w3, w2, g_next, n_dtype, emit_h, riders=(), tm=512, tf=512):
    M, D = h.shape
    F = w1.shape[1]
    tm = min(tm, M)
    nf = F // tf
    steps = (M // tm) * nf
    rowblk = pl.BlockSpec((tm, D), lambda i, f: (i, 0))
    out_specs = [rowblk]
    out_shape = [jax.ShapeDtypeStruct((M, D), n_dtype)]
    if emit_h:
        out_specs = [rowblk, rowblk]
        out_shape = [jax.ShapeDtypeStruct((M, D), F32)] + out_shape
    n_main = len(out_shape)

    def slab_specs(w, ct):
        E, R, C = w.shape
        per = R // RIDER_SLAB_ROWS
        n_slabs = E * per
        hold = steps // n_slabs
        slab = lambda i, f: jnp.minimum((i * nf + f) // hold, n_slabs - 1)
        src = pl.BlockSpec((None, RIDER_SLAB_ROWS, C), lambda i, f: (slab(i, f) // per, slab(i, f) % per, 0))
        if ct is None:
            return src, src, jax.ShapeDtypeStruct(w.shape, BF16)
        dst = pl.BlockSpec((None, C // ct, RIDER_SLAB_ROWS, ct),
                           lambda i, f: (slab(i, f) // per, 0, slab(i, f) % per, 0))
        return src, dst, jax.ShapeDtypeStruct((E, C // ct, R, ct), BF16)

    ride = [slab_specs(w, ct) for w, ct in riders]
    need = 2 * tm * D * (2 + 4 + 4 * n_main) + tm * D * 4 + 2 * 3 * D * tf * 2 + 4 * tm * tf * 4 \
        + sum(2 * RIDER_SLAB_ROWS * w.shape[2] * 6 for w, _ in riders)
    outs = pl.pallas_call(
        functools.partial(_ffn_kernel, emit_h=emit_h, n_cast=len(riders)),
        grid=(M // tm, nf),
        in_specs=[rowblk, rowblk,
                  pl.BlockSpec((D, tf), lambda i, f: (0, f)),
                  pl.BlockSpec((D, tf), lambda i, f: (0, f)),
                  pl.BlockSpec((tf, D), lambda i, f: (f, 0)),
                  pl.BlockSpec((1, D), lambda i, f: (0, 0))] + [r[0] for r in ride],
        out_specs=out_specs + [r[1] for r in ride],
        out_shape=out_shape + [r[2] for r in ride],
        scratch_shapes=[pltpu.VMEM((tm, D), F32)],
        compiler_params=pltpu.CompilerParams(dimension_semantics=("arbitrary", "arbitrary"),
                                             vmem_limit_bytes=_vmem_limit(need + (8 << 20))),
        name="swiglu_ffn",
    )(n, h, w1, w3, w2, g_next.reshape(1, D), *[w for w, _ in riders])
    return outs[:n_main], outs[n_main:]


def moe_plan(ids, M, tm):
    E = N_EXPERTS
    n_tiles = -(-(TOP_K * M) // tm) + E + 2
    e_flat = ids[:, :TOP_K].reshape(-1)
    onehot = (e_flat[:, None] == jnp.arange(E, dtype=jnp.int32)[None, :]).astype(jnp.int32)
    csum = jnp.cumsum(onehot, axis=0)
    rank = jnp.sum((csum - onehot) * onehot, axis=1)
    counts = csum[-1]
    tiles_per_e = (counts + tm - 1) // tm
    tile_end = jnp.cumsum(tiles_per_e)
    offs = (tile_end - tiles_per_e) * tm
    dest = offs[e_flat] + rank
    pair = jnp.full((n_tiles * tm,), -1, jnp.int32).at[dest].set(
        jnp.arange(TOP_K * M, dtype=jnp.int32), unique_indices=True)
    spare = TOP_K * M + jnp.arange(n_tiles * tm, dtype=jnp.int32) % tm
    src_token = jnp.where(pair >= 0, pair // TOP_K, 0)
    out_row = jnp.where(pair >= 0, (pair % TOP_K) * M + pair // TOP_K, spare)
    prev_out_row = jnp.concatenate([spare[:tm], out_row[:-tm]])
    n_used = tile_end[-1]
    t = jnp.arange(n_tiles, dtype=jnp.int32)
    tile_e = jnp.searchsorted(tile_end, jnp.minimum(t, n_used - 1), side="right").astype(jnp.int32)
    return (src_token.reshape(n_tiles, 1, tm), prev_out_row.reshape(n_tiles, 1, tm), tile_e,
            n_used.astype(jnp.int32).reshape(1))


def _bf16_bits(x):
    u = lax.bitcast_convert_type(x, jnp.uint32)
    return (u + jnp.uint32(0x7FFF) + ((u >> 16) & jnp.uint32(1))) & jnp.uint32(0xFFFF0000)


def _pack_bf16_pairs(x):
    half = x.shape[1] // 2
    return _bf16_bits(x[:, half:]) | (_bf16_bits(x[:, :half]) >> 16)


def _unpack_bf16_pairs(w):
    lo = lax.bitcast_convert_type(w << 16, F32)
    hi = lax.bitcast_convert_type(w & jnp.uint32(0xFFFF0000), F32)
    return jnp.concatenate([lo, hi], axis=1)


def _moe_kernel(tile_e_ref, n_used_ref, src_ref, nxt_ref, dst_ref, h_hbm, g_ref, w1_ref, w3_ref, w2_ref,
                y_hbm, xbuf, xb, acc, ostage, gsem, ssem):
    t = pl.program_id(0)
    f = pl.program_id(1)
    nf = pl.num_programs(1)
    tm = xb.shape[0]
    per_step = tm // nf
    n_used = n_used_ref[0]
    used = t < n_used
    cur = t % 2

    def fetch(idx_ref, r, slot):
        return pltpu.make_async_copy(h_hbm.at[pl.ds(idx_ref[0, r], 1)], xbuf.at[slot, pl.ds(r, 1)],
                                     gsem.at[slot])

    def put(r):
        return pltpu.make_async_copy(ostage.at[1 - cur, pl.ds(r, 1)], y_hbm.at[pl.ds(dst_ref[0, r], 1)],
                                     ssem.at[1 - cur])

    @pl.when((f == 0) & (t == 0))
    def _():
        def issue(r, c):
            fetch(src_ref, r, 0).start()
            return c
        lax.fori_loop(0, tm, issue, 0)
        ostage[1] = jnp.zeros((tm, ostage.shape[2]), ostage.dtype)

    @pl.when((f == 0) & (t <= n_used))
    def _():
        pltpu.make_async_copy(h_hbm.at[pl.ds(0, tm)], xbuf.at[cur], gsem.at[cur]).wait()

    @pl.when((f == 0) & (t >= 1) & (t - 1 <= n_used))
    def _():
        pltpu.make_async_copy(ostage.at[cur], y_hbm.at[pl.ds(0, tm)], ssem.at[cur]).wait()

    @pl.when((f == 0) & used)
    def _():
        xb[...] = _rms(xbuf[cur], g_ref[...]).astype(BF16)
        acc[...] = jnp.zeros_like(acc)

    @pl.when(used)
    def _():
        for j in range(per_step):
            fetch(nxt_ref, f * per_step + j, 1 - cur).start(priority=ROW_DMA_PRIORITY)
            put(f * per_step + j).start(priority=ROW_DMA_PRIORITY)
        x = xb[...]
        act = jax.nn.silu(_dot(x, w1_ref[...])) * _dot(x, w3_ref[...])
        acc[...] += _dot(act.astype(BF16), w2_ref[...])

    @pl.when(used & (f == nf - 1))
    def _():
        ostage[cur] = _pack_bf16_pairs(acc[...])

    @pl.when((f == 0) & (t == n_used))
    def _():
        def issue(r, c):
            put(r).start()
            return c
        lax.fori_loop(0, tm, issue, 0)


def moe_call(h, g, src_token, prev_out_row, tile_e, n_used, w1t, w3t, w2, tm):
    M, D = h.shape
    n_tiles = src_token.shape[0]
    E, nf, _, tf = w1t.shape
    assert tm % nf == 0

    def wmap(t, f, tile_e_ref, n_used_ref):
        ff = jnp.where(t < n_used_ref[0], f, nf - 1)
        return tile_e_ref[t], ff

    smem_tile = lambda fn: pl.BlockSpec((None, 1, tm), fn, memory_space=pltpu.SMEM)
    grid_spec = pltpu.PrefetchScalarGridSpec(
        num_scalar_prefetch=2,
        grid=(n_tiles, nf),
        in_specs=[smem_tile(lambda t, f, te, nu: (t, 0, 0)),
                  smem_tile(lambda t, f, te, nu: (jnp.minimum(t + 1, n_tiles - 1), 0, 0)),
                  smem_tile(lambda t, f, te, nu: (t, 0, 0)),
                  pl.BlockSpec(memory_space=pl.ANY),
                  pl.BlockSpec((1, D), lambda t, f, te, nu: (0, 0)),
                  pl.BlockSpec((None, None, D, tf), lambda t, f, te, nu: wmap(t, f, te, nu) + (0, 0)),
                  pl.BlockSpec((None, None, D, tf), lambda t, f, te, nu: wmap(t, f, te, nu) + (0, 0)),
                  pl.BlockSpec((None, tf, D), lambda t, f, te, nu: wmap(t, f, te, nu) + (0,))],
        out_specs=pl.BlockSpec(memory_space=pl.ANY),
        scratch_shapes=[pltpu.VMEM((2, tm, D), F32), pltpu.VMEM((tm, D), BF16), pltpu.VMEM((tm, D), F32),
                        pltpu.VMEM((2, tm, D // 2), jnp.uint32),
                        pltpu.SemaphoreType.DMA((2,)), pltpu.SemaphoreType.DMA((2,))])
    need = tm * D * (8 + 2 + 4 + 4) + 2 * 3 * D * tf * 2 + 4 * tm * tf * 4
    return pl.pallas_call(
        _moe_kernel,
        grid_spec=grid_spec,
        out_shape=jax.ShapeDtypeStruct((TOP_K * M + tm, D // 2), jnp.uint32),
        compiler_params=pltpu.CompilerParams(dimension_semantics=("arbitrary", "arbitrary"),
                                             vmem_limit_bytes=_vmem_limit(need + (8 << 20))),
        name="moe_experts",
    )(tile_e, n_used, src_token, src_token, prev_out_row, h, g.reshape(1, D), w1t, w3t, w2)


def _moe_combine_kernel(h_ref, p_ref, g_ref, *refs, emit_h):
    y_refs, out_refs = refs[:TOP_K], refs[TOP_K:]
    hn = h_ref[...]
    for k in range(TOP_K):
        hn = hn + p_ref[:, k:k + 1] * _unpack_bf16_pairs(y_refs[k][...])
    if emit_h:
        out_refs[0][...] = hn
    out_refs[-1][...] = _rms(hn, g_ref[...]).astype(out_refs[-1].dtype)


def moe_combine_call(h, y_tok, probs, g_next, n_dtype, emit_h, tm=MOE_COMBINE_TM):
    M, D = h.shape
    tm = min(tm, M)
    rowblk = pl.BlockSpec((tm, D), lambda i: (i, 0))
    out_specs = [rowblk]
    out_shape = [jax.ShapeDtypeStruct((M, D), n_dtype)]
    if emit_h:
        out_specs = [rowblk, rowblk]
        out_shape = [jax.ShapeDtypeStruct((M, D), F32)] + out_shape
    y_specs = [pl.BlockSpec((tm, D // 2), functools.partial(lambda k, i: (k * (M // tm) + i, 0), k))
               for k in range(TOP_K)]
    return pl.pallas_call(
        functools.partial(_moe_combine_kernel, emit_h=emit_h),
        grid=(M // tm,),
        in_specs=[rowblk, pl.BlockSpec((tm, ROUTER_LANES), lambda i: (i, 0)),
                  pl.BlockSpec((1, D), lambda i: (0, 0))] + y_specs,
        out_specs=out_specs,
        out_shape=out_shape,
        compiler_params=pltpu.CompilerParams(dimension_semantics=("parallel",),
                                             vmem_limit_bytes=_vmem_limit(14 * tm * D * 4)),
        name="moe_combine",
    )(h, probs, g_next.reshape(1, D), *([y_tok] * TOP_K))


def moe_weight_tiling(w1, w3, w2):
    return (("w1", w1, MOE_TF), ("w3", w3, MOE_TF), ("w2", w2, None))


def kernel(x, norm_mix, w_in, gate_b, gm_ln_g, gm_ln_b, gm_ws, gm_bs, rw_mu, rw_w0, rw_w2, rw_a0, rw_a2,
           rw_g2, rw_kk, rw_ka, rw_rk, rw_ln_g, rw_ln_b, p_gm, p_rw, p_sb, w_o, norm_ffn, ffn_w1, ffn_w3,
           ffn_w2, router_w, router_b, moe_w1, moe_w3, moe_w2, norm_out):
    B, T, D = x.shape
    M = B * T
    bf = lambda a: a.astype(BF16)
    h = x.reshape(M, D)
    n = rmsnorm_call(h, norm_mix[0], BF16)
    moe_bf16 = {}
    for l in range(DEPTH):
        last = l == DEPTH - 1
        g_next = norm_out if last else norm_mix[l + 1]
        n_dtype = x.dtype if last else BF16
        w = w_in[l]
        j = l // 2
        uv = matmul_call(n, bf(w[:, GM_OFF:RW_OFF]), BF16, tn=GM_COLS, name="in_proj_gm")
        p = matmul_call(n, bf(w[:, RW_OFF:SB_OFF]), BF16, tn=RW_COLS, name="in_proj_rw")
        qkv = matmul_call(n, bf(w[:, SB_OFF:GATE_OFF]), BF16, tn=1024, name="in_proj_sb")
        gates = matmul_call(n, bf(w[:, GATE_OFF:]), BF16, tn=1024, bias=gate_b[l].reshape(-1),
                            name="in_proj_gate")
        y_gm = gm_call(uv, gm_ln_g[l], gm_ln_b[l], gm_ws[l], gm_bs[l])
        y_rw = rw_call(p.reshape(B, T, RW_COLS), B, T, rw_mu[l], rw_w0[l], rw_w2[l], rw_a0[l], rw_a2[l],
                       rw_g2[l], rw_kk[l], rw_ka[l], rw_rk[l], rw_ln_g[l], rw_ln_b[l]).reshape(M, RW_WIDTH)
        y_sb = sb_call(qkv.reshape(B, T, SB_COLS), B, T).reshape(M, SB_WIDTH)
        merge_args = (y_gm, y_rw, y_sb, gates, h, bf(p_gm[l]), bf(p_rw[l]), bf(p_sb[l]), bf(w_o[l]), norm_ffn[l])
        if l % 2 == 0:
            h, n2 = merge_call(*merge_args)
            ahead = []
            if not last:
                ahead = [(k, wt[(l + 1) // 2], ct) for k, wt, ct in moe_weight_tiling(moe_w1, moe_w3, moe_w2)
                         if can_ride(wt[(l + 1) // 2], ffn_steps(M, D_FF))]
            outs, casts = ffn_call(n2, h, bf(ffn_w1[j]), bf(ffn_w3[j]), bf(ffn_w2[j]), g_next, n_dtype,
                                   emit_h=not last, riders=[(wt, ct) for _, wt, ct in ahead])
            moe_bf16 = {k: c for (k, _, _), c in zip(ahead, casts)}
        else:
            h, ids, probs = merge_call(*merge_args, router=(router_w[j], router_b[j]))
            src_token, prev_out_row, tile_e, n_used = moe_plan(ids, M, MOE_TM)
            ew = {k: moe_bf16[k] if k in moe_bf16 else bf(wt[j] if ct is None else col_tiled(wt[j], ct))
                  for k, wt, ct in moe_weight_tiling(moe_w1, moe_w3, moe_w2)}
            y_tok = moe_call(h, norm_ffn[l], src_token, prev_out_row, tile_e, n_used, ew["w1"], ew["w3"],
                             ew["w2"], MOE_TM)
            outs = moe_combine_call(h, y_tok, probs, g_next, n_dtype, emit_h=not last)
            moe_bf16 = {}
        if last:
            n = outs[0]
        else:
            h, n = outs
    return n.reshape(B, T, D)
```

```python
import functools

import jax
import jax.numpy as jnp
from jax import lax
from jax.experimental import pallas as pl
from jax.experimental.pallas import tpu as pltpu

F32 = jnp.float32
BF16 = jnp.bfloat16

D_MODEL = 2048
DEPTH = 2
HEAD_DIM = 64
GM_HEADS = 8
GM_WIDTH = GM_HEADS * HEAD_DIM
CHUNK = 128
RW_HEADS = 8
RW_WIDTH = RW_HEADS * HEAD_DIM
DECAY_LORA = 64
AAA_LORA = 64
GATE_LORA = 128
SB_HEADS = 16
SB_WIDTH = SB_HEADS * HEAD_DIM
N_BRANCH = 3
D_FF = 5632
N_EXPERTS = 8
TOP_K = 2
D_FF_EXPERT = D_FF // TOP_K
RMS_EPS = 1e-6
LN_EPS = 1e-5
RW_GN_EPS = 64e-5

GM_COLS = 2 * GM_WIDTH
RW_COLS = 3 * RW_WIDTH + DECAY_LORA + AAA_LORA + GATE_LORA
SB_COLS = 3 * SB_WIDTH
GATE_COLS = N_BRANCH * D_MODEL
GM_OFF = 0
RW_OFF = GM_OFF + GM_COLS
SB_OFF = RW_OFF + RW_COLS
GATE_OFF = SB_OFF + SB_COLS
N_IN = GATE_OFF + GATE_COLS

V7X_VMEM_BYTES = 64 * 1024 * 1024
LANES = 128
BF16_SUBLANES = 16
RW_CHUNK = 64
RW_TILE = 256
RW_GROUP = 4
SB_TILE = 256
LOG2_E = 1.4426950408889634
SB_EXP2_UNDERFLOW = -151.0
ROUTER_LANES = 128
MOE_TF = 256
MOE_TM = 528
MOE_COMBINE_TM = 256


def _vmem_limit(nbytes):
    return int(min(max(nbytes, 32 * 1024 * 1024), V7X_VMEM_BYTES - 6 * 1024 * 1024))


def _dot(a, b):
    return jnp.dot(a, b, preferred_element_type=F32)


def _dot_nt(a, b):
    return lax.dot_general(a, b, (((1,), (1,)), ((), ())), preferred_element_type=F32)


def _dot_tn(a, b):
    return lax.dot_general(a, b, (((0,), (0,)), ((), ())), preferred_element_type=F32)


def _dot_split(x, ones_bf16):
    hi = x.astype(BF16)
    lo = (x - hi.astype(F32)).astype(BF16)
    return _dot(hi, ones_bf16) + _dot(lo, ones_bf16)


def _rms(x, g):
    ms = jnp.mean(x * x, axis=-1, keepdims=True)
    return x * lax.rsqrt(ms + RMS_EPS) * g


def _rmsnorm_kernel(x_ref, g_ref, o_ref):
    o_ref[...] = _rms(x_ref[...], g_ref[...]).astype(o_ref.dtype)


def rmsnorm_call(x, g, out_dtype, tm=512):
    M, D = x.shape
    tm = min(tm, M)
    return pl.pallas_call(
        _rmsnorm_kernel,
        grid=(M // tm,),
        in_specs=[pl.BlockSpec((tm, D), lambda i: (i, 0)),
                  pl.BlockSpec((1, D), lambda i: (0, 0))],
        out_specs=pl.BlockSpec((tm, D), lambda i: (i, 0)),
        out_shape=jax.ShapeDtypeStruct((M, D), out_dtype),
        compiler_params=pltpu.CompilerParams(dimension_semantics=("parallel",)),
        name="rmsnorm",
    )(x, g.reshape(1, D))


def _mm_kernel(a_ref, w_ref, o_ref):
    o_ref[...] = _dot(a_ref[...], w_ref[...]).astype(o_ref.dtype)


def _mm_gate_kernel(a_ref, w_ref, b_ref, o_ref):
    o_ref[...] = jax.nn.sigmoid(_dot(a_ref[...], w_ref[...]) + b_ref[...]).astype(o_ref.dtype)


def matmul_call(a, w, out_dtype, tn, bias=None, tm=1024, name="matmul"):
    M, K = a.shape
    N = w.shape[1]
    tm = min(tm, M)
    in_specs = [pl.BlockSpec((tm, K), lambda i, j: (i, 0)),
                pl.BlockSpec((K, tn), lambda i, j: (0, j))]
    args = [a, w]
    kern = _mm_kernel
    if bias is not None:
        in_specs.append(pl.BlockSpec((1, tn), lambda i, j: (0, j)))
        args.append(bias.reshape(1, N))
        kern = _mm_gate_kernel
    osz = jnp.dtype(out_dtype).itemsize
    need = 2 * (tm * K * 2 + K * tn * 2 + tm * tn * osz) + 2 * tm * tn * 4
    return pl.pallas_call(
        kern,
        grid=(M // tm, N // tn),
        in_specs=in_specs,
        out_specs=pl.BlockSpec((tm, tn), lambda i, j: (i, j)),
        out_shape=jax.ShapeDtypeStruct((M, N), out_dtype),
        compiler_params=pltpu.CompilerParams(
            dimension_semantics=("parallel", "arbitrary"),
            vmem_limit_bytes=_vmem_limit(need + (8 << 20))),
        name=name,
    )(*args)


def _gm_kernel(uv_ref, lng_ref, lnb_ref, ws_ref, bias_ref, o_ref, *, n_chunks):
    ii = lax.broadcasted_iota(jnp.int32, (CHUNK, CHUNK), 0)
    jj = lax.broadcasted_iota(jnp.int32, (CHUNK, CHUNK), 1)
    causal = jj <= ii
    ws = [jnp.where(causal, ws_ref[h], 0.0).astype(BF16) for h in range(GM_HEADS)]
    for c in range(n_chunks):
        rows = slice(c * CHUNK, (c + 1) * CHUNK)
        u = jax.nn.gelu(uv_ref[rows, 0:GM_WIDTH].astype(F32))
        v = jax.nn.gelu(uv_ref[rows, GM_WIDTH:2 * GM_WIDTH].astype(F32))
        mu = jnp.mean(v, axis=-1, keepdims=True)
        vc = v - mu
        var = jnp.mean(vc * vc, axis=-1, keepdims=True)
        vn = (vc * lax.rsqrt(var + LN_EPS) * lng_ref[...] + lnb_ref[...]).astype(BF16)
        mixed = jnp.concatenate(
            [_dot(ws[h], vn[:, h * HEAD_DIM:(h + 1) * HEAD_DIM]) for h in range(GM_HEADS)], axis=-1)
        o_ref[rows, :] = (u * (mixed + bias_ref[...])).astype(o_ref.dtype)


def gm_call(uv, ln_g, ln_b, w_s, b_s, n_chunks=2):
    M = uv.shape[0]
    tt = CHUNK * n_chunks
    bias = jnp.repeat(b_s.T, HEAD_DIM, axis=1)
    return pl.pallas_call(
        functools.partial(_gm_kernel, n_chunks=n_chunks),
        grid=(M // tt,),
        in_specs=[pl.BlockSpec((tt, GM_COLS), lambda i: (i, 0)),
                  pl.BlockSpec((1, GM_WIDTH), lambda i: (0, 0)),
                  pl.BlockSpec((1, GM_WIDTH), lambda i: (0, 0)),
                  pl.BlockSpec((GM_HEADS, CHUNK, CHUNK), lambda i: (0, 0, 0)),
                  pl.BlockSpec((CHUNK, GM_WIDTH), lambda i: (0, 0))],
        out_specs=pl.BlockSpec((tt, GM_WIDTH), lambda i: (i, 0)),
        out_shape=jax.ShapeDtypeStruct((M, GM_WIDTH), BF16),
        compiler_params=pltpu.CompilerParams(dimension_semantics=("parallel",)),
        name="gmlp_gating",
    )(uv, ln_g.reshape(1, -1), ln_b.reshape(1, -1), w_s, bias)


def _rw_kernel(p_ref, prev_ref, mu_ref, w0_ref, w2_ref, a0_ref, a2_ref, g2_ref, kk_ref, ka_ref,
               rk_ref, lng_ref, lnb_ref, hsum_ref, o_ref, st_ref, y_ref):
    C = RW_CHUNK
    i = pl.program_id(1)

    @pl.when(i == 0)
    def _():
        st_ref[...] = jnp.zeros_like(st_ref)

    hsum = hsum_ref[...]
    gw = hsum.shape[0]

    def head_sums(x):
        return jnp.concatenate([_dot_split(x[:, q * gw:(q + 1) * gw], hsum)
                                for q in range(x.shape[1] // gw)], axis=1)

    p = p_ref[...].astype(F32)
    prev_row = prev_ref[BF16_SUBLANES - 1:BF16_SUBLANES, :].astype(F32)
    prev_row = jnp.where(i == 0, 0.0, prev_row)
    row = lax.broadcasted_iota(jnp.int32, p.shape, 0)
    p_prev = jnp.where(row == 0, prev_row, pltpu.roll(p, 1, axis=0))
    ps = p + (p_prev - p) * mu_ref[...]
    W = RW_WIDTH
    r = ps[:, 0:W]
    k = ps[:, W:2 * W]
    v = ps[:, 2 * W:3 * W]
    wl = ps[:, 3 * W:3 * W + DECAY_LORA]
    al = ps[:, 3 * W + DECAY_LORA:3 * W + DECAY_LORA + AAA_LORA]
    gl = ps[:, 3 * W + DECAY_LORA + AAA_LORA:RW_COLS]

    d = w0_ref[...] + _dot(jnp.tanh(wl).astype(BF16), w2_ref[...])
    log_w = -jnp.exp(-jax.nn.softplus(-d) - 0.5)
    a = jax.nn.sigmoid(a0_ref[...] + _dot(al.astype(BF16), a2_ref[...]))
    g = _dot(jax.nn.sigmoid(gl).astype(BF16), g2_ref[...])
    kk = k * kk_ref[...]
    kk = kk / jnp.maximum(jnp.sqrt(head_sums(kk * kk)), 1e-12)
    k = k * (1.0 + (a - 1.0) * ka_ref[...])
    bonus = head_sums(r * k * rk_ref[...]) * v

    HD = HEAD_DIM
    GW = RW_GROUP * HD
    ci = lax.broadcasted_iota(jnp.int32, (C, C), 0)
    cj = lax.broadcasted_iota(jnp.int32, (C, C), 1)
    tri_incl = (cj <= ci).astype(BF16)
    ti = lax.broadcasted_iota(jnp.int32, (C, GW), 0)
    tj = lax.broadcasted_iota(jnp.int32, (C, GW), 1) % HD
    strict = tj < ti
    incl = tj <= ti
    eye = ti == tj
    blk_masks = []
    s = 2
    while s < C:
        blk_masks.append((ti // (2 * s) == tj // (2 * s)) & (ti // s != tj // s))
        s *= 2
    pair_mask = (ti // 2 == tj // 2) & strict
    gi = lax.broadcasted_iota(jnp.int32, (GW, GW), 0)
    gj = lax.broadcasted_iota(jnp.int32, (GW, GW), 1)
    bd_mask = gi // HD == gj // HD
    bd_mask_b = jnp.where(bd_mask, 1.0, 0.0).astype(BF16)
    gw_eye = gi == gj

    def bd(x):
        xb = x.astype(BF16)
        return jnp.where(bd_mask_b > 0, jnp.concatenate([xb] * RW_GROUP, axis=0), 0)

    def bd_of_full(full):
        return jnp.where(bd_mask, full, 0.0)

    n_chunks = RW_TILE // C
    n_groups = RW_WIDTH // GW
    systems = [(c, q) for c in range(n_chunks) for q in range(n_groups)]

    pre = []
    for c in range(n_chunks):
        rows = slice(c * C, (c + 1) * C)
        lw = log_w[rows]
        cum = _cumsum_rows(lw, tri_incl)
        gam = jnp.exp(cum)
        gam_prev = jnp.exp(cum - lw)
        gam_inv = jnp.exp(-cum)
        g_end = gam[C - 1:C, :]
        bh_f = kk[rows] * a[rows] * gam_inv
        kh_f = k[rows] * gam_inv
        pre.append(dict(
            rh=(r[rows] * gam).astype(BF16), ah=(-kk[rows] * gam_prev).astype(BF16),
            bh=bh_f.astype(BF16), kh=kh_f.astype(BF16),
            bt=(bh_f * g_end).astype(BF16), kt=(kh_f * g_end).astype(BF16),
            v=v[rows].astype(BF16), g_end=g_end))

    def grp(c, q, name):
        return pre[c][name][:, q * GW:(q + 1) * GW]

    st = {}
    for (c, q) in systems:
        ar = jnp.concatenate([grp(c, q, "ah"), grp(c, q, "rh")], axis=0)
        pb = _dot_nt(ar, bd(grp(c, q, "bh")))
        pk = _dot_nt(ar, bd(grp(c, q, "kh")))
        lab = jnp.where(strict, pb[0:C], 0.0)
        st[(c, q)] = dict(
            lab=lab, mrb=jnp.where(incl, pb[C:2 * C], 0.0).astype(BF16),
            lm=jnp.concatenate([jnp.where(strict, pk[0:C], 0.0), jnp.where(incl, pk[C:2 * C], 0.0)],
                               axis=0).astype(BF16),
            tinv=jnp.where(eye, 1.0, jnp.where(pair_mask, lab, 0.0)))

    for bm in blk_masks:
        xs = {}
        for key in systems:
            d_ = st[key]
            xs[key] = _dot(d_["tinv"].astype(BF16), bd(jnp.where(bm, d_["lab"], 0.0))).astype(BF16)
        for key in systems:
            d_ = st[key]
            d_["tinv"] = d_["tinv"] + _dot(xs[key], bd(d_["tinv"]))

    for (c, q) in systems:
        d_ = st[(c, q)]
        lv = _dot(d_["lm"], bd(grp(c, q, "v")))
        tb = d_["tinv"].astype(BF16)
        ap = _dot(tb, bd(grp(c, q, "ah"))).astype(BF16)
        up = _dot(tb, bd(lv[0:C])).astype(BF16)
        d_["rp"] = (grp(c, q, "rh").astype(F32) + _dot(d_["mrb"], bd(ap))).astype(BF16)
        d_["ypp"] = _dot(d_["mrb"], bd(up)) + lv[C:2 * C]
        g_end = pre[c]["g_end"][:, q * GW:(q + 1) * GW]
        gt = _dot_tn(ap, grp(c, q, "bt"))
        d_["gt"] = jnp.where(gw_eye, gt + g_end, bd_of_full(gt)).astype(BF16)
        ht = bd_of_full(_dot_tn(up, grp(c, q, "bt")) + _dot_tn(grp(c, q, "v"), grp(c, q, "kt")))
        d_["ht"] = sum(ht[hh * HD:(hh + 1) * HD] for hh in range(1, RW_GROUP)) + ht[0:HD]

    for q in range(n_groups):
        s_cur = st_ref[q]
        for c in range(n_chunks):
            d_ = st[(c, q)]
            y_ref[c * C:(c + 1) * C, q * GW:(q + 1) * GW] = _dot_nt(d_["rp"], bd(s_cur)) + d_["ypp"]
            s_cur = _dot(s_cur.astype(BF16), d_["gt"]) + d_["ht"]
        st_ref[q] = s_cur

    y = y_ref[...]
    inv_n = 1.0 / HEAD_DIM
    m = head_sums(y) * inv_n
    yc = y - m
    var = head_sums(yc * yc) * inv_n
    yn = yc * lax.rsqrt(var + RW_GN_EPS) * lng_ref[...] + lnb_ref[...]
    o_ref[...] = ((yn + bonus) * g).astype(o_ref.dtype)


def _cumsum_rows(x, tri_incl_bf16):
    hi = x.astype(BF16)
    lo = (x - hi.astype(F32)).astype(BF16)
    return _dot(tri_incl_bf16, hi) + _dot(tri_incl_bf16, lo)


def rw_call(p, B, T, mu, w0, w2, a0, a2, g2, k_k, k_a, r_k, ln_g, ln_b):
    nt = T // RW_TILE
    sub = RW_TILE // BF16_SUBLANES
    hid = jnp.arange(RW_GROUP * HEAD_DIM) // HEAD_DIM
    hsum = (hid[:, None] == hid[None, :]).astype(BF16)
    row = lambda a: a.reshape(1, -1).astype(F32)
    vec = lambda n: pl.BlockSpec((1, n), lambda b, i: (0, 0))
    return pl.pallas_call(
        _rw_kernel,
        grid=(B, nt),
        in_specs=[pl.BlockSpec((None, RW_TILE, RW_COLS), lambda b, i: (b, i, 0)),
                  pl.BlockSpec((None, BF16_SUBLANES, RW_COLS),
                               lambda b, i: (b, jnp.maximum(i * sub - 1, 0), 0)),
                  vec(RW_COLS), vec(RW_WIDTH),
                  pl.BlockSpec((DECAY_LORA, RW_WIDTH), lambda b, i: (0, 0)),
                  vec(RW_WIDTH),
                  pl.BlockSpec((AAA_LORA, RW_WIDTH), lambda b, i: (0, 0)),
                  pl.BlockSpec((GATE_LORA, RW_WIDTH), lambda b, i: (0, 0)),
                  vec(RW_WIDTH), vec(RW_WIDTH), vec(RW_WIDTH), vec(RW_WIDTH), vec(RW_WIDTH),
                  pl.BlockSpec((RW_GROUP * HEAD_DIM, RW_GROUP * HEAD_DIM), lambda b, i: (0, 0))],
        out_specs=pl.BlockSpec((None, RW_TILE, RW_WIDTH), lambda b, i: (b, i, 0)),
        out_shape=jax.ShapeDtypeStruct((B, T, RW_WIDTH), BF16),
        scratch_shapes=[pltpu.VMEM((RW_HEADS // RW_GROUP, HEAD_DIM, RW_GROUP * HEAD_DIM), F32),
                        pltpu.VMEM((RW_TILE, RW_WIDTH), F32)],
        compiler_params=pltpu.CompilerParams(dimension_semantics=("parallel", "arbitrary")),
        name="rwkv7_time_mix",
    )(p, p, row(mu), row(w0), w2.astype(BF16), row(a0), a2.astype(BF16), g2.astype(BF16),
      row(k_k), row(k_a), row(r_k), row(ln_g), row(ln_b), hsum)


def _sb_scores(qs, ks, masks, neg_suffix):
    n = range(len(qs))
    zs = [_dot_nt(qs[s], ks[s]) * LOG2_E for s in n]
    sps = [jnp.maximum(z, 0.0) + jnp.log2(1.0 + jnp.exp2(-jnp.abs(z))) for z in zs]
    vis = [sp if masks[s] is None else jnp.where(masks[s], sp, 0.0) for s, sp in enumerate(sps)]
    his = [sp.astype(BF16) for sp in vis]
    los = [(vis[s] - his[s].astype(F32)).astype(BF16) for s in n]
    sufs = [_dot(his[s], neg_suffix[s]) + _dot(los[s], neg_suffix[s]) for s in n]
    rowsums = [-jnp.sum(sp, axis=-1, keepdims=True) for sp in vis]
    return zs, sps, sufs, rowsums


def _sb_weighted(scores, s, carry, mask, v):
    zs, sps, sufs, _ = scores
    att = jnp.exp2(zs[s] - sps[s] + (sufs[s] + carry))
    if mask is not None:
        att = jnp.where(mask, att, 0.0)
    return _dot(att.astype(BF16), v)


def _sb_kernel(q_ref, k_ref, v_ref, o_ref, acc_ref, carry_ref, *, T):
    tq = SB_TILE
    nq = T // tq
    th = tq // 2
    ti = lax.broadcasted_iota(jnp.int32, (tq, tq), 0)
    tj = lax.broadcasted_iota(jnp.int32, (tq, tq), 1)
    neg_full = jnp.where(ti > tj, -1.0, 0.0).astype(BF16)
    neg_half = neg_full[:th, :th]
    causal_top = (tj < ti)[:th, :th]
    causal_bot = (tj < ti)[th:, :]
    scale = HEAD_DIM ** -0.5
    heads = [slice(hh * HEAD_DIM, (hh + 1) * HEAD_DIM) for hh in range(LANES // HEAD_DIM)]
    nh = len(heads)

    def q_body(i, _):
        r0 = pl.multiple_of(i * tq, tq)
        has_prev = jnp.where(i > 0, 1.0, 0.0)
        p0 = pl.multiple_of(jnp.maximum(i - 1, 0) * tq, tq)
        qs = [q_ref[pl.ds(r0, tq), cs] * scale for cs in heads]
        kd = [k_ref[pl.ds(r0, tq), cs] for cs in heads]
        vd = [v_ref[pl.ds(r0, tq), cs] for cs in heads]
        kp = [k_ref[pl.ds(p0, tq), cs] for cs in heads]
        vp = [v_ref[pl.ds(p0, tq), cs] for cs in heads]
        scores = _sb_scores([q[:th] for q in qs] + [q[th:] for q in qs] + qs,
                            [k[:th] for k in kd] + kd + kp,
                            [causal_top] * nh + [causal_bot] * nh + [None] * nh,
                            [neg_half] * nh + [neg_full] * (2 * nh))
        rss = scores[3]
        live = jnp.float32(-jnp.inf)
        for hh, cs in enumerate(heads):
            out = jnp.concatenate([_sb_weighted(scores, hh, 0.0, causal_top, vd[hh][:th]),
                                   _sb_weighted(scores, nh + hh, 0.0, causal_bot, vd[hh])], axis=0)
            rs_diag = jnp.concatenate([rss[hh], rss[nh + hh]], axis=0)
            out_prev = _sb_weighted(scores, 2 * nh + hh, rs_diag, None, vp[hh])
            acc_ref[:, cs] = out + has_prev * out_prev
            carry = rs_diag + has_prev * rss[2 * nh + hh]
            carry_ref[hh] = carry
            live = jnp.maximum(live, jnp.max(carry))

        def k_cond(state):
            jr, live = state
            return (jr < i) & (live > SB_EXP2_UNDERFLOW)

        def k_body(state):
            jr, _ = state
            c0 = pl.multiple_of((i - 1 - jr) * tq, tq)
            kt = [k_ref[pl.ds(c0, tq), cs] for cs in heads]
            vt = [v_ref[pl.ds(c0, tq), cs] for cs in heads]
            scores = _sb_scores(qs, kt, [None] * nh, [neg_full] * nh)
            live = jnp.float32(-jnp.inf)
            for hh, cs in enumerate(heads):
                carry = carry_ref[hh]
                acc_ref[:, cs] += _sb_weighted(scores, hh, carry, None, vt[hh])
                carry = carry + scores[3][hh]
                carry_ref[hh] = carry
                live = jnp.maximum(live, jnp.max(carry))
            return jr + 1, live

        lax.while_loop(k_cond, k_body, (jnp.int32(1), live))
        o_ref[pl.ds(r0, tq), :] = acc_ref[...].astype(o_ref.dtype)
        return 0

    lax.fori_loop(0, nq, q_body, 0)


def sb_call(qkv, B, T):
    npair = SB_WIDTH // LANES
    blk = lambda off: pl.BlockSpec((None, T, LANES), lambda b, hp: (b, 0, off + hp))
    return pl.pallas_call(
        functools.partial(_sb_kernel, T=T),
        grid=(B, npair),
        in_specs=[blk(0), blk(npair), blk(2 * npair)],
        out_specs=pl.BlockSpec((None, T, LANES), lambda b, hp: (b, 0, hp)),
        out_shape=jax.ShapeDtypeStruct((B, T, SB_WIDTH), BF16),
        scratch_shapes=[pltpu.VMEM((SB_TILE, LANES), F32),
                        pltpu.VMEM((LANES // HEAD_DIM, SB_TILE, 1), F32)],
        compiler_params=pltpu.CompilerParams(dimension_semantics=("parallel", "parallel")),
        name="stick_breaking_attention",
    )(qkv, qkv, qkv)


def _route(n, w_ref, b_ref, id_ref, p_ref):
    nh = n.astype(BF16)
    nl = (n - nh.astype(F32)).astype(BF16)
    a = _dot(nh, w_ref[...])
    b = _dot(nl, w_ref[...])
    L = ROUTER_LANES
    logits = a[:, :L] + (a[:, L:] + b[:, :L] + b[:, L:]) + b_ref[...]
    lane = lax.broadcasted_iota(jnp.int32, logits.shape, 1)
    neg = jnp.float32(-jnp.inf)
    logits = jnp.where(lane < N_EXPERTS, logits, neg)
    m1 = jnp.max(logits, axis=-1, keepdims=True)
    i1 = jnp.min(jnp.where(logits == m1, lane, ROUTER_LANES), axis=-1, keepdims=True)
    rest = jnp.where(lane == i1, neg, logits)
    m2 = jnp.max(rest, axis=-1, keepdims=True)
    i2 = jnp.min(jnp.where(rest == m2, lane, ROUTER_LANES), axis=-1, keepdims=True)
    e2 = jnp.exp(m2 - m1)
    p1 = 1.0 / (1.0 + e2)
    p2 = e2 / (1.0 + e2)
    id_ref[...] = jnp.where(lane == 0, i1, jnp.where(lane == 1, i2, 0))
    p_ref[...] = jnp.where(lane == 0, p1, jnp.where(lane == 1, p2, 0.0))


def _merge_kernel(ygm_ref, yrw_ref, ysb_ref, gate_ref, h_ref, pgm_ref, prw_ref, psb_ref, wo_ref,
                  g_ref, *refs, route):
    D = D_MODEL
    merged = gate_ref[:, 0:D].astype(F32) * _dot(ygm_ref[...], pgm_ref[...])
    merged += gate_ref[:, D:2 * D].astype(F32) * _dot(yrw_ref[...], prw_ref[...])
    merged += gate_ref[:, 2 * D:3 * D].astype(F32) * _dot(ysb_ref[...], psb_ref[...])
    hn = h_ref[...] + _dot(merged.astype(BF16), wo_ref[...])
    n = _rms(hn, g_ref[...])
    if route:
        w_ref, b_ref, ho_ref, id_ref, p_ref = refs
        _route(n, w_ref, b_ref, id_ref, p_ref)
    else:
        ho_ref, no_ref = refs
        no_ref[...] = n.astype(no_ref.dtype)
    ho_ref[...] = hn


def _resident(shape):
    return pl.BlockSpec(shape, lambda i: (0,) * len(shape), pipeline_mode=pl.Buffered(1))


def merge_call(y_gm, y_rw, y_sb, gates, h, p_gm, p_rw, p_sb, w_o, g_next, router=None, tm=256):
    M, D = h.shape
    tm = min(tm, M)
    rowblk = lambda n: pl.BlockSpec((tm, n), lambda i: (i, 0))
    wbytes = 2 * (GM_WIDTH + RW_WIDTH + SB_WIDTH + D) * D
    need = wbytes + 2 * tm * (2 * (GM_WIDTH + RW_WIDTH + SB_WIDTH + GATE_COLS) + 4 * D + 4 * D + 2 * D) \
        + 6 * tm * D * 4
    in_specs = [rowblk(GM_WIDTH), rowblk(RW_WIDTH), rowblk(SB_WIDTH), rowblk(GATE_COLS), rowblk(D),
                _resident((GM_WIDTH, D)), _resident((RW_WIDTH, D)), _resident((SB_WIDTH, D)),
                _resident((D, D)), _resident((1, D))]
    args = [y_gm, y_rw, y_sb, gates, h, p_gm, p_rw, p_sb, w_o, g_next.reshape(1, D)]
    if router is None:
        out_specs = [rowblk(D), rowblk(D)]
        out_shape = [jax.ShapeDtypeStruct((M, D), F32), jax.ShapeDtypeStruct((M, D), BF16)]
    else:
        router_w, router_b = router
        wpad = jnp.zeros((D, ROUTER_LANES), F32).at[:, :N_EXPERTS].set(router_w)
        w_hi = wpad.astype(BF16)
        w_lo = (wpad - w_hi.astype(F32)).astype(BF16)
        bpad = jnp.zeros((1, ROUTER_LANES), F32).at[0, :N_EXPERTS].set(router_b)
        in_specs += [_resident((D, 2 * ROUTER_LANES)), _resident((1, ROUTER_LANES))]
        args += [jnp.concatenate([w_hi, w_lo], axis=1), bpad]
        out_specs = [rowblk(D), rowblk(ROUTER_LANES), rowblk(ROUTER_LANES)]
        out_shape = [jax.ShapeDtypeStruct((M, D), F32), jax.ShapeDtypeStruct((M, ROUTER_LANES), jnp.int32),
                     jax.ShapeDtypeStruct((M, ROUTER_LANES), F32)]
    return pl.pallas_call(
        functools.partial(_merge_kernel, route=router is not None),
        grid=(M // tm,),
        in_specs=in_specs,
        out_specs=out_specs,
        out_shape=out_shape,
        compiler_params=pltpu.CompilerParams(dimension_semantics=("parallel",),
                                             vmem_limit_bytes=_vmem_limit(need)),
        name="merge_out_proj",
    )(*args)


def _ffn_kernel(n_ref, h_ref, w1_ref, w3_ref, w2_ref, g_ref, *refs, emit_h, n_cast):
    cast_in, refs = refs[:n_cast], refs[n_cast:]
    if emit_h:
        ho_ref, no_ref = refs[:2]
        refs = refs[2:]
    else:
        no_ref = refs[0]
        refs = refs[1:]
    cast_out, acc_ref = refs[:n_cast], refs[n_cast]
    f = pl.program_id(1)

    @pl.when(f == 0)
    def _():
        acc_ref[...] = jnp.zeros_like(acc_ref)

    n = n_ref[...]
    a = _dot(n, w1_ref[...])
    b = _dot(n, w3_ref[...])
    for src, dst in zip(cast_in, cast_out):
        dst[...] = src[...].astype(dst.dtype)
    act = (jax.nn.silu(a) * b).astype(BF16)
    acc_ref[...] += _dot(act, w2_ref[...])

    @pl.when(f == pl.num_programs(1) - 1)
    def _():
        hn = h_ref[...] + acc_ref[...]
        if emit_h:
            ho_ref[...] = hn
        no_ref[...] = _rms(hn, g_ref[...]).astype(no_ref.dtype)


def ffn_steps(M, F, tm=512, tf=512):
    return (M // min(tm, M)) * (F // tf)


RIDER_SLAB_ROWS = 64


def can_ride(w, steps):
    return w.shape[1] % RIDER_SLAB_ROWS == 0 and w.shape[0] * (w.shape[1] // RIDER_SLAB_ROWS) <= steps


def ffn_call(n, h, w1, w3, w2, g_next, n_dtype, emit_h, riders=(), tm=512, tf=512):
    M, D = h.shape
    F = w1.shape[1]
    tm = min(tm, M)
    nf = F // tf
    steps = (M // tm) * nf
    rowblk = pl.BlockSpec((tm, D), lambda i, f: (i, 0))
    out_specs = [rowblk]
    out_shape = [jax.ShapeDtypeStruct((M, D), n_dtype)]
    if emit_h:
        out_specs = [rowblk, rowblk]
        out_shape = [jax.ShapeDtypeStruct((M, D), F32)] + out_shape
    n_main = len(out_shape)

    def slab_spec(w):
        E, R, C = w.shape
        per = R // RIDER_SLAB_ROWS
        n_slabs = E * per
        hold = steps // n_slabs
        slab = lambda i, f: jnp.minimum((i * nf + f) // hold, n_slabs - 1)
        return pl.BlockSpec((None, RIDER_SLAB_ROWS, C), lambda i, f: (slab(i, f) // per, slab(i, f) % per, 0))

    ride_specs = [slab_spec(w) for w in riders]
    need = 2 * tm * D * (2 + 4 + 4 * n_main) + tm * D * 4 + 2 * 3 * D * tf * 2 + 4 * tm * tf * 4 \
        + sum(2 * RIDER_SLAB_ROWS * w.shape[2] * 6 for w in riders)
    outs = pl.pallas_call(
        functools.partial(_ffn_kernel, emit_h=emit_h, n_cast=len(riders)),
        grid=(M // tm, nf),
        in_specs=[rowblk, rowblk,
                  pl.BlockSpec((D, tf), lambda i, f: (0, f)),
                  pl.BlockSpec((D, tf), lambda i, f: (0, f)),
                  pl.BlockSpec((tf, D), lambda i, f: (f, 0)),
                  pl.BlockSpec((1, D), lambda i, f: (0, 0))] + ride_specs,
        out_specs=out_specs + ride_specs,
        out_shape=out_shape + [jax.ShapeDtypeStruct(w.shape, BF16) for w in riders],
        scratch_shapes=[pltpu.VMEM((tm, D), F32)],
        compiler_params=pltpu.CompilerParams(dimension_semantics=("arbitrary", "arbitrary"),
                                             vmem_limit_bytes=_vmem_limit(need + (8 << 20))),
        name="swiglu_ffn",
    )(n, h, w1, w3, w2, g_next.reshape(1, D), *riders)
    return outs[:n_main], outs[n_main:]


def moe_plan(ids, M, tm):
    E = N_EXPERTS
    n_tiles = -(-(TOP_K * M) // tm) + E + 2
    e_flat = ids[:, :TOP_K].reshape(-1)
    onehot = (e_flat[:, None] == jnp.arange(E, dtype=jnp.int32)[None, :]).astype(jnp.int32)
    csum = jnp.cumsum(onehot, axis=0)
    rank = jnp.sum((csum - onehot) * onehot, axis=1)
    counts = csum[-1]
    tiles_per_e = (counts + tm - 1) // tm
    tile_end = jnp.cumsum(tiles_per_e)
    offs = (tile_end - tiles_per_e) * tm
    dest = offs[e_flat] + rank
    pair = jnp.full((n_tiles * tm,), -1, jnp.int32).at[dest].set(
        jnp.arange(TOP_K * M, dtype=jnp.int32), unique_indices=True)
    spare = TOP_K * M + jnp.arange(n_tiles * tm, dtype=jnp.int32) % tm
    src_token = jnp.where(pair >= 0, pair // TOP_K, 0)
    out_row = jnp.where(pair >= 0, (pair % TOP_K) * M + pair // TOP_K, spare)
    prev_out_row = jnp.concatenate([spare[:tm], out_row[:-tm]])
    n_used = tile_end[-1]
    t = jnp.arange(n_tiles, dtype=jnp.int32)
    tile_e = jnp.searchsorted(tile_end, jnp.minimum(t, n_used - 1), side="right").astype(jnp.int32)
    return (src_token.reshape(n_tiles, 1, tm), prev_out_row.reshape(n_tiles, 1, tm), tile_e,
            n_used.astype(jnp.int32).reshape(1))


def _bf16_bits(x):
    u = lax.bitcast_convert_type(x, jnp.uint32)
    return (u + jnp.uint32(0x7FFF) + ((u >> 16) & jnp.uint32(1))) & jnp.uint32(0xFFFF0000)


def _pack_bf16_pairs(x):
    half = x.shape[1] // 2
    return _bf16_bits(x[:, half:]) | (_bf16_bits(x[:, :half]) >> 16)


def _unpack_bf16_pairs(w):
    lo = lax.bitcast_convert_type(w << 16, F32)
    hi = lax.bitcast_convert_type(w & jnp.uint32(0xFFFF0000), F32)
    return jnp.concatenate([lo, hi], axis=1)


def _moe_kernel(tile_e_ref, n_used_ref, src_ref, nxt_ref, dst_ref, h_hbm, g_ref, w1_ref, w3_ref, w2_ref,
                y_hbm, xbuf, xb, acc, ostage, gsem, ssem):
    t = pl.program_id(0)
    f = pl.program_id(1)
    nf = pl.num_programs(1)
    tm = xb.shape[0]
    per_step = tm // nf
    n_used = n_used_ref[0]
    used = t < n_used
    cur = t % 2

    def fetch(idx_ref, r, slot):
        return pltpu.make_async_copy(h_hbm.at[pl.ds(idx_ref[0, r], 1)], xbuf.at[slot, pl.ds(r, 1)],
                                     gsem.at[slot])

    def put(r):
        return pltpu.make_async_copy(ostage.at[1 - cur, pl.ds(r, 1)], y_hbm.at[pl.ds(dst_ref[0, r], 1)],
                                     ssem.at[1 - cur])

    @pl.when((f == 0) & (t == 0))
    def _():
        def issue(r, c):
            fetch(src_ref, r, 0).start()
            return c
        lax.fori_loop(0, tm, issue, 0)
        ostage[1] = jnp.zeros((tm, ostage.shape[2]), ostage.dtype)

    @pl.when((f == 0) & (t <= n_used))
    def _():
        pltpu.make_async_copy(h_hbm.at[pl.ds(0, tm)], xbuf.at[cur], gsem.at[cur]).wait()

    @pl.when((f == 0) & (t >= 1) & (t - 1 <= n_used))
    def _():
        pltpu.make_async_copy(ostage.at[cur], y_hbm.at[pl.ds(0, tm)], ssem.at[cur]).wait()

    @pl.when((f == 0) & used)
    def _():
        xb[...] = _rms(xbuf[cur], g_ref[...]).astype(BF16)
        acc[...] = jnp.zeros_like(acc)

    @pl.when(used)
    def _():
        for j in range(per_step):
            fetch(nxt_ref, f * per_step + j, 1 - cur).start()
            put(f * per_step + j).start()
        x = xb[...]
        act = jax.nn.silu(_dot(x, w1_ref[...])) * _dot(x, w3_ref[...])
        acc[...] += _dot(act.astype(BF16), w2_ref[...])

    @pl.when(used & (f == nf - 1))
    def _():
        ostage[cur] = _pack_bf16_pairs(acc[...])

    @pl.when((f == 0) & (t == n_used))
    def _():
        def issue(r, c):
            put(r).start()
            return c
        lax.fori_loop(0, tm, issue, 0)


def moe_call(h, g, src_token, prev_out_row, tile_e, n_used, w1, w3, w2, tm, tf):
    M, D = h.shape
    n_tiles = src_token.shape[0]
    E, _, Fe = w1.shape
    nf = Fe // tf
    assert tm % nf == 0 and Fe % tf == 0

    def wmap(t, f, tile_e_ref, n_used_ref):
        ff = jnp.where(t < n_used_ref[0], f, nf - 1)
        return tile_e_ref[t], ff

    smem_tile = lambda fn: pl.BlockSpec((None, 1, tm), fn, memory_space=pltpu.SMEM)
    grid_spec = pltpu.PrefetchScalarGridSpec(
        num_scalar_prefetch=2,
        grid=(n_tiles, nf),
        in_specs=[smem_tile(lambda t, f, te, nu: (t, 0, 0)),
                  smem_tile(lambda t, f, te, nu: (jnp.minimum(t + 1, n_tiles - 1), 0, 0)),
                  smem_tile(lambda t, f, te, nu: (t, 0, 0)),
                  pl.BlockSpec(memory_space=pl.ANY),
                  pl.BlockSpec((1, D), lambda t, f, te, nu: (0, 0)),
                  pl.BlockSpec((None, D, tf), lambda t, f, te, nu: (wmap(t, f, te, nu)[0], 0,
                                                                     wmap(t, f, te, nu)[1])),
                  pl.BlockSpec((None, D, tf), lambda t, f, te, nu: (wmap(t, f, te, nu)[0], 0,
                                                                     wmap(t, f, te, nu)[1])),
                  pl.BlockSpec((None, tf, D), lambda t, f, te, nu: wmap(t, f, te, nu) + (0,))],
        out_specs=pl.BlockSpec(memory_space=pl.ANY),
        scratch_shapes=[pltpu.VMEM((2, tm, D), F32), pltpu.VMEM((tm, D), BF16), pltpu.VMEM((tm, D), F32),
                        pltpu.VMEM((2, tm, D // 2), jnp.uint32),
                        pltpu.SemaphoreType.DMA((2,)), pltpu.SemaphoreType.DMA((2,))])
    need = tm * D * (8 + 2 + 4 + 4) + 2 * 3 * D * tf * 2 + 4 * tm * tf * 4
    return pl.pallas_call(
        _moe_kernel,
        grid_spec=grid_spec,
        out_shape=jax.ShapeDtypeStruct((TOP_K * M + tm, D // 2), jnp.uint32),
        compiler_params=pltpu.CompilerParams(dimension_semantics=("arbitrary", "arbitrary"),
                                             vmem_limit_bytes=_vmem_limit(need + (8 << 20))),
        name="moe_experts",
    )(tile_e, n_used, src_token, src_token, prev_out_row, h, g.reshape(1, D), w1, w3, w2)


def _moe_combine_kernel(h_ref, p_ref, g_ref, *refs, emit_h):
    y_refs, out_refs = refs[:TOP_K], refs[TOP_K:]
    hn = h_ref[...]
    for k in range(TOP_K):
        hn = hn + p_ref[:, k:k + 1] * _unpack_bf16_pairs(y_refs[k][...])
    if emit_h:
        out_refs[0][...] = hn
    out_refs[-1][...] = _rms(hn, g_ref[...]).astype(out_refs[-1].dtype)


def moe_combine_call(h, y_tok, probs, g_next, n_dtype, emit_h, tm=MOE_COMBINE_TM):
    M, D = h.shape
    tm = min(tm, M)
    rowblk = pl.BlockSpec((tm, D), lambda i: (i, 0))
    out_specs = [rowblk]
    out_shape = [jax.ShapeDtypeStruct((M, D), n_dtype)]
    if emit_h:
        out_specs = [rowblk, rowblk]
        out_shape = [jax.ShapeDtypeStruct((M, D), F32)] + out_shape
    y_specs = [pl.BlockSpec((tm, D // 2), functools.partial(lambda k, i: (k * (M // tm) + i, 0), k))
               for k in range(TOP_K)]
    return pl.pallas_call(
        functools.partial(_moe_combine_kernel, emit_h=emit_h),
        grid=(M // tm,),
        in_specs=[rowblk, pl.BlockSpec((tm, ROUTER_LANES), lambda i: (i, 0)),
                  pl.BlockSpec((1, D), lambda i: (0, 0))] + y_specs,
        out_specs=out_specs,
        out_shape=out_shape,
        compiler_params=pltpu.CompilerParams(dimension_semantics=("parallel",),
                                             vmem_limit_bytes=_vmem_limit(14 * tm * D * 4)),
        name="moe_combine",
    )(h, probs, g_next.reshape(1, D), *([y_tok] * TOP_K))


def kernel(x, norm_mix, w_in, gate_b, gm_ln_g, gm_ln_b, gm_ws, gm_bs, rw_mu, rw_w0, rw_w2, rw_a0, rw_a2,
           rw_g2, rw_kk, rw_ka, rw_rk, rw_ln_g, rw_ln_b, p_gm, p_rw, p_sb, w_o, norm_ffn, ffn_w1, ffn_w3,
           ffn_w2, router_w, router_b, moe_w1, moe_w3, moe_w2, norm_out):
    B, T, D = x.shape
    M = B * T
    bf = lambda a: a.astype(BF16)
    h = x.reshape(M, D)
    n = rmsnorm_call(h, norm_mix[0], BF16)
    moe_bf16 = {}
    for l in range(DEPTH):
        last = l == DEPTH - 1
        g_next = norm_out if last else norm_mix[l + 1]
        n_dtype = x.dtype if last else BF16
        w = w_in[l]
        j = l // 2
        uv = matmul_call(n, bf(w[:, GM_OFF:RW_OFF]), BF16, tn=GM_COLS, name="in_proj_gm")
        p = matmul_call(n, bf(w[:, RW_OFF:SB_OFF]), BF16, tn=RW_COLS, name="in_proj_rw")
        qkv = matmul_call(n, bf(w[:, SB_OFF:GATE_OFF]), BF16, tn=1024, name="in_proj_sb")
        gates = matmul_call(n, bf(w[:, GATE_OFF:]), BF16, tn=1024, bias=gate_b[l].reshape(-1),
                            name="in_proj_gate")
        y_gm = gm_call(uv, gm_ln_g[l], gm_ln_b[l], gm_ws[l], gm_bs[l])
        y_rw = rw_call(p.reshape(B, T, RW_COLS), B, T, rw_mu[l], rw_w0[l], rw_w2[l], rw_a0[l], rw_a2[l],
                       rw_g2[l], rw_kk[l], rw_ka[l], rw_rk[l], rw_ln_g[l], rw_ln_b[l]).reshape(M, RW_WIDTH)
        y_sb = sb_call(qkv.reshape(B, T, SB_COLS), B, T).reshape(M, SB_WIDTH)
        merge_args = (y_gm, y_rw, y_sb, gates, h, bf(p_gm[l]), bf(p_rw[l]), bf(p_sb[l]), bf(w_o[l]), norm_ffn[l])
        if l % 2 == 0:
            h, n2 = merge_call(*merge_args)
            ahead = []
            if not last:
                ahead = [(k, wt[(l + 1) // 2]) for k, wt in (("w1", moe_w1), ("w3", moe_w3), ("w2", moe_w2))
                         if can_ride(wt[(l + 1) // 2], ffn_steps(M, D_FF))]
            outs, casts = ffn_call(n2, h, bf(ffn_w1[j]), bf(ffn_w3[j]), bf(ffn_w2[j]), g_next, n_dtype,
                                   emit_h=not last, riders=[wt for _, wt in ahead])
            moe_bf16 = {k: c for (k, _), c in zip(ahead, casts)}
        else:
            h, ids, probs = merge_call(*merge_args, router=(router_w[j], router_b[j]))
            src_token, prev_out_row, tile_e, n_used = moe_plan(ids, M, MOE_TM)
            ew = {k: moe_bf16[k] if k in moe_bf16 else bf(wt[j])
                  for k, wt in (("w1", moe_w1), ("w3", moe_w3), ("w2", moe_w2))}
            y_tok = moe_call(h, norm_ffn[l], src_token, prev_out_row, tile_e, n_used, ew["w1"], ew["w3"],
                             ew["w2"], MOE_TM, MOE_TF)
            outs = moe_combine_call(h, y_tok, probs, g_next, n_dtype, emit_h=not last)
            moe_bf16 = {}
        if last:
            n = outs[0]
        else:
            h, n = outs
    return n.reshape(B, T, D)
```

```python
import functools

import jax
import jax.numpy as jnp
from jax import lax
from jax.experimental import pallas as pl
from jax.experimental.pallas import tpu as pltpu

F32 = jnp.float32
BF16 = jnp.bfloat16

D_MODEL = 2048
DEPTH = 2
HEAD_DIM = 64
GM_HEADS = 8
GM_WIDTH = GM_HEADS * HEAD_DIM
CHUNK = 128
RW_HEADS = 8
RW_WIDTH = RW_HEADS * HEAD_DIM
DECAY_LORA = 64
AAA_LORA = 64
GATE_LORA = 128
SB_HEADS = 16
SB_WIDTH = SB_HEADS * HEAD_DIM
N_BRANCH = 3
D_FF = 5632
N_EXPERTS = 8
TOP_K = 2
D_FF_EXPERT = D_FF // TOP_K
RMS_EPS = 1e-6
LN_EPS = 1e-5
RW_GN_EPS = 64e-5

GM_COLS = 2 * GM_WIDTH
RW_COLS = 3 * RW_WIDTH + DECAY_LORA + AAA_LORA + GATE_LORA
SB_COLS = 3 * SB_WIDTH
GATE_COLS = N_BRANCH * D_MODEL
GM_OFF = 0
RW_OFF = GM_OFF + GM_COLS
SB_OFF = RW_OFF + RW_COLS
GATE_OFF = SB_OFF + SB_COLS
N_IN = GATE_OFF + GATE_COLS

V7X_VMEM_BYTES = 64 * 1024 * 1024
LANES = 128
BF16_SUBLANES = 16
RW_CHUNK = 64
RW_TILE = 256
RW_GROUP = 4
SB_TILE = 256
LOG2_E = 1.4426950408889634
SB_EXP2_UNDERFLOW = -151.0
ROUTER_LANES = 128
MOE_TF = 256
MOE_TM = 528
MOE_COMBINE_TM = 256


def _vmem_limit(nbytes):
    return int(min(max(nbytes, 32 * 1024 * 1024), V7X_VMEM_BYTES - 6 * 1024 * 1024))


def _dot(a, b):
    return jnp.dot(a, b, preferred_element_type=F32)


def _dot_nt(a, b):
    return lax.dot_general(a, b, (((1,), (1,)), ((), ())), preferred_element_type=F32)


def _dot_tn(a, b):
    return lax.dot_general(a, b, (((0,), (0,)), ((), ())), preferred_element_type=F32)


def _dot_split(x, ones_bf16):
    hi = x.astype(BF16)
    lo = (x - hi.astype(F32)).astype(BF16)
    return _dot(hi, ones_bf16) + _dot(lo, ones_bf16)


def _rms(x, g):
    ms = jnp.mean(x * x, axis=-1, keepdims=True)
    return x * lax.rsqrt(ms + RMS_EPS) * g


def _rmsnorm_kernel(x_ref, g_ref, o_ref):
    o_ref[...] = _rms(x_ref[...], g_ref[...]).astype(o_ref.dtype)


def rmsnorm_call(x, g, out_dtype, tm=512):
    M, D = x.shape
    tm = min(tm, M)
    return pl.pallas_call(
        _rmsnorm_kernel,
        grid=(M // tm,),
        in_specs=[pl.BlockSpec((tm, D), lambda i: (i, 0)),
                  pl.BlockSpec((1, D), lambda i: (0, 0))],
        out_specs=pl.BlockSpec((tm, D), lambda i: (i, 0)),
        out_shape=jax.ShapeDtypeStruct((M, D), out_dtype),
        compiler_params=pltpu.CompilerParams(dimension_semantics=("parallel",)),
        name="rmsnorm",
    )(x, g.reshape(1, D))


def _mm_kernel(a_ref, w_ref, o_ref):
    o_ref[...] = _dot(a_ref[...], w_ref[...]).astype(o_ref.dtype)


def _mm_gate_kernel(a_ref, w_ref, b_ref, o_ref):
    o_ref[...] = jax.nn.sigmoid(_dot(a_ref[...], w_ref[...]) + b_ref[...]).astype(o_ref.dtype)


def matmul_call(a, w, out_dtype, tn, bias=None, tm=2048, name="matmul"):
    M, K = a.shape
    N = w.shape[1]
    tm = min(tm, M)
    in_specs = [pl.BlockSpec((tm, K), lambda i, j: (i, 0)),
                pl.BlockSpec((K, tn), lambda i, j: (0, j))]
    args = [a, w]
    kern = _mm_kernel
    if bias is not None:
        in_specs.append(pl.BlockSpec((1, tn), lambda i, j: (0, j)))
        args.append(bias.reshape(1, N))
        kern = _mm_gate_kernel
    osz = jnp.dtype(out_dtype).itemsize
    need = 2 * (tm * K * 2 + K * tn * 2 + tm * tn * osz) + 2 * tm * tn * 4
    return pl.pallas_call(
        kern,
        grid=(M // tm, N // tn),
        in_specs=in_specs,
        out_specs=pl.BlockSpec((tm, tn), lambda i, j: (i, j)),
        out_shape=jax.ShapeDtypeStruct((M, N), out_dtype),
        compiler_params=pltpu.CompilerParams(
            dimension_semantics=("parallel", "arbitrary"),
            vmem_limit_bytes=_vmem_limit(need + (8 << 20))),
        name=name,
    )(*args)


def _gm_kernel(uv_ref, lng_ref, lnb_ref, ws_ref, bias_ref, o_ref, *, n_chunks):
    ii = lax.broadcasted_iota(jnp.int32, (CHUNK, CHUNK), 0)
    jj = lax.broadcasted_iota(jnp.int32, (CHUNK, CHUNK), 1)
    causal = jj <= ii
    ws = [jnp.where(causal, ws_ref[h], 0.0).astype(BF16) for h in range(GM_HEADS)]
    for c in range(n_chunks):
        rows = slice(c * CHUNK, (c + 1) * CHUNK)
        u = jax.nn.gelu(uv_ref[rows, 0:GM_WIDTH].astype(F32))
        v = jax.nn.gelu(uv_ref[rows, GM_WIDTH:2 * GM_WIDTH].astype(F32))
        mu = jnp.mean(v, axis=-1, keepdims=True)
        vc = v - mu
        var = jnp.mean(vc * vc, axis=-1, keepdims=True)
        vn = (vc * lax.rsqrt(var + LN_EPS) * lng_ref[...] + lnb_ref[...]).astype(BF16)
        mixed = jnp.concatenate(
            [_dot(ws[h], vn[:, h * HEAD_DIM:(h + 1) * HEAD_DIM]) for h in range(GM_HEADS)], axis=-1)
        o_ref[rows, :] = (u * (mixed + bias_ref[...])).astype(o_ref.dtype)


def gm_call(uv, ln_g, ln_b, w_s, b_s, n_chunks=2):
    M = uv.shape[0]
    tt = CHUNK * n_chunks
    bias = jnp.repeat(b_s.T, HEAD_DIM, axis=1)
    return pl.pallas_call(
        functools.partial(_gm_kernel, n_chunks=n_chunks),
        grid=(M // tt,),
        in_specs=[pl.BlockSpec((tt, GM_COLS), lambda i: (i, 0)),
                  pl.BlockSpec((1, GM_WIDTH), lambda i: (0, 0)),
                  pl.BlockSpec((1, GM_WIDTH), lambda i: (0, 0)),
                  pl.BlockSpec((GM_HEADS, CHUNK, CHUNK), lambda i: (0, 0, 0)),
                  pl.BlockSpec((CHUNK, GM_WIDTH), lambda i: (0, 0))],
        out_specs=pl.BlockSpec((tt, GM_WIDTH), lambda i: (i, 0)),
        out_shape=jax.ShapeDtypeStruct((M, GM_WIDTH), BF16),
        compiler_params=pltpu.CompilerParams(dimension_semantics=("parallel",)),
        name="gmlp_gating",
    )(uv, ln_g.reshape(1, -1), ln_b.reshape(1, -1), w_s, bias)


def _rw_kernel(p_ref, prev_ref, mu_ref, w0_ref, w2_ref, a0_ref, a2_ref, g2_ref, kk_ref, ka_ref,
               rk_ref, lng_ref, lnb_ref, hsum_ref, o_ref, st_ref, y_ref):
    C = RW_CHUNK
    i = pl.program_id(1)

    @pl.when(i == 0)
    def _():
        st_ref[...] = jnp.zeros_like(st_ref)

    hsum = hsum_ref[...]
    gw = hsum.shape[0]

    def head_sums(x):
        return jnp.concatenate([_dot_split(x[:, q * gw:(q + 1) * gw], hsum)
                                for q in range(x.shape[1] // gw)], axis=1)

    p = p_ref[...].astype(F32)
    prev_row = prev_ref[BF16_SUBLANES - 1:BF16_SUBLANES, :].astype(F32)
    prev_row = jnp.where(i == 0, 0.0, prev_row)
    row = lax.broadcasted_iota(jnp.int32, p.shape, 0)
    p_prev = jnp.where(row == 0, prev_row, pltpu.roll(p, 1, axis=0))
    ps = p + (p_prev - p) * mu_ref[...]
    W = RW_WIDTH
    r = ps[:, 0:W]
    k = ps[:, W:2 * W]
    v = ps[:, 2 * W:3 * W]
    wl = ps[:, 3 * W:3 * W + DECAY_LORA]
    al = ps[:, 3 * W + DECAY_LORA:3 * W + DECAY_LORA + AAA_LORA]
    gl = ps[:, 3 * W + DECAY_LORA + AAA_LORA:RW_COLS]

    d = w0_ref[...] + _dot(jnp.tanh(wl).astype(BF16), w2_ref[...])
    log_w = -jnp.exp(-jax.nn.softplus(-d) - 0.5)
    a = jax.nn.sigmoid(a0_ref[...] + _dot(al.astype(BF16), a2_ref[...]))
    g = _dot(jax.nn.sigmoid(gl).astype(BF16), g2_ref[...])
    kk = k * kk_ref[...]
    kk = kk / jnp.maximum(jnp.sqrt(head_sums(kk * kk)), 1e-12)
    k = k * (1.0 + (a - 1.0) * ka_ref[...])
    bonus = head_sums(r * k * rk_ref[...]) * v

    HD = HEAD_DIM
    GW = RW_GROUP * HD
    ci = lax.broadcasted_iota(jnp.int32, (C, C), 0)
    cj = lax.broadcasted_iota(jnp.int32, (C, C), 1)
    tri_incl = (cj <= ci).astype(BF16)
    ti = lax.broadcasted_iota(jnp.int32, (C, GW), 0)
    tj = lax.broadcasted_iota(jnp.int32, (C, GW), 1) % HD
    strict = tj < ti
    incl = tj <= ti
    eye = ti == tj
    blk_masks = []
    s = 2
    while s < C:
        blk_masks.append((ti // (2 * s) == tj // (2 * s)) & (ti // s != tj // s))
        s *= 2
    pair_mask = (ti // 2 == tj // 2) & strict
    gi = lax.broadcasted_iota(jnp.int32, (GW, GW), 0)
    gj = lax.broadcasted_iota(jnp.int32, (GW, GW), 1)
    bd_mask = gi // HD == gj // HD
    bd_mask_b = jnp.where(bd_mask, 1.0, 0.0).astype(BF16)
    gw_eye = gi == gj

    def bd(x):
        xb = x.astype(BF16)
        return jnp.where(bd_mask_b > 0, jnp.concatenate([xb] * RW_GROUP, axis=0), 0)

    def bd_of_full(full):
        return jnp.where(bd_mask, full, 0.0)

    n_chunks = RW_TILE // C
    n_groups = RW_WIDTH // GW
    systems = [(c, q) for c in range(n_chunks) for q in range(n_groups)]

    pre = []
    for c in range(n_chunks):
        rows = slice(c * C, (c + 1) * C)
        lw = log_w[rows]
        cum = _cumsum_rows(lw, tri_incl)
        gam = jnp.exp(cum)
        gam_prev = jnp.exp(cum - lw)
        gam_inv = jnp.exp(-cum)
        g_end = gam[C - 1:C, :]
        bh_f = kk[rows] * a[rows] * gam_inv
        kh_f = k[rows] * gam_inv
        pre.append(dict(
            rh=(r[rows] * gam).astype(BF16), ah=(-kk[rows] * gam_prev).astype(BF16),
            bh=bh_f.astype(BF16), kh=kh_f.astype(BF16),
            bt=(bh_f * g_end).astype(BF16), kt=(kh_f * g_end).astype(BF16),
            v=v[rows].astype(BF16), g_end=g_end))

    def grp(c, q, name):
        return pre[c][name][:, q * GW:(q + 1) * GW]

    st = {}
    for (c, q) in systems:
        ar = jnp.concatenate([grp(c, q, "ah"), grp(c, q, "rh")], axis=0)
        pb = _dot_nt(ar, bd(grp(c, q, "bh")))
        pk = _dot_nt(ar, bd(grp(c, q, "kh")))
        lab = jnp.where(strict, pb[0:C], 0.0)
        st[(c, q)] = dict(
            lab=lab, mrb=jnp.where(incl, pb[C:2 * C], 0.0).astype(BF16),
            lm=jnp.concatenate([jnp.where(strict, pk[0:C], 0.0), jnp.where(incl, pk[C:2 * C], 0.0)],
                               axis=0).astype(BF16),
            tinv=jnp.where(eye, 1.0, jnp.where(pair_mask, lab, 0.0)))

    for bm in blk_masks:
        xs = {}
        for key in systems:
            d_ = st[key]
            xs[key] = _dot(d_["tinv"].astype(BF16), bd(jnp.where(bm, d_["lab"], 0.0))).astype(BF16)
        for key in systems:
            d_ = st[key]
            d_["tinv"] = d_["tinv"] + _dot(xs[key], bd(d_["tinv"]))

    for (c, q) in systems:
        d_ = st[(c, q)]
        lv = _dot(d_["lm"], bd(grp(c, q, "v")))
        tb = d_["tinv"].astype(BF16)
        ap = _dot(tb, bd(grp(c, q, "ah"))).astype(BF16)
        up = _dot(tb, bd(lv[0:C])).astype(BF16)
        d_["rp"] = (grp(c, q, "rh").astype(F32) + _dot(d_["mrb"], bd(ap))).astype(BF16)
        d_["ypp"] = _dot(d_["mrb"], bd(up)) + lv[C:2 * C]
        g_end = pre[c]["g_end"][:, q * GW:(q + 1) * GW]
        gt = _dot_tn(ap, grp(c, q, "bt"))
        d_["gt"] = jnp.where(gw_eye, gt + g_end, bd_of_full(gt)).astype(BF16)
        ht = bd_of_full(_dot_tn(up, grp(c, q, "bt")) + _dot_tn(grp(c, q, "v"), grp(c, q, "kt")))
        d_["ht"] = sum(ht[hh * HD:(hh + 1) * HD] for hh in range(1, RW_GROUP)) + ht[0:HD]

    for q in range(n_groups):
        s_cur = st_ref[q]
        for c in range(n_chunks):
            d_ = st[(c, q)]
            y_ref[c * C:(c + 1) * C, q * GW:(q + 1) * GW] = _dot_nt(d_["rp"], bd(s_cur)) + d_["ypp"]
            s_cur = _dot(s_cur.astype(BF16), d_["gt"]) + d_["ht"]
        st_ref[q] = s_cur

    y = y_ref[...]
    inv_n = 1.0 / HEAD_DIM
    m = head_sums(y) * inv_n
    yc = y - m
    var = head_sums(yc * yc) * inv_n
    yn = yc * lax.rsqrt(var + RW_GN_EPS) * lng_ref[...] + lnb_ref[...]
    o_ref[...] = ((yn + bonus) * g).astype(o_ref.dtype)


def _cumsum_rows(x, tri_incl_bf16):
    hi = x.astype(BF16)
    lo = (x - hi.astype(F32)).astype(BF16)
    return _dot(tri_incl_bf16, hi) + _dot(tri_incl_bf16, lo)


def rw_call(p, B, T, mu, w0, w2, a0, a2, g2, k_k, k_a, r_k, ln_g, ln_b):
    nt = T // RW_TILE
    sub = RW_TILE // BF16_SUBLANES
    hid = jnp.arange(RW_GROUP * HEAD_DIM) // HEAD_DIM
    hsum = (hid[:, None] == hid[None, :]).astype(BF16)
    row = lambda a: a.reshape(1, -1).astype(F32)
    vec = lambda n: pl.BlockSpec((1, n), lambda b, i: (0, 0))
    return pl.pallas_call(
        _rw_kernel,
        grid=(B, nt),
        in_specs=[pl.BlockSpec((None, RW_TILE, RW_COLS), lambda b, i: (b, i, 0)),
                  pl.BlockSpec((None, BF16_SUBLANES, RW_COLS),
                               lambda b, i: (b, jnp.maximum(i * sub - 1, 0), 0)),
                  vec(RW_COLS), vec(RW_WIDTH),
                  pl.BlockSpec((DECAY_LORA, RW_WIDTH), lambda b, i: (0, 0)),
                  vec(RW_WIDTH),
                  pl.BlockSpec((AAA_LORA, RW_WIDTH), lambda b, i: (0, 0)),
                  pl.BlockSpec((GATE_LORA, RW_WIDTH), lambda b, i: (0, 0)),
                  vec(RW_WIDTH), vec(RW_WIDTH), vec(RW_WIDTH), vec(RW_WIDTH), vec(RW_WIDTH),
                  pl.BlockSpec((RW_GROUP * HEAD_DIM, RW_GROUP * HEAD_DIM), lambda b, i: (0, 0))],
        out_specs=pl.BlockSpec((None, RW_TILE, RW_WIDTH), lambda b, i: (b, i, 0)),
        out_shape=jax.ShapeDtypeStruct((B, T, RW_WIDTH), BF16),
        scratch_shapes=[pltpu.VMEM((RW_HEADS // RW_GROUP, HEAD_DIM, RW_GROUP * HEAD_DIM), F32),
                        pltpu.VMEM((RW_TILE, RW_WIDTH), F32)],
        compiler_params=pltpu.CompilerParams(dimension_semantics=("parallel", "arbitrary")),
        name="rwkv7_time_mix",
    )(p, p, row(mu), row(w0), w2.astype(BF16), row(a0), a2.astype(BF16), g2.astype(BF16),
      row(k_k), row(k_a), row(r_k), row(ln_g), row(ln_b), hsum)


def _sb_scores(qs, ks, masks, neg_suffix):
    n = range(len(qs))
    zs = [_dot_nt(qs[s], ks[s]) * LOG2_E for s in n]
    sps = [jnp.maximum(z, 0.0) + jnp.log2(1.0 + jnp.exp2(-jnp.abs(z))) for z in zs]
    vis = [sp if masks[s] is None else jnp.where(masks[s], sp, 0.0) for s, sp in enumerate(sps)]
    his = [sp.astype(BF16) for sp in vis]
    los = [(vis[s] - his[s].astype(F32)).astype(BF16) for s in n]
    sufs = [_dot(his[s], neg_suffix[s]) + _dot(los[s], neg_suffix[s]) for s in n]
    rowsums = [-jnp.sum(sp, axis=-1, keepdims=True) for sp in vis]
    return zs, sps, sufs, rowsums


def _sb_weighted(scores, s, carry, mask, v):
    zs, sps, sufs, _ = scores
    att = jnp.exp2(zs[s] - sps[s] + (sufs[s] + carry))
    if mask is not None:
        att = jnp.where(mask, att, 0.0)
    return _dot(att.astype(BF16), v)


def _sb_kernel(q_ref, k_ref, v_ref, o_ref, acc_ref, carry_ref, *, T):
    tq = SB_TILE
    nq = T // tq
    th = tq // 2
    ti = lax.broadcasted_iota(jnp.int32, (tq, tq), 0)
    tj = lax.broadcasted_iota(jnp.int32, (tq, tq), 1)
    neg_full = jnp.where(ti > tj, -1.0, 0.0).astype(BF16)
    neg_half = neg_full[:th, :th]
    causal_top = (tj < ti)[:th, :th]
    causal_bot = (tj < ti)[th:, :]
    scale = HEAD_DIM ** -0.5
    heads = [slice(hh * HEAD_DIM, (hh + 1) * HEAD_DIM) for hh in range(LANES // HEAD_DIM)]
    nh = len(heads)

    def q_body(i, _):
        r0 = pl.multiple_of(i * tq, tq)
        has_prev = jnp.where(i > 0, 1.0, 0.0)
        p0 = pl.multiple_of(jnp.maximum(i - 1, 0) * tq, tq)
        qs = [q_ref[pl.ds(r0, tq), cs] * scale for cs in heads]
        kd = [k_ref[pl.ds(r0, tq), cs] for cs in heads]
        vd = [v_ref[pl.ds(r0, tq), cs] for cs in heads]
        kp = [k_ref[pl.ds(p0, tq), cs] for cs in heads]
        vp = [v_ref[pl.ds(p0, tq), cs] for cs in heads]
        scores = _sb_scores([q[:th] for q in qs] + [q[th:] for q in qs] + qs,
                            [k[:th] for k in kd] + kd + kp,
                            [causal_top] * nh + [causal_bot] * nh + [None] * nh,
                            [neg_half] * nh + [neg_full] * (2 * nh))
        rss = scores[3]
        live = jnp.float32(-jnp.inf)
        for hh, cs in enumerate(heads):
            out = jnp.concatenate([_sb_weighted(scores, hh, 0.0, causal_top, vd[hh][:th]),
                                   _sb_weighted(scores, nh + hh, 0.0, causal_bot, vd[hh])], axis=0)
            rs_diag = jnp.concatenate([rss[hh], rss[nh + hh]], axis=0)
            out_prev = _sb_weighted(scores, 2 * nh + hh, rs_diag, None, vp[hh])
            acc_ref[:, cs] = out + has_prev * out_prev
            carry = rs_diag + has_prev * rss[2 * nh + hh]
            carry_ref[hh] = carry
            live = jnp.maximum(live, jnp.max(carry))

        def k_cond(state):
            jr, live = state
            return (jr < i) & (live > SB_EXP2_UNDERFLOW)

        def k_body(state):
            jr, _ = state
            c0 = pl.multiple_of((i - 1 - jr) * tq, tq)
            kt = [k_ref[pl.ds(c0, tq), cs] for cs in heads]
            vt = [v_ref[pl.ds(c0, tq), cs] for cs in heads]
            scores = _sb_scores(qs, kt, [None] * nh, [neg_full] * nh)
            live = jnp.float32(-jnp.inf)
            for hh, cs in enumerate(heads):
                carry = carry_ref[hh]
                acc_ref[:, cs] += _sb_weighted(scores, hh, carry, None, vt[hh])
                carry = carry + scores[3][hh]
                carry_ref[hh] = carry
                live = jnp.maximum(live, jnp.max(carry))
            return jr + 1, live

        lax.while_loop(k_cond, k_body, (jnp.int32(1), live))
        o_ref[pl.ds(r0, tq), :] = acc_ref[...].astype(o_ref.dtype)
        return 0

    lax.fori_loop(0, nq, q_body, 0)


def sb_call(qkv, B, T):
    npair = SB_WIDTH // LANES
    blk = lambda off: pl.BlockSpec((None, T, LANES), lambda b, hp: (b, 0, off + hp))
    return pl.pallas_call(
        functools.partial(_sb_kernel, T=T),
        grid=(B, npair),
        in_specs=[blk(0), blk(npair), blk(2 * npair)],
        out_specs=pl.BlockSpec((None, T, LANES), lambda b, hp: (b, 0, hp)),
        out_shape=jax.ShapeDtypeStruct((B, T, SB_WIDTH), BF16),
        scratch_shapes=[pltpu.VMEM((SB_TILE, LANES), F32),
                        pltpu.VMEM((LANES // HEAD_DIM, SB_TILE, 1), F32)],
        compiler_params=pltpu.CompilerParams(dimension_semantics=("parallel", "parallel")),
        name="stick_breaking_attention",
    )(qkv, qkv, qkv)


def _route(n, w_ref, b_ref, id_ref, p_ref):
    nh = n.astype(BF16)
    nl = (n - nh.astype(F32)).astype(BF16)
    a = _dot(nh, w_ref[...])
    b = _dot(nl, w_ref[...])
    L = ROUTER_LANES
    logits = a[:, :L] + (a[:, L:] + b[:, :L] + b[:, L:]) + b_ref[...]
    lane = lax.broadcasted_iota(jnp.int32, logits.shape, 1)
    neg = jnp.float32(-jnp.inf)
    logits = jnp.where(lane < N_EXPERTS, logits, neg)
    m1 = jnp.max(logits, axis=-1, keepdims=True)
    i1 = jnp.min(jnp.where(logits == m1, lane, ROUTER_LANES), axis=-1, keepdims=True)
    rest = jnp.where(lane == i1, neg, logits)
    m2 = jnp.max(rest, axis=-1, keepdims=True)
    i2 = jnp.min(jnp.where(rest == m2, lane, ROUTER_LANES), axis=-1, keepdims=True)
    e2 = jnp.exp(m2 - m1)
    p1 = 1.0 / (1.0 + e2)
    p2 = e2 / (1.0 + e2)
    id_ref[...] = jnp.where(lane == 0, i1, jnp.where(lane == 1, i2, 0))
    p_ref[...] = jnp.where(lane == 0, p1, jnp.where(lane == 1, p2, 0.0))


def _merge_kernel(ygm_ref, yrw_ref, ysb_ref, gate_ref, h_ref, pgm_ref, prw_ref, psb_ref, wo_ref,
                  g_ref, *refs, route):
    D = D_MODEL
    merged = gate_ref[:, 0:D].astype(F32) * _dot(ygm_ref[...], pgm_ref[...])
    merged += gate_ref[:, D:2 * D].astype(F32) * _dot(yrw_ref[...], prw_ref[...])
    merged += gate_ref[:, 2 * D:3 * D].astype(F32) * _dot(ysb_ref[...], psb_ref[...])
    hn = h_ref[...] + _dot(merged.astype(BF16), wo_ref[...])
    n = _rms(hn, g_ref[...])
    if route:
        w_ref, b_ref, ho_ref, id_ref, p_ref = refs
        _route(n, w_ref, b_ref, id_ref, p_ref)
    else:
        ho_ref, no_ref = refs
        no_ref[...] = n.astype(no_ref.dtype)
    ho_ref[...] = hn


def _resident(shape):
    return pl.BlockSpec(shape, lambda i: (0,) * len(shape), pipeline_mode=pl.Buffered(1))


def merge_call(y_gm, y_rw, y_sb, gates, h, p_gm, p_rw, p_sb, w_o, g_next, router=None, tm=256):
    M, D = h.shape
    tm = min(tm, M)
    rowblk = lambda n: pl.BlockSpec((tm, n), lambda i: (i, 0))
    wbytes = 2 * (GM_WIDTH + RW_WIDTH + SB_WIDTH + D) * D
    need = wbytes + 2 * tm * (2 * (GM_WIDTH + RW_WIDTH + SB_WIDTH + GATE_COLS) + 4 * D + 4 * D + 2 * D) \
        + 6 * tm * D * 4
    in_specs = [rowblk(GM_WIDTH), rowblk(RW_WIDTH), rowblk(SB_WIDTH), rowblk(GATE_COLS), rowblk(D),
                _resident((GM_WIDTH, D)), _resident((RW_WIDTH, D)), _resident((SB_WIDTH, D)),
                _resident((D, D)), _resident((1, D))]
    args = [y_gm, y_rw, y_sb, gates, h, p_gm, p_rw, p_sb, w_o, g_next.reshape(1, D)]
    if router is None:
        out_specs = [rowblk(D), rowblk(D)]
        out_shape = [jax.ShapeDtypeStruct((M, D), F32), jax.ShapeDtypeStruct((M, D), BF16)]
    else:
        router_w, router_b = router
        wpad = jnp.zeros((D, ROUTER_LANES), F32).at[:, :N_EXPERTS].set(router_w)
        w_hi = wpad.astype(BF16)
        w_lo = (wpad - w_hi.astype(F32)).astype(BF16)
        bpad = jnp.zeros((1, ROUTER_LANES), F32).at[0, :N_EXPERTS].set(router_b)
        in_specs += [_resident((D, 2 * ROUTER_LANES)), _resident((1, ROUTER_LANES))]
        args += [jnp.concatenate([w_hi, w_lo], axis=1), bpad]
        out_specs = [rowblk(D), rowblk(ROUTER_LANES), rowblk(ROUTER_LANES)]
        out_shape = [jax.ShapeDtypeStruct((M, D), F32), jax.ShapeDtypeStruct((M, ROUTER_LANES), jnp.int32),
                     jax.ShapeDtypeStruct((M, ROUTER_LANES), F32)]
    return pl.pallas_call(
        functools.partial(_merge_kernel, route=router is not None),
        grid=(M // tm,),
        in_specs=in_specs,
        out_specs=out_specs,
        out_shape=out_shape,
        compiler_params=pltpu.CompilerParams(dimension_semantics=("parallel",),
                                             vmem_limit_bytes=_vmem_limit(need)),
        name="merge_out_proj",
    )(*args)


def _ffn_kernel(n_ref, h_ref, w1_ref, w3_ref, w2_ref, g_ref, *refs, emit_h, n_cast):
    cast_in, refs = refs[:n_cast], refs[n_cast:]
    if emit_h:
        ho_ref, no_ref = refs[:2]
        refs = refs[2:]
    else:
        no_ref = refs[0]
        refs = refs[1:]
    cast_out, acc_ref = refs[:n_cast], refs[n_cast]
    f = pl.program_id(1)

    @pl.when(f == 0)
    def _():
        acc_ref[...] = jnp.zeros_like(acc_ref)

    n = n_ref[...]
    a = _dot(n, w1_ref[...])
    b = _dot(n, w3_ref[...])
    for src, dst in zip(cast_in, cast_out):
        dst[...] = src[...].astype(dst.dtype)
    act = (jax.nn.silu(a) * b).astype(BF16)
    acc_ref[...] += _dot(act, w2_ref[...])

    @pl.when(f == pl.num_programs(1) - 1)
    def _():
        hn = h_ref[...] + acc_ref[...]
        if emit_h:
            ho_ref[...] = hn
        no_ref[...] = _rms(hn, g_ref[...]).astype(no_ref.dtype)


def ffn_steps(M, F, tm=512, tf=512):
    return (M // min(tm, M)) * (F // tf)


RIDER_SLAB_ROWS = 64


def can_ride(w, steps):
    return w.shape[1] % RIDER_SLAB_ROWS == 0 and w.shape[0] * (w.shape[1] // RIDER_SLAB_ROWS) <= steps


def ffn_call(n, h, w1, w3, w2, g_next, n_dtype, emit_h, riders=(), tm=512, tf=512):
    M, D = h.shape
    F = w1.shape[1]
    tm = min(tm, M)
    nf = F // tf
    steps = (M // tm) * nf
    rowblk = pl.BlockSpec((tm, D), lambda i, f: (i, 0))
    out_specs = [rowblk]
    out_shape = [jax.ShapeDtypeStruct((M, D), n_dtype)]
    if emit_h:
        out_specs = [rowblk, rowblk]
        out_shape = [jax.ShapeDtypeStruct((M, D), F32)] + out_shape
    n_main = len(out_shape)

    def slab_spec(w):
        E, R, C = w.shape
        per = R // RIDER_SLAB_ROWS
        n_slabs = E * per
        hold = steps // n_slabs
        slab = lambda i, f: jnp.minimum((i * nf + f) // hold, n_slabs - 1)
        return pl.BlockSpec((None, RIDER_SLAB_ROWS, C), lambda i, f: (slab(i, f) // per, slab(i, f) % per, 0))

    ride_specs = [slab_spec(w) for w in riders]
    need = 2 * tm * D * (2 + 4 + 4 * n_main) + tm * D * 4 + 2 * 3 * D * tf * 2 + 4 * tm * tf * 4 \
        + sum(2 * RIDER_SLAB_ROWS * w.shape[2] * 6 for w in riders)
    outs = pl.pallas_call(
        functools.partial(_ffn_kernel, emit_h=emit_h, n_cast=len(riders)),
        grid=(M // tm, nf),
        in_specs=[rowblk, rowblk,
                  pl.BlockSpec((D, tf), lambda i, f: (0, f)),
                  pl.BlockSpec((D, tf), lambda i, f: (0, f)),
                  pl.BlockSpec((tf, D), lambda i, f: (f, 0)),
                  pl.BlockSpec((1, D), lambda i, f: (0, 0))] + ride_specs,
        out_specs=out_specs + ride_specs,
        out_shape=out_shape + [jax.ShapeDtypeStruct(w.shape, BF16) for w in riders],
        scratch_shapes=[pltpu.VMEM((tm, D), F32)],
        compiler_params=pltpu.CompilerParams(dimension_semantics=("arbitrary", "arbitrary"),
                                             vmem_limit_bytes=_vmem_limit(need + (8 << 20))),
        name="swiglu_ffn",
    )(n, h, w1, w3, w2, g_next.reshape(1, D), *riders)
    return outs[:n_main], outs[n_main:]


def moe_plan(ids, M, tm):
    E = N_EXPERTS
    n_tiles = -(-(TOP_K * M) // tm) + E + 2
    e_flat = ids[:, :TOP_K].reshape(-1)
    onehot = (e_flat[:, None] == jnp.arange(E, dtype=jnp.int32)[None, :]).astype(jnp.int32)
    csum = jnp.cumsum(onehot, axis=0)
    rank = jnp.sum((csum - onehot) * onehot, axis=1)
    counts = csum[-1]
    tiles_per_e = (counts + tm - 1) // tm
    tile_end = jnp.cumsum(tiles_per_e)
    offs = (tile_end - tiles_per_e) * tm
    dest = offs[e_flat] + rank
    pair = jnp.full((n_tiles * tm,), -1, jnp.int32).at[dest].set(
        jnp.arange(TOP_K * M, dtype=jnp.int32), unique_indices=True)
    spare = TOP_K * M + jnp.arange(n_tiles * tm, dtype=jnp.int32) % tm
    src_token = jnp.where(pair >= 0, pair // TOP_K, 0)
    out_row = jnp.where(pair >= 0, (pair % TOP_K) * M + pair // TOP_K, spare)
    prev_out_row = jnp.concatenate([spare[:tm], out_row[:-tm]])
    n_used = tile_end[-1]
    t = jnp.arange(n_tiles, dtype=jnp.int32)
    tile_e = jnp.searchsorted(tile_end, jnp.minimum(t, n_used - 1), side="right").astype(jnp.int32)
    return (src_token.reshape(n_tiles, 1, tm), prev_out_row.reshape(n_tiles, 1, tm), tile_e,
            n_used.astype(jnp.int32).reshape(1))


def _bf16_bits(x):
    u = lax.bitcast_convert_type(x, jnp.uint32)
    return (u + jnp.uint32(0x7FFF) + ((u >> 16) & jnp.uint32(1))) & jnp.uint32(0xFFFF0000)


def _pack_bf16_pairs(x):
    half = x.shape[1] // 2
    return _bf16_bits(x[:, half:]) | (_bf16_bits(x[:, :half]) >> 16)


def _unpack_bf16_pairs(w):
    lo = lax.bitcast_convert_type(w << 16, F32)
    hi = lax.bitcast_convert_type(w & jnp.uint32(0xFFFF0000), F32)
    return jnp.concatenate([lo, hi], axis=1)


def _moe_kernel(tile_e_ref, n_used_ref, src_ref, nxt_ref, dst_ref, h_hbm, g_ref, w1_ref, w3_ref, w2_ref,
                y_hbm, xbuf, xb, acc, ostage, gsem, ssem):
    t = pl.program_id(0)
    f = pl.program_id(1)
    nf = pl.num_programs(1)
    tm = xb.shape[0]
    per_step = tm // nf
    n_used = n_used_ref[0]
    used = t < n_used
    cur = t % 2

    def fetch(idx_ref, r, slot):
        return pltpu.make_async_copy(h_hbm.at[pl.ds(idx_ref[0, r], 1)], xbuf.at[slot, pl.ds(r, 1)],
                                     gsem.at[slot])

    def put(r):
        return pltpu.make_async_copy(ostage.at[1 - cur, pl.ds(r, 1)], y_hbm.at[pl.ds(dst_ref[0, r], 1)],
                                     ssem.at[1 - cur])

    @pl.when((f == 0) & (t == 0))
    def _():
        def issue(r, c):
            fetch(src_ref, r, 0).start()
            return c
        lax.fori_loop(0, tm, issue, 0)
        ostage[1] = jnp.zeros((tm, ostage.shape[2]), ostage.dtype)

    @pl.when((f == 0) & (t <= n_used))
    def _():
        pltpu.make_async_copy(h_hbm.at[pl.ds(0, tm)], xbuf.at[cur], gsem.at[cur]).wait()

    @pl.when((f == 0) & (t >= 1) & (t - 1 <= n_used))
    def _():
        pltpu.make_async_copy(ostage.at[cur], y_hbm.at[pl.ds(0, tm)], ssem.at[cur]).wait()

    @pl.when((f == 0) & used)
    def _():
        xb[...] = _rms(xbuf[cur], g_ref[...]).astype(BF16)
        acc[...] = jnp.zeros_like(acc)

    @pl.when(used)
    def _():
        for j in range(per_step):
            fetch(nxt_ref, f * per_step + j, 1 - cur).start()
            put(f * per_step + j).start()
        x = xb[...]
        act = jax.nn.silu(_dot(x, w1_ref[...])) * _dot(x, w3_ref[...])
        acc[...] += _dot(act.astype(BF16), w2_ref[...])

    @pl.when(used & (f == nf - 1))
    def _():
        ostage[cur] = _pack_bf16_pairs(acc[...])

    @pl.when((f == 0) & (t == n_used))
    def _():
        def issue(r, c):
            put(r).start()
            return c
        lax.fori_loop(0, tm, issue, 0)


def moe_call(h, g, src_token, prev_out_row, tile_e, n_used, w1, w3, w2, tm, tf):
    M, D = h.shape
    n_tiles = src_token.shape[0]
    E, _, Fe = w1.shape
    nf = Fe // tf
    assert tm % nf == 0 and Fe % tf == 0

    def wmap(t, f, tile_e_ref, n_used_ref):
        ff = jnp.where(t < n_used_ref[0], f, nf - 1)
        return tile_e_ref[t], ff

    smem_tile = lambda fn: pl.BlockSpec((None, 1, tm), fn, memory_space=pltpu.SMEM)
    grid_spec = pltpu.PrefetchScalarGridSpec(
        num_scalar_prefetch=2,
        grid=(n_tiles, nf),
        in_specs=[smem_tile(lambda t, f, te, nu: (t, 0, 0)),
                  smem_tile(lambda t, f, te, nu: (jnp.minimum(t + 1, n_tiles - 1), 0, 0)),
                  smem_tile(lambda t, f, te, nu: (t, 0, 0)),
                  pl.BlockSpec(memory_space=pl.ANY),
                  pl.BlockSpec((1, D), lambda t, f, te, nu: (0, 0)),
                  pl.BlockSpec((None, D, tf), lambda t, f, te, nu: (wmap(t, f, te, nu)[0], 0,
                                                                     wmap(t, f, te, nu)[1])),
                  pl.BlockSpec((None, D, tf), lambda t, f, te, nu: (wmap(t, f, te, nu)[0], 0,
                                                                     wmap(t, f, te, nu)[1])),
                  pl.BlockSpec((None, tf, D), lambda t, f, te, nu: wmap(t, f, te, nu) + (0,))],
        out_specs=pl.BlockSpec(memory_space=pl.ANY),
        scratch_shapes=[pltpu.VMEM((2, tm, D), F32), pltpu.VMEM((tm, D), BF16), pltpu.VMEM((tm, D), F32),
                        pltpu.VMEM((2, tm, D // 2), jnp.uint32),
                        pltpu.SemaphoreType.DMA((2,)), pltpu.SemaphoreType.DMA((2,))])
    need = tm * D * (8 + 2 + 4 + 4) + 2 * 3 * D * tf * 2 + 4 * tm * tf * 4
    return pl.pallas_call(
        _moe_kernel,
        grid_spec=grid_spec,
        out_shape=jax.ShapeDtypeStruct((TOP_K * M + tm, D // 2), jnp.uint32),
        compiler_params=pltpu.CompilerParams(dimension_semantics=("arbitrary", "arbitrary"),
                                             vmem_limit_bytes=_vmem_limit(need + (8 << 20))),
        name="moe_experts",
    )(tile_e, n_used, src_token, src_token, prev_out_row, h, g.reshape(1, D), w1, w3, w2)


def _moe_combine_kernel(h_ref, p_ref, g_ref, *refs, emit_h):
    y_refs, out_refs = refs[:TOP_K], refs[TOP_K:]
    hn = h_ref[...]
    for k in range(TOP_K):
        hn = hn + p_ref[:, k:k + 1] * _unpack_bf16_pairs(y_refs[k][...])
    if emit_h:
        out_refs[0][...] = hn
    out_refs[-1][...] = _rms(hn, g_ref[...]).astype(out_refs[-1].dtype)


def moe_combine_call(h, y_tok, probs, g_next, n_dtype, emit_h, tm=MOE_COMBINE_TM):
    M, D = h.shape
    tm = min(tm, M)
    rowblk = pl.BlockSpec((tm, D), lambda i: (i, 0))
    out_specs = [rowblk]
    out_shape = [jax.ShapeDtypeStruct((M, D), n_dtype)]
    if emit_h:
        out_specs = [rowblk, rowblk]
        out_shape = [jax.ShapeDtypeStruct((M, D), F32)] + out_shape
    y_specs = [pl.BlockSpec((tm, D // 2), functools.partial(lambda k, i: (k * (M // tm) + i, 0), k))
               for k in range(TOP_K)]
    return pl.pallas_call(
        functools.partial(_moe_combine_kernel, emit_h=emit_h),
        grid=(M // tm,),
        in_specs=[rowblk, pl.BlockSpec((tm, ROUTER_LANES), lambda i: (i, 0)),
                  pl.BlockSpec((1, D), lambda i: (0, 0))] + y_specs,
        out_specs=out_specs,
        out_shape=out_shape,
        compiler_params=pltpu.CompilerParams(dimension_semantics=("parallel",),
                                             vmem_limit_bytes=_vmem_limit(14 * tm * D * 4)),
        name="moe_combine",
    )(h, probs, g_next.reshape(1, D), *([y_tok] * TOP_K))


def kernel(x, norm_mix, w_in, gate_b, gm_ln_g, gm_ln_b, gm_ws, gm_bs, rw_mu, rw_w0, rw_w2, rw_a0, rw_a2,
           rw_g2, rw_kk, rw_ka, rw_rk, rw_ln_g, rw_ln_b, p_gm, p_rw, p_sb, w_o, norm_ffn, ffn_w1, ffn_w3,
           ffn_w2, router_w, router_b, moe_w1, moe_w3, moe_w2, norm_out):
    B, T, D = x.shape
    M = B * T
    bf = lambda a: a.astype(BF16)
    h = x.reshape(M, D)
    n = rmsnorm_call(h, norm_mix[0], BF16)
    moe_bf16 = {}
    for l in range(DEPTH):
        last = l == DEPTH - 1
        g_next = norm_out if last else norm_mix[l + 1]
        n_dtype = x.dtype if last else BF16
        w = w_in[l]
        j = l // 2
        uv = matmul_call(n, bf(w[:, GM_OFF:RW_OFF]), BF16, tn=GM_COLS, name="in_proj_gm")
        p = matmul_call(n, bf(w[:, RW_OFF:SB_OFF]), BF16, tn=RW_COLS, name="in_proj_rw")
        qkv = matmul_call(n, bf(w[:, SB_OFF:GATE_OFF]), BF16, tn=1024, name="in_proj_sb")
        gates = matmul_call(n, bf(w[:, GATE_OFF:]), BF16, tn=1024, bias=gate_b[l].reshape(-1),
                            name="in_proj_gate")
        y_gm = gm_call(uv, gm_ln_g[l], gm_ln_b[l], gm_ws[l], gm_bs[l])
        y_rw = rw_call(p.reshape(B, T, RW_COLS), B, T, rw_mu[l], rw_w0[l], rw_w2[l], rw_a0[l], rw_a2[l],
                       rw_g2[l], rw_kk[l], rw_ka[l], rw_rk[l], rw_ln_g[l], rw_ln_b[l]).reshape(M, RW_WIDTH)
        y_sb = sb_call(qkv.reshape(B, T, SB_COLS), B, T).reshape(M, SB_WIDTH)
        merge_args = (y_gm, y_rw, y_sb, gates, h, bf(p_gm[l]), bf(p_rw[l]), bf(p_sb[l]), bf(w_o[l]), norm_ffn[l])
        if l % 2 == 0:
            h, n2 = merge_call(*merge_args)
            ahead = []
            if not last:
                ahead = [(k, wt[(l + 1) // 2]) for k, wt in (("w1", moe_w1), ("w3", moe_w3), ("w2", moe_w2))
                         if can_ride(wt[(l + 1) // 2], ffn_steps(M, D_FF))]
            outs, casts = ffn_call(n2, h, bf(ffn_w1[j]), bf(ffn_w3[j]), bf(ffn_w2[j]), g_next, n_dtype,
                                   emit_h=not last, riders=[wt for _, wt in ahead])
            moe_bf16 = {k: c for (k, _), c in zip(ahead, casts)}
        else:
            h, ids, probs = merge_call(*merge_args, router=(router_w[j], router_b[j]))
            src_token, prev_out_row, tile_e, n_used = moe_plan(ids, M, MOE_TM)
            ew = {k: moe_bf16[k] if k in moe_bf16 else bf(wt[j])
                  for k, wt in (("w1", moe_w1), ("w3", moe_w3), ("w2", moe_w2))}
            y_tok = moe_call(h, norm_ffn[l], src_token, prev_out_row, tile_e, n_used, ew["w1"], ew["w3"],
                             ew["w2"], MOE_TM, MOE_TF)
            outs = moe_combine_call(h, y_tok, probs, g_next, n_dtype, emit_h=not last)
            moe_bf16 = {}
        if last:
            n = outs[0]
        else:
            h, n = outs
    return n.reshape(B, T, D)
```

```python
import functools

import jax
import jax.numpy as jnp
from jax import lax
from jax.experimental import pallas as pl
from jax.experimental.pallas import tpu as pltpu

F32 = jnp.float32
BF16 = jnp.bfloat16

D_MODEL = 2048
DEPTH = 2
HEAD_DIM = 64
GM_HEADS = 8
GM_WIDTH = GM_HEADS * HEAD_DIM
CHUNK = 128
RW_HEADS = 8
RW_WIDTH = RW_HEADS * HEAD_DIM
DECAY_LORA = 64
AAA_LORA = 64
GATE_LORA = 128
SB_HEADS = 16
SB_WIDTH = SB_HEADS * HEAD_DIM
N_BRANCH = 3
D_FF = 5632
N_EXPERTS = 8
TOP_K = 2
D_FF_EXPERT = D_FF // TOP_K
RMS_EPS = 1e-6
LN_EPS = 1e-5
RW_GN_EPS = 64e-5

GM_COLS = 2 * GM_WIDTH
RW_COLS = 3 * RW_WIDTH + DECAY_LORA + AAA_LORA + GATE_LORA
SB_COLS = 3 * SB_WIDTH
GATE_COLS = N_BRANCH * D_MODEL
GM_OFF = 0
RW_OFF = GM_OFF + GM_COLS
SB_OFF = RW_OFF + RW_COLS
GATE_OFF = SB_OFF + SB_COLS
N_IN = GATE_OFF + GATE_COLS

V7X_VMEM_BYTES = 64 * 1024 * 1024
LANES = 128
BF16_SUBLANES = 16
RW_CHUNK = 64
RW_TILE = 256
RW_GROUP = 4
SB_TILE = 256
LOG2_E = 1.4426950408889634
SB_EXP2_UNDERFLOW = -151.0
ROUTER_LANES = 128
MOE_TF = 256
MOE_TM = 528
MOE_COMBINE_TM = 256


def _vmem_limit(nbytes):
    return int(min(max(nbytes, 32 * 1024 * 1024), V7X_VMEM_BYTES - 6 * 1024 * 1024))


def _dot(a, b):
    return jnp.dot(a, b, preferred_element_type=F32)


def _dot_nt(a, b):
    return lax.dot_general(a, b, (((1,), (1,)), ((), ())), preferred_element_type=F32)


def _dot_tn(a, b):
    return lax.dot_general(a, b, (((0,), (0,)), ((), ())), preferred_element_type=F32)


def _dot_split(x, ones_bf16):
    hi = x.astype(BF16)
    lo = (x - hi.astype(F32)).astype(BF16)
    return _dot(hi, ones_bf16) + _dot(lo, ones_bf16)


def _rms(x, g):
    ms = jnp.mean(x * x, axis=-1, keepdims=True)
    return x * lax.rsqrt(ms + RMS_EPS) * g


def _rmsnorm_kernel(x_ref, g_ref, o_ref):
    o_ref[...] = _rms(x_ref[...], g_ref[...]).astype(o_ref.dtype)


def rmsnorm_call(x, g, out_dtype, tm=512):
    M, D = x.shape
    tm = min(tm, M)
    return pl.pallas_call(
        _rmsnorm_kernel,
        grid=(M // tm,),
        in_specs=[pl.BlockSpec((tm, D), lambda i: (i, 0)),
                  pl.BlockSpec((1, D), lambda i: (0, 0))],
        out_specs=pl.BlockSpec((tm, D), lambda i: (i, 0)),
        out_shape=jax.ShapeDtypeStruct((M, D), out_dtype),
        compiler_params=pltpu.CompilerParams(dimension_semantics=("parallel",)),
        name="rmsnorm",
    )(x, g.reshape(1, D))


def _mm_kernel(a_ref, w_ref, o_ref):
    o_ref[...] = _dot(a_ref[...], w_ref[...]).astype(o_ref.dtype)


def _mm_gate_kernel(a_ref, w_ref, b_ref, o_ref):
    o_ref[...] = jax.nn.sigmoid(_dot(a_ref[...], w_ref[...]) + b_ref[...]).astype(o_ref.dtype)


def matmul_call(a, w, out_dtype, tn, bias=None, tm=2048, name="matmul"):
    M, K = a.shape
    N = w.shape[1]
    tm = min(tm, M)
    in_specs = [pl.BlockSpec((tm, K), lambda i, j: (i, 0)),
                pl.BlockSpec((K, tn), lambda i, j: (0, j))]
    args = [a, w]
    kern = _mm_kernel
    if bias is not None:
        in_specs.append(pl.BlockSpec((1, tn), lambda i, j: (0, j)))
        args.append(bias.reshape(1, N))
        kern = _mm_gate_kernel
    osz = jnp.dtype(out_dtype).itemsize
    need = 2 * (tm * K * 2 + K * tn * 2 + tm * tn * osz) + 2 * tm * tn * 4
    return pl.pallas_call(
        kern,
        grid=(M // tm, N // tn),
        in_specs=in_specs,
        out_specs=pl.BlockSpec((tm, tn), lambda i, j: (i, j)),
        out_shape=jax.ShapeDtypeStruct((M, N), out_dtype),
        compiler_params=pltpu.CompilerParams(
            dimension_semantics=("parallel", "arbitrary"),
            vmem_limit_bytes=_vmem_limit(need + (8 << 20))),
        name=name,
    )(*args)


def _gm_kernel(uv_ref, lng_ref, lnb_ref, ws_ref, bias_ref, o_ref, *, n_chunks):
    ii = lax.broadcasted_iota(jnp.int32, (CHUNK, CHUNK), 0)
    jj = lax.broadcasted_iota(jnp.int32, (CHUNK, CHUNK), 1)
    causal = jj <= ii
    ws = [jnp.where(causal, ws_ref[h], 0.0).astype(BF16) for h in range(GM_HEADS)]
    for c in range(n_chunks):
        rows = slice(c * CHUNK, (c + 1) * CHUNK)
        u = jax.nn.gelu(uv_ref[rows, 0:GM_WIDTH].astype(F32))
        v = jax.nn.gelu(uv_ref[rows, GM_WIDTH:2 * GM_WIDTH].astype(F32))
        mu = jnp.mean(v, axis=-1, keepdims=True)
        vc = v - mu
        var = jnp.mean(vc * vc, axis=-1, keepdims=True)
        vn = (vc * lax.rsqrt(var + LN_EPS) * lng_ref[...] + lnb_ref[...]).astype(BF16)
        mixed = jnp.concatenate(
            [_dot(ws[h], vn[:, h * HEAD_DIM:(h + 1) * HEAD_DIM]) for h in range(GM_HEADS)], axis=-1)
        o_ref[rows, :] = (u * (mixed + bias_ref[...])).astype(o_ref.dtype)


def gm_call(uv, ln_g, ln_b, w_s, b_s, n_chunks=2):
    M = uv.shape[0]
    tt = CHUNK * n_chunks
    bias = jnp.repeat(b_s.T, HEAD_DIM, axis=1)
    return pl.pallas_call(
        functools.partial(_gm_kernel, n_chunks=n_chunks),
        grid=(M // tt,),
        in_specs=[pl.BlockSpec((tt, GM_COLS), lambda i: (i, 0)),
                  pl.BlockSpec((1, GM_WIDTH), lambda i: (0, 0)),
                  pl.BlockSpec((1, GM_WIDTH), lambda i: (0, 0)),
                  pl.BlockSpec((GM_HEADS, CHUNK, CHUNK), lambda i: (0, 0, 0)),
                  pl.BlockSpec((CHUNK, GM_WIDTH), lambda i: (0, 0))],
        out_specs=pl.BlockSpec((tt, GM_WIDTH), lambda i: (i, 0)),
        out_shape=jax.ShapeDtypeStruct((M, GM_WIDTH), BF16),
        compiler_params=pltpu.CompilerParams(dimension_semantics=("parallel",)),
        name="gmlp_gating",
    )(uv, ln_g.reshape(1, -1), ln_b.reshape(1, -1), w_s, bias)


def _rw_kernel(p_ref, prev_ref, mu_ref, w0_ref, w2_ref, a0_ref, a2_ref, g2_ref, kk_ref, ka_ref,
               rk_ref, lng_ref, lnb_ref, hsum_ref, o_ref, st_ref, y_ref):
    C = RW_CHUNK
    i = pl.program_id(1)

    @pl.when(i == 0)
    def _():
        st_ref[...] = jnp.zeros_like(st_ref)

    hsum = hsum_ref[...]
    gw = hsum.shape[0]

    def head_sums(x):
        return jnp.concatenate([_dot_split(x[:, q * gw:(q + 1) * gw], hsum)
                                for q in range(x.shape[1] // gw)], axis=1)

    p = p_ref[...].astype(F32)
    prev_row = prev_ref[BF16_SUBLANES - 1:BF16_SUBLANES, :].astype(F32)
    prev_row = jnp.where(i == 0, 0.0, prev_row)
    row = lax.broadcasted_iota(jnp.int32, p.shape, 0)
    p_prev = jnp.where(row == 0, prev_row, pltpu.roll(p, 1, axis=0))
    ps = p + (p_prev - p) * mu_ref[...]
    W = RW_WIDTH
    r = ps[:, 0:W]
    k = ps[:, W:2 * W]
    v = ps[:, 2 * W:3 * W]
    wl = ps[:, 3 * W:3 * W + DECAY_LORA]
    al = ps[:, 3 * W + DECAY_LORA:3 * W + DECAY_LORA + AAA_LORA]
    gl = ps[:, 3 * W + DECAY_LORA + AAA_LORA:RW_COLS]

    d = w0_ref[...] + _dot(jnp.tanh(wl).astype(BF16), w2_ref[...])
    log_w = -jnp.exp(-jax.nn.softplus(-d) - 0.5)
    a = jax.nn.sigmoid(a0_ref[...] + _dot(al.astype(BF16), a2_ref[...]))
    g = _dot(jax.nn.sigmoid(gl).astype(BF16), g2_ref[...])
    kk = k * kk_ref[...]
    kk = kk / jnp.maximum(jnp.sqrt(head_sums(kk * kk)), 1e-12)
    k = k * (1.0 + (a - 1.0) * ka_ref[...])
    bonus = head_sums(r * k * rk_ref[...]) * v

    HD = HEAD_DIM
    GW = RW_GROUP * HD
    ci = lax.broadcasted_iota(jnp.int32, (C, C), 0)
    cj = lax.broadcasted_iota(jnp.int32, (C, C), 1)
    tri_incl = (cj <= ci).astype(BF16)
    ti = lax.broadcasted_iota(jnp.int32, (C, GW), 0)
    tj = lax.broadcasted_iota(jnp.int32, (C, GW), 1) % HD
    strict = tj < ti
    incl = tj <= ti
    eye = ti == tj
    blk_masks = []
    s = 2
    while s < C:
        blk_masks.append((ti // (2 * s) == tj // (2 * s)) & (ti // s != tj // s))
        s *= 2
    pair_mask = (ti // 2 == tj // 2) & strict
    gi = lax.broadcasted_iota(jnp.int32, (GW, GW), 0)
    gj = lax.broadcasted_iota(jnp.int32, (GW, GW), 1)
    bd_mask = gi // HD == gj // HD
    bd_mask_b = jnp.where(bd_mask, 1.0, 0.0).astype(BF16)
    gw_eye = gi == gj

    def bd(x):
        xb = x.astype(BF16)
        return jnp.where(bd_mask_b > 0, jnp.concatenate([xb] * RW_GROUP, axis=0), 0)

    def bd_of_full(full):
        return jnp.where(bd_mask, full, 0.0)

    n_chunks = RW_TILE // C
    n_groups = RW_WIDTH // GW
    systems = [(c, q) for c in range(n_chunks) for q in range(n_groups)]

    pre = []
    for c in range(n_chunks):
        rows = slice(c * C, (c + 1) * C)
        lw = log_w[rows]
        cum = _cumsum_rows(lw, tri_incl)
        gam = jnp.exp(cum)
        gam_prev = jnp.exp(cum - lw)
        gam_inv = jnp.exp(-cum)
        g_end = gam[C - 1:C, :]
        bh_f = kk[rows] * a[rows] * gam_inv
        kh_f = k[rows] * gam_inv
        pre.append(dict(
            rh=(r[rows] * gam).astype(BF16), ah=(-kk[rows] * gam_prev).astype(BF16),
            bh=bh_f.astype(BF16), kh=kh_f.astype(BF16),
            bt=(bh_f * g_end).astype(BF16), kt=(kh_f * g_end).astype(BF16),
            v=v[rows].astype(BF16), g_end=g_end))

    def grp(c, q, name):
        return pre[c][name][:, q * GW:(q + 1) * GW]

    st = {}
    for (c, q) in systems:
        ar = jnp.concatenate([grp(c, q, "ah"), grp(c, q, "rh")], axis=0)
        pb = _dot_nt(ar, bd(grp(c, q, "bh")))
        pk = _dot_nt(ar, bd(grp(c, q, "kh")))
        lab = jnp.where(strict, pb[0:C], 0.0)
        st[(c, q)] = dict(
            lab=lab, mrb=jnp.where(incl, pb[C:2 * C], 0.0).astype(BF16),
            lm=jnp.concatenate([jnp.where(strict, pk[0:C], 0.0), jnp.where(incl, pk[C:2 * C], 0.0)],
                               axis=0).astype(BF16),
            tinv=jnp.where(eye, 1.0, jnp.where(pair_mask, lab, 0.0)))

    for bm in blk_masks:
        xs = {}
        for key in systems:
            d_ = st[key]
            xs[key] = _dot(d_["tinv"].astype(BF16), bd(jnp.where(bm, d_["lab"], 0.0))).astype(BF16)
        for key in systems:
            d_ = st[key]
            d_["tinv"] = d_["tinv"] + _dot(xs[key], bd(d_["tinv"]))

    for (c, q) in systems:
        d_ = st[(c, q)]
        lv = _dot(d_["lm"], bd(grp(c, q, "v")))
        tb = d_["tinv"].astype(BF16)
        ap = _dot(tb, bd(grp(c, q, "ah"))).astype(BF16)
        up = _dot(tb, bd(lv[0:C])).astype(BF16)
        d_["rp"] = (grp(c, q, "rh").astype(F32) + _dot(d_["mrb"], bd(ap))).astype(BF16)
        d_["ypp"] = _dot(d_["mrb"], bd(up)) + lv[C:2 * C]
        g_end = pre[c]["g_end"][:, q * GW:(q + 1) * GW]
        gt = _dot_tn(ap, grp(c, q, "bt"))
        d_["gt"] = jnp.where(gw_eye, gt + g_end, bd_of_full(gt)).astype(BF16)
        ht = bd_of_full(_dot_tn(up, grp(c, q, "bt")) + _dot_tn(grp(c, q, "v"), grp(c, q, "kt")))
        d_["ht"] = sum(ht[hh * HD:(hh + 1) * HD] for hh in range(1, RW_GROUP)) + ht[0:HD]

    for q in range(n_groups):
        s_cur = st_ref[q]
        for c in range(n_chunks):
            d_ = st[(c, q)]
            y_ref[c * C:(c + 1) * C, q * GW:(q + 1) * GW] = _dot_nt(d_["rp"], bd(s_cur)) + d_["ypp"]
            s_cur = _dot(s_cur.astype(BF16), d_["gt"]) + d_["ht"]
        st_ref[q] = s_cur

    y = y_ref[...]
    inv_n = 1.0 / HEAD_DIM
    m = head_sums(y) * inv_n
    yc = y - m
    var = head_sums(yc * yc) * inv_n
    yn = yc * lax.rsqrt(var + RW_GN_EPS) * lng_ref[...] + lnb_ref[...]
    o_ref[...] = ((yn + bonus) * g).astype(o_ref.dtype)


def _cumsum_rows(x, tri_incl_bf16):
    hi = x.astype(BF16)
    lo = (x - hi.astype(F32)).astype(BF16)
    return _dot(tri_incl_bf16, hi) + _dot(tri_incl_bf16, lo)


def rw_call(p, B, T, mu, w0, w2, a0, a2, g2, k_k, k_a, r_k, ln_g, ln_b):
    nt = T // RW_TILE
    sub = RW_TILE // BF16_SUBLANES
    hid = jnp.arange(RW_GROUP * HEAD_DIM) // HEAD_DIM
    hsum = (hid[:, None] == hid[None, :]).astype(BF16)
    row = lambda a: a.reshape(1, -1).astype(F32)
    vec = lambda n: pl.BlockSpec((1, n), lambda b, i: (0, 0))
    return pl.pallas_call(
        _rw_kernel,
        grid=(B, nt),
        in_specs=[pl.BlockSpec((None, RW_TILE, RW_COLS), lambda b, i: (b, i, 0)),
                  pl.BlockSpec((None, BF16_SUBLANES, RW_COLS),
                               lambda b, i: (b, jnp.maximum(i * sub - 1, 0), 0)),
                  vec(RW_COLS), vec(RW_WIDTH),
                  pl.BlockSpec((DECAY_LORA, RW_WIDTH), lambda b, i: (0, 0)),
                  vec(RW_WIDTH),
                  pl.BlockSpec((AAA_LORA, RW_WIDTH), lambda b, i: (0, 0)),
                  pl.BlockSpec((GATE_LORA, RW_WIDTH), lambda b, i: (0, 0)),
                  vec(RW_WIDTH), vec(RW_WIDTH), vec(RW_WIDTH), vec(RW_WIDTH), vec(RW_WIDTH),
                  pl.BlockSpec((RW_GROUP * HEAD_DIM, RW_GROUP * HEAD_DIM), lambda b, i: (0, 0))],
        out_specs=pl.BlockSpec((None, RW_TILE, RW_WIDTH), lambda b, i: (b, i, 0)),
        out_shape=jax.ShapeDtypeStruct((B, T, RW_WIDTH), BF16),
        scratch_shapes=[pltpu.VMEM((RW_HEADS // RW_GROUP, HEAD_DIM, RW_GROUP * HEAD_DIM), F32),
                        pltpu.VMEM((RW_TILE, RW_WIDTH), F32)],
        compiler_params=pltpu.CompilerParams(dimension_semantics=("parallel", "arbitrary")),
        name="rwkv7_time_mix",
    )(p, p, row(mu), row(w0), w2.astype(BF16), row(a0), a2.astype(BF16), g2.astype(BF16),
      row(k_k), row(k_a), row(r_k), row(ln_g), row(ln_b), hsum)


def _neg_abs(x):
    return lax.bitcast_convert_type(lax.bitcast_convert_type(x, jnp.uint32) | jnp.uint32(0x80000000), F32)


def _sb_scores(qs, ks, masks, neg_suffix):
    n = range(len(qs))
    zs = [_dot_nt(qs[s], ks[s]) * LOG2_E for s in n]
    sps = [jnp.maximum(z, 0.0) + jnp.log2(1.0 + jnp.exp2(_neg_abs(z))) for z in zs]
    vis = [sp if masks[s] is None else jnp.where(masks[s], sp, 0.0) for s, sp in enumerate(sps)]
    his = [sp.astype(BF16) for sp in vis]
    los = [(vis[s] - his[s].astype(F32)).astype(BF16) for s in n]
    sufs = [_dot(his[s], neg_suffix[s]) + _dot(los[s], neg_suffix[s]) for s in n]
    rowsums = [-jnp.sum(sp, axis=-1, keepdims=True) for sp in vis]
    return zs, sps, sufs, rowsums


def _sb_weighted(scores, s, carry, mask, v):
    zs, sps, sufs, _ = scores
    att = jnp.exp2(zs[s] - sps[s] + (sufs[s] + carry))
    if mask is not None:
        att = jnp.where(mask, att, 0.0)
    return _dot(att.astype(BF16), v)


def _sb_kernel(q_ref, k_ref, v_ref, o_ref, acc_ref, carry_ref, *, T):
    tq = SB_TILE
    nq = T // tq
    th = tq // 2
    ti = lax.broadcasted_iota(jnp.int32, (tq, tq), 0)
    tj = lax.broadcasted_iota(jnp.int32, (tq, tq), 1)
    neg_full = jnp.where(ti > tj, -1.0, 0.0).astype(BF16)
    neg_half = neg_full[:th, :th]
    causal_top = (tj < ti)[:th, :th]
    causal_bot = (tj < ti)[th:, :]
    scale = HEAD_DIM ** -0.5
    heads = [slice(hh * HEAD_DIM, (hh + 1) * HEAD_DIM) for hh in range(LANES // HEAD_DIM)]
    nh = len(heads)

    def q_body(i, _):
        r0 = pl.multiple_of(i * tq, tq)
        has_prev = jnp.where(i > 0, 1.0, 0.0)
        p0 = pl.multiple_of(jnp.maximum(i - 1, 0) * tq, tq)
        qs = [q_ref[pl.ds(r0, tq), cs] * scale for cs in heads]
        kd = [k_ref[pl.ds(r0, tq), cs] for cs in heads]
        vd = [v_ref[pl.ds(r0, tq), cs] for cs in heads]
        kp = [k_ref[pl.ds(p0, tq), cs] for cs in heads]
        vp = [v_ref[pl.ds(p0, tq), cs] for cs in heads]
        scores = _sb_scores([q[:th] for q in qs] + [q[th:] for q in qs] + qs,
                            [k[:th] for k in kd] + kd + kp,
                            [causal_top] * nh + [causal_bot] * nh + [None] * nh,
                            [neg_half] * nh + [neg_full] * (2 * nh))
        rss = scores[3]
        live = jnp.float32(-jnp.inf)
        for hh, cs in enumerate(heads):
            out = jnp.concatenate([_sb_weighted(scores, hh, 0.0, causal_top, vd[hh][:th]),
                                   _sb_weighted(scores, nh + hh, 0.0, causal_bot, vd[hh])], axis=0)
            rs_diag = jnp.concatenate([rss[hh], rss[nh + hh]], axis=0)
            out_prev = _sb_weighted(scores, 2 * nh + hh, rs_diag, None, vp[hh])
            acc_ref[:, cs] = out + has_prev * out_prev
            carry = rs_diag + has_prev * rss[2 * nh + hh]
            carry_ref[hh] = carry
            live = jnp.maximum(live, jnp.max(carry))

        def k_cond(state):
            jr, live = state
            return (jr < i) & (live > SB_EXP2_UNDERFLOW)

        def k_body(state):
            jr, _ = state
            c0 = pl.multiple_of((i - 1 - jr) * tq, tq)
            kt = [k_ref[pl.ds(c0, tq), cs] for cs in heads]
            vt = [v_ref[pl.ds(c0, tq), cs] for cs in heads]
            scores = _sb_scores(qs, kt, [None] * nh, [neg_full] * nh)
            live = jnp.float32(-jnp.inf)
            for hh, cs in enumerate(heads):
                carry = carry_ref[hh]
                acc_ref[:, cs] += _sb_weighted(scores, hh, carry, None, vt[hh])
                carry = carry + scores[3][hh]
                carry_ref[hh] = carry
                live = jnp.maximum(live, jnp.max(carry))
            return jr + 1, live

        lax.while_loop(k_cond, k_body, (jnp.int32(1), live))
        o_ref[pl.ds(r0, tq), :] = acc_ref[...].astype(o_ref.dtype)
        return 0

    lax.fori_loop(0, nq, q_body, 0)


def sb_call(qkv, B, T):
    npair = SB_WIDTH // LANES
    blk = lambda off: pl.BlockSpec((None, T, LANES), lambda b, hp: (b, 0, off + hp))
    return pl.pallas_call(
        functools.partial(_sb_kernel, T=T),
        grid=(B, npair),
        in_specs=[blk(0), blk(npair), blk(2 * npair)],
        out_specs=pl.BlockSpec((None, T, LANES), lambda b, hp: (b, 0, hp)),
        out_shape=jax.ShapeDtypeStruct((B, T, SB_WIDTH), BF16),
        scratch_shapes=[pltpu.VMEM((SB_TILE, LANES), F32),
                        pltpu.VMEM((LANES // HEAD_DIM, SB_TILE, 1), F32)],
        compiler_params=pltpu.CompilerParams(dimension_semantics=("parallel", "parallel")),
        name="stick_breaking_attention",
    )(qkv, qkv, qkv)


def _route(n, w_ref, b_ref, id_ref, p_ref):
    nh = n.astype(BF16)
    nl = (n - nh.astype(F32)).astype(BF16)
    a = _dot(nh, w_ref[...])
    b = _dot(nl, w_ref[...])
    L = ROUTER_LANES
    logits = a[:, :L] + (a[:, L:] + b[:, :L] + b[:, L:]) + b_ref[...]
    lane = lax.broadcasted_iota(jnp.int32, logits.shape, 1)
    neg = jnp.float32(-jnp.inf)
    logits = jnp.where(lane < N_EXPERTS, logits, neg)
    m1 = jnp.max(logits, axis=-1, keepdims=True)
    i1 = jnp.min(jnp.where(logits == m1, lane, ROUTER_LANES), axis=-1, keepdims=True)
    rest = jnp.where(lane == i1, neg, logits)
    m2 = jnp.max(rest, axis=-1, keepdims=True)
    i2 = jnp.min(jnp.where(rest == m2, lane, ROUTER_LANES), axis=-1, keepdims=True)
    e2 = jnp.exp(m2 - m1)
    p1 = 1.0 / (1.0 + e2)
    p2 = e2 / (1.0 + e2)
    id_ref[...] = jnp.where(lane == 0, i1, jnp.where(lane == 1, i2, 0))
    p_ref[...] = jnp.where(lane == 0, p1, jnp.where(lane == 1, p2, 0.0))


def _merge_kernel(ygm_ref, yrw_ref, ysb_ref, gate_ref, h_ref, pgm_ref, prw_ref, psb_ref, wo_ref,
                  g_ref, *refs, route):
    D = D_MODEL
    merged = gate_ref[:, 0:D].astype(F32) * _dot(ygm_ref[...], pgm_ref[...])
    merged += gate_ref[:, D:2 * D].astype(F32) * _dot(yrw_ref[...], prw_ref[...])
    merged += gate_ref[:, 2 * D:3 * D].astype(F32) * _dot(ysb_ref[...], psb_ref[...])
    hn = h_ref[...] + _dot(merged.astype(BF16), wo_ref[...])
    n = _rms(hn, g_ref[...])
    if route:
        w_ref, b_ref, ho_ref, id_ref, p_ref = refs
        _route(n, w_ref, b_ref, id_ref, p_ref)
    else:
        ho_ref, no_ref = refs
        no_ref[...] = n.astype(no_ref.dtype)
    ho_ref[...] = hn


def _resident(shape):
    return pl.BlockSpec(shape, lambda i: (0,) * len(shape), pipeline_mode=pl.Buffered(1))


def merge_call(y_gm, y_rw, y_sb, gates, h, p_gm, p_rw, p_sb, w_o, g_next, router=None, tm=256):
    M, D = h.shape
    tm = min(tm, M)
    rowblk = lambda n: pl.BlockSpec((tm, n), lambda i: (i, 0))
    wbytes = 2 * (GM_WIDTH + RW_WIDTH + SB_WIDTH + D) * D
    need = wbytes + 2 * tm * (2 * (GM_WIDTH + RW_WIDTH + SB_WIDTH + GATE_COLS) + 4 * D + 4 * D + 2 * D) \
        + 6 * tm * D * 4
    in_specs = [rowblk(GM_WIDTH), rowblk(RW_WIDTH), rowblk(SB_WIDTH), rowblk(GATE_COLS), rowblk(D),
                _resident((GM_WIDTH, D)), _resident((RW_WIDTH, D)), _resident((SB_WIDTH, D)),
                _resident((D, D)), _resident((1, D))]
    args = [y_gm, y_rw, y_sb, gates, h, p_gm, p_rw, p_sb, w_o, g_next.reshape(1, D)]
    if router is None:
        out_specs = [rowblk(D), rowblk(D)]
        out_shape = [jax.ShapeDtypeStruct((M, D), F32), jax.ShapeDtypeStruct((M, D), BF16)]
    else:
        router_w, router_b = router
        wpad = jnp.zeros((D, ROUTER_LANES), F32).at[:, :N_EXPERTS].set(router_w)
        w_hi = wpad.astype(BF16)
        w_lo = (wpad - w_hi.astype(F32)).astype(BF16)
        bpad = jnp.zeros((1, ROUTER_LANES), F32).at[0, :N_EXPERTS].set(router_b)
        in_specs += [_resident((D, 2 * ROUTER_LANES)), _resident((1, ROUTER_LANES))]
        args += [jnp.concatenate([w_hi, w_lo], axis=1), bpad]
        out_specs = [rowblk(D), rowblk(ROUTER_LANES), rowblk(ROUTER_LANES)]
        out_shape = [jax.ShapeDtypeStruct((M, D), F32), jax.ShapeDtypeStruct((M, ROUTER_LANES), jnp.int32),
                     jax.ShapeDtypeStruct((M, ROUTER_LANES), F32)]
    return pl.pallas_call(
        functools.partial(_merge_kernel, route=router is not None),
        grid=(M // tm,),
        in_specs=in_specs,
        out_specs=out_specs,
        out_shape=out_shape,
        compiler_params=pltpu.CompilerParams(dimension_semantics=("parallel",),
                                             vmem_limit_bytes=_vmem_limit(need)),
        name="merge_out_proj",
    )(*args)


def _ffn_kernel(n_ref, h_ref, w1_ref, w3_ref, w2_ref, g_ref, *refs, emit_h, n_cast):
    cast_in, refs = refs[:n_cast], refs[n_cast:]
    if emit_h:
        ho_ref, no_ref = refs[:2]
        refs = refs[2:]
    else:
        no_ref = refs[0]
        refs = refs[1:]
    cast_out, acc_ref = refs[:n_cast], refs[n_cast]
    f = pl.program_id(1)

    @pl.when(f == 0)
    def _():
        acc_ref[...] = jnp.zeros_like(acc_ref)

    n = n_ref[...]
    a = _dot(n, w1_ref[...])
    b = _dot(n, w3_ref[...])
    for src, dst in zip(cast_in, cast_out):
        dst[...] = src[...].astype(dst.dtype)
    act = (jax.nn.silu(a) * b).astype(BF16)
    acc_ref[...] += _dot(act, w2_ref[...])

    @pl.when(f == pl.num_programs(1) - 1)
    def _():
        hn = h_ref[...] + acc_ref[...]
        if emit_h:
            ho_ref[...] = hn
        no_ref[...] = _rms(hn, g_ref[...]).astype(no_ref.dtype)


def ffn_steps(M, F, tm=512, tf=512):
    return (M // min(tm, M)) * (F // tf)


RIDER_SLAB_ROWS = 64


def can_ride(w, steps):
    return w.shape[1] % RIDER_SLAB_ROWS == 0 and w.shape[0] * (w.shape[1] // RIDER_SLAB_ROWS) <= steps


def ffn_call(n, h, w1, w3, w2, g_next, n_dtype, emit_h, riders=(), tm=512, tf=512):
    M, D = h.shape
    F = w1.shape[1]
    tm = min(tm, M)
    nf = F // tf
    steps = (M // tm) * nf
    rowblk = pl.BlockSpec((tm, D), lambda i, f: (i, 0))
    out_specs = [rowblk]
    out_shape = [jax.ShapeDtypeStruct((M, D), n_dtype)]
    if emit_h:
        out_specs = [rowblk, rowblk]
        out_shape = [jax.ShapeDtypeStruct((M, D), F32)] + out_shape
    n_main = len(out_shape)

    def slab_spec(w):
        E, R, C = w.shape
        per = R // RIDER_SLAB_ROWS
        n_slabs = E * per
        hold = steps // n_slabs
        slab = lambda i, f: jnp.minimum((i * nf + f) // hold, n_slabs - 1)
        return pl.BlockSpec((None, RIDER_SLAB_ROWS, C), lambda i, f: (slab(i, f) // per, slab(i, f) % per, 0))

    ride_specs = [slab_spec(w) for w in riders]
    need = 2 * tm * D * (2 + 4 + 4 * n_main) + tm * D * 4 + 2 * 3 * D * tf * 2 + 4 * tm * tf * 4 \
        + sum(2 * RIDER_SLAB_ROWS * w.shape[2] * 6 for w in riders)
    outs = pl.pallas_call(
        functools.partial(_ffn_kernel, emit_h=emit_h, n_cast=len(riders)),
        grid=(M // tm, nf),
        in_specs=[rowblk, rowblk,
                  pl.BlockSpec((D, tf), lambda i, f: (0, f)),
                  pl.BlockSpec((D, tf), lambda i, f: (0, f)),
                  pl.BlockSpec((tf, D), lambda i, f: (f, 0)),
                  pl.BlockSpec((1, D), lambda i, f: (0, 0))] + ride_specs,
        out_specs=out_specs + ride_specs,
        out_shape=out_shape + [jax.ShapeDtypeStruct(w.shape, BF16) for w in riders],
        scratch_shapes=[pltpu.VMEM((tm, D), F32)],
        compiler_params=pltpu.CompilerParams(dimension_semantics=("arbitrary", "arbitrary"),
                                             vmem_limit_bytes=_vmem_limit(need + (8 << 20))),
        name="swiglu_ffn",
    )(n, h, w1, w3, w2, g_next.reshape(1, D), *riders)
    return outs[:n_main], outs[n_main:]


def moe_plan(ids, M, tm):
    E = N_EXPERTS
    n_tiles = -(-(TOP_K * M) // tm) + E + 2
    e_flat = ids[:, :TOP_K].reshape(-1)
    onehot = (e_flat[:, None] == jnp.arange(E, dtype=jnp.int32)[None, :]).astype(jnp.int32)
    csum = jnp.cumsum(onehot, axis=0)
    rank = jnp.sum((csum - onehot) * onehot, axis=1)
    counts = csum[-1]
    tiles_per_e = (counts + tm - 1) // tm
    tile_end = jnp.cumsum(tiles_per_e)
    offs = (tile_end - tiles_per_e) * tm
    dest = offs[e_flat] + rank
    pair = jnp.full((n_tiles * tm,), -1, jnp.int32).at[dest].set(
        jnp.arange(TOP_K * M, dtype=jnp.int32), unique_indices=True)
    spare = TOP_K * M + jnp.arange(n_tiles * tm, dtype=jnp.int32) % tm
    src_token = jnp.where(pair >= 0, pair // TOP_K, 0)
    out_row = jnp.where(pair >= 0, (pair % TOP_K) * M + pair // TOP_K, spare)
    prev_out_row = jnp.concatenate([spare[:tm], out_row[:-tm]])
    n_used = tile_end[-1]
    t = jnp.arange(n_tiles, dtype=jnp.int32)
    tile_e = jnp.searchsorted(tile_end, jnp.minimum(t, n_used - 1), side="right").astype(jnp.int32)
    return (src_token.reshape(n_tiles, 1, tm), prev_out_row.reshape(n_tiles, 1, tm), tile_e,
            n_used.astype(jnp.int32).reshape(1))


def _bf16_bits(x):
    u = lax.bitcast_convert_type(x, jnp.uint32)
    return (u + jnp.uint32(0x7FFF) + ((u >> 16) & jnp.uint32(1))) & jnp.uint32(0xFFFF0000)


def _pack_bf16_pairs(x):
    half = x.shape[1] // 2
    return _bf16_bits(x[:, half:]) | (_bf16_bits(x[:, :half]) >> 16)


def _unpack_bf16_pairs(w):
    lo = lax.bitcast_convert_type(w << 16, F32)
    hi = lax.bitcast_convert_type(w & jnp.uint32(0xFFFF0000), F32)
    return jnp.concatenate([lo, hi], axis=1)


def _moe_kernel(tile_e_ref, n_used_ref, src_ref, nxt_ref, dst_ref, h_hbm, g_ref, w1_ref, w3_ref, w2_ref,
                y_hbm, xbuf, xb, acc, ostage, gsem, ssem):
    t = pl.program_id(0)
    f = pl.program_id(1)
    nf = pl.num_programs(1)
    tm = xb.shape[0]
    per_step = tm // nf
    n_used = n_used_ref[0]
    used = t < n_used
    cur = t % 2

    def fetch(idx_ref, r, slot):
        return pltpu.make_async_copy(h_hbm.at[pl.ds(idx_ref[0, r], 1)], xbuf.at[slot, pl.ds(r, 1)],
                                     gsem.at[slot])

    def put(r):
        return pltpu.make_async_copy(ostage.at[1 - cur, pl.ds(r, 1)], y_hbm.at[pl.ds(dst_ref[0, r], 1)],
                                     ssem.at[1 - cur])

    @pl.when((f == 0) & (t == 0))
    def _():
        def issue(r, c):
            fetch(src_ref, r, 0).start()
            return c
        lax.fori_loop(0, tm, issue, 0)
        ostage[1] = jnp.zeros((tm, ostage.shape[2]), ostage.dtype)

    @pl.when((f == 0) & (t <= n_used))
    def _():
        pltpu.make_async_copy(h_hbm.at[pl.ds(0, tm)], xbuf.at[cur], gsem.at[cur]).wait()

    @pl.when((f == 0) & (t >= 1) & (t - 1 <= n_used))
    def _():
        pltpu.make_async_copy(ostage.at[cur], y_hbm.at[pl.ds(0, tm)], ssem.at[cur]).wait()

    @pl.when((f == 0) & used)
    def _():
        xb[...] = _rms(xbuf[cur], g_ref[...]).astype(BF16)
        acc[...] = jnp.zeros_like(acc)

    @pl.when(used)
    def _():
        for j in range(per_step):
            fetch(nxt_ref, f * per_step + j, 1 - cur).start()
            put(f * per_step + j).start()
        x = xb[...]
        act = jax.nn.silu(_dot(x, w1_ref[...])) * _dot(x, w3_ref[...])
        acc[...] += _dot(act.astype(BF16), w2_ref[...])

    @pl.when(used & (f == nf - 1))
    def _():
        ostage[cur] = _pack_bf16_pairs(acc[...])

    @pl.when((f == 0) & (t == n_used))
    def _():
        def issue(r, c):
            put(r).start()
            return c
        lax.fori_loop(0, tm, issue, 0)


def moe_call(h, g, src_token, prev_out_row, tile_e, n_used, w1, w3, w2, tm, tf):
    M, D = h.shape
    n_tiles = src_token.shape[0]
    E, _, Fe = w1.shape
    nf = Fe // tf
    assert tm % nf == 0 and Fe % tf == 0

    def wmap(t, f, tile_e_ref, n_used_ref):
        ff = jnp.where(t < n_used_ref[0], f, nf - 1)
        return tile_e_ref[t], ff

    smem_tile = lambda fn: pl.BlockSpec((None, 1, tm), fn, memory_space=pltpu.SMEM)
    grid_spec = pltpu.PrefetchScalarGridSpec(
        num_scalar_prefetch=2,
        grid=(n_tiles, nf),
        in_specs=[smem_tile(lambda t, f, te, nu: (t, 0, 0)),
                  smem_tile(lambda t, f, te, nu: (jnp.minimum(t + 1, n_tiles - 1), 0, 0)),
                  smem_tile(lambda t, f, te, nu: (t, 0, 0)),
                  pl.BlockSpec(memory_space=pl.ANY),
                  pl.BlockSpec((1, D), lambda t, f, te, nu: (0, 0)),
                  pl.BlockSpec((None, D, tf), lambda t, f, te, nu: (wmap(t, f, te, nu)[0], 0,
                                                                     wmap(t, f, te, nu)[1])),
                  pl.BlockSpec((None, D, tf), lambda t, f, te, nu: (wmap(t, f, te, nu)[0], 0,
                                                                     wmap(t, f, te, nu)[1])),
                  pl.BlockSpec((None, tf, D), lambda t, f, te, nu: wmap(t, f, te, nu) + (0,))],
        out_specs=pl.BlockSpec(memory_space=pl.ANY),
        scratch_shapes=[pltpu.VMEM((2, tm, D), F32), pltpu.VMEM((tm, D), BF16), pltpu.VMEM((tm, D), F32),
                        pltpu.VMEM((2, tm, D // 2), jnp.uint32),
                        pltpu.SemaphoreType.DMA((2,)), pltpu.SemaphoreType.DMA((2,))])
    need = tm * D * (8 + 2 + 4 + 4) + 2 * 3 * D * tf * 2 + 4 * tm * tf * 4
    return pl.pallas_call(
        _moe_kernel,
        grid_spec=grid_spec,
        out_shape=jax.ShapeDtypeStruct((TOP_K * M + tm, D // 2), jnp.uint32),
        compiler_params=pltpu.CompilerParams(dimension_semantics=("arbitrary", "arbitrary"),
                                             vmem_limit_bytes=_vmem_limit(need + (8 << 20))),
        name="moe_experts",
    )(tile_e, n_used, src_token, src_token, prev_out_row, h, g.reshape(1, D), w1, w3, w2)


def _moe_combine_kernel(h_ref, p_ref, g_ref, *refs, emit_h):
    y_refs, out_refs = refs[:TOP_K], refs[TOP_K:]
    hn = h_ref[...]
    for k in range(TOP_K):
        hn = hn + p_ref[:, k:k + 1] * _unpack_bf16_pairs(y_refs[k][...])
    if emit_h:
        out_refs[0][...] = hn
    out_refs[-1][...] = _rms(hn, g_ref[...]).astype(out_refs[-1].dtype)


def moe_combine_call(h, y_tok, probs, g_next, n_dtype, emit_h, tm=MOE_COMBINE_TM):
    M, D = h.shape
    tm = min(tm, M)
    rowblk = pl.BlockSpec((tm, D), lambda i: (i, 0))
    out_specs = [rowblk]
    out_shape = [jax.ShapeDtypeStruct((M, D), n_dtype)]
    if emit_h:
        out_specs = [rowblk, rowblk]
        out_shape = [jax.ShapeDtypeStruct((M, D), F32)] + out_shape
    y_specs = [pl.BlockSpec((tm, D // 2), functools.partial(lambda k, i: (k * (M // tm) + i, 0), k))
               for k in range(TOP_K)]
    return pl.pallas_call(
        functools.partial(_moe_combine_kernel, emit_h=emit_h),
        grid=(M // tm,),
        in_specs=[rowblk, pl.BlockSpec((tm, ROUTER_LANES), lambda i: (i, 0)),
                  pl.BlockSpec((1, D), lambda i: (0, 0))] + y_specs,
        out_specs=out_specs,
        out_shape=out_shape,
        compiler_params=pltpu.CompilerParams(dimension_semantics=("parallel",),
                                             vmem_limit_bytes=_vmem_limit(14 * tm * D * 4)),
        name="moe_combine",
    )(h, probs, g_next.reshape(1, D), *([y_tok] * TOP_K))


def kernel(x, norm_mix, w_in, gate_b, gm_ln_g, gm_ln_b, gm_ws, gm_bs, rw_mu, rw_w0, rw_w2, rw_a0, rw_a2,
           rw_g2, rw_kk, rw_ka, rw_rk, rw_ln_g, rw_ln_b, p_gm, p_rw, p_sb, w_o, norm_ffn, ffn_w1, ffn_w3,
           ffn_w2, router_w, router_b, moe_w1, moe_w3, moe_w2, norm_out):
    B, T, D = x.shape
    M = B * T
    bf = lambda a: a.astype(BF16)
    h = x.reshape(M, D)
    n = rmsnorm_call(h, norm_mix[0], BF16)
    moe_bf16 = {}
    for l in range(DEPTH):
        last = l == DEPTH - 1
        g_next = norm_out if last else norm_mix[l + 1]
        n_dtype = x.dtype if last else BF16
        w = w_in[l]
        j = l // 2
        uv = matmul_call(n, bf(w[:, GM_OFF:RW_OFF]), BF16, tn=GM_COLS, name="in_proj_gm")
        p = matmul_call(n, bf(w[:, RW_OFF:SB_OFF]), BF16, tn=RW_COLS, name="in_proj_rw")
        qkv = matmul_call(n, bf(w[:, SB_OFF:GATE_OFF]), BF16, tn=1024, name="in_proj_sb")
        gates = matmul_call(n, bf(w[:, GATE_OFF:]), BF16, tn=1024, bias=gate_b[l].reshape(-1),
                            name="in_proj_gate")
        y_gm = gm_call(uv, gm_ln_g[l], gm_ln_b[l], gm_ws[l], gm_bs[l])
        y_rw = rw_call(p.reshape(B, T, RW_COLS), B, T, rw_mu[l], rw_w0[l], rw_w2[l], rw_a0[l], rw_a2[l],
                       rw_g2[l], rw_kk[l], rw_ka[l], rw_rk[l], rw_ln_g[l], rw_ln_b[l]).reshape(M, RW_WIDTH)
        y_sb = sb_call(qkv.reshape(B, T, SB_COLS), B, T).reshape(M, SB_WIDTH)
        merge_args = (y_gm, y_rw, y_sb, gates, h, bf(p_gm[l]), bf(p_rw[l]), bf(p_sb[l]), bf(w_o[l]), norm_ffn[l])
        if l % 2 == 0:
            h, n2 = merge_call(*merge_args)
            ahead = []
            if not last:
                ahead = [(k, wt[(l + 1) // 2]) for k, wt in (("w1", moe_w1), ("w3", moe_w3), ("w2", moe_w2))
                         if can_ride(wt[(l + 1) // 2], ffn_steps(M, D_FF))]
            outs, casts = ffn_call(n2, h, bf(ffn_w1[j]), bf(ffn_w3[j]), bf(ffn_w2[j]), g_next, n_dtype,
                                   emit_h=not last, riders=[wt for _, wt in ahead])
            moe_bf16 = {k: c for (k, _), c in zip(ahead, casts)}
        else:
            h, ids, probs = merge_call(*merge_args, router=(router_w[j], router_b[j]))
            src_token, prev_out_row, tile_e, n_used = moe_plan(ids, M, MOE_TM)
            ew = {k: moe_bf16[k] if k in moe_bf16 else bf(wt[j])
                  for k, wt in (("w1", moe_w1), ("w3", moe_w3), ("w2", moe_w2))}
            y_tok = moe_call(h, norm_ffn[l], src_token, prev_out_row, tile_e, n_used, ew["w1"], ew["w3"],
                             ew["w2"], MOE_TM, MOE_TF)
            outs = moe_combine_call(h, y_tok, probs, g_next, n_dtype, emit_h=not last)
            moe_bf16 = {}
        if last:
            n = outs[0]
        else:
            h, n = outs
    return n.reshape(B, T, D)
```

```python
import functools

import jax
import jax.numpy as jnp
from jax import lax
from jax.experimental import pallas as pl
from jax.experimental.pallas import tpu as pltpu

F32 = jnp.float32
BF16 = jnp.bfloat16

D_MODEL = 2048
DEPTH = 2
HEAD_DIM = 64
GM_HEADS = 8
GM_WIDTH = GM_HEADS * HEAD_DIM
CHUNK = 128
RW_HEADS = 8
RW_WIDTH = RW_HEADS * HEAD_DIM
DECAY_LORA = 64
AAA_LORA = 64
GATE_LORA = 128
SB_HEADS = 16
SB_WIDTH = SB_HEADS * HEAD_DIM
N_BRANCH = 3
D_FF = 5632
N_EXPERTS = 8
TOP_K = 2
D_FF_EXPERT = D_FF // TOP_K
RMS_EPS = 1e-6
LN_EPS = 1e-5
RW_GN_EPS = 64e-5

GM_COLS = 2 * GM_WIDTH
RW_COLS = 3 * RW_WIDTH + DECAY_LORA + AAA_LORA + GATE_LORA
SB_COLS = 3 * SB_WIDTH
GATE_COLS = N_BRANCH * D_MODEL
GM_OFF = 0
RW_OFF = GM_OFF + GM_COLS
SB_OFF = RW_OFF + RW_COLS
GATE_OFF = SB_OFF + SB_COLS
N_IN = GATE_OFF + GATE_COLS

V7X_VMEM_BYTES = 64 * 1024 * 1024
LANES = 128
BF16_SUBLANES = 16
RW_CHUNK = 64
RW_TILE = 256
RW_GROUP = 4
SB_TILE = 256
LOG2_E = 1.4426950408889634
SB_EXP2_UNDERFLOW = -151.0
ROUTER_LANES = 128
MOE_TF = 256
MOE_TM = 528
MOE_COMBINE_TM = 256


def _vmem_limit(nbytes):
    return int(min(max(nbytes, 32 * 1024 * 1024), V7X_VMEM_BYTES - 6 * 1024 * 1024))


def _dot(a, b):
    return jnp.dot(a, b, preferred_element_type=F32)


def _dot_nt(a, b):
    return lax.dot_general(a, b, (((1,), (1,)), ((), ())), preferred_element_type=F32)


def _dot_tn(a, b):
    return lax.dot_general(a, b, (((0,), (0,)), ((), ())), preferred_element_type=F32)


def _dot_split(x, ones_bf16):
    hi = x.astype(BF16)
    lo = (x - hi.astype(F32)).astype(BF16)
    return _dot(hi, ones_bf16) + _dot(lo, ones_bf16)


def _rms(x, g):
    ms = jnp.mean(x * x, axis=-1, keepdims=True)
    return x * lax.rsqrt(ms + RMS_EPS) * g


def _rmsnorm_kernel(x_ref, g_ref, o_ref):
    o_ref[...] = _rms(x_ref[...], g_ref[...]).astype(o_ref.dtype)


def rmsnorm_call(x, g, out_dtype, tm=512):
    M, D = x.shape
    tm = min(tm, M)
    return pl.pallas_call(
        _rmsnorm_kernel,
        grid=(M // tm,),
        in_specs=[pl.BlockSpec((tm, D), lambda i: (i, 0)),
                  pl.BlockSpec((1, D), lambda i: (0, 0))],
        out_specs=pl.BlockSpec((tm, D), lambda i: (i, 0)),
        out_shape=jax.ShapeDtypeStruct((M, D), out_dtype),
        compiler_params=pltpu.CompilerParams(dimension_semantics=("parallel",)),
        name="rmsnorm",
    )(x, g.reshape(1, D))


def _mm_kernel(a_ref, w_ref, o_ref):
    o_ref[...] = _dot(a_ref[...], w_ref[...]).astype(o_ref.dtype)


def _mm_gate_kernel(a_ref, w_ref, b_ref, o_ref):
    o_ref[...] = jax.nn.sigmoid(_dot(a_ref[...], w_ref[...]) + b_ref[...]).astype(o_ref.dtype)


def matmul_call(a, w, out_dtype, tn, bias=None, tm=2048, name="matmul"):
    M, K = a.shape
    N = w.shape[1]
    tm = min(tm, M)
    in_specs = [pl.BlockSpec((tm, K), lambda i, j: (i, 0)),
                pl.BlockSpec((K, tn), lambda i, j: (0, j))]
    args = [a, w]
    kern = _mm_kernel
    if bias is not None:
        in_specs.append(pl.BlockSpec((1, tn), lambda i, j: (0, j)))
        args.append(bias.reshape(1, N))
        kern = _mm_gate_kernel
    osz = jnp.dtype(out_dtype).itemsize
    need = 2 * (tm * K * 2 + K * tn * 2 + tm * tn * osz) + 2 * tm * tn * 4
    return pl.pallas_call(
        kern,
        grid=(M // tm, N // tn),
        in_specs=in_specs,
        out_specs=pl.BlockSpec((tm, tn), lambda i, j: (i, j)),
        out_shape=jax.ShapeDtypeStruct((M, N), out_dtype),
        compiler_params=pltpu.CompilerParams(
            dimension_semantics=("parallel", "arbitrary"),
            vmem_limit_bytes=_vmem_limit(need + (8 << 20))),
        name=name,
    )(*args)


def _gm_kernel(uv_ref, lng_ref, lnb_ref, ws_ref, bias_ref, o_ref, *, n_chunks):
    ii = lax.broadcasted_iota(jnp.int32, (CHUNK, CHUNK), 0)
    jj = lax.broadcasted_iota(jnp.int32, (CHUNK, CHUNK), 1)
    causal = jj <= ii
    ws = [jnp.where(causal, ws_ref[h], 0.0).astype(BF16) for h in range(GM_HEADS)]
    for c in range(n_chunks):
        rows = slice(c * CHUNK, (c + 1) * CHUNK)
        u = jax.nn.gelu(uv_ref[rows, 0:GM_WIDTH].astype(F32))
        v = jax.nn.gelu(uv_ref[rows, GM_WIDTH:2 * GM_WIDTH].astype(F32))
        mu = jnp.mean(v, axis=-1, keepdims=True)
        vc = v - mu
        var = jnp.mean(vc * vc, axis=-1, keepdims=True)
        vn = (vc * lax.rsqrt(var + LN_EPS) * lng_ref[...] + lnb_ref[...]).astype(BF16)
        mixed = jnp.concatenate(
            [_dot(ws[h], vn[:, h * HEAD_DIM:(h + 1) * HEAD_DIM]) for h in range(GM_HEADS)], axis=-1)
        o_ref[rows, :] = (u * (mixed + bias_ref[...])).astype(o_ref.dtype)


def gm_call(uv, ln_g, ln_b, w_s, b_s, n_chunks=2):
    M = uv.shape[0]
    tt = CHUNK * n_chunks
    bias = jnp.repeat(b_s.T, HEAD_DIM, axis=1)
    return pl.pallas_call(
        functools.partial(_gm_kernel, n_chunks=n_chunks),
        grid=(M // tt,),
        in_specs=[pl.BlockSpec((tt, GM_COLS), lambda i: (i, 0)),
                  pl.BlockSpec((1, GM_WIDTH), lambda i: (0, 0)),
                  pl.BlockSpec((1, GM_WIDTH), lambda i: (0, 0)),
                  pl.BlockSpec((GM_HEADS, CHUNK, CHUNK), lambda i: (0, 0, 0)),
                  pl.BlockSpec((CHUNK, GM_WIDTH), lambda i: (0, 0))],
        out_specs=pl.BlockSpec((tt, GM_WIDTH), lambda i: (i, 0)),
        out_shape=jax.ShapeDtypeStruct((M, GM_WIDTH), BF16),
        compiler_params=pltpu.CompilerParams(dimension_semantics=("parallel",)),
        name="gmlp_gating",
    )(uv, ln_g.reshape(1, -1), ln_b.reshape(1, -1), w_s, bias)


def _rw_kernel(p_ref, prev_ref, mu_ref, w0_ref, w2_ref, a0_ref, a2_ref, g2_ref, kk_ref, ka_ref,
               rk_ref, lng_ref, lnb_ref, hsum_ref, o_ref, st_ref, y_ref):
    C = RW_CHUNK
    i = pl.program_id(1)

    @pl.when(i == 0)
    def _():
        st_ref[...] = jnp.zeros_like(st_ref)

    hsum = hsum_ref[...]
    gw = hsum.shape[0]

    def head_sums(x):
        return jnp.concatenate([_dot_split(x[:, q * gw:(q + 1) * gw], hsum)
                                for q in range(x.shape[1] // gw)], axis=1)

    p = p_ref[...].astype(F32)
    prev_row = prev_ref[BF16_SUBLANES - 1:BF16_SUBLANES, :].astype(F32)
    prev_row = jnp.where(i == 0, 0.0, prev_row)
    row = lax.broadcasted_iota(jnp.int32, p.shape, 0)
    p_prev = jnp.where(row == 0, prev_row, pltpu.roll(p, 1, axis=0))
    ps = p + (p_prev - p) * mu_ref[...]
    W = RW_WIDTH
    r = ps[:, 0:W]
    k = ps[:, W:2 * W]
    v = ps[:, 2 * W:3 * W]
    wl = ps[:, 3 * W:3 * W + DECAY_LORA]
    al = ps[:, 3 * W + DECAY_LORA:3 * W + DECAY_LORA + AAA_LORA]
    gl = ps[:, 3 * W + DECAY_LORA + AAA_LORA:RW_COLS]

    d = w0_ref[...] + _dot(jnp.tanh(wl).astype(BF16), w2_ref[...])
    log_w = -jnp.exp(-jax.nn.softplus(-d) - 0.5)
    a = jax.nn.sigmoid(a0_ref[...] + _dot(al.astype(BF16), a2_ref[...]))
    g = _dot(jax.nn.sigmoid(gl).astype(BF16), g2_ref[...])
    kk = k * kk_ref[...]
    kk = kk / jnp.maximum(jnp.sqrt(head_sums(kk * kk)), 1e-12)
    k = k * (1.0 + (a - 1.0) * ka_ref[...])
    bonus = head_sums(r * k * rk_ref[...]) * v

    HD = HEAD_DIM
    GW = RW_GROUP * HD
    ci = lax.broadcasted_iota(jnp.int32, (C, C), 0)
    cj = lax.broadcasted_iota(jnp.int32, (C, C), 1)
    tri_incl = (cj <= ci).astype(BF16)
    ti = lax.broadcasted_iota(jnp.int32, (C, GW), 0)
    tj = lax.broadcasted_iota(jnp.int32, (C, GW), 1) % HD
    strict = tj < ti
    incl = tj <= ti
    eye = ti == tj
    blk_masks = []
    s = 2
    while s < C:
        blk_masks.append((ti // (2 * s) == tj // (2 * s)) & (ti // s != tj // s))
        s *= 2
    pair_mask = (ti // 2 == tj // 2) & strict
    gi = lax.broadcasted_iota(jnp.int32, (GW, GW), 0)
    gj = lax.broadcasted_iota(jnp.int32, (GW, GW), 1)
    bd_mask = gi // HD == gj // HD
    bd_mask_b = jnp.where(bd_mask, 1.0, 0.0).astype(BF16)
    gw_eye = gi == gj

    def bd(x):
        xb = x.astype(BF16)
        return jnp.where(bd_mask_b > 0, jnp.concatenate([xb] * RW_GROUP, axis=0), 0)

    def bd_of_full(full):
        return jnp.where(bd_mask, full, 0.0)

    n_chunks = RW_TILE // C
    n_groups = RW_WIDTH // GW
    systems = [(c, q) for c in range(n_chunks) for q in range(n_groups)]

    pre = []
    for c in range(n_chunks):
        rows = slice(c * C, (c + 1) * C)
        lw = log_w[rows]
        cum = _cumsum_rows(lw, tri_incl)
        gam = jnp.exp(cum)
        gam_prev = jnp.exp(cum - lw)
        gam_inv = jnp.exp(-cum)
        g_end = gam[C - 1:C, :]
        bh_f = kk[rows] * a[rows] * gam_inv
        kh_f = k[rows] * gam_inv
        pre.append(dict(
            rh=(r[rows] * gam).astype(BF16), ah=(-kk[rows] * gam_prev).astype(BF16),
            bh=bh_f.astype(BF16), kh=kh_f.astype(BF16),
            bt=(bh_f * g_end).astype(BF16), kt=(kh_f * g_end).astype(BF16),
            v=v[rows].astype(BF16), g_end=g_end))

    def grp(c, q, name):
        return pre[c][name][:, q * GW:(q + 1) * GW]

    st = {}
    for (c, q) in systems:
        ar = jnp.concatenate([grp(c, q, "ah"), grp(c, q, "rh")], axis=0)
        pb = _dot_nt(ar, bd(grp(c, q, "bh")))
        pk = _dot_nt(ar, bd(grp(c, q, "kh")))
        lab = jnp.where(strict, pb[0:C], 0.0)
        st[(c, q)] = dict(
            lab=lab, mrb=jnp.where(incl, pb[C:2 * C], 0.0).astype(BF16),
            lm=jnp.concatenate([jnp.where(strict, pk[0:C], 0.0), jnp.where(incl, pk[C:2 * C], 0.0)],
                               axis=0).astype(BF16),
            tinv=jnp.where(eye, 1.0, jnp.where(pair_mask, lab, 0.0)))

    for bm in blk_masks:
        xs = {}
        for key in systems:
            d_ = st[key]
            xs[key] = _dot(d_["tinv"].astype(BF16), bd(jnp.where(bm, d_["lab"], 0.0))).astype(BF16)
        for key in systems:
            d_ = st[key]
            d_["tinv"] = d_["tinv"] + _dot(xs[key], bd(d_["tinv"]))

    for (c, q) in systems:
        d_ = st[(c, q)]
        lv = _dot(d_["lm"], bd(grp(c, q, "v")))
        tb = d_["tinv"].astype(BF16)
        ap = _dot(tb, bd(grp(c, q, "ah"))).astype(BF16)
        up = _dot(tb, bd(lv[0:C])).astype(BF16)
        d_["rp"] = (grp(c, q, "rh").astype(F32) + _dot(d_["mrb"], bd(ap))).astype(BF16)
        d_["ypp"] = _dot(d_["mrb"], bd(up)) + lv[C:2 * C]
        g_end = pre[c]["g_end"][:, q * GW:(q + 1) * GW]
        gt = _dot_tn(ap, grp(c, q, "bt"))
        d_["gt"] = jnp.where(gw_eye, gt + g_end, bd_of_full(gt)).astype(BF16)
        ht = bd_of_full(_dot_tn(up, grp(c, q, "bt")) + _dot_tn(grp(c, q, "v"), grp(c, q, "kt")))
        d_["ht"] = sum(ht[hh * HD:(hh + 1) * HD] for hh in range(1, RW_GROUP)) + ht[0:HD]

    for q in range(n_groups):
        s_cur = st_ref[q]
        for c in range(n_chunks):
            d_ = st[(c, q)]
            y_ref[c * C:(c + 1) * C, q * GW:(q + 1) * GW] = _dot_nt(d_["rp"], bd(s_cur)) + d_["ypp"]
            s_cur = _dot(s_cur.astype(BF16), d_["gt"]) + d_["ht"]
        st_ref[q] = s_cur

    y = y_ref[...]
    inv_n = 1.0 / HEAD_DIM
    m = head_sums(y) * inv_n
    yc = y - m
    var = head_sums(yc * yc) * inv_n
    yn = yc * lax.rsqrt(var + RW_GN_EPS) * lng_ref[...] + lnb_ref[...]
    o_ref[...] = ((yn + bonus) * g).astype(o_ref.dtype)


def _cumsum_rows(x, tri_incl_bf16):
    hi = x.astype(BF16)
    lo = (x - hi.astype(F32)).astype(BF16)
    return _dot(tri_incl_bf16, hi) + _dot(tri_incl_bf16, lo)


def rw_call(p, B, T, mu, w0, w2, a0, a2, g2, k_k, k_a, r_k, ln_g, ln_b):
    nt = T // RW_TILE
    sub = RW_TILE // BF16_SUBLANES
    hid = jnp.arange(RW_GROUP * HEAD_DIM) // HEAD_DIM
    hsum = (hid[:, None] == hid[None, :]).astype(BF16)
    row = lambda a: a.reshape(1, -1).astype(F32)
    vec = lambda n: pl.BlockSpec((1, n), lambda b, i: (0, 0))
    return pl.pallas_call(
        _rw_kernel,
        grid=(B, nt),
        in_specs=[pl.BlockSpec((None, RW_TILE, RW_COLS), lambda b, i: (b, i, 0)),
                  pl.BlockSpec((None, BF16_SUBLANES, RW_COLS),
                               lambda b, i: (b, jnp.maximum(i * sub - 1, 0), 0)),
                  vec(RW_COLS), vec(RW_WIDTH),
                  pl.BlockSpec((DECAY_LORA, RW_WIDTH), lambda b, i: (0, 0)),
                  vec(RW_WIDTH),
                  pl.BlockSpec((AAA_LORA, RW_WIDTH), lambda b, i: (0, 0)),
                  pl.BlockSpec((GATE_LORA, RW_WIDTH), lambda b, i: (0, 0)),
                  vec(RW_WIDTH), vec(RW_WIDTH), vec(RW_WIDTH), vec(RW_WIDTH), vec(RW_WIDTH),
                  pl.BlockSpec((RW_GROUP * HEAD_DIM, RW_GROUP * HEAD_DIM), lambda b, i: (0, 0))],
        out_specs=pl.BlockSpec((None, RW_TILE, RW_WIDTH), lambda b, i: (b, i, 0)),
        out_shape=jax.ShapeDtypeStruct((B, T, RW_WIDTH), BF16),
        scratch_shapes=[pltpu.VMEM((RW_HEADS // RW_GROUP, HEAD_DIM, RW_GROUP * HEAD_DIM), F32),
                        pltpu.VMEM((RW_TILE, RW_WIDTH), F32)],
        compiler_params=pltpu.CompilerParams(dimension_semantics=("parallel", "arbitrary")),
        name="rwkv7_time_mix",
    )(p, p, row(mu), row(w0), w2.astype(BF16), row(a0), a2.astype(BF16), g2.astype(BF16),
      row(k_k), row(k_a), row(r_k), row(ln_g), row(ln_b), hsum)


def _neg_abs(x):
    return lax.bitcast_convert_type(lax.bitcast_convert_type(x, jnp.uint32) | jnp.uint32(0x80000000), F32)


def _sb_scores(qs, ks, masks, neg_suffix):
    n = range(len(qs))
    zs = [_dot_nt(qs[s], ks[s]) * LOG2_E for s in n]
    sps = [jnp.maximum(z, 0.0) + jnp.log2(1.0 + jnp.exp2(_neg_abs(z))) for z in zs]
    vis = [sp if masks[s] is None else jnp.where(masks[s], sp, 0.0) for s, sp in enumerate(sps)]
    his = [sp.astype(BF16) for sp in vis]
    los = [(vis[s] - his[s].astype(F32)).astype(BF16) for s in n]
    sufs = [_dot(his[s], neg_suffix[s]) + _dot(los[s], neg_suffix[s]) for s in n]
    rowsums = [-jnp.sum(sp, axis=-1, keepdims=True) for sp in vis]
    return zs, sps, sufs, rowsums


def _sb_weighted(scores, s, carry, mask, v):
    zs, sps, sufs, _ = scores
    att = jnp.exp2(zs[s] - sps[s] + (sufs[s] + carry))
    if mask is not None:
        att = jnp.where(mask, att, 0.0)
    return _dot(att.astype(BF16), v)


def _sb_kernel(q_ref, k_ref, v_ref, o_ref, acc_ref, carry_ref, *, T):
    tq = SB_TILE
    nq = T // tq
    th = tq // 2
    ti = lax.broadcasted_iota(jnp.int32, (tq, tq), 0)
    tj = lax.broadcasted_iota(jnp.int32, (tq, tq), 1)
    neg_full = jnp.where(ti > tj, -1.0, 0.0).astype(BF16)
    neg_half = neg_full[:th, :th]
    causal_top = (tj < ti)[:th, :th]
    causal_bot = (tj < ti)[th:, :]
    scale = HEAD_DIM ** -0.5
    heads = [slice(hh * HEAD_DIM, (hh + 1) * HEAD_DIM) for hh in range(LANES // HEAD_DIM)]
    nh = len(heads)

    def q_body(i, _):
        r0 = pl.multiple_of(i * tq, tq)
        has_prev = jnp.where(i > 0, 1.0, 0.0)
        p0 = pl.multiple_of(jnp.maximum(i - 1, 0) * tq, tq)
        qs = [q_ref[pl.ds(r0, tq), cs] * scale for cs in heads]
        kd = [k_ref[pl.ds(r0, tq), cs] for cs in heads]
        vd = [v_ref[pl.ds(r0, tq), cs] for cs in heads]
        kp = [k_ref[pl.ds(p0, tq), cs] for cs in heads]
        vp = [v_ref[pl.ds(p0, tq), cs] for cs in heads]
        scores = _sb_scores([q[:th] for q in qs] + [q[th:] for q in qs] + qs,
                            [k[:th] for k in kd] + kd + kp,
                            [causal_top] * nh + [causal_bot] * nh + [None] * nh,
                            [neg_half] * nh + [neg_full] * (2 * nh))
        rss = scores[3]
        live = jnp.float32(-jnp.inf)
        for hh, cs in enumerate(heads):
            out = jnp.concatenate([_sb_weighted(scores, hh, 0.0, causal_top, vd[hh][:th]),
                                   _sb_weighted(scores, nh + hh, 0.0, causal_bot, vd[hh])], axis=0)
            rs_diag = jnp.concatenate([rss[hh], rss[nh + hh]], axis=0)
            out_prev = _sb_weighted(scores, 2 * nh + hh, rs_diag, None, vp[hh])
            acc_ref[:, cs] = out + has_prev * out_prev
            carry = rs_diag + has_prev * rss[2 * nh + hh]
            carry_ref[hh] = carry
            live = jnp.maximum(live, jnp.max(carry))

        def k_cond(state):
            jr, live = state
            return (jr < i) & (live > SB_EXP2_UNDERFLOW)

        def k_body(state):
            jr, _ = state
            c0 = pl.multiple_of((i - 1 - jr) * tq, tq)
            kt = [k_ref[pl.ds(c0, tq), cs] for cs in heads]
            vt = [v_ref[pl.ds(c0, tq), cs] for cs in heads]
            scores = _sb_scores(qs, kt, [None] * nh, [neg_full] * nh)
            live = jnp.float32(-jnp.inf)
            for hh, cs in enumerate(heads):
                carry = carry_ref[hh]
                acc_ref[:, cs] += _sb_weighted(scores, hh, carry, None, vt[hh])
                carry = carry + scores[3][hh]
                carry_ref[hh] = carry
                live = jnp.maximum(live, jnp.max(carry))
            return jr + 1, live

        lax.while_loop(k_cond, k_body, (jnp.int32(1), live))
        o_ref[pl.ds(r0, tq), :] = acc_ref[...].astype(o_ref.dtype)
        return 0

    lax.fori_loop(0, nq, q_body, 0)


def sb_call(qkv, B, T):
    npair = SB_WIDTH // LANES
    blk = lambda off: pl.BlockSpec((None, T, LANES), lambda b, hp: (b, 0, off + hp))
    return pl.pallas_call(
        functools.partial(_sb_kernel, T=T),
        grid=(B, npair),
        in_specs=[blk(0), blk(npair), blk(2 * npair)],
        out_specs=pl.BlockSpec((None, T, LANES), lambda b, hp: (b, 0, hp)),
        out_shape=jax.ShapeDtypeStruct((B, T, SB_WIDTH), BF16),
        scratch_shapes=[pltpu.VMEM((SB_TILE, LANES), F32),
                        pltpu.VMEM((LANES // HEAD_DIM, SB_TILE, 1), F32)],
        compiler_params=pltpu.CompilerParams(dimension_semantics=("parallel", "parallel")),
        name="stick_breaking_attention",
    )(qkv, qkv, qkv)


def _route(n, w_ref, b_ref, id_ref, p_ref):
    nh = n.astype(BF16)
    nl = (n - nh.astype(F32)).astype(BF16)
    a = _dot(nh, w_ref[...])
    b = _dot(nl, w_ref[...])
    L = ROUTER_LANES
    logits = a[:, :L] + (a[:, L:] + b[:, :L] + b[:, L:]) + b_ref[...]
    lane = lax.broadcasted_iota(jnp.int32, logits.shape, 1)
    neg = jnp.float32(-jnp.inf)
    logits = jnp.where(lane < N_EXPERTS, logits, neg)
    m1 = jnp.max(logits, axis=-1, keepdims=True)
    i1 = jnp.min(jnp.where(logits == m1, lane, ROUTER_LANES), axis=-1, keepdims=True)
    rest = jnp.where(lane == i1, neg, logits)
    m2 = jnp.max(rest, axis=-1, keepdims=True)
    i2 = jnp.min(jnp.where(rest == m2, lane, ROUTER_LANES), axis=-1, keepdims=True)
    e2 = jnp.exp(m2 - m1)
    p1 = 1.0 / (1.0 + e2)
    p2 = e2 / (1.0 + e2)
    id_ref[...] = jnp.where(lane == 0, i1, jnp.where(lane == 1, i2, 0))
    p_ref[...] = jnp.where(lane == 0, p1, jnp.where(lane == 1, p2, 0.0))


def _merge_kernel(ygm_ref, yrw_ref, ysb_ref, gate_ref, h_ref, pgm_ref, prw_ref, psb_ref, wo_ref,
                  g_ref, *refs, route):
    D = D_MODEL
    merged = gate_ref[:, 0:D].astype(F32) * _dot(ygm_ref[...], pgm_ref[...])
    merged += gate_ref[:, D:2 * D].astype(F32) * _dot(yrw_ref[...], prw_ref[...])
    merged += gate_ref[:, 2 * D:3 * D].astype(F32) * _dot(ysb_ref[...], psb_ref[...])
    hn = h_ref[...] + _dot(merged.astype(BF16), wo_ref[...])
    n = _rms(hn, g_ref[...])
    if route:
        w_ref, b_ref, ho_ref, id_ref, p_ref = refs
        _route(n, w_ref, b_ref, id_ref, p_ref)
    else:
        ho_ref, no_ref = refs
        no_ref[...] = n.astype(no_ref.dtype)
    ho_ref[...] = hn


def _resident(shape):
    return pl.BlockSpec(shape, lambda i: (0,) * len(shape), pipeline_mode=pl.Buffered(1))


def merge_call(y_gm, y_rw, y_sb, gates, h, p_gm, p_rw, p_sb, w_o, g_next, router=None, tm=256):
    M, D = h.shape
    tm = min(tm, M)
    rowblk = lambda n: pl.BlockSpec((tm, n), lambda i: (i, 0))
    wbytes = 2 * (GM_WIDTH + RW_WIDTH + SB_WIDTH + D) * D
    need = wbytes + 2 * tm * (2 * (GM_WIDTH + RW_WIDTH + SB_WIDTH + GATE_COLS) + 4 * D + 4 * D + 2 * D) \
        + 6 * tm * D * 4
    in_specs = [rowblk(GM_WIDTH), rowblk(RW_WIDTH), rowblk(SB_WIDTH), rowblk(GATE_COLS), rowblk(D),
                _resident((GM_WIDTH, D)), _resident((RW_WIDTH, D)), _resident((SB_WIDTH, D)),
                _resident((D, D)), _resident((1, D))]
    args = [y_gm, y_rw, y_sb, gates, h, p_gm, p_rw, p_sb, w_o, g_next.reshape(1, D)]
    if router is None:
        out_specs = [rowblk(D), rowblk(D)]
        out_shape = [jax.ShapeDtypeStruct((M, D), F32), jax.ShapeDtypeStruct((M, D), BF16)]
    else:
        router_w, router_b = router
        wpad = jnp.zeros((D, ROUTER_LANES), F32).at[:, :N_EXPERTS].set(router_w)
        w_hi = wpad.astype(BF16)
        w_lo = (wpad - w_hi.astype(F32)).astype(BF16)
        bpad = jnp.zeros((1, ROUTER_LANES), F32).at[0, :N_EXPERTS].set(router_b)
        in_specs += [_resident((D, 2 * ROUTER_LANES)), _resident((1, ROUTER_LANES))]
        args += [jnp.concatenate([w_hi, w_lo], axis=1), bpad]
        out_specs = [rowblk(D), rowblk(ROUTER_LANES), rowblk(ROUTER_LANES)]
        out_shape = [jax.ShapeDtypeStruct((M, D), F32), jax.ShapeDtypeStruct((M, ROUTER_LANES), jnp.int32),
                     jax.ShapeDtypeStruct((M, ROUTER_LANES), F32)]
    return pl.pallas_call(
        functools.partial(_merge_kernel, route=router is not None),
        grid=(M // tm,),
        in_specs=in_specs,
        out_specs=out_specs,
        out_shape=out_shape,
        compiler_params=pltpu.CompilerParams(dimension_semantics=("parallel",),
                                             vmem_limit_bytes=_vmem_limit(need)),
        name="merge_out_proj",
    )(*args)


def _ffn_kernel(n_ref, h_ref, w1_ref, w3_ref, w2_ref, g_ref, *refs, emit_h, n_cast):
    cast_in, refs = refs[:n_cast], refs[n_cast:]
    if emit_h:
        ho_ref, no_ref = refs[:2]
        refs = refs[2:]
    else:
        no_ref = refs[0]
        refs = refs[1:]
    cast_out, acc_ref = refs[:n_cast], refs[n_cast]
    f = pl.program_id(1)

    @pl.when(f == 0)
    def _():
        acc_ref[...] = jnp.zeros_like(acc_ref)

    n = n_ref[...]
    a = _dot(n, w1_ref[...])
    b = _dot(n, w3_ref[...])
    for src, dst in zip(cast_in, cast_out):
        dst[...] = src[...].astype(dst.dtype)
    act = (jax.nn.silu(a) * b).astype(BF16)
    acc_ref[...] += _dot(act, w2_ref[...])

    @pl.when(f == pl.num_programs(1) - 1)
    def _():
        hn = h_ref[...] + acc_ref[...]
        if emit_h:
            ho_ref[...] = hn
        no_ref[...] = _rms(hn, g_ref[...]).astype(no_ref.dtype)


def ffn_steps(M, F, tm=512, tf=512):
    return (M // min(tm, M)) * (F // tf)


RIDER_SLAB_ROWS = 64


def can_ride(w, steps):
    return w.shape[1] % RIDER_SLAB_ROWS == 0 and w.shape[0] * (w.shape[1] // RIDER_SLAB_ROWS) <= steps


def ffn_call(n, h, w1, w3, w2, g_next, n_dtype, emit_h, riders=(), tm=512, tf=512):
    M, D = h.shape
    F = w1.shape[1]
    tm = min(tm, M)
    nf = F // tf
    steps = (M // tm) * nf
    rowblk = pl.BlockSpec((tm, D), lambda i, f: (i, 0))
    out_specs = [rowblk]
    out_shape = [jax.ShapeDtypeStruct((M, D), n_dtype)]
    if emit_h:
        out_specs = [rowblk, rowblk]
        out_shape = [jax.ShapeDtypeStruct((M, D), F32)] + out_shape
    n_main = len(out_shape)

    def slab_spec(w):
        E, R, C = w.shape
        per = R // RIDER_SLAB_ROWS
        n_slabs = E * per
        hold = steps // n_slabs
        slab = lambda i, f: jnp.minimum((i * nf + f) // hold, n_slabs - 1)
        return pl.BlockSpec((None, RIDER_SLAB_ROWS, C), lambda i, f: (slab(i, f) // per, slab(i, f) % per, 0))

    ride_specs = [slab_spec(w) for w in riders]
    need = 2 * tm * D * (2 + 4 + 4 * n_main) + tm * D * 4 + 2 * 3 * D * tf * 2 + 4 * tm * tf * 4 \
        + sum(2 * RIDER_SLAB_ROWS * w.shape[2] * 6 for w in riders)
    outs = pl.pallas_call(
        functools.partial(_ffn_kernel, emit_h=emit_h, n_cast=len(riders)),
        grid=(M // tm, nf),
        in_specs=[rowblk, rowblk,
                  pl.BlockSpec((D, tf), lambda i, f: (0, f)),
                  pl.BlockSpec((D, tf), lambda i, f: (0, f)),
                  pl.BlockSpec((tf, D), lambda i, f: (f, 0)),
                  pl.BlockSpec((1, D), lambda i, f: (0, 0))] + ride_specs,
        out_specs=out_specs + ride_specs,
        out_shape=out_shape + [jax.ShapeDtypeStruct(w.shape, BF16) for w in riders],
        scratch_shapes=[pltpu.VMEM((tm, D), F32)],
        compiler_params=pltpu.CompilerParams(dimension_semantics=("arbitrary", "arbitrary"),
                                             vmem_limit_bytes=_vmem_limit(need + (8 << 20))),
        name="swiglu_ffn",
    )(n, h, w1, w3, w2, g_next.reshape(1, D), *riders)
    return outs[:n_main], outs[n_main:]


def moe_plan(ids, M, tm):
    E = N_EXPERTS
    n_tiles = -(-(TOP_K * M) // tm) + E + 2
    e_flat = ids[:, :TOP_K].reshape(-1)
    onehot = (e_flat[:, None] == jnp.arange(E, dtype=jnp.int32)[None, :]).astype(jnp.int32)
    csum = jnp.cumsum(onehot, axis=0)
    rank = jnp.sum((csum - onehot) * onehot, axis=1)
    counts = csum[-1]
    tiles_per_e = (counts + tm - 1) // tm
    tile_end = jnp.cumsum(tiles_per_e)
    offs = (tile_end - tiles_per_e) * tm
    dest = offs[e_flat] + rank
    pair = jnp.full((n_tiles * tm,), -1, jnp.int32).at[dest].set(
        jnp.arange(TOP_K * M, dtype=jnp.int32), unique_indices=True)
    spare = TOP_K * M + jnp.arange(n_tiles * tm, dtype=jnp.int32) % tm
    src_token = jnp.where(pair >= 0, pair // TOP_K, 0)
    out_row = jnp.where(pair >= 0, (pair % TOP_K) * M + pair // TOP_K, spare)
    prev_out_row = jnp.concatenate([spare[:tm], out_row[:-tm]])
    n_used = tile_end[-1]
    t = jnp.arange(n_tiles, dtype=jnp.int32)
    tile_e = jnp.searchsorted(tile_end, jnp.minimum(t, n_used - 1), side="right").astype(jnp.int32)
    return (src_token.reshape(n_tiles, 1, tm), prev_out_row.reshape(n_tiles, 1, tm), tile_e,
            n_used.astype(jnp.int32).reshape(1))


def _bf16_bits(x):
    u = lax.bitcast_convert_type(x, jnp.uint32)
    return (u + jnp.uint32(0x7FFF) + ((u >> 16) & jnp.uint32(1))) & jnp.uint32(0xFFFF0000)


def _pack_bf16_pairs(x):
    half = x.shape[1] // 2
    return _bf16_bits(x[:, half:]) | (_bf16_bits(x[:, :half]) >> 16)


def _unpack_bf16_pairs(w):
    lo = lax.bitcast_convert_type(w << 16, F32)
    hi = lax.bitcast_convert_type(w & jnp.uint32(0xFFFF0000), F32)
    return jnp.concatenate([lo, hi], axis=1)


def _moe_kernel(tile_e_ref, n_used_ref, src_ref, nxt_ref, dst_ref, h_hbm, g_ref, w1_ref, w3_ref, w2_ref,
                y_hbm, xbuf, xb, acc, ostage, gsem, ssem):
    t = pl.program_id(0)
    f = pl.program_id(1)
    nf = pl.num_programs(1)
    tm = xb.shape[0]
    per_step = tm // nf
    n_used = n_used_ref[0]
    used = t < n_used
    cur = t % 2

    def fetch(idx_ref, r, slot):
        return pltpu.make_async_copy(h_hbm.at[pl.ds(idx_ref[0, r], 1)], xbuf.at[slot, pl.ds(r, 1)],
                                     gsem.at[slot])

    def put(r):
        return pltpu.make_async_copy(ostage.at[1 - cur, pl.ds(r, 1)], y_hbm.at[pl.ds(dst_ref[0, r], 1)],
                                     ssem.at[1 - cur])

    @pl.when((f == 0) & (t == 0))
    def _():
        def issue(r, c):
            fetch(src_ref, r, 0).start()
            return c
        lax.fori_loop(0, tm, issue, 0)
        ostage[1] = jnp.zeros((tm, ostage.shape[2]), ostage.dtype)

    @pl.when((f == 0) & (t <= n_used))
    def _():
        pltpu.make_async_copy(h_hbm.at[pl.ds(0, tm)], xbuf.at[cur], gsem.at[cur]).wait()

    @pl.when((f == 0) & (t >= 1) & (t - 1 <= n_used))
    def _():
        pltpu.make_async_copy(ostage.at[cur], y_hbm.at[pl.ds(0, tm)], ssem.at[cur]).wait()

    @pl.when((f == 0) & used)
    def _():
        xb[...] = _rms(xbuf[cur], g_ref[...]).astype(BF16)
        acc[...] = jnp.zeros_like(acc)

    @pl.when(used)
    def _():
        for j in range(per_step):
            fetch(nxt_ref, f * per_step + j, 1 - cur).start()
            put(f * per_step + j).start(priority=j % 2)
        x = xb[...]
        act = jax.nn.silu(_dot(x, w1_ref[...])) * _dot(x, w3_ref[...])
        acc[...] += _dot(act.astype(BF16), w2_ref[...])

    @pl.when(used & (f == nf - 1))
    def _():
        ostage[cur] = _pack_bf16_pairs(acc[...])

    @pl.when((f == 0) & (t == n_used))
    def _():
        def issue(r, c):
            put(r).start()
            return c
        lax.fori_loop(0, tm, issue, 0)


def moe_call(h, g, src_token, prev_out_row, tile_e, n_used, w1, w3, w2, tm, tf):
    M, D = h.shape
    n_tiles = src_token.shape[0]
    E, _, Fe = w1.shape
    nf = Fe // tf
    assert tm % nf == 0 and Fe % tf == 0

    def wmap(t, f, tile_e_ref, n_used_ref):
        ff = jnp.where(t < n_used_ref[0], f, nf - 1)
        return tile_e_ref[t], ff

    smem_tile = lambda fn: pl.BlockSpec((None, 1, tm), fn, memory_space=pltpu.SMEM)
    grid_spec = pltpu.PrefetchScalarGridSpec(
        num_scalar_prefetch=2,
        grid=(n_tiles, nf),
        in_specs=[smem_tile(lambda t, f, te, nu: (t, 0, 0)),
                  smem_tile(lambda t, f, te, nu: (jnp.minimum(t + 1, n_tiles - 1), 0, 0)),
                  smem_tile(lambda t, f, te, nu: (t, 0, 0)),
                  pl.BlockSpec(memory_space=pl.ANY),
                  pl.BlockSpec((1, D), lambda t, f, te, nu: (0, 0)),
                  pl.BlockSpec((None, D, tf), lambda t, f, te, nu: (wmap(t, f, te, nu)[0], 0,
                                                                     wmap(t, f, te, nu)[1])),
                  pl.BlockSpec((None, D, tf), lambda t, f, te, nu: (wmap(t, f, te, nu)[0], 0,
                                                                     wmap(t, f, te, nu)[1])),
                  pl.BlockSpec((None, tf, D), lambda t, f, te, nu: wmap(t, f, te, nu) + (0,))],
        out_specs=pl.BlockSpec(memory_space=pl.ANY),
        scratch_shapes=[pltpu.VMEM((2, tm, D), F32), pltpu.VMEM((tm, D), BF16), pltpu.VMEM((tm, D), F32),
                        pltpu.VMEM((2, tm, D // 2), jnp.uint32),
                        pltpu.SemaphoreType.DMA((2,)), pltpu.SemaphoreType.DMA((2,))])
    need = tm * D * (8 + 2 + 4 + 4) + 2 * 3 * D * tf * 2 + 4 * tm * tf * 4
    return pl.pallas_call(
        _moe_kernel,
        grid_spec=grid_spec,
        out_shape=jax.ShapeDtypeStruct((TOP_K * M + tm, D // 2), jnp.uint32),
        compiler_params=pltpu.CompilerParams(dimension_semantics=("arbitrary", "arbitrary"),
                                             vmem_limit_bytes=_vmem_limit(need + (8 << 20))),
        name="moe_experts",
    )(tile_e, n_used, src_token, src_token, prev_out_row, h, g.reshape(1, D), w1, w3, w2)


def _moe_combine_kernel(h_ref, p_ref, g_ref, *refs, emit_h):
    y_refs, out_refs = refs[:TOP_K], refs[TOP_K:]
    hn = h_ref[...]
    for k in range(TOP_K):
        hn = hn + p_ref[:, k:k + 1] * _unpack_bf16_pairs(y_refs[k][...])
    if emit_h:
        out_refs[0][...] = hn
    out_refs[-1][...] = _rms(hn, g_ref[...]).astype(out_refs[-1].dtype)


def moe_combine_call(h, y_tok, probs, g_next, n_dtype, emit_h, tm=MOE_COMBINE_TM):
    M, D = h.shape
    tm = min(tm, M)
    rowblk = pl.BlockSpec((tm, D), lambda i: (i, 0))
    out_specs = [rowblk]
    out_shape = [jax.ShapeDtypeStruct((M, D), n_dtype)]
    if emit_h:
        out_specs = [rowblk, rowblk]
        out_shape = [jax.ShapeDtypeStruct((M, D), F32)] + out_shape
    y_specs = [pl.BlockSpec((tm, D // 2), functools.partial(lambda k, i: (k * (M // tm) + i, 0), k))
               for k in range(TOP_K)]
    return pl.pallas_call(
        functools.partial(_moe_combine_kernel, emit_h=emit_h),
        grid=(M // tm,),
        in_specs=[rowblk, pl.BlockSpec((tm, ROUTER_LANES), lambda i: (i, 0)),
                  pl.BlockSpec((1, D), lambda i: (0, 0))] + y_specs,
        out_specs=out_specs,
        out_shape=out_shape,
        compiler_params=pltpu.CompilerParams(dimension_semantics=("parallel",),
                                             vmem_limit_bytes=_vmem_limit(14 * tm * D * 4)),
        name="moe_combine",
    )(h, probs, g_next.reshape(1, D), *([y_tok] * TOP_K))


def kernel(x, norm_mix, w_in, gate_b, gm_ln_g, gm_ln_b, gm_ws, gm_bs, rw_mu, rw_w0, rw_w2, rw_a0, rw_a2,
           rw_g2, rw_kk, rw_ka, rw_rk, rw_ln_g, rw_ln_b, p_gm, p_rw, p_sb, w_o, norm_ffn, ffn_w1, ffn_w3,
           ffn_w2, router_w, router_b, moe_w1, moe_w3, moe_w2, norm_out):
    B, T, D = x.shape
    M = B * T
    bf = lambda a: a.astype(BF16)
    h = x.reshape(M, D)
    n = rmsnorm_call(h, norm_mix[0], BF16)
    moe_bf16 = {}
    for l in range(DEPTH):
        last = l == DEPTH - 1
        g_next = norm_out if last else norm_mix[l + 1]
        n_dtype = x.dtype if last else BF16
        w = w_in[l]
        j = l // 2
        uv = matmul_call(n, bf(w[:, GM_OFF:RW_OFF]), BF16, tn=GM_COLS, name="in_proj_gm")
        p = matmul_call(n, bf(w[:, RW_OFF:SB_OFF]), BF16, tn=RW_COLS, name="in_proj_rw")
        qkv = matmul_call(n, bf(w[:, SB_OFF:GATE_OFF]), BF16, tn=1024, name="in_proj_sb")
        gates = matmul_call(n, bf(w[:, GATE_OFF:]), BF16, tn=1024, bias=gate_b[l].reshape(-1),
                            name="in_proj_gate")
        y_gm = gm_call(uv, gm_ln_g[l], gm_ln_b[l], gm_ws[l], gm_bs[l])
        y_rw = rw_call(p.reshape(B, T, RW_COLS), B, T, rw_mu[l], rw_w0[l], rw_w2[l], rw_a0[l], rw_a2[l],
                       rw_g2[l], rw_kk[l], rw_ka[l], rw_rk[l], rw_ln_g[l], rw_ln_b[l]).reshape(M, RW_WIDTH)
        y_sb = sb_call(qkv.reshape(B, T, SB_COLS), B, T).reshape(M, SB_WIDTH)
        merge_args = (y_gm, y_rw, y_sb, gates, h, bf(p_gm[l]), bf(p_rw[l]), bf(p_sb[l]), bf(w_o[l]), norm_ffn[l])
        if l % 2 == 0:
            h, n2 = merge_call(*merge_args)
            ahead = []
            if not last:
                ahead = [(k, wt[(l + 1) // 2]) for k, wt in (("w1", moe_w1), ("w3", moe_w3), ("w2", moe_w2))
                         if can_ride(wt[(l + 1) // 2], ffn_steps(M, D_FF))]
            outs, casts = ffn_call(n2, h, bf(ffn_w1[j]), bf(ffn_w3[j]), bf(ffn_w2[j]), g_next, n_dtype,
                                   emit_h=not last, riders=[wt for _, wt in ahead])
            moe_bf16 = {k: c for (k, _), c in zip(ahead, casts)}
        else:
            h, ids, probs = merge_call(*merge_args, router=(router_w[j], router_b[j]))
            src_token, prev_out_row, tile_e, n_used = moe_plan(ids, M, MOE_TM)
            ew = {k: moe_bf16[k] if k in moe_bf16 else bf(wt[j])
                  for k, wt in (("w1", moe_w1), ("w3", moe_w3), ("w2", moe_w2))}
            y_tok = moe_call(h, norm_ffn[l], src_token, prev_out_row, tile_e, n_used, ew["w1"], ew["w3"],
                             ew["w2"], MOE_TM, MOE_TF)
            outs = moe_combine_call(h, y_tok, probs, g_next, n_dtype, emit_h=not last)
            moe_bf16 = {}
        if last:
            n = outs[0]
        else:
            h, n = outs
    return n.reshape(B, T, D)
```
